```python
import math
import jax, jax.numpy as jnp
from jax import lax
import numpy as np

D_MODEL = 1024
BATCH = 2
SEQ = 8192
DEPTH = 2
DEC_BATCH = 32
DEC_SEQ = 1
PAST_LEN = 16384
PAGE_SIZE = 128

N_EVEN = (DEPTH + 1) // 2
N_ODD = DEPTH // 2
RET_HEADS = 4
RET_DK = 128
RET_DV = 128
ROPE_BASE = 10000.0
ML_HEADS = 4
ML_DK = 128
ML_DV = 128
CONV_W = 4
CHUNK = 128
AB_SPLITS = (RET_HEADS * RET_DK, RET_HEADS * RET_DK, RET_HEADS * RET_DV, RET_HEADS * RET_DV,
             2 * ML_HEADS * ML_DK, ML_HEADS * ML_DV, ML_HEADS * ML_DV, ML_HEADS, ML_HEADS)
AB_IN = sum(AB_SPLITS)
AB_OUT = RET_HEADS * RET_DV + ML_HEADS * ML_DV
NSA_HEADS = 16
NSA_KV_HEADS = 4
NSA_REP = NSA_HEADS // NSA_KV_HEADS
NSA_HD = 64
CMP_LEN = 32
CMP_STRIDE = 16
CMP_HID = 128
SEL_BLOCK = 64
SEL_TOPK = 16
WINDOW = 512
Q_BLOCK = 128
FORCE_SCORE = 1e4
NSA_SPLITS = (NSA_HEADS * NSA_HD, 6 * NSA_KV_HEADS * NSA_HD, 3 * NSA_HEADS)
NSA_IN = sum(NSA_SPLITS)
REL_BUCKETS = 32
REL_MAX_DIST = 1024
MOE_GROUPS = 4
MOE_EXP_PER_GROUP = 8
MOE_EXPERTS = MOE_GROUPS * MOE_EXP_PER_GROUP
MOE_TOPK = 2
MOE_FF = 512
MOE_ROW_BLOCK = 128

NEG = -1e30
EPS = 1e-6
F32 = jnp.float32

kernel_name = 'hybrid_retention_mlstm_nsa_hmoe_step'


def split_cols(z, sizes):
    offsets = [int(v) for v in np.cumsum(sizes)[:-1]]
    return jnp.split(z, offsets, axis=-1)


def rmsnorm(x, g):
    xf = x.astype(F32)
    y = xf * lax.rsqrt(jnp.mean(xf * xf, axis=-1, keepdims=True) + EPS)
    return (y * g.astype(F32)).astype(x.dtype)


def rope(x, pos):
    half = x.shape[-1] // 2
    freqs = ROPE_BASE ** (-jnp.arange(half, dtype=F32) / half)
    ang = pos.astype(F32)[:, None] * freqs[None, :]
    cos = jnp.cos(ang)[None, :, None, :]
    sin = jnp.sin(ang)[None, :, None, :]
    xf = x.astype(F32)
    x1, x2 = xf[..., :half], xf[..., half:]
    return jnp.concatenate([x1 * cos - x2 * sin, x1 * sin + x2 * cos], axis=-1)


def chunk_len(L):
    return CHUNK if L % CHUNK == 0 else L


def to_chunks(x, c):
    B, L, H = x.shape[:3]
    x = x.reshape((B, L // c, c, H) + x.shape[3:])
    return jnp.moveaxis(x, (1, 3), (0, 2))


def from_chunks(y):
    nC, B, H, c = y.shape[:4]
    y = jnp.moveaxis(y, (0, 2), (1, 3))
    return y.reshape((B, nC * c, H) + y.shape[4:])


def retention(q, k, v, S0):
    q, k, v = q.astype(F32), k.astype(F32), v.astype(F32)
    c = chunk_len(q.shape[1])
    log_g = jnp.log1p(-jnp.exp2(-5.0 - jnp.arange(RET_HEADS, dtype=F32)))
    i = jnp.arange(c, dtype=F32)
    diff = i[:, None] - i[None, :]
    decay = jnp.where(diff >= 0, jnp.exp(jnp.maximum(diff, 0.0)[None] * log_g[:, None, None]), 0.0)
    q_dec = jnp.exp((i + 1.0)[None, :] * log_g[:, None])[None, :, :, None]
    k_dec = jnp.exp((c - 1.0 - i)[None, :] * log_g[:, None])[None, :, :, None]
    s_dec = jnp.exp(c * log_g)[None, :, None, None]

    def step(S, inp):
        qc, kc, vc = inp
        a = jnp.einsum('bhid,bhjd->bhij', qc, kc) * decay
        o = jnp.einsum('bhij,bhje->bhie', a, vc) + q_dec * jnp.einsum('bhid,bhde->bhie', qc, S)
        S = s_dec * S + jnp.einsum('bhjd,bhje->bhde', kc * k_dec, vc)
        return S, o

    S, o = lax.scan(step, S0.astype(F32), (to_chunks(q, c), to_chunks(k, c), to_chunks(v, c)))
    return from_chunks(o), S


def mlstm(q, k, v, ig, lf, C0, n0, m0):
    q, k, v = q.astype(F32), k.astype(F32), v.astype(F32)
    c = chunk_len(q.shape[1])
    causal = jnp.tril(jnp.ones((c, c), dtype=bool))

    def step(carry, inp):
        C, n, m = carry
        qc, kc, vc, ic, fc = inp
        b = jnp.cumsum(fc, axis=-1)
        dlog = jnp.where(causal, b[..., :, None] - b[..., None, :] + ic[..., None, :], -jnp.inf)
        inter = b + m[..., None]
        m_t = jnp.maximum(inter, jnp.max(dlog, axis=-1))
        w = jnp.einsum('bhid,bhjd->bhij', qc, kc) * jnp.exp(dlog - m_t[..., None])
        e_inter = jnp.exp(inter - m_t)
        num = jnp.einsum('bhij,bhje->bhie', w, vc) + e_inter[..., None] * jnp.einsum('bhek,bhik->bhie', C, qc)
        den = jnp.sum(w, axis=-1) + e_inter * jnp.einsum('bhk,bhik->bhi', n, qc)
        hc = num / jnp.maximum(jnp.abs(den), jnp.exp(-m_t))[..., None]
        bL = b[..., -1]
        m_new = jnp.maximum(bL + m, jnp.max(bL[..., None] - b + ic, axis=-1))
        ws = jnp.exp(bL[..., None] - b + ic - m_new[..., None])
        f_state = jnp.exp(bL + m - m_new)
        C = f_state[..., None, None] * C + jnp.einsum('bhj,bhje,bhjk->bhek', ws, vc, kc)
        n = f_state[..., None] * n + jnp.einsum('bhj,bhjk->bhk', ws, kc)
        return (C, n, m_new), hc

    xs = (to_chunks(q, c), to_chunks(k, c), to_chunks(v, c), to_chunks(ig, c), to_chunks(lf, c))
    (C, n, m), h = lax.scan(step, (C0.astype(F32), n0.astype(F32), m0.astype(F32)), xs)
    return from_chunks(h), C, n, m


def mixer_ab(h, pos, S0, C0, n0, m0, conv0, w_in, conv_w, conv_b, b_ig, b_fg, gn_g, gn_b, hn_g, w_out):
    B, L, _ = h.shape
    rq, rk, rv, rg, mqk, mv, mo, mi, mf = split_cols(h @ w_in, AB_SPLITS)
    rq = rope(rq.reshape(B, L, RET_HEADS, RET_DK), pos)
    rk = rope(rk.reshape(B, L, RET_HEADS, RET_DK), pos) * RET_DK ** -0.5
    o_r, S = retention(rq, rk, rv.reshape(B, L, RET_HEADS, RET_DV), S0)
    mu = jnp.mean(o_r, axis=-1, keepdims=True)
    var = jnp.mean(jnp.square(o_r - mu), axis=-1, keepdims=True)
    o_r = ((o_r - mu) * lax.rsqrt(var + EPS) * gn_g.astype(F32).reshape(RET_HEADS, RET_DV)
           + gn_b.astype(F32).reshape(RET_HEADS, RET_DV))
    y_a = jax.nn.silu(rg.astype(F32)) * o_r.reshape(B, L, RET_HEADS * RET_DV)
    conv_in = jnp.concatenate([conv0.astype(mqk.dtype), mqk], axis=1)
    qk = lax.conv_general_dilated(conv_in, conv_w[:, None, :].astype(mqk.dtype), window_strides=(1,),
                                  padding='VALID', dimension_numbers=('NWC', 'WIO', 'NWC'),
                                  feature_group_count=conv_in.shape[-1])
    qk = jax.nn.silu(qk.astype(F32) + conv_b.astype(F32))
    mq, mk = jnp.split(qk, 2, axis=-1)
    mq = mq.reshape(B, L, ML_HEADS, ML_DK)
    mk = mk.reshape(B, L, ML_HEADS, ML_DK) * ML_DK ** -0.5
    ig = mi.astype(F32) + b_ig.astype(F32)
    lf = jax.nn.log_sigmoid(mf.astype(F32) + b_fg.astype(F32))
    h_m, C, n, m = mlstm(mq, mk, mv.reshape(B, L, ML_HEADS, ML_DV), ig, lf, C0, n0, m0)
    h_m = jax.nn.sigmoid(mo.astype(F32)).reshape(B, L, ML_HEADS, ML_DV) * h_m
    y_b = rmsnorm(h_m, hn_g.reshape(ML_HEADS, ML_DV)).reshape(B, L, ML_HEADS * ML_DV)
    y = jnp.concatenate([y_a, y_b], axis=-1) @ w_out.astype(F32)
    return y.astype(h.dtype), S, C, n, m, conv_in[:, -(CONV_W - 1):]


def t5_bucket(dist):
    n = jnp.maximum(dist, 0)
    exact = REL_BUCKETS // 2
    nf = jnp.maximum(n, 1).astype(F32)
    large = exact + (jnp.log(nf / exact) / math.log(REL_MAX_DIST / exact) * (REL_BUCKETS - exact)).astype(jnp.int32)
    return jnp.where(n < exact, n, jnp.minimum(large, REL_BUCKETS - 1))


def dense_bias(tbl, dist):
    nq, nk = dist.shape
    b = tbl.astype(F32)[t5_bucket(dist)].reshape(nq, nk, NSA_KV_HEADS, NSA_REP)
    return jnp.transpose(b, (0, 2, 3, 1))


def grouped_bias(tbl, dist):
    tg = tbl.astype(F32).reshape(REL_BUCKETS, NSA_KV_HEADS, NSA_REP)
    b = tg[t5_bucket(dist), jnp.arange(NSA_KV_HEADS)[None, None, :, None]]
    return jnp.swapaxes(b, -1, -2)


def masked_softmax(s, valid):
    return jax.nn.softmax(jnp.where(valid, s, NEG), axis=-1) * valid


def nsa_project(h, w_in, q_norm, k_norm):
    B, L, _ = h.shape
    zq, zkv, zg = split_cols(h @ w_in, NSA_SPLITS)
    q = rmsnorm(zq.reshape(B, L, NSA_HEADS, NSA_HD), q_norm).astype(F32) * NSA_HD ** -0.5
    kv = zkv.reshape(B, L, 3, 2, NSA_KV_HEADS, NSA_HD)
    kv_cmp = kv[:, :, 0]
    kv_sel = jnp.stack([rmsnorm(kv[:, :, 1, 0], k_norm[1]), kv[:, :, 1, 1]], axis=2)
    kv_win = jnp.stack([rmsnorm(kv[:, :, 2, 0], k_norm[2]), kv[:, :, 2, 1]], axis=2)
    gates = jax.nn.sigmoid(zg.astype(F32)).reshape(B, L, NSA_KV_HEADS, NSA_REP, 3)
    return q.reshape(B, L, NSA_KV_HEADS, NSA_REP, NSA_HD), kv_cmp, kv_sel, kv_win, gates


def cmp_partial(rows, w1):
    B, L = rows.shape[:2]
    n_sub = L // CMP_STRIDE
    sub = rows[:, :n_sub * CMP_STRIDE].reshape(B, n_sub, CMP_STRIDE, 2, NSA_KV_HEADS, NSA_HD)
    w = w1.reshape(2, CMP_LEN // CMP_STRIDE, CMP_STRIDE, NSA_HD, CMP_HID)
    return jnp.einsum('bnsvgd,vasdf->bnvagf', sub, w, preferred_element_type=F32)


def compress(P, b1, w2, b2, k_norm0):
    r = CMP_LEN // CMP_STRIDE
    n_cmp = P.shape[1] - r + 1
    hsum = sum(P[:, a:a + n_cmp, :, a] for a in range(r))
    hid = jax.nn.gelu(hsum + b1.astype(F32)[None, None, :, None, :])
    out = jnp.einsum('bnvgf,vfd->bnvgd', hid, w2.astype(F32)) + b2.astype(F32)[None, None, :, None, :]
    return rmsnorm(out[:, :, 0], k_norm0), out[:, :, 1]


def nsa_attend(q, gates, q_pos, k_c, v_c, c_end, gather_sel, n_sel, kv_w, w_pos, tbl):
    B, nq = q.shape[:2]
    dist_c = q_pos[:, None] - c_end[None, :]
    s_c = jnp.einsum('bqgrd,bkgd->bqgrk', q, k_c.astype(F32)) + dense_bias(tbl, dist_c)[None]
    p_c = masked_softmax(s_c, (dist_c >= 0)[None, :, None, None, :])
    o_c = jnp.einsum('bqgrk,bkgd->bqgrd', p_c, v_c.astype(F32))
    imp = p_c.sum(axis=3)
    r = CMP_LEN // CMP_STRIDE
    sub = sum(jnp.pad(imp, ((0, 0), (0, 0), (0, 0), (a, r - 1 - a))) for a in range(r))
    spb = SEL_BLOCK // CMP_STRIDE
    sub = jnp.pad(sub, ((0, 0), (0, 0), (0, 0), (0, n_sel * spb - sub.shape[-1])))
    score = sub.reshape(B, nq, NSA_KV_HEADS, n_sel, spb).sum(-1)
    j = jnp.arange(n_sel)[None, :]
    cur = (q_pos // SEL_BLOCK)[:, None]
    forced = (j == 0) | (j == cur) | (j == cur - 1)
    future = j * SEL_BLOCK > q_pos[:, None]
    score = jnp.where(forced[None, :, None, :], FORCE_SCORE, score)
    score = jnp.where(future[None, :, None, :], NEG, score)
    n_top = min(SEL_TOPK, n_sel)
    _, idx = lax.top_k(score, n_top)
    kv_s = gather_sel(idx).reshape(B, nq, NSA_KV_HEADS, n_top * SEL_BLOCK, 2, NSA_HD)
    pos_s = (idx[..., None] * SEL_BLOCK + jnp.arange(SEL_BLOCK)).reshape(B, nq, NSA_KV_HEADS, n_top * SEL_BLOCK)
    dist_s = q_pos[None, :, None, None] - pos_s
    s_s = jnp.einsum('bqgrd,bqgkd->bqgrk', q, kv_s[..., 0, :].astype(F32)) + grouped_bias(tbl, dist_s)
    p_s = masked_softmax(s_s, (dist_s >= 0)[:, :, :, None, :])
    o_s = jnp.einsum('bqgrk,bqgkd->bqgrd', p_s, kv_s[..., 1, :].astype(F32))
    dist_w = q_pos[:, None] - w_pos[None, :]
    valid_w = ((dist_w >= 0) & (dist_w < WINDOW))[None, :, None, None, :]
    s_w = jnp.einsum('bqgrd,bkgd->bqgrk', q, kv_w[:, :, 0].astype(F32)) + dense_bias(tbl, dist_w)[None]
    p_w = masked_softmax(s_w, valid_w)
    o_w = jnp.einsum('bqgrk,bkgd->bqgrd', p_w, kv_w[:, :, 1].astype(F32))
    return gates[..., 0:1] * o_c + gates[..., 1:2] * o_s + gates[..., 2:3] * o_w


def nsa_prompt(h, w_in, q_norm, k_norm, cmp_w1, cmp_b1, cmp_w2, cmp_b2, w_out, tbl):
    B, L, _ = h.shape
    q, kv_cmp, kv_sel, kv_win, gates = nsa_project(h, w_in, q_norm, k_norm)
    k_c, v_c = compress(cmp_partial(kv_cmp, cmp_w1), cmp_b1, cmp_w2, cmp_b2, k_norm[0])
    c_end = jnp.arange(k_c.shape[1]) * CMP_STRIDE + CMP_LEN - 1
    n_sel = -(-L // SEL_BLOCK)
    blocks = jnp.pad(kv_sel, ((0, 0), (0, n_sel * SEL_BLOCK - L), (0, 0), (0, 0), (0, 0)))
    blocks = blocks.reshape(B, n_sel, SEL_BLOCK, 2, NSA_KV_HEADS, NSA_HD).transpose(0, 4, 1, 2, 3, 5)
    bi = jnp.arange(B)[:, None, None, None]
    gi = jnp.arange(NSA_KV_HEADS)[None, None, :, None]

    def gather_sel(idx):
        return blocks[bi, gi, idx]

    win_pad = jnp.pad(kv_win, ((0, 0), (WINDOW, 0), (0, 0), (0, 0), (0, 0)))
    n_qb = L // Q_BLOCK

    def body(args):
        qb, gb, i = args
        s0 = i * Q_BLOCK
        q_pos = s0 + jnp.arange(Q_BLOCK)
        kv_w = lax.dynamic_slice_in_dim(win_pad, s0, WINDOW + Q_BLOCK, axis=1)
        w_pos = s0 - WINDOW + jnp.arange(WINDOW + Q_BLOCK)
        return nsa_attend(qb, gb, q_pos, k_c, v_c, c_end, gather_sel, n_sel, kv_w, w_pos, tbl)

    qbs = jnp.moveaxis(q.reshape(B, n_qb, Q_BLOCK, NSA_KV_HEADS, NSA_REP, NSA_HD), 1, 0)
    gbs = jnp.moveaxis(gates.reshape(B, n_qb, Q_BLOCK, NSA_KV_HEADS, NSA_REP, 3), 1, 0)
    o = lax.map(body, (qbs, gbs, jnp.arange(n_qb)))
    o = jnp.moveaxis(o, 0, 1).reshape(B, L, NSA_HEADS * NSA_HD)
    y = (o @ w_out.astype(F32)).astype(h.dtype)
    return y, kv_cmp, kv_sel, kv_win[:, L - min(WINDOW, L):]


def nsa_sample(h, cache_cmp, cache_sel, win_buf, page_table, li, w_in, q_norm, k_norm,
               cmp_w1, cmp_b1, cmp_w2, cmp_b2, w_out, tbl):
    B, L, _ = h.shape
    past_len = page_table.shape[1] * PAGE_SIZE
    q, kv_cmp, kv_sel, kv_win, gates = nsa_project(h, w_in, q_norm, k_norm)
    q_pos = past_len + jnp.arange(L)
    past_cmp = cache_cmp[li, page_table].reshape(B, past_len, 2, NSA_KV_HEADS, NSA_HD)
    P = jnp.concatenate([cmp_partial(past_cmp, cmp_w1), cmp_partial(kv_cmp, cmp_w1)], axis=1)
    k_c, v_c = compress(P, cmp_b1, cmp_w2, cmp_b2, k_norm[0])
    c_end = jnp.arange(k_c.shape[1]) * CMP_STRIDE + CMP_LEN - 1
    bpp = PAGE_SIZE // SEL_BLOCK
    n_past_blk = past_len // SEL_BLOCK
    n_new_blk = -(-L // SEL_BLOCK)
    n_sel = n_past_blk + n_new_blk
    new_blocks = jnp.pad(kv_sel, ((0, 0), (0, n_new_blk * SEL_BLOCK - L), (0, 0), (0, 0), (0, 0)))
    new_blocks = new_blocks.reshape(B, n_new_blk, SEL_BLOCK, 2, NSA_KV_HEADS, NSA_HD).transpose(0, 4, 1, 2, 3, 5)
    bi = jnp.arange(B)[:, None, None, None]
    gi = jnp.arange(NSA_KV_HEADS)[None, None, :, None]

    def gather_sel(idx):
        jp = jnp.minimum(idx, n_past_blk - 1)
        phys = page_table[bi, jp // bpp]
        rows = (jp % bpp)[..., None] * SEL_BLOCK + jnp.arange(SEL_BLOCK)
        past = cache_sel[li, phys[..., None], rows, :, gi[..., None]]
        new = new_blocks[bi, gi, jnp.clip(idx - n_past_blk, 0, n_new_blk - 1)]
        return jnp.where((idx >= n_past_blk)[..., None, None, None], new, past.astype(new.dtype))

    wb = win_buf.shape[1]
    kv_w = jnp.concatenate([win_buf.astype(kv_win.dtype), kv_win], axis=1)
    w_pos = past_len - wb + jnp.arange(wb + L)
    o = nsa_attend(q, gates, q_pos, k_c, v_c, c_end, gather_sel, n_sel, kv_w, w_pos, tbl)
    y = (o.reshape(B, L, NSA_HEADS * NSA_HD) @ w_out.astype(F32)).astype(h.dtype)
    return y, kv_cmp, kv_sel, kv_w[:, L:]


def grouped_expert_ffn(xt, e_flat, w_flat, w_g, w_u, w_d):
    N, D = xt.shape
    A = e_flat.shape[0]
    K = A // N
    E = w_g.shape[0]
    RB = MOE_ROW_BLOCK
    tok = jnp.arange(A, dtype=jnp.int32) // K
    counts = jax.ops.segment_sum(jnp.ones((A,), jnp.int32), e_flat, num_segments=E)
    starts = jnp.cumsum(counts) - counts
    padded = (counts + RB - 1) // RB * RB
    pend = jnp.cumsum(padded)
    pstart = pend - padded
    order = jnp.argsort(e_flat)
    e_sorted = e_flat[order]
    dest = pstart[e_sorted] + jnp.arange(A, dtype=jnp.int32) - starts[e_sorted]
    R = -(-(A + E * (RB - 1)) // RB) * RB
    row_tok = jnp.full((R,), N, jnp.int32).at[dest].set(tok[order])
    row_w = jnp.zeros((R,), F32).at[dest].set(w_flat[order])
    n_blk = R // RB
    blk_e = jnp.minimum(jnp.searchsorted(pend, jnp.arange(n_blk, dtype=jnp.int32) * RB, side='right'), E - 1)
    x_rows = jnp.concatenate([xt, jnp.zeros((1, D), xt.dtype)], axis=0)[row_tok].reshape(n_blk, RB, D)

    def expert_block(args):
        xb, e = args
        hb = jax.nn.silu(xb @ w_g[e]) * (xb @ w_u[e])
        return hb @ w_d[e]

    y_rows = lax.map(expert_block, (x_rows, blk_e)).reshape(R, D)
    y = jax.ops.segment_sum(y_rows.astype(F32) * row_w[:, None], row_tok, num_segments=N + 1)[:N]
    return y.astype(xt.dtype)


def hier_moe(x, w_group, b_group, w_expert, b_expert, w_g, w_u, w_d):
    B, L, D = x.shape
    N = B * L
    xt = x.reshape(N, D)
    gl = jnp.einsum('nd,dg->ng', xt, w_group, preferred_element_type=F32) + b_group.astype(F32)
    pg = jax.nn.softmax(gl, axis=-1)
    g_top = jnp.argmax(gl, axis=-1)
    el = (jnp.einsum('nd,de->ne', xt, w_expert, preferred_element_type=F32) + b_expert.astype(F32))
    el = el.reshape(N, MOE_GROUPS, MOE_EXP_PER_GROUP)
    el = jnp.take_along_axis(el, g_top[:, None, None], axis=1)[:, 0]
    top_v, top_i = lax.top_k(el, MOE_TOPK)
    gate = jnp.take_along_axis(pg, g_top[:, None], axis=1) * jax.nn.softmax(top_v, axis=-1)
    expert = (g_top[:, None] * MOE_EXP_PER_GROUP + top_i).astype(jnp.int32)
    y = grouped_expert_ffn(xt, expert.reshape(-1), gate.reshape(-1), w_g, w_u, w_d)
    return y.reshape(B, L, D)


def setup_inputs(seed: int = 0) -> dict:
    key = jax.random.key(seed)
    keys = iter(jax.random.split(key, 64))

    def nrm(shape, scale):
        return scale * jax.random.normal(next(keys), shape, F32)

    def gain(shape):
        return 1.0 + nrm(shape, 0.01)

    n_pages = PAST_LEN // PAGE_SIZE
    n_used = DEC_BATCH * n_pages
    n_pool = n_used + (n_used + 3) // 4
    win_buf = min(WINDOW, PAST_LEN)
    page_table = jax.random.permutation(next(keys), n_pool)[:n_used].reshape(DEC_BATCH, n_pages).astype(jnp.int32)
    nsa_w = NSA_HEADS * NSA_HD
    qk_w = 2 * ML_HEADS * ML_DK
    return {
        'x_prompt': nrm((BATCH, SEQ, D_MODEL), 1.0),
        'x_sample': nrm((DEC_BATCH, DEC_SEQ, D_MODEL), 1.0),
        'state_ret': nrm((N_EVEN, DEC_BATCH, RET_HEADS, RET_DK, RET_DV), 0.5),
        'state_mlstm_C': nrm((N_EVEN, DEC_BATCH, ML_HEADS, ML_DV, ML_DK), 0.5),
        'state_mlstm_n': nrm((N_EVEN, DEC_BATCH, ML_HEADS, ML_DK), 0.5),
        'state_mlstm_m': nrm((N_EVEN, DEC_BATCH, ML_HEADS), 0.5),
        'state_conv': nrm((N_EVEN, DEC_BATCH, CONV_W - 1, qk_w), 1.0),
        'cache_nsa_cmp': nrm((N_ODD, n_pool, PAGE_SIZE, 2, NSA_KV_HEADS, NSA_HD), 1.0),
        'cache_nsa_sel': nrm((N_ODD, n_pool, PAGE_SIZE, 2, NSA_KV_HEADS, NSA_HD), 1.0),
        'state_nsa_win': nrm((N_ODD, DEC_BATCH, win_buf, 2, NSA_KV_HEADS, NSA_HD), 1.0),
        'page_table': page_table,
        'rel_bias': nrm((REL_BUCKETS, NSA_HEADS), 0.5),
        'norm_mix': gain((DEPTH, D_MODEL)),
        'norm_ffn': gain((DEPTH, D_MODEL)),
        'ab_w_in': nrm((N_EVEN, D_MODEL, AB_IN), D_MODEL ** -0.5),
        'ab_conv_w': nrm((N_EVEN, CONV_W, qk_w), CONV_W ** -0.5),
        'ab_conv_b': nrm((N_EVEN, qk_w), 0.01),
        'ab_b_igate': nrm((N_EVEN, ML_HEADS), 0.1),
        'ab_b_fgate': jnp.linspace(3.0, 6.0, ML_HEADS, dtype=F32)[None, :] + nrm((N_EVEN, ML_HEADS), 0.1),
        'ab_gn_g': gain((N_EVEN, RET_HEADS * RET_DV)),
        'ab_gn_b': nrm((N_EVEN, RET_HEADS * RET_DV), 0.01),
        'ab_hn_g': gain((N_EVEN, ML_HEADS * ML_DV)),
        'ab_w_out': nrm((N_EVEN, AB_OUT, D_MODEL), AB_OUT ** -0.5),
        'nsa_w_in': nrm((N_ODD, D_MODEL, NSA_IN), D_MODEL ** -0.5),
        'nsa_q_norm': gain((N_ODD, NSA_HD)),
        'nsa_k_norm': gain((N_ODD, 3, NSA_HD)),
        'nsa_cmp_w1': nrm((N_ODD, 2, CMP_LEN * NSA_HD, CMP_HID), (CMP_LEN * NSA_HD) ** -0.5),
        'nsa_cmp_b1': nrm((N_ODD, 2, CMP_HID), 0.01),
        'nsa_cmp_w2': nrm((N_ODD, 2, CMP_HID, NSA_HD), CMP_HID ** -0.5),
        'nsa_cmp_b2': nrm((N_ODD, 2, NSA_HD), 0.01),
        'nsa_w_out': nrm((N_ODD, nsa_w, D_MODEL), nsa_w ** -0.5),
        'moe_w_group': nrm((DEPTH, D_MODEL, MOE_GROUPS), D_MODEL ** -0.5),
        'moe_b_group': nrm((DEPTH, MOE_GROUPS), 0.01),
        'moe_w_expert': nrm((DEPTH, D_MODEL, MOE_EXPERTS), D_MODEL ** -0.5),
        'moe_b_expert': nrm((DEPTH, MOE_EXPERTS), 0.01),
        'moe_w_gate': nrm((DEPTH, MOE_EXPERTS, D_MODEL, MOE_FF), D_MODEL ** -0.5),
        'moe_w_up': nrm((DEPTH, MOE_EXPERTS, D_MODEL, MOE_FF), D_MODEL ** -0.5),
        'moe_w_down': nrm((DEPTH, MOE_EXPERTS, MOE_FF, D_MODEL), MOE_FF ** -0.5),
    }


def reference(x_prompt, x_sample, state_ret, state_mlstm_C, state_mlstm_n, state_mlstm_m, state_conv,
              cache_nsa_cmp, cache_nsa_sel, state_nsa_win, page_table, rel_bias, norm_mix, norm_ffn,
              ab_w_in, ab_conv_w, ab_conv_b, ab_b_igate, ab_b_fgate, ab_gn_g, ab_gn_b, ab_hn_g, ab_w_out,
              nsa_w_in, nsa_q_norm, nsa_k_norm, nsa_cmp_w1, nsa_cmp_b1, nsa_cmp_w2, nsa_cmp_b2, nsa_w_out,
              moe_w_group, moe_b_group, moe_w_expert, moe_b_expert, moe_w_gate, moe_w_up, moe_w_down):
    past_len = page_table.shape[1] * PAGE_SIZE
    bp, lp, _ = x_prompt.shape
    ls = x_sample.shape[1]
    pos_p = jnp.arange(lp, dtype=jnp.int32)
    pos_s = past_len + jnp.arange(ls, dtype=jnp.int32)
    new = {name: ([], []) for name in ('ret', 'mC', 'mn', 'mm', 'conv', 'cmp', 'sel', 'win')}

    def keep(names, vals_p, vals_s):
        for name, vp, vs in zip(names, vals_p, vals_s):
            new[name][0].append(vp)
            new[name][1].append(vs)

    def stk(name, side):
        return jnp.stack(new[name][side], axis=0)

    xp, xs = x_prompt, x_sample
    for layer in range(DEPTH):
        li = layer // 2
        hp = rmsnorm(xp, norm_mix[layer])
        hs = rmsnorm(xs, norm_mix[layer])
        if layer % 2 == 0:
            ab = (ab_w_in[li], ab_conv_w[li], ab_conv_b[li], ab_b_igate[li], ab_b_fgate[li],
                  ab_gn_g[li], ab_gn_b[li], ab_hn_g[li], ab_w_out[li])
            init = (jnp.zeros((bp, RET_HEADS, RET_DK, RET_DV), F32), jnp.zeros((bp, ML_HEADS, ML_DV, ML_DK), F32),
                    jnp.zeros((bp, ML_HEADS, ML_DK), F32), jnp.zeros((bp, ML_HEADS), F32),
                    jnp.zeros((bp, CONV_W - 1, 2 * ML_HEADS * ML_DK), hp.dtype))
            op, *st_p = mixer_ab(hp, pos_p, *init, *ab)
            os_, *st_s = mixer_ab(hs, pos_s, state_ret[li], state_mlstm_C[li], state_mlstm_n[li],
                                  state_mlstm_m[li], state_conv[li], *ab)
            keep(('ret', 'mC', 'mn', 'mm', 'conv'), st_p, st_s)
        else:
            nsa = (nsa_w_in[li], nsa_q_norm[li], nsa_k_norm[li], nsa_cmp_w1[li], nsa_cmp_b1[li],
                   nsa_cmp_w2[li], nsa_cmp_b2[li], nsa_w_out[li], rel_bias)
            op, *st_p = nsa_prompt(hp, *nsa)
            os_, *st_s = nsa_sample(hs, cache_nsa_cmp, cache_nsa_sel, state_nsa_win[li], page_table, li, *nsa)
            keep(('cmp', 'sel', 'win'), st_p, st_s)
        xp = xp + op
        xs = xs + os_
        moe = (moe_w_group[layer], moe_b_group[layer], moe_w_expert[layer], moe_b_expert[layer],
               moe_w_gate[layer], moe_w_up[layer], moe_w_down[layer])
        xp = xp + hier_moe(rmsnorm(xp, norm_ffn[layer]), *moe)
        xs = xs + hier_moe(rmsnorm(xs, norm_ffn[layer]), *moe)
    return (xp, xs, stk('ret', 0), stk('ret', 1), stk('mC', 0), stk('mC', 1), stk('mn', 0), stk('mn', 1),
            stk('mm', 0), stk('mm', 1), stk('conv', 0), stk('conv', 1), stk('cmp', 0), stk('cmp', 1),
            stk('sel', 0), stk('sel', 1), stk('win', 0), stk('win', 1))
```

```python
import functools
import math

import jax
import jax.numpy as jnp
import numpy as np
from jax import lax
from jax.experimental import pallas as pl
from jax.experimental.pallas import tpu as pltpu

F32 = jnp.float32
BF16 = jnp.bfloat16
LANES = 128
SUBLANES = 8
VMEM_LIMIT = 56 * 1024 * 1024

D_MODEL = 1024
RET_HEADS = 4
ML_HEADS = 4
HEAD_D = 128
CONV_W = 4
CHUNK = 128
ROPE_BASE = 10000.0
AB_IN = 4104
AB_IN_PAD = 4224
NSA_HEADS = 16
NSA_KV_HEADS = 4
NSA_REP = 4
NSA_HD = 64
NSA_IN_PAD = 2688
CMP_LEN = 32
CMP_STRIDE = 16
CMP_HID = 128
SEL_BLOCK = 64
SEL_TOPK = 16
WINDOW = 512
REL_BUCKETS = 32
REL_MAX_DIST = 1024
FORCE_SCORE = 1e4
MOE_GROUPS = 4
MOE_EXP_PER_GROUP = 8
MOE_EXPERTS = 32
NEG = -1e30
EPS = 1e-6


def _params(n_grid):
    return pltpu.CompilerParams(dimension_semantics=("arbitrary",) * n_grid, vmem_limit_bytes=VMEM_LIMIT)


def _dot(a, b):
    return jnp.dot(a.astype(BF16), b.astype(BF16), preferred_element_type=F32)


def _dot_nt(a, b):
    return lax.dot_general(a.astype(BF16), b.astype(BF16), (((1,), (1,)), ((), ())), preferred_element_type=F32)


def _dot_tn(a, b):
    return lax.dot_general(a.astype(BF16), b.astype(BF16), (((0,), (0,)), ((), ())), preferred_element_type=F32)


def _dot_f32(a, b):
    return jnp.dot(a, b, preferred_element_type=F32, precision=lax.Precision.HIGHEST)


def _sigmoid(x):
    return 1.0 / (1.0 + jnp.exp(-x))


def _silu(x):
    return x * _sigmoid(x)


def _log_sigmoid(x):
    return -(jnp.maximum(-x, 0.0) + jnp.log1p(jnp.exp(-jnp.abs(x))))


def _norm_matmul_body(x_ref, g_ref, w_ref, o_ref, *, col_tile):
    x = x_ref[...]
    y = x * lax.rsqrt(jnp.mean(x * x, axis=-1, keepdims=True) + EPS) * g_ref[...]
    yb = y.astype(BF16)
    for c0 in range(0, o_ref.shape[1], col_tile):
        o_ref[:, c0:c0 + col_tile] = jnp.dot(yb, w_ref[:, c0:c0 + col_tile], preferred_element_type=F32)


def norm_matmul(x, g, w, row_tile=256):
    n, d = x.shape
    c = w.shape[1]
    tm = min(row_tile, n)
    col_tile = 384 if c % 384 == 0 else LANES
    return pl.pallas_call(
        functools.partial(_norm_matmul_body, col_tile=col_tile),
        out_shape=jax.ShapeDtypeStruct((n, c), F32),
        grid=(n // tm,),
        in_specs=[pl.BlockSpec((tm, d), lambda i: (i, 0)),
                  pl.BlockSpec((1, d), lambda i: (0, 0)),
                  pl.BlockSpec((d, c), lambda i: (0, 0))],
        out_specs=pl.BlockSpec((tm, c), lambda i: (i, 0)),
        compiler_params=_params(1),
        name="norm_matmul",
    )(x, g.reshape(1, d), w)


def _retention_constants(c):
    h = np.arange(RET_HEADS, dtype=np.float64)
    log_g = np.log1p(-np.exp2(-5.0 - h))
    i = np.arange(c, dtype=np.float64)
    diff = i[:, None] - i[None, :]
    decay = np.where(diff >= 0, np.exp(np.maximum(diff, 0.0)[None] * log_g[:, None, None]), 0.0)
    q_dec = np.exp((i + 1.0)[None, :] * log_g[:, None])[:, :, None]
    k_dec = np.exp((c - 1.0 - i)[None, :] * log_g[:, None])[:, :, None]
    s_dec = np.exp(c * log_g)
    return (jnp.asarray(decay, F32), jnp.asarray(q_dec, F32), jnp.asarray(k_dec, F32),
            [float(v) for v in s_dec])


def _rope_tables(pos):
    half = HEAD_D // 2
    freqs = ROPE_BASE ** (-jnp.arange(half, dtype=F32) / half)
    ang = pos.astype(F32)[:, None] * freqs[None, :]
    cos, sin = jnp.cos(ang), jnp.sin(ang)
    return jnp.concatenate([cos, cos], axis=-1), jnp.concatenate([-sin, sin], axis=-1)


def _rope(x, cosf, sinf):
    return x * cosf + pltpu.roll(x, HEAD_D // 2, 1) * sinf


def _ab_prompt_body(rq_ref, rk_ref, rv_ref, rg_ref, mqk_ref, mv_ref, mo_ref, gz_ref, x_ref, cos_ref, sin_ref,
                    decay_ref, qdec_ref, kdec_ref, convw_ref, convb_ref, gbias_ref, gng_ref, gnb_ref, hng_ref,
                    wout_ref,
                    y_ref, s_ref, c_ref, n_ref, m_ref, conv_ref,
                    cbuf_ref, ycat_ref, *, s_dec):
    c = pl.program_id(1)
    tail = CONV_W - 1

    @pl.when(c == 0)
    def _():
        s_ref[...] = jnp.zeros_like(s_ref)
        c_ref[...] = jnp.zeros_like(c_ref)
        n_ref[...] = jnp.zeros_like(n_ref)
        m_ref[...] = jnp.zeros_like(m_ref)
        cbuf_ref[0:SUBLANES, :] = jnp.zeros((SUBLANES, cbuf_ref.shape[1]), F32)

    cosf = cos_ref[...]
    sinf = sin_ref[...]
    row = lax.broadcasted_iota(jnp.int32, (CHUNK, CHUNK), 0)
    col = lax.broadcasted_iota(jnp.int32, (CHUNK, CHUNK), 1)
    eye = row == col
    tril = row >= col
    triu = row <= col

    cbuf_ref[SUBLANES:SUBLANES + CHUNK, :] = mqk_ref[...]
    conv = convb_ref[...]
    for w in range(CONV_W):
        conv = conv + cbuf_ref[SUBLANES - tail + w:SUBLANES - tail + w + CHUNK, :] * convw_ref[w:w + 1, :]
    qk = _silu(conv)
    last = cbuf_ref[CHUNK + SUBLANES - tail:CHUNK + SUBLANES, :]
    cbuf_ref[SUBLANES - tail:SUBLANES, :] = last
    conv_ref[0] = last

    gz = gz_ref[...] + gbias_ref[...]
    for h in range(RET_HEADS):
        sl = slice(h * HEAD_D, (h + 1) * HEAD_D)
        q = _rope(rq_ref[:, sl], cosf, sinf)
        k = _rope(rk_ref[:, sl], cosf, sinf) * (HEAD_D ** -0.5)
        v = rv_ref[:, sl]
        a = _dot_nt(q, k) * decay_ref[h]
        s_old = s_ref[0, h]
        o = _dot(a, v) + qdec_ref[h] * _dot(q, s_old)
        s_ref[0, h] = s_dec[h] * s_old + _dot_tn(k * kdec_ref[h], v)
        mu = jnp.mean(o, axis=-1, keepdims=True)
        var = jnp.mean(jnp.square(o - mu), axis=-1, keepdims=True)
        o = (o - mu) * lax.rsqrt(var + EPS) * gng_ref[:, sl] + gnb_ref[:, sl]
        ycat_ref[:, sl] = _silu(rg_ref[:, sl]) * o

        mq = qk[:, sl]
        mk = qk[:, ML_HEADS * HEAD_D + h * HEAD_D:ML_HEADS * HEAD_D + (h + 1) * HEAD_D] * (HEAD_D ** -0.5)
        mv = mv_ref[:, sl]
        i_col = gz[:, h:h + 1]
        f_col = _log_sigmoid(gz[:, ML_HEADS + h:ML_HEADS + h + 1])
        i_row = jnp.sum(jnp.where(eye, i_col, 0.0), axis=0, keepdims=True)
        f_row = jnp.sum(jnp.where(eye, f_col, 0.0), axis=0, keepdims=True)
        b_col = jnp.sum(jnp.where(tril, f_row, 0.0), axis=1, keepdims=True)
        b_row = jnp.sum(jnp.where(triu, f_col, 0.0), axis=0, keepdims=True)
        m_old = m_ref[0, h:h + 1, 0:1]
        dlog = jnp.where(tril, b_col - b_row + i_row, -jnp.inf)
        inter = b_col + m_old
        m_t = jnp.maximum(inter, jnp.max(dlog, axis=1, keepdims=True))
        wgt = _dot_nt(mq, mk) * jnp.exp(dlog - m_t)
        e_inter = jnp.exp(inter - m_t)
        c_old = c_ref[0, h]
        n_old = n_ref[0, h:h + 1, :]
        num = _dot(wgt, mv) + e_inter * _dot_nt(mq, c_old)
        den = jnp.sum(wgt, axis=1, keepdims=True) + e_inter * jnp.sum(mq * n_old, axis=1, keepdims=True)
        hc = num / jnp.maximum(jnp.abs(den), jnp.exp(-m_t))
        b_last = b_col[CHUNK - 1:CHUNK, :]
        u_row = b_last - b_row + i_row
        u_col = b_last - b_col + i_col
        m_new = jnp.maximum(b_last + m_old, jnp.max(u_row, axis=1, keepdims=True))
        ws_col = jnp.exp(u_col - m_new)
        f_state = jnp.exp(b_last + m_old - m_new)
        c_ref[0, h] = f_state * c_old + _dot_tn(mv * ws_col, mk)
        n_ref[0, h:h + 1, :] = f_state * n_old + jnp.sum(ws_col * mk, axis=0, keepdims=True)
        m_ref[0, h:h + 1, :] = jnp.broadcast_to(m_new, (1, LANES))
        hm = _sigmoid(mo_ref[:, sl]) * hc
        hm = hm * lax.rsqrt(jnp.mean(hm * hm, axis=-1, keepdims=True) + EPS) * hng_ref[:, sl]
        ycat_ref[:, RET_HEADS * HEAD_D + h * HEAD_D:RET_HEADS * HEAD_D + (h + 1) * HEAD_D] = hm

    y_ref[...] = x_ref[...] + jnp.dot(ycat_ref[...].astype(BF16), wout_ref[...], preferred_element_type=F32)


def ab_prompt(z, x, batch, seq, conv_w, conv_b, b_ig, b_fg, gn_g, gn_b, hn_g, w_out):
    n_chunk = seq // CHUNK
    decay, q_dec, k_dec, s_dec = _retention_constants(CHUNK)
    cosf, sinf = _rope_tables(jnp.arange(seq, dtype=jnp.int32))
    gbias = jnp.zeros((1, LANES), F32).at[0, :ML_HEADS].set(b_ig).at[0, ML_HEADS:2 * ML_HEADS].set(b_fg)
    hw = RET_HEADS * HEAD_D
    qkw = 2 * ML_HEADS * HEAD_D

    def zspec(width, blk):
        return pl.BlockSpec((CHUNK, width), lambda b, c, blk=blk: (b * n_chunk + c, blk))

    def const(shape):
        return pl.BlockSpec(shape, lambda b, c: (0,) * len(shape))

    in_specs = [zspec(hw, 0), zspec(hw, 1), zspec(hw, 2), zspec(hw, 3), zspec(qkw, 2), zspec(hw, 6), zspec(hw, 7),
                zspec(LANES, (AB_IN_PAD - LANES) // LANES),
                pl.BlockSpec((CHUNK, D_MODEL), lambda b, c: (b * n_chunk + c, 0)),
                pl.BlockSpec((CHUNK, HEAD_D), lambda b, c: (c, 0)),
                pl.BlockSpec((CHUNK, HEAD_D), lambda b, c: (c, 0)),
                const((RET_HEADS, CHUNK, CHUNK)), const((RET_HEADS, CHUNK, 1)), const((RET_HEADS, CHUNK, 1)),
                const((CONV_W, qkw)), const((1, qkw)), const((1, LANES)),
                const((1, hw)), const((1, hw)), const((1, hw)), const((2 * hw, D_MODEL))]
    out_shape = (jax.ShapeDtypeStruct((batch * seq, D_MODEL), F32),
                 jax.ShapeDtypeStruct((batch, RET_HEADS, HEAD_D, HEAD_D), F32),
                 jax.ShapeDtypeStruct((batch, ML_HEADS, HEAD_D, HEAD_D), F32),
                 jax.ShapeDtypeStruct((batch, ML_HEADS, HEAD_D), F32),
                 jax.ShapeDtypeStruct((batch, SUBLANES, LANES), F32),
                 jax.ShapeDtypeStruct((batch, CONV_W - 1, qkw), F32))
    out_specs = (pl.BlockSpec((CHUNK, D_MODEL), lambda b, c: (b * n_chunk + c, 0)),
                 pl.BlockSpec((1, RET_HEADS, HEAD_D, HEAD_D), lambda b, c: (b, 0, 0, 0)),
                 pl.BlockSpec((1, ML_HEADS, HEAD_D, HEAD_D), lambda b, c: (b, 0, 0, 0)),
                 pl.BlockSpec((1, ML_HEADS, HEAD_D), lambda b, c: (b, 0, 0)),
                 pl.BlockSpec((1, SUBLANES, LANES), lambda b, c: (b, 0, 0)),
                 pl.BlockSpec((1, CONV_W - 1, qkw), lambda b, c: (b, 0, 0)))
    y, s, cc, n, m, conv = pl.pallas_call(
        functools.partial(_ab_prompt_body, s_dec=s_dec),
        out_shape=out_shape, grid=(batch, n_chunk), in_specs=in_specs, out_specs=out_specs,
        scratch_shapes=[pltpu.VMEM((CHUNK + SUBLANES, qkw), F32), pltpu.VMEM((CHUNK, 2 * hw), F32)],
        compiler_params=_params(2), name="ab_prompt",
    )(z, z, z, z, z, z, z, z, x, cosf, sinf, decay, q_dec, k_dec, conv_w, conv_b.reshape(1, qkw), gbias,
      gn_g.reshape(1, hw), gn_b.reshape(1, hw), hn_g.reshape(1, hw), w_out.astype(BF16))
    return y, s, cc, n, m[:, :ML_HEADS, 0], conv


def _matmul_residual_body(a_ref, w_ref, x_ref, o_ref):
    o_ref[...] = x_ref[...] + jnp.dot(a_ref[...].astype(BF16), w_ref[...], preferred_element_type=F32)


def matmul_residual(a, w, x, row_tile=256):
    n, kk = a.shape
    d = w.shape[1]
    tm = min(row_tile, n)
    return pl.pallas_call(
        _matmul_residual_body, out_shape=jax.ShapeDtypeStruct((n, d), F32), grid=(n // tm,),
        in_specs=[pl.BlockSpec((tm, kk), lambda i: (i, 0)), pl.BlockSpec((kk, d), lambda i: (0, 0)),
                  pl.BlockSpec((tm, d), lambda i: (i, 0))],
        out_specs=pl.BlockSpec((tm, d), lambda i: (i, 0)),
        compiler_params=_params(1), name="matmul_residual",
    )(a, w, x)


def _ab_sample_body(m0_ref, z_ref, cos_ref, sin_ref, s0_ref, c0_ref, n0_ref, conv0_ref,
                    convw_ref, convb_ref, gbias_ref, gng_ref, gnb_ref, hng_ref,
                    y_ref, s_ref, c_ref, n_ref, m_ref, conv_ref, *, g_dec):
    b = pl.program_id(0)
    hw = RET_HEADS * HEAD_D
    qkw = 2 * ML_HEADS * HEAD_D
    tail = CONV_W - 1
    cosf = cos_ref[...]
    sinf = sin_ref[...]
    row = lax.broadcasted_iota(jnp.int32, (HEAD_D, HEAD_D), 0)
    col = lax.broadcasted_iota(jnp.int32, (HEAD_D, HEAD_D), 1)
    eye = row == col

    def to_col(r):
        return jnp.sum(jnp.where(eye, r, 0.0), axis=1, keepdims=True)

    def to_row(cv):
        return jnp.sum(jnp.where(eye, cv, 0.0), axis=0, keepdims=True)

    mqk = z_ref[0, :, 4 * hw:4 * hw + qkw]
    conv = convb_ref[...] + mqk * convw_ref[tail:CONV_W, :]
    for w in range(tail):
        conv = conv + conv0_ref[0, w:w + 1, :] * convw_ref[w:w + 1, :]
    qk = _silu(conv)
    conv_ref[0, 0:tail - 1, :] = conv0_ref[0, 1:tail, :]
    conv_ref[0, tail - 1:tail, :] = mqk
    gz = z_ref[0, :, AB_IN_PAD - LANES:AB_IN_PAD] + gbias_ref[...]

    for h in range(RET_HEADS):
        sl = slice(h * HEAD_D, (h + 1) * HEAD_D)
        q = _rope(z_ref[0, :, sl], cosf, sinf)
        k = _rope(z_ref[0, :, hw + h * HEAD_D:hw + (h + 1) * HEAD_D], cosf, sinf) * (HEAD_D ** -0.5)
        v = z_ref[0, :, 2 * hw + h * HEAD_D:2 * hw + (h + 1) * HEAD_D]
        rg = z_ref[0, :, 3 * hw + h * HEAD_D:3 * hw + (h + 1) * HEAD_D]
        s_old = s0_ref[0, h]
        qk_s = jnp.sum(q * k, axis=1, keepdims=True)
        o = qk_s * v + g_dec[h] * jnp.sum(to_col(q) * s_old, axis=0, keepdims=True)
        s_ref[0, h] = g_dec[h] * s_old + to_col(k) * v
        mu = jnp.mean(o, axis=-1, keepdims=True)
        var = jnp.mean(jnp.square(o - mu), axis=-1, keepdims=True)
        o = (o - mu) * lax.rsqrt(var + EPS) * gng_ref[:, sl] + gnb_ref[:, sl]
        y_ref[0, :, sl] = _silu(rg) * o

        mq = qk[:, sl]
        mk = qk[:, ML_HEADS * HEAD_D + h * HEAD_D:ML_HEADS * HEAD_D + (h + 1) * HEAD_D] * (HEAD_D ** -0.5)
        mv = z_ref[0, :, 4 * hw + qkw + h * HEAD_D:4 * hw + qkw + (h + 1) * HEAD_D]
        mo = z_ref[0, :, 5 * hw + qkw + h * HEAD_D:5 * hw + qkw + (h + 1) * HEAD_D]
        ig = gz[:, h:h + 1]
        lf = _log_sigmoid(gz[:, ML_HEADS + h:ML_HEADS + h + 1])
        m_old = m0_ref[b, h]
        inter = lf + m_old
        m_t = jnp.maximum(inter, ig)
        wgt = jnp.sum(mq * mk, axis=1, keepdims=True) * jnp.exp(ig - m_t)
        e_inter = jnp.exp(inter - m_t)
        c_old = c0_ref[0, h]
        n_old = n0_ref[0, h:h + 1, :]
        cq = to_row(jnp.sum(c_old * mq, axis=1, keepdims=True))
        num = wgt * mv + e_inter * cq
        den = wgt + e_inter * jnp.sum(n_old * mq, axis=1, keepdims=True)
        hc = num / jnp.maximum(jnp.abs(den), jnp.exp(-m_t))
        ws = jnp.exp(ig - m_t)
        c_ref[0, h] = e_inter * c_old + (ws * to_col(mv)) * mk
        n_ref[0, h:h + 1, :] = e_inter * n_old + ws * mk
        m_ref[0, h:h + 1, :] = jnp.broadcast_to(m_t, (1, LANES))
        hm = _sigmoid(mo) * hc
        hm = hm * lax.rsqrt(jnp.mean(hm * hm, axis=-1, keepdims=True) + EPS) * hng_ref[:, sl]
        y_ref[0, :, hw + h * HEAD_D:hw + (h + 1) * HEAD_D] = hm
    m_ref[0, ML_HEADS:SUBLANES, :] = jnp.zeros((SUBLANES - ML_HEADS, LANES), F32)


def ab_sample(z, pos, s0, c0, n0, m0, conv0, conv_w, conv_b, b_ig, b_fg, gn_g, gn_b, hn_g):
    batch = z.shape[0]
    h = np.arange(RET_HEADS, dtype=np.float64)
    g_dec = [float(v) for v in np.exp(np.log1p(-np.exp2(-5.0 - h)))]
    cosf, sinf = _rope_tables(pos)
    gbias = jnp.zeros((1, LANES), F32).at[0, :ML_HEADS].set(b_ig).at[0, ML_HEADS:2 * ML_HEADS].set(b_fg)
    hw = RET_HEADS * HEAD_D
    qkw = 2 * ML_HEADS * HEAD_D

    def per_b(shape):
        return pl.BlockSpec((1,) + shape, lambda b: (b,) + (0,) * len(shape))

    def const(shape):
        return pl.BlockSpec(shape, lambda b: (0,) * len(shape))

    in_specs = [pl.BlockSpec(memory_space=pltpu.SMEM), per_b((1, AB_IN_PAD)), const((1, HEAD_D)), const((1, HEAD_D)),
                per_b((RET_HEADS, HEAD_D, HEAD_D)), per_b((ML_HEADS, HEAD_D, HEAD_D)), per_b((ML_HEADS, HEAD_D)),
                per_b((CONV_W - 1, qkw)), const((CONV_W, qkw)), const((1, qkw)), const((1, LANES)),
                const((1, hw)), const((1, hw)), const((1, hw))]
    out_shape = (jax.ShapeDtypeStruct((batch, 1, 2 * hw), F32),
                 jax.ShapeDtypeStruct((batch, RET_HEADS, HEAD_D, HEAD_D), F32),
                 jax.ShapeDtypeStruct((batch, ML_HEADS, HEAD_D, HEAD_D), F32),
                 jax.ShapeDtypeStruct((batch, ML_HEADS, HEAD_D), F32),
                 jax.ShapeDtypeStruct((batch, SUBLANES, LANES), F32),
                 jax.ShapeDtypeStruct((batch, CONV_W - 1, qkw), F32))
    out_specs = (per_b((1, 2 * hw)), per_b((RET_HEADS, HEAD_D, HEAD_D)), per_b((ML_HEADS, HEAD_D, HEAD_D)),
                 per_b((ML_HEADS, HEAD_D)), per_b((SUBLANES, LANES)), per_b((CONV_W - 1, qkw)))
    y, s, cc, n, m, conv = pl.pallas_call(
        functools.partial(_ab_sample_body, g_dec=g_dec),
        out_shape=out_shape, grid=(batch,), in_specs=in_specs, out_specs=out_specs,
        compiler_params=_params(1), name="ab_sample",
    )(m0, z.reshape(batch, 1, AB_IN_PAD), cosf, sinf, s0, c0, n0, conv0, conv_w, conv_b.reshape(1, qkw), gbias,
      gn_g.reshape(1, hw), gn_b.reshape(1, hw), hn_g.reshape(1, hw))
    return y.reshape(batch, 2 * hw), s, cc, n, m[:, :ML_HEADS, 0], conv


MOE_TILE = 256


def _moe_router_body(x_ref, g_ref, wr_ref, br_ref, hn_ref, route_ref):
    x = x_ref[...]
    hn = x * lax.rsqrt(jnp.mean(x * x, axis=-1, keepdims=True) + EPS) * g_ref[...]
    hn_ref[...] = hn.astype(BF16)
    z = _dot_f32(hn, wr_ref[...]) + br_ref[...]
    lane = lax.broadcasted_iota(jnp.int32, z.shape, 1)
    lanef = lane.astype(F32)
    is_group = lane < MOE_GROUPS
    gl = jnp.where(is_group, z, -jnp.inf)
    gmax = jnp.max(gl, axis=1, keepdims=True)
    g_top = jnp.min(jnp.where(gl == gmax, lanef, float(LANES)), axis=1, keepdims=True)
    pg_top = 1.0 / jnp.sum(jnp.where(is_group, jnp.exp(z - gmax), 0.0), axis=1, keepdims=True)
    grp = ((lane - MOE_GROUPS) // MOE_EXP_PER_GROUP).astype(F32)
    in_group = (lane >= MOE_GROUPS) & (lane < MOE_GROUPS + MOE_EXPERTS) & (grp == g_top)
    el = jnp.where(in_group, z, -jnp.inf)
    v1 = jnp.max(el, axis=1, keepdims=True)
    i1 = jnp.min(jnp.where(el == v1, lanef, float(LANES)), axis=1, keepdims=True)
    el2 = jnp.where(lanef == i1, -jnp.inf, el)
    v2 = jnp.max(el2, axis=1, keepdims=True)
    i2 = jnp.min(jnp.where(el2 == v2, lanef, float(LANES)), axis=1, keepdims=True)
    t = jnp.exp(v2 - v1)
    p1 = 1.0 / (1.0 + t)
    out = jnp.where(lane == 0, i1 - MOE_GROUPS,
                    jnp.where(lane == 1, i2 - MOE_GROUPS,
                              jnp.where(lane == 2, pg_top * p1,
                                        jnp.where(lane == 3, pg_top * (t * p1), 0.0))))
    route_ref[...] = out


def moe_router(x, g, w_group, b_group, w_expert, b_expert, row_tile=256):
    n, d = x.shape
    tm = min(row_tile, n)
    used = MOE_GROUPS + MOE_EXPERTS
    wr = jnp.pad(jnp.concatenate([w_group, w_expert], axis=1), ((0, 0), (0, LANES - used)))
    br = jnp.pad(jnp.concatenate([b_group, b_expert]), (0, LANES - used)).reshape(1, LANES)
    hn, route = pl.pallas_call(
        _moe_router_body,
        out_shape=(jax.ShapeDtypeStruct((n, d), BF16), jax.ShapeDtypeStruct((n, LANES), F32)),
        grid=(n // tm,),
        in_specs=[pl.BlockSpec((tm, d), lambda i: (i, 0)), pl.BlockSpec((1, d), lambda i: (0, 0)),
                  pl.BlockSpec((d, LANES), lambda i: (0, 0)), pl.BlockSpec((1, LANES), lambda i: (0, 0))],
        out_specs=(pl.BlockSpec((tm, d), lambda i: (i, 0)), pl.BlockSpec((tm, LANES), lambda i: (i, 0))),
        compiler_params=_params(1), name="moe_router",
    )(x, g.reshape(1, d), wr, br)
    return hn, route[:, 0:2].astype(jnp.int32), route[:, 2:4]


def _moe_ffn_body(blk_e_ref, x_ref, rw_ref, wg_ref, wu_ref, wd_ref, o_ref, wg_s, wu_s, wd_s):
    i = pl.program_id(0)
    prev = blk_e_ref[jnp.maximum(i - 1, 0)]

    @pl.when((i == 0) | (blk_e_ref[i] != prev))
    def _():
        wg_s[...] = wg_ref[0].astype(BF16)
        wu_s[...] = wu_ref[0].astype(BF16)
        wd_s[...] = wd_ref[0].astype(BF16)

    x = x_ref[...]
    hg = jnp.dot(x, wg_s[...], preferred_element_type=F32)
    hu = jnp.dot(x, wu_s[...], preferred_element_type=F32)
    hb = (_silu(hg) * hu).astype(BF16)
    o_ref[...] = jnp.dot(hb, wd_s[...], preferred_element_type=F32) * rw_ref[...]


def moe_ffn(x_rows, row_w, blk_e, w_g, w_u, w_d):
    r, d = x_rows.shape
    ff = w_g.shape[2]
    n_blk = r // MOE_TILE
    grid_spec = pltpu.PrefetchScalarGridSpec(
        num_scalar_prefetch=1, grid=(n_blk,),
        in_specs=[pl.BlockSpec((MOE_TILE, d), lambda i, e: (i, 0)),
                  pl.BlockSpec((MOE_TILE, 1), lambda i, e: (i, 0)),
                  pl.BlockSpec((1, d, ff), lambda i, e: (e[i], 0, 0)),
                  pl.BlockSpec((1, d, ff), lambda i, e: (e[i], 0, 0)),
                  pl.BlockSpec((1, ff, d), lambda i, e: (e[i], 0, 0))],
        out_specs=pl.BlockSpec((MOE_TILE, d), lambda i, e: (i, 0)),
        scratch_shapes=[pltpu.VMEM((d, ff), BF16), pltpu.VMEM((d, ff), BF16), pltpu.VMEM((ff, d), BF16)])
    return pl.pallas_call(
        _moe_ffn_body, out_shape=jax.ShapeDtypeStruct((r, d), F32), grid_spec=grid_spec,
        compiler_params=_params(1), name="moe_ffn",
    )(blk_e, x_rows, row_w.reshape(r, 1), w_g, w_u, w_d)


def hier_moe_residual(x, g, w_group, b_group, w_expert, b_expert, w_g, w_u, w_d):
    n, d = x.shape
    hn, expert, gate = moe_router(x, g, w_group, b_group, w_expert, b_expert)
    n_exp = w_g.shape[0]
    kk = expert.shape[1]
    a = n * kk
    e_flat = expert.reshape(-1)
    counts = jnp.sum((e_flat[:, None] == jnp.arange(n_exp, dtype=jnp.int32)[None, :]).astype(jnp.int32), axis=0)
    starts = jnp.cumsum(counts) - counts
    padded = (counts + MOE_TILE - 1) // MOE_TILE * MOE_TILE
    pend = jnp.cumsum(padded)
    pstart = pend - padded
    order = jnp.argsort(e_flat)
    e_sorted = e_flat[order]
    dest_sorted = pstart[e_sorted] + jnp.arange(a, dtype=jnp.int32) - starts[e_sorted]
    r = -(-(a + n_exp * (MOE_TILE - 1)) // MOE_TILE) * MOE_TILE
    tok = jnp.arange(a, dtype=jnp.int32) // kk
    row_tok = jnp.zeros((r,), jnp.int32).at[dest_sorted].set(tok[order])
    row_w = jnp.zeros((r,), F32).at[dest_sorted].set(gate.reshape(-1)[order])
    dest = jnp.zeros((a,), jnp.int32).at[order].set(dest_sorted).reshape(n, kk)
    n_blk = r // MOE_TILE
    blk_e = jnp.minimum(jnp.searchsorted(pend, jnp.arange(n_blk, dtype=jnp.int32) * MOE_TILE, side='right'),
                        n_exp - 1).astype(jnp.int32)
    y_rows = moe_ffn(hn[row_tok], row_w, blk_e, w_g, w_u, w_d)
    return x + jnp.sum(y_rows[dest], axis=1)


KV_W = 2 * NSA_KV_HEADS * NSA_HD
Q_W = NSA_HEADS * NSA_HD
MASK_BUCKET = REL_BUCKETS
KEY_TILE = 512


def _t5_thresholds():
    exact = REL_BUCKETS // 2
    dist = np.arange(0, 4 * REL_MAX_DIST, dtype=np.int64)
    nf = np.maximum(dist, 1).astype(np.float64)
    large = exact + np.floor(np.log(nf / exact) / math.log(REL_MAX_DIST / exact) * (REL_BUCKETS - exact) + 1e-9)
    bucket = np.where(dist < exact, dist, np.minimum(large, REL_BUCKETS - 1)).astype(np.int64)
    return [int(np.argmax(bucket >= b)) for b in range(1, REL_BUCKETS)]


def _bucket_index(dist, valid):
    idx = jnp.zeros(dist.shape, jnp.int32)
    for thr in _t5_thresholds():
        idx = idx + (dist >= thr).astype(jnp.int32)
    return jnp.where(valid, idx, MASK_BUCKET)


def _bias_table(rel_bias):
    t = jnp.zeros((NSA_HEADS, LANES), F32).at[:, :REL_BUCKETS].set(rel_bias.T.astype(F32))
    return t.at[:, MASK_BUCKET].set(NEG)


def _group_mean_matrix():
    i = np.arange(LANES)
    return jnp.asarray((i[:, None] // NSA_HD == i[None, :] // NSA_HD) / NSA_HD, F32)


def _nsa_prep_body(zq_ref, zc_ref, zs_ref, zw_ref, zg_ref, bd_ref, qn_ref, kns_ref, knw_ref,
                   q_ref, cmp_ref, sel_ref, win_ref, selb_ref, winb_ref, gate_ref):
    bd = bd_ref[...]

    def head_norm(x, gain):
        ms = _dot_f32(x * x, bd)
        return x * lax.rsqrt(ms + EPS) * gain

    for c in range(Q_W // LANES):
        sl = slice(c * LANES, (c + 1) * LANES)
        q_ref[:, sl] = (head_norm(zq_ref[:, sl], qn_ref[...]) * (NSA_HD ** -0.5)).astype(BF16)
    cmp_ref[...] = zc_ref[...]
    half = KV_W // 2
    for z_ref, kn_ref, o_ref, ob_ref in ((zs_ref, kns_ref, sel_ref, selb_ref), (zw_ref, knw_ref, win_ref, winb_ref)):
        for c in range(half // LANES):
            sl = slice(c * LANES, (c + 1) * LANES)
            kn = head_norm(z_ref[:, sl], kn_ref[...])
            o_ref[:, sl] = kn
            ob_ref[:, sl] = kn.astype(BF16)
        v = z_ref[:, half:KV_W]
        o_ref[:, half:KV_W] = v
        ob_ref[:, half:KV_W] = v.astype(BF16)
    gate_ref[...] = _sigmoid(zg_ref[...])


def nsa_prep(z, q_norm, k_norm, row_tile=256):
    n = z.shape[0]
    tm = min(row_tile, n)

    def zspec(width, blk):
        return pl.BlockSpec((tm, width), lambda i, blk=blk: (i, blk))

    def const(shape):
        return pl.BlockSpec(shape, lambda i: (0,) * len(shape))

    def tile2(v):
        return jnp.concatenate([v, v]).reshape(1, LANES).astype(F32)

    def rows(w, dt):
        return jax.ShapeDtypeStruct((n, w), dt)

    def out_spec(w):
        return pl.BlockSpec((tm, w), lambda i: (i, 0))

    return pl.pallas_call(
        _nsa_prep_body,
        out_shape=(rows(Q_W, BF16), rows(KV_W, F32), rows(KV_W, F32), rows(KV_W, F32), rows(KV_W, BF16),
                   rows(KV_W, BF16), rows(LANES, F32)),
        grid=(n // tm,),
        in_specs=[zspec(Q_W, 0), zspec(KV_W, 2), zspec(KV_W, 3), zspec(KV_W, 4),
                  zspec(LANES, (Q_W + 3 * KV_W) // LANES), const((LANES, LANES)),
                  const((1, LANES)), const((1, LANES)), const((1, LANES))],
        out_specs=(out_spec(Q_W), out_spec(KV_W), out_spec(KV_W), out_spec(KV_W), out_spec(KV_W), out_spec(KV_W),
                   out_spec(LANES)),
        compiler_params=_params(1), name="nsa_prep",
    )(z, z, z, z, z, _group_mean_matrix(), tile2(q_norm), tile2(k_norm[1]), tile2(k_norm[2]))


PAGES_PER_STEP = 16
SUBS_PER_PAGE = 8
SUB_W = CMP_STRIDE * KV_W
P_W = 2 * NSA_KV_HEADS * 2 * CMP_HID


def _gelu_tanh(x):
    return 0.5 * x * (1.0 + jnp.tanh(math.sqrt(2.0 / math.pi) * (x + 0.044715 * x * x * x)))


def _compress_body(pt_ref, *refs, n_steps):
    page_refs = refs[:PAGES_PER_STEP]
    w1_ref, b1_ref, w2_ref, b2_ref, kn_ref, kc_ref, vc_ref, p_ref = refs[PAGES_PER_STEP:]
    j = pl.program_id(1)
    rows = PAGES_PER_STEP * SUBS_PER_PAGE
    r0 = pl.multiple_of(j * rows, rows)
    for v in range(2):
        for gp in range(NSA_KV_HEADS // 2):
            c0 = (v * NSA_KV_HEADS + 2 * gp) * NSA_HD
            acc = jnp.zeros((rows, 4 * CMP_HID), F32)
            for s in range(CMP_STRIDE):
                x = jnp.concatenate([pr[0, :, s * KV_W + c0:s * KV_W + c0 + LANES] for pr in page_refs], axis=0)
                acc = acc + jnp.dot(x.astype(BF16), w1_ref[v, s], preferred_element_type=F32)
            p0 = (v * NSA_KV_HEADS + 2 * gp) * 2 * CMP_HID
            p_ref[pl.ds(r0, rows), p0:p0 + 4 * CMP_HID] = acc

    @pl.when(j == n_steps - 1)
    def _():
        n_sub = p_ref.shape[0]
        for v in range(2):
            for g in range(NSA_KV_HEADS):
                p0 = (v * NSA_KV_HEADS + g) * 2 * CMP_HID
                hs = p_ref[:, p0:p0 + CMP_HID] + pltpu.roll(p_ref[:, p0 + CMP_HID:p0 + 2 * CMP_HID], n_sub - 1, 0)
                hid = _gelu_tanh(hs + b1_ref[v:v + 1, :])
                out = _dot(hid, w2_ref[v]) + b2_ref[v:v + 1, :]
                if v == 0:
                    out = out * lax.rsqrt(jnp.mean(out * out, axis=-1, keepdims=True) + EPS) * kn_ref[...]
                    kc_ref[0, :, g * NSA_HD:(g + 1) * NSA_HD] = out
                else:
                    vc_ref[0, :, g * NSA_HD:(g + 1) * NSA_HD] = out


def nsa_compress(pages, page_table, w1, b1, w2, b2, k_norm0):
    batch, n_pp = page_table.shape
    n_steps = n_pp // PAGES_PER_STEP
    n_sub = n_pp * SUBS_PER_PAGE
    pages = pages.reshape(pages.shape[0], SUBS_PER_PAGE, SUB_W)
    w = w1.reshape(2, 2, CMP_STRIDE, NSA_HD, CMP_HID)
    w = jnp.transpose(w, (0, 2, 3, 1, 4)).reshape(2, CMP_STRIDE, NSA_HD, 2 * CMP_HID)
    zero = jnp.zeros_like(w)
    wpair = jnp.concatenate([jnp.concatenate([w, zero], axis=-1), jnp.concatenate([zero, w], axis=-1)], axis=2)
    wpair = wpair.astype(BF16)

    def page_spec(t):
        return pl.BlockSpec((1, SUBS_PER_PAGE, SUB_W), lambda b, j, pt, t=t: (pt[b, j * PAGES_PER_STEP + t], 0, 0))

    def const(shape):
        return pl.BlockSpec(shape, lambda b, j, pt: (0,) * len(shape))

    grid_spec = pltpu.PrefetchScalarGridSpec(
        num_scalar_prefetch=1, grid=(batch, n_steps),
        in_specs=[page_spec(t) for t in range(PAGES_PER_STEP)] + [
            const((2, CMP_STRIDE, LANES, 4 * CMP_HID)), const((2, CMP_HID)), const((2, CMP_HID, NSA_HD)),
            const((2, NSA_HD)), const((1, NSA_HD))],
        out_specs=(pl.BlockSpec((1, n_sub, KV_W // 2), lambda b, j, pt: (b, 0, 0)),
                   pl.BlockSpec((1, n_sub, KV_W // 2), lambda b, j, pt: (b, 0, 0))),
        scratch_shapes=[pltpu.VMEM((n_sub, P_W), F32)])
    return pl.pallas_call(
        functools.partial(_compress_body, n_steps=n_steps),
        out_shape=(jax.ShapeDtypeStruct((batch, n_sub, KV_W // 2), F32),
                   jax.ShapeDtypeStruct((batch, n_sub, KV_W // 2), F32)),
        grid_spec=grid_spec, compiler_params=_params(2), name="nsa_compress",
    )(page_table, *([pages] * PAGES_PER_STEP), wpair, b1, w2.astype(BF16), b2, k_norm0.reshape(1, NSA_HD))


Q_TILE = 128
WIN_BLOCKS = (WINDOW + Q_TILE) // Q_TILE
NEAR_TILES = 3
M_INIT = -1e29


def _gather_bias(tbh, idx_slices):
    return jnp.concatenate([jnp.take_along_axis(tbh, idx, axis=1) for idx in idx_slices], axis=1)


def _nsa_attend_body(q_ref, gate_ref, x_ref, kc_ref, vc_ref, kst_ref, vs_ref, w0_ref, w1_ref, w2_ref, w3_ref, w4_ref,
                     idxc_ref, idxw_ref, nb_ref, mmat_ref, tb_ref, wout_ref, y_ref,
                     pc_s, oc_s, sel_s, m_s, l_s, a_s, acc_s, ps_s, pw_s, o_s, *, n_sel):
    win_refs = (w0_ref, w1_ref, w2_ref, w3_ref, w4_ref)
    t = pl.program_id(1)
    s0 = t * Q_TILE
    n_sub = kc_ref.shape[1]
    rep_rows = NSA_REP * Q_TILE
    q_pos = s0 + lax.broadcasted_iota(jnp.int32, (Q_TILE, LANES), 0)
    blk = lax.broadcasted_iota(jnp.int32, (Q_TILE, LANES), 1)
    cur = q_pos // SEL_BLOCK
    forced = (blk == 0) | (blk == cur) | (blk == cur - 1)
    future = blk * SEL_BLOCK > q_pos

    def group_q(g):
        return jnp.concatenate([q_ref[:, (g * NSA_REP + r) * NSA_HD:(g * NSA_REP + r + 1) * NSA_HD]
                                for r in range(NSA_REP)], axis=0)

    def head_table(h):
        return jnp.broadcast_to(tb_ref[h:h + 1, :], (Q_TILE, LANES))

    score_t = []
    for g in range(NSA_KV_HEADS):
        gs = slice(g * NSA_HD, (g + 1) * NSA_HD)
        sc = _dot_nt(group_q(g), kc_ref[0, :, gs])
        imp = jnp.zeros((Q_TILE, n_sub), F32)
        for r in range(NSA_REP):
            rs = slice(r * Q_TILE, (r + 1) * Q_TILE)
            bias = _gather_bias(head_table(g * NSA_REP + r),
                                [idxc_ref[0, :, c * LANES:(c + 1) * LANES] for c in range(n_sub // LANES)])
            s_r = sc[rs] + bias
            m = jnp.maximum(jnp.max(s_r, axis=1, keepdims=True), M_INIT)
            e = jnp.exp(s_r - m)
            p = e / jnp.maximum(jnp.sum(e, axis=1, keepdims=True), 1e-30)
            imp = imp + p
            pc_s[rs, :] = p.astype(BF16)
        oc_s[g] = jnp.dot(pc_s[...], vc_ref[0, :, gs], preferred_element_type=F32)
        score = _dot_f32(imp, mmat_ref[...])
        score = jnp.where(forced, FORCE_SCORE, score)
        score = jnp.where(future, NEG, score)
        score = jnp.where(blk >= n_sel, -jnp.inf, score)
        score_t.append(score.T)

    blk_t = lax.broadcasted_iota(jnp.int32, (LANES, Q_TILE), 0).astype(F32)
    sel_t = [jnp.zeros((LANES, Q_TILE), F32) for _ in range(NSA_KV_HEADS)]
    for _ in range(min(SEL_TOPK, n_sel)):
        for g in range(NSA_KV_HEADS):
            mx = jnp.max(score_t[g], axis=0, keepdims=True)
            first = jnp.min(jnp.where(score_t[g] == mx, blk_t, float(LANES)), axis=0, keepdims=True)
            pick = blk_t == first
            sel_t[g] = jnp.where(pick, 1.0, sel_t[g])
            score_t[g] = jnp.where(pick, -jnp.inf, score_t[g])
    for g in range(NSA_KV_HEADS):
        sel_s[g] = sel_t[g].T.astype(BF16)

    n_kt = (s0 + Q_TILE + KEY_TILE - 1) // KEY_TILE
    n_far = jnp.maximum(n_kt - NEAR_TILES, 0)
    blk_row = lax.broadcasted_iota(jnp.int32, (LANES, KEY_TILE), 0)
    key_col = lax.broadcasted_iota(jnp.int32, (LANES, KEY_TILE), 1)

    for g in range(NSA_KV_HEADS):
        gs = slice(g * NSA_HD, (g + 1) * NSA_HD)
        qg = group_q(g)

        m_s[...] = jnp.full((rep_rows, 1), M_INIT, F32)
        l_s[...] = jnp.zeros((rep_rows, 1), F32)
        acc_s[...] = jnp.zeros((rep_rows, NSA_HD), F32)

        def key_tile(kt, near, g=g, gs=gs, qg=qg):
            k0 = pl.multiple_of(kt * KEY_TILE, KEY_TILE)
            s = jnp.dot(qg, kst_ref[0, gs, pl.ds(k0, KEY_TILE)], preferred_element_type=F32)
            expand = jnp.where((k0 + key_col) // SEL_BLOCK == blk_row, 1.0, 0.0).astype(BF16)
            chosen = jnp.dot(sel_s[g], expand, preferred_element_type=F32)
            negm = (chosen - 1.0) * (-NEG)
            for r in range(NSA_REP):
                rs = slice(r * Q_TILE, (r + 1) * Q_TILE)
                h = g * NSA_REP + r
                if near:
                    d_idx = (s0 - k0) // Q_TILE
                    bias = _gather_bias(head_table(h), [nb_ref[d_idx, :, c * LANES:(c + 1) * LANES]
                                                        for c in range(KEY_TILE // LANES)])
                else:
                    bias = tb_ref[h:h + 1, REL_BUCKETS - 1:REL_BUCKETS]
                s_r = s[rs] + (negm + bias)
                m_old = m_s[rs]
                m_new = jnp.maximum(m_old, jnp.max(s_r, axis=1, keepdims=True))
                alpha = jnp.exp(m_old - m_new)
                p = jnp.exp(s_r - m_new)
                l_s[rs] = alpha * l_s[rs] + jnp.sum(p, axis=1, keepdims=True)
                m_s[rs] = m_new
                a_s[rs] = alpha
                ps_s[rs, :] = p.astype(BF16)
            acc_s[...] = a_s[...] * acc_s[...] + jnp.dot(ps_s[...], vs_ref[pl.ds(k0, KEY_TILE), gs],
                                                         preferred_element_type=F32)

        def far_body(kt, carry):
            key_tile(kt, False)
            return carry

        def near_body(kt, carry):
            key_tile(kt, True)
            return carry

        lax.fori_loop(0, n_far, far_body, 0)
        lax.fori_loop(n_far, n_kt, near_body, 0)

        kw = jnp.concatenate([wr[0, :, gs] for wr in win_refs], axis=0)
        vw = jnp.concatenate([wr[0, :, KV_W // 2 + g * NSA_HD:KV_W // 2 + (g + 1) * NSA_HD] for wr in win_refs], axis=0)
        sw = _dot_nt(qg, kw)
        for r in range(NSA_REP):
            rs = slice(r * Q_TILE, (r + 1) * Q_TILE)
            bias = _gather_bias(head_table(g * NSA_REP + r),
                                [idxw_ref[:, c * LANES:(c + 1) * LANES] for c in range(WIN_BLOCKS)])
            s_r = sw[rs] + bias
            e = jnp.exp(s_r - jnp.max(s_r, axis=1, keepdims=True))
            pw_s[rs, :] = (e / jnp.sum(e, axis=1, keepdims=True)).astype(BF16)
        o_w = jnp.dot(pw_s[...], vw, preferred_element_type=F32)

        for r in range(NSA_REP):
            rs = slice(r * Q_TILE, (r + 1) * Q_TILE)
            h = g * NSA_REP + r
            o_h = (gate_ref[:, 3 * h:3 * h + 1] * oc_s[g, rs, :]
                   + gate_ref[:, 3 * h + 1:3 * h + 2] * (acc_s[rs, :] / l_s[rs])
                   + gate_ref[:, 3 * h + 2:3 * h + 3] * o_w[rs])
            o_s[:, h * NSA_HD:(h + 1) * NSA_HD] = o_h.astype(BF16)

    y_ref[...] = x_ref[...] + jnp.dot(o_s[...], wout_ref[...], preferred_element_type=F32)


def nsa_attend_prompt(q, gates, x, k_c, v_c, selb, winb, rel_bias, w_out, batch, seq):
    n_qt = seq // Q_TILE
    n_sub = k_c.shape[1]
    n_sel = seq // SEL_BLOCK
    assert CMP_LEN == 2 * CMP_STRIDE and SEL_BLOCK == 4 * CMP_STRIDE and SEL_TOPK <= n_sel <= LANES
    kst = jnp.swapaxes(selb[:, :KV_W // 2].reshape(batch, seq, KV_W // 2), 1, 2)
    win_pad = jnp.pad(winb.reshape(batch, seq, KV_W), ((0, 0), (WINDOW, 0), (0, 0)))
    iq = jnp.arange(Q_TILE, dtype=jnp.int32)
    dist_c = (jnp.arange(n_qt, dtype=jnp.int32)[:, None, None] * Q_TILE + iq[None, :, None]
              - (jnp.arange(n_sub, dtype=jnp.int32)[None, None, :] * CMP_STRIDE + CMP_LEN - 1))
    idx_c = _bucket_index(dist_c, dist_c >= 0)
    dist_w = iq[:, None] - jnp.arange(WINDOW + Q_TILE, dtype=jnp.int32)[None, :] + WINDOW
    idx_w = _bucket_index(dist_w, (dist_w >= 0) & (dist_w < WINDOW))
    n_near = NEAR_TILES * KEY_TILE // Q_TILE
    dist_n = (jnp.arange(n_near, dtype=jnp.int32)[:, None, None] * Q_TILE + iq[None, :, None]
              - jnp.arange(KEY_TILE, dtype=jnp.int32)[None, None, :])
    nb = _bucket_index(dist_n, dist_n >= 0)
    ci = np.arange(n_sub)[:, None]
    bj = np.arange(LANES)[None, :]
    mmat = ((ci // 4 == bj).astype(np.float32) + ((ci + 1) // 4 == bj).astype(np.float32)) * (ci < n_sub - 1)

    def rows(width):
        return pl.BlockSpec((Q_TILE, width), lambda b, t: (b * n_qt + t, 0))

    def per_b(shape):
        return pl.BlockSpec((1,) + shape, lambda b, t: (b,) + (0,) * len(shape))

    def const(shape):
        return pl.BlockSpec(shape, lambda b, t: (0,) * len(shape))

    in_specs = [rows(Q_W), rows(LANES), rows(D_MODEL), per_b((n_sub, KV_W // 2)), per_b((n_sub, KV_W // 2)),
                per_b((KV_W // 2, seq)), pl.BlockSpec((seq, KV_W // 2), lambda b, t: (b, 1))]
    in_specs += [pl.BlockSpec((1, Q_TILE, KV_W), lambda b, t, j=j: (b, t + j, 0)) for j in range(WIN_BLOCKS)]
    in_specs += [pl.BlockSpec((1, Q_TILE, n_sub), lambda b, t: (t, 0, 0)), const((Q_TILE, WINDOW + Q_TILE)),
                 const((n_near, Q_TILE, KEY_TILE)), const((n_sub, LANES)), const((NSA_HEADS, LANES)),
                 const((Q_W, D_MODEL))]
    rep_rows = NSA_REP * Q_TILE
    scratch = [pltpu.VMEM((rep_rows, n_sub), BF16), pltpu.VMEM((NSA_KV_HEADS, rep_rows, NSA_HD), F32),
               pltpu.VMEM((NSA_KV_HEADS, Q_TILE, LANES), BF16), pltpu.VMEM((rep_rows, 1), F32),
               pltpu.VMEM((rep_rows, 1), F32), pltpu.VMEM((rep_rows, 1), F32), pltpu.VMEM((rep_rows, NSA_HD), F32),
               pltpu.VMEM((rep_rows, KEY_TILE), BF16), pltpu.VMEM((rep_rows, WINDOW + Q_TILE), BF16),
               pltpu.VMEM((Q_TILE, Q_W), BF16)]
    return pl.pallas_call(
        functools.partial(_nsa_attend_body, n_sel=n_sel),
        out_shape=jax.ShapeDtypeStruct((batch * seq, D_MODEL), F32), grid=(batch, n_qt),
        in_specs=in_specs, out_specs=rows(D_MODEL), scratch_shapes=scratch,
        compiler_params=_params(2), name="nsa_attend_prompt",
    )(q, gates, x, k_c.astype(BF16), v_c.astype(BF16), kst, selb, *([win_pad] * WIN_BLOCKS),
      idx_c, idx_w, nb, jnp.asarray(mmat), _bias_table(rel_bias), w_out.astype(BF16))


SCORE_W = 384


def _nsa_sample_cmp_body(q_ref, kc_ref, vc_ref, idx_ref, mmat_ref, gsum_ref, tb_ref, oc_ref, top_ref, *,
                         n_sel, q_pos):
    n_sub = kc_ref.shape[1]
    q = q_ref[0]
    row_g = lax.broadcasted_iota(jnp.int32, (NSA_HEADS, 1), 0) // NSA_REP
    s = jnp.zeros((NSA_HEADS, n_sub), F32)
    for g in range(NSA_KV_HEADS):
        s = jnp.where(row_g == g, _dot_nt(q, kc_ref[0, :, g * NSA_HD:(g + 1) * NSA_HD]), s)
    tb = tb_ref[...]
    bias = jnp.concatenate([jnp.take_along_axis(
        tb, jnp.broadcast_to(idx_ref[:, c * LANES:(c + 1) * LANES], (NSA_HEADS, LANES)), axis=1)
        for c in range(n_sub // LANES)], axis=1)
    s = s + bias
    m = jnp.maximum(jnp.max(s, axis=1, keepdims=True), M_INIT)
    e = jnp.exp(s - m)
    p = e / jnp.maximum(jnp.sum(e, axis=1, keepdims=True), 1e-30)
    pb = p.astype(BF16)
    o = jnp.zeros((NSA_HEADS, NSA_HD), F32)
    for g in range(NSA_KV_HEADS):
        o = jnp.where(row_g == g, jnp.dot(pb, vc_ref[0, :, g * NSA_HD:(g + 1) * NSA_HD],
                                          preferred_element_type=F32), o)
    oc_ref[0] = o
    imp = _dot_f32(gsum_ref[...], p)
    score = _dot_f32(imp, mmat_ref[...])
    blk = lax.broadcasted_iota(jnp.int32, score.shape, 1)
    cur = q_pos // SEL_BLOCK
    score = jnp.where((blk == 0) | (blk == cur) | (blk == cur - 1), FORCE_SCORE, score)
    score = jnp.where(blk * SEL_BLOCK > q_pos, NEG, score)
    score = jnp.where(blk >= n_sel, -jnp.inf, score)
    blkf = blk.astype(F32)
    lane = lax.broadcasted_iota(jnp.int32, (SUBLANES, LANES), 1)
    top = jnp.zeros((SUBLANES, LANES), F32)
    for it in range(min(SEL_TOPK, n_sel)):
        mx = jnp.max(score, axis=1, keepdims=True)
        first = jnp.min(jnp.where(score == mx, blkf, float(SCORE_W)), axis=1, keepdims=True)
        top = jnp.where(lane == it, first, top)
        score = jnp.where(blkf == first, -jnp.inf, score)
    top_ref[0] = top.astype(jnp.int32)


def nsa_sample_cmp(q, k_c, v_c, rel_bias, past_len):
    batch = q.shape[0]
    n_sub = k_c.shape[1]
    n_sel = past_len // SEL_BLOCK + 1
    assert SEL_TOPK <= n_sel <= SCORE_W
    dist = past_len - (jnp.arange(n_sub, dtype=jnp.int32) * CMP_STRIDE + CMP_LEN - 1)
    idx = _bucket_index(dist, dist >= 0).reshape(1, n_sub)
    ci = np.arange(n_sub)[:, None]
    bj = np.arange(SCORE_W)[None, :]
    mmat = ((ci // 4 == bj).astype(np.float32) + ((ci + 1) // 4 == bj).astype(np.float32)) * (ci < n_sub - 1)
    gsum = (np.arange(SUBLANES)[:, None] == np.arange(NSA_HEADS)[None, :] // NSA_REP).astype(np.float32)

    def per_b(shape):
        return pl.BlockSpec((1,) + shape, lambda b: (b,) + (0,) * len(shape))

    def const(shape):
        return pl.BlockSpec(shape, lambda b: (0,) * len(shape))

    o_c, top = pl.pallas_call(
        functools.partial(_nsa_sample_cmp_body, n_sel=n_sel, q_pos=past_len),
        out_shape=(jax.ShapeDtypeStruct((batch, NSA_HEADS, NSA_HD), F32),
                   jax.ShapeDtypeStruct((batch, SUBLANES, LANES), jnp.int32)),
        grid=(batch,),
        in_specs=[per_b((NSA_HEADS, NSA_HD)), per_b((n_sub, KV_W // 2)), per_b((n_sub, KV_W // 2)),
                  const((1, n_sub)), const((n_sub, SCORE_W)), const((SUBLANES, NSA_HEADS)), const((NSA_HEADS, LANES))],
        out_specs=(per_b((NSA_HEADS, NSA_HD)), per_b((SUBLANES, LANES))),
        compiler_params=_params(1), name="nsa_sample_cmp",
    )(q, k_c.astype(BF16), v_c.astype(BF16), idx, jnp.asarray(mmat), jnp.asarray(gsum), _bias_table(rel_bias))
    return o_c, top[:, :NSA_KV_HEADS, :SEL_TOPK]


SEL_PER_STEP = 2


def _nsa_sample_attend_body(row_ref, sidx_ref, *refs, q_pos, n_past_blk, n_steps):
    n_blk_refs = NSA_KV_HEADS * SEL_PER_STEP
    blk_refs = refs[:n_blk_refs]
    (q_ref, knew_ref, win_ref, wnew_ref, gate_ref, oc_ref, idxw_ref, tb_ref, o_ref,
     m_s, l_s, acc_s, ow_s) = refs[n_blk_refs:]
    b = pl.program_id(0)
    step = pl.program_id(1)
    q = q_ref[0]
    row_g = lax.broadcasted_iota(jnp.int32, (NSA_HEADS, 1), 0) // NSA_REP
    tb = tb_ref[...]
    half = KV_W // 2

    def by_group(fn):
        out = fn(0)
        for g in range(1, NSA_KV_HEADS):
            out = jnp.where(row_g == g, fn(g), out)
        return out

    @pl.when(step == 0)
    def _():
        m_s[...] = jnp.full(m_s.shape, M_INIT, F32)
        l_s[...] = jnp.zeros(l_s.shape, F32)
        acc_s[...] = jnp.zeros(acc_s.shape, F32)
        s_w = by_group(lambda g: _dot_nt(q, win_ref[0, :, g * NSA_HD:(g + 1) * NSA_HD]))
        bias = jnp.concatenate([jnp.take_along_axis(
            tb, jnp.broadcast_to(idxw_ref[:, c * LANES:(c + 1) * LANES], (NSA_HEADS, LANES)), axis=1)
            for c in range(s_w.shape[1] // LANES)], axis=1)
        s_w = s_w + bias
        k_new = by_group(lambda g: jnp.broadcast_to(wnew_ref[0, :, g * NSA_HD:(g + 1) * NSA_HD], (NSA_HEADS, NSA_HD)))
        v_new = by_group(lambda g: jnp.broadcast_to(wnew_ref[0, :, half + g * NSA_HD:half + (g + 1) * NSA_HD],
                                                    (NSA_HEADS, NSA_HD)))
        s_n = (jnp.sum(q.astype(F32) * k_new.astype(BF16).astype(F32), axis=1, keepdims=True) + tb[:, 0:1])
        m = jnp.maximum(jnp.max(s_w, axis=1, keepdims=True), s_n)
        e_w = jnp.exp(s_w - m)
        e_n = jnp.exp(s_n - m)
        eb = e_w.astype(BF16)
        pv = by_group(lambda g: jnp.dot(eb, win_ref[0, :, half + g * NSA_HD:half + (g + 1) * NSA_HD].astype(BF16),
                                        preferred_element_type=F32))
        ow_s[...] = (pv + e_n * v_new) / (jnp.sum(e_w, axis=1, keepdims=True) + e_n)

    keys = SEL_PER_STEP * SEL_BLOCK
    key_row = lax.broadcasted_iota(jnp.int32, (SEL_BLOCK, NSA_HD), 0)
    lane = lax.broadcasted_iota(jnp.int32, (1, keys), 1)
    thresholds = _t5_thresholds()

    def tile_kv(g, col0):
        parts = []
        for j in range(SEL_PER_STEP):
            is_new = sidx_ref[b, g, step * SEL_PER_STEP + j] >= n_past_blk
            cached = blk_refs[g * SEL_PER_STEP + j][0, :, col0 + g * NSA_HD:col0 + (g + 1) * NSA_HD]
            fresh = jnp.where(key_row == 0, knew_ref[0, :, col0 + g * NSA_HD:col0 + (g + 1) * NSA_HD], 0.0)
            parts.append(jnp.where(is_new, fresh, cached))
        return jnp.concatenate(parts, axis=0).astype(BF16)

    def tile_bias(g):
        pos = jnp.zeros((1, keys), jnp.int32)
        for j in range(SEL_PER_STEP):
            blk = sidx_ref[b, g, step * SEL_PER_STEP + j]
            pos = jnp.where(lane // SEL_BLOCK == j, blk * SEL_BLOCK + lane % SEL_BLOCK, pos)
        dist = q_pos - pos
        idx = jnp.zeros((1, keys), jnp.int32)
        for thr in thresholds:
            idx = idx + (dist >= thr).astype(jnp.int32)
        idx = jnp.where(dist >= 0, idx, MASK_BUCKET)
        return jnp.take_along_axis(tb, jnp.broadcast_to(idx, (NSA_HEADS, keys)), axis=1)

    s = by_group(lambda g: _dot_nt(q, tile_kv(g, 0)) + tile_bias(g))
    m_old = m_s[...]
    m_new = jnp.maximum(m_old, jnp.max(s, axis=1, keepdims=True))
    alpha = jnp.exp(m_old - m_new)
    p = jnp.exp(s - m_new)
    pb = p.astype(BF16)
    l_s[...] = alpha * l_s[...] + jnp.sum(p, axis=1, keepdims=True)
    acc_s[...] = alpha * acc_s[...] + by_group(lambda g: jnp.dot(pb, tile_kv(g, half), preferred_element_type=F32))
    m_s[...] = m_new

    @pl.when(step == n_steps - 1)
    def _():
        gate = gate_ref[0]
        o_ref[0] = (gate[:, 0:1] * oc_ref[0] + gate[:, 1:2] * (acc_s[...] / l_s[...]) + gate[:, 2:3] * ow_s[...])


def nsa_sample_attend(q, kv_sel_new, kv_win_new, gates, o_c, top, cache_sel, win_buf, page_table, rel_bias,
                      past_len):
    batch = q.shape[0]
    n_past_blk = past_len // SEL_BLOCK
    bpp = cache_sel.shape[1] // SEL_BLOCK
    n_steps = SEL_TOPK // SEL_PER_STEP
    blocks = cache_sel.reshape(cache_sel.shape[0] * bpp, SEL_BLOCK, KV_W)
    jp = jnp.minimum(top, n_past_blk - 1)
    phys = jnp.take_along_axis(page_table, (jp // bpp).reshape(batch, -1), axis=1).reshape(top.shape)
    blk_row = (phys * bpp + jp % bpp).astype(jnp.int32)
    wb = win_buf.shape[1]
    dist_w = past_len - (past_len - wb + jnp.arange(wb, dtype=jnp.int32))
    idx_w = _bucket_index(dist_w, (dist_w >= 0) & (dist_w < WINDOW)).reshape(1, wb)
    gate3 = gates[:, :3 * NSA_HEADS].reshape(batch, NSA_HEADS, 3)

    def blk_spec(g, j):
        return pl.BlockSpec((1, SEL_BLOCK, KV_W),
                            lambda b, s, rows, sidx, g=g, j=j: (rows[b, g, s * SEL_PER_STEP + j], 0, 0))

    def per_b(shape):
        return pl.BlockSpec((1,) + shape, lambda b, s, rows, sidx: (b,) + (0,) * len(shape))

    def const(shape):
        return pl.BlockSpec(shape, lambda b, s, rows, sidx: (0,) * len(shape))

    grid_spec = pltpu.PrefetchScalarGridSpec(
        num_scalar_prefetch=2, grid=(batch, n_steps),
        in_specs=[blk_spec(g, j) for g in range(NSA_KV_HEADS) for j in range(SEL_PER_STEP)] + [
            per_b((NSA_HEADS, NSA_HD)), per_b((1, KV_W)), per_b((wb, KV_W)), per_b((1, KV_W)),
            per_b((NSA_HEADS, 3)), per_b((NSA_HEADS, NSA_HD)), const((1, wb)), const((NSA_HEADS, LANES))],
        out_specs=per_b((NSA_HEADS, NSA_HD)),
        scratch_shapes=[pltpu.VMEM((NSA_HEADS, 1), F32), pltpu.VMEM((NSA_HEADS, 1), F32),
                        pltpu.VMEM((NSA_HEADS, NSA_HD), F32), pltpu.VMEM((NSA_HEADS, NSA_HD), F32)])
    o = pl.pallas_call(
        functools.partial(_nsa_sample_attend_body, q_pos=past_len, n_past_blk=n_past_blk, n_steps=n_steps),
        out_shape=jax.ShapeDtypeStruct((batch, NSA_HEADS, NSA_HD), F32), grid_spec=grid_spec,
        compiler_params=_params(2), name="nsa_sample_attend",
    )(blk_row, top.astype(jnp.int32), *([blocks] * (NSA_KV_HEADS * SEL_PER_STEP)), q,
      kv_sel_new.reshape(batch, 1, KV_W), win_buf, kv_win_new.reshape(batch, 1, KV_W), gate3, o_c, idx_w,
      _bias_table(rel_bias))
    return o.reshape(batch, Q_W)


def _pad_cols(w, width):
    return jnp.pad(w, ((0, 0), (0, width - w.shape[1]))).astype(BF16)


def kernel(x_prompt, x_sample, state_ret, state_mlstm_C, state_mlstm_n, state_mlstm_m, state_conv, cache_nsa_cmp, cache_nsa_sel, state_nsa_win, page_table, rel_bias, norm_mix, norm_ffn, ab_w_in, ab_conv_w, ab_conv_b, ab_b_igate, ab_b_fgate, ab_gn_g, ab_gn_b, ab_hn_g, ab_w_out, nsa_w_in, nsa_q_norm, nsa_k_norm, nsa_cmp_w1, nsa_cmp_b1, nsa_cmp_w2, nsa_cmp_b2, nsa_w_out, moe_w_group, moe_b_group, moe_w_expert, moe_b_expert, moe_w_gate, moe_w_up, moe_w_down):
    bp, lp, d = x_prompt.shape
    bs, ls, _ = x_sample.shape
    page_size = cache_nsa_cmp.shape[2]
    past_len = page_table.shape[1] * page_size
    assert norm_mix.shape[0] == 2 and ls == 1 and d == D_MODEL and lp % KEY_TILE == 0
    xp = x_prompt.reshape(bp * lp, d)
    xs = x_sample.reshape(bs, d)

    def moe(x, layer):
        return hier_moe_residual(x, norm_ffn[layer], moe_w_group[layer], moe_b_group[layer], moe_w_expert[layer],
                                 moe_b_expert[layer], moe_w_gate[layer], moe_w_up[layer], moe_w_down[layer])

    w_in = _pad_cols(ab_w_in[0], AB_IN_PAD)
    ab = (ab_conv_w[0], ab_conv_b[0], ab_b_igate[0], ab_b_fgate[0], ab_gn_g[0], ab_gn_b[0], ab_hn_g[0])
    zp = norm_matmul(xp, norm_mix[0], w_in)
    xp, ret_p, mc_p, mn_p, mm_p, conv_p = ab_prompt(zp, xp, bp, lp, *ab, ab_w_out[0])
    zs = norm_matmul(xs, norm_mix[0], w_in)
    pos_s = past_len + jnp.arange(ls, dtype=jnp.int32)
    ys, ret_s, mc_s, mn_s, mm_s, conv_s = ab_sample(zs, pos_s, state_ret[0], state_mlstm_C[0], state_mlstm_n[0],
                                                    state_mlstm_m[0], state_conv[0], *ab)
    xs = matmul_residual(ys, ab_w_out[0].astype(BF16), xs)
    xp = moe(xp, 0)
    xs = moe(xs, 0)

    w_in = _pad_cols(nsa_w_in[0], NSA_IN_PAD)
    cmp_w = (nsa_cmp_w1[0], nsa_cmp_b1[0], nsa_cmp_w2[0], nsa_cmp_b2[0], nsa_k_norm[0, 0])
    kv_shape = (2, NSA_KV_HEADS, NSA_HD)
    zp = norm_matmul(xp, norm_mix[1], w_in)
    q, cmp_p, sel_p, win_p, selb, winb, gates = nsa_prep(zp, nsa_q_norm[0], nsa_k_norm[0])
    own_pages = jnp.arange(bp * lp // page_size, dtype=jnp.int32).reshape(bp, lp // page_size)
    k_c, v_c = nsa_compress(cmp_p.reshape(bp * lp // page_size, page_size, KV_W), own_pages, *cmp_w)
    xp = nsa_attend_prompt(q, gates, xp, k_c, v_c, selb, winb, rel_bias, nsa_w_out[0], bp, lp)
    win_keep = min(WINDOW, lp)
    cmp_p = cmp_p.reshape((1, bp, lp) + kv_shape)
    sel_p = sel_p.reshape((1, bp, lp) + kv_shape)
    win_p = win_p.reshape((bp, lp) + kv_shape)[None, :, lp - win_keep:]

    zs = norm_matmul(xs, norm_mix[1], w_in)
    q, cmp_s, sel_s, win_s, _, _, gates = nsa_prep(zs, nsa_q_norm[0], nsa_k_norm[0])
    n_pool = cache_nsa_cmp.shape[1]
    k_c, v_c = nsa_compress(cache_nsa_cmp[0].reshape(n_pool, page_size, KV_W), page_table, *cmp_w)
    q3 = q.reshape(bs, NSA_HEADS, NSA_HD)
    o_c, top = nsa_sample_cmp(q3, k_c, v_c, rel_bias, past_len)
    wb = state_nsa_win.shape[2]
    win_buf = state_nsa_win[0].reshape(bs, wb, KV_W)
    o = nsa_sample_attend(q3, sel_s, win_s, gates, o_c, top, cache_nsa_sel[0].reshape(n_pool, page_size, KV_W),
                          win_buf, page_table, rel_bias, past_len)
    xs = matmul_residual(o, nsa_w_out[0].astype(BF16), xs)
    win_s = jnp.concatenate([win_buf, win_s.reshape(bs, ls, KV_W)], axis=1)[:, ls:]
    cmp_s = cmp_s.reshape((1, bs, ls) + kv_shape)
    sel_s = sel_s.reshape((1, bs, ls) + kv_shape)
    win_s = win_s.reshape((1, bs, wb) + kv_shape)
    xp = moe(xp, 1)
    xs = moe(xs, 1)

    return (xp.reshape(bp, lp, d), xs.reshape(bs, ls, d), ret_p[None], ret_s[None], mc_p[None], mc_s[None],
            mn_p[None], mn_s[None], mm_p[None], mm_s[None], conv_p[None], conv_s[None],
            cmp_p, cmp_s, sel_p, sel_s, win_p, win_s)
```

```python
import functools
import math

import jax
import jax.numpy as jnp
import numpy as np
from jax import lax
from jax.experimental import pallas as pl
from jax.experimental.pallas import tpu as pltpu

F32 = jnp.float32
BF16 = jnp.bfloat16
LANES = 128
SUBLANES = 8
VMEM_LIMIT = 56 * 1024 * 1024

D_MODEL = 1024
RET_HEADS = 4
ML_HEADS = 4
HEAD_D = 128
CONV_W = 4
CHUNK = 128
ROPE_BASE = 10000.0
AB_IN = 4104
AB_IN_PAD = 4224
NSA_HEADS = 16
NSA_KV_HEADS = 4
NSA_REP = 4
NSA_HD = 64
NSA_IN_PAD = 2688
CMP_LEN = 32
CMP_STRIDE = 16
CMP_HID = 128
SEL_BLOCK = 64
SEL_TOPK = 16
WINDOW = 512
REL_BUCKETS = 32
REL_MAX_DIST = 1024
FORCE_SCORE = 1e4
MOE_GROUPS = 4
MOE_EXP_PER_GROUP = 8
MOE_EXPERTS = 32
NEG = -1e30
EPS = 1e-6


def _params(n_grid):
    return pltpu.CompilerParams(dimension_semantics=("arbitrary",) * n_grid, vmem_limit_bytes=VMEM_LIMIT)


def _dot(a, b):
    return jnp.dot(a.astype(BF16), b.astype(BF16), preferred_element_type=F32)


def _dot_nt(a, b):
    return lax.dot_general(a.astype(BF16), b.astype(BF16), (((1,), (1,)), ((), ())), preferred_element_type=F32)


def _dot_tn(a, b):
    return lax.dot_general(a.astype(BF16), b.astype(BF16), (((0,), (0,)), ((), ())), preferred_element_type=F32)


def _dot_f32(a, b):
    return jnp.dot(a, b, preferred_element_type=F32, precision=lax.Precision.HIGHEST)


def _r16(x):
    return x.astype(BF16).astype(F32)


def _sigmoid(x):
    return 1.0 / (1.0 + jnp.exp(-x))


def _silu(x):
    return x * _sigmoid(x)


def _log_sigmoid(x):
    return -(jnp.maximum(-x, 0.0) + jnp.log1p(jnp.exp(-jnp.abs(x))))


def _norm_matmul_body(x_ref, g_ref, w_ref, o_ref, *, col_tile):
    x = x_ref[...]
    y = x * lax.rsqrt(jnp.mean(x * x, axis=-1, keepdims=True) + EPS) * g_ref[...]
    yb = y.astype(BF16)
    for c0 in range(0, o_ref.shape[1], col_tile):
        o_ref[:, c0:c0 + col_tile] = jnp.dot(yb, w_ref[:, c0:c0 + col_tile], preferred_element_type=F32)


def norm_matmul(x, g, w, row_tile=256):
    n, d = x.shape
    c = w.shape[1]
    tm = min(row_tile, n)
    col_tile = 384 if c % 384 == 0 else LANES
    return pl.pallas_call(
        functools.partial(_norm_matmul_body, col_tile=col_tile),
        out_shape=jax.ShapeDtypeStruct((n, c), F32),
        grid=(n // tm,),
        in_specs=[pl.BlockSpec((tm, d), lambda i: (i, 0)),
                  pl.BlockSpec((1, d), lambda i: (0, 0)),
                  pl.BlockSpec((d, c), lambda i: (0, 0))],
        out_specs=pl.BlockSpec((tm, c), lambda i: (i, 0)),
        compiler_params=_params(1),
        name="norm_matmul",
    )(x, g.reshape(1, d), w)


def _retention_constants(c):
    h = np.arange(RET_HEADS, dtype=np.float64)
    log_g = np.log1p(-np.exp2(-5.0 - h))
    i = np.arange(c, dtype=np.float64)
    diff = i[:, None] - i[None, :]
    decay = np.where(diff >= 0, np.exp(np.maximum(diff, 0.0)[None] * log_g[:, None, None]), 0.0)
    q_dec = np.exp((i + 1.0)[None, :] * log_g[:, None])[:, :, None]
    k_dec = np.exp((c - 1.0 - i)[None, :] * log_g[:, None])[:, :, None]
    s_dec = np.exp(c * log_g)
    return (jnp.asarray(decay, F32), jnp.asarray(q_dec, F32), jnp.asarray(k_dec, F32),
            [float(v) for v in s_dec])


def _rope_tables(pos):
    half = HEAD_D // 2
    freqs = ROPE_BASE ** (-jnp.arange(half, dtype=F32) / half)
    ang = pos.astype(F32)[:, None] * freqs[None, :]
    cos, sin = jnp.cos(ang), jnp.sin(ang)
    return jnp.concatenate([cos, cos], axis=-1), jnp.concatenate([-sin, sin], axis=-1)


def _rope(x, cosf, sinf):
    return x * cosf + pltpu.roll(x, HEAD_D // 2, 1) * sinf


def _ab_prompt_body(rq_ref, rk_ref, rv_ref, rg_ref, mqk_ref, mv_ref, mo_ref, gz_ref, x_ref, cos_ref, sin_ref,
                    decay_ref, qdec_ref, kdec_ref, convw_ref, convb_ref, gbias_ref, gng_ref, gnb_ref, hng_ref,
                    wout_ref,
                    y_ref, s_ref, c_ref, n_ref, m_ref, conv_ref,
                    cbuf_ref, ycat_ref, *, s_dec):
    c = pl.program_id(1)
    tail = CONV_W - 1

    @pl.when(c == 0)
    def _():
        s_ref[...] = jnp.zeros_like(s_ref)
        c_ref[...] = jnp.zeros_like(c_ref)
        n_ref[...] = jnp.zeros_like(n_ref)
        m_ref[...] = jnp.zeros_like(m_ref)
        cbuf_ref[0:SUBLANES, :] = jnp.zeros((SUBLANES, cbuf_ref.shape[1]), F32)

    cosf = cos_ref[...]
    sinf = sin_ref[...]
    row = lax.broadcasted_iota(jnp.int32, (CHUNK, CHUNK), 0)
    col = lax.broadcasted_iota(jnp.int32, (CHUNK, CHUNK), 1)
    eye = row == col
    tril = row >= col
    triu = row <= col

    cbuf_ref[SUBLANES:SUBLANES + CHUNK, :] = mqk_ref[...]
    conv = convb_ref[...]
    for w in range(CONV_W):
        conv = conv + (_r16(cbuf_ref[SUBLANES - tail + w:SUBLANES - tail + w + CHUNK, :])
                       * _r16(convw_ref[w:w + 1, :]))
    qk = _silu(conv)
    last = cbuf_ref[CHUNK + SUBLANES - tail:CHUNK + SUBLANES, :]
    cbuf_ref[SUBLANES - tail:SUBLANES, :] = last
    conv_ref[0] = last

    gz = gz_ref[...] + gbias_ref[...]
    for h in range(RET_HEADS):
        sl = slice(h * HEAD_D, (h + 1) * HEAD_D)
        q = _rope(rq_ref[:, sl], cosf, sinf)
        k = _rope(rk_ref[:, sl], cosf, sinf) * (HEAD_D ** -0.5)
        v = rv_ref[:, sl]
        a = _dot_nt(q, k) * decay_ref[h]
        s_old = s_ref[0, h]
        o = _dot(a, v) + qdec_ref[h] * _dot(q, s_old)
        s_ref[0, h] = s_dec[h] * s_old + _dot_tn(k * kdec_ref[h], v)
        mu = jnp.mean(o, axis=-1, keepdims=True)
        var = jnp.mean(jnp.square(o - mu), axis=-1, keepdims=True)
        o = (o - mu) * lax.rsqrt(var + EPS) * gng_ref[:, sl] + gnb_ref[:, sl]
        ycat_ref[:, sl] = _silu(rg_ref[:, sl]) * o

        mq = qk[:, sl]
        mk = qk[:, ML_HEADS * HEAD_D + h * HEAD_D:ML_HEADS * HEAD_D + (h + 1) * HEAD_D] * (HEAD_D ** -0.5)
        mv = mv_ref[:, sl]
        i_col = gz[:, h:h + 1]
        f_col = _log_sigmoid(gz[:, ML_HEADS + h:ML_HEADS + h + 1])
        i_row = jnp.sum(jnp.where(eye, i_col, 0.0), axis=0, keepdims=True)
        f_row = jnp.sum(jnp.where(eye, f_col, 0.0), axis=0, keepdims=True)
        b_col = jnp.sum(jnp.where(tril, f_row, 0.0), axis=1, keepdims=True)
        b_row = jnp.sum(jnp.where(triu, f_col, 0.0), axis=0, keepdims=True)
        m_old = m_ref[0, h:h + 1, 0:1]
        dlog = jnp.where(tril, b_col - b_row + i_row, -jnp.inf)
        inter = b_col + m_old
        m_t = jnp.maximum(inter, jnp.max(dlog, axis=1, keepdims=True))
        wgt = _dot_nt(mq, mk) * jnp.exp(dlog - m_t)
        e_inter = jnp.exp(inter - m_t)
        c_old = c_ref[0, h]
        n_old = n_ref[0, h:h + 1, :]
        num = _dot(wgt, mv) + e_inter * _dot_nt(mq, c_old)
        den = (jnp.sum(wgt, axis=1, keepdims=True)
               + e_inter * jnp.sum(_r16(mq) * _r16(n_old), axis=1, keepdims=True))
        hc = num / jnp.maximum(jnp.abs(den), jnp.exp(-m_t))
        b_last = b_col[CHUNK - 1:CHUNK, :]
        u_row = b_last - b_row + i_row
        u_col = b_last - b_col + i_col
        m_new = jnp.maximum(b_last + m_old, jnp.max(u_row, axis=1, keepdims=True))
        ws_col = jnp.exp(u_col - m_new)
        f_state = jnp.exp(b_last + m_old - m_new)
        c_ref[0, h] = f_state * c_old + _dot_tn(mv * ws_col, mk)
        n_ref[0, h:h + 1, :] = f_state * n_old + jnp.sum(_r16(ws_col) * _r16(mk), axis=0, keepdims=True)
        m_ref[0, h:h + 1, :] = jnp.broadcast_to(m_new, (1, LANES))
        hm = _sigmoid(mo_ref[:, sl]) * hc
        hm = hm * lax.rsqrt(jnp.mean(hm * hm, axis=-1, keepdims=True) + EPS) * hng_ref[:, sl]
        ycat_ref[:, RET_HEADS * HEAD_D + h * HEAD_D:RET_HEADS * HEAD_D + (h + 1) * HEAD_D] = hm

    y_ref[...] = x_ref[...] + jnp.dot(ycat_ref[...].astype(BF16), wout_ref[...], preferred_element_type=F32)


def ab_prompt(z, x, batch, seq, conv_w, conv_b, b_ig, b_fg, gn_g, gn_b, hn_g, w_out):
    n_chunk = seq // CHUNK
    decay, q_dec, k_dec, s_dec = _retention_constants(CHUNK)
    cosf, sinf = _rope_tables(jnp.arange(seq, dtype=jnp.int32))
    gbias = jnp.zeros((1, LANES), F32).at[0, :ML_HEADS].set(b_ig).at[0, ML_HEADS:2 * ML_HEADS].set(b_fg)
    hw = RET_HEADS * HEAD_D
    qkw = 2 * ML_HEADS * HEAD_D

    def zspec(width, blk):
        return pl.BlockSpec((CHUNK, width), lambda b, c, blk=blk: (b * n_chunk + c, blk))

    def const(shape):
        return pl.BlockSpec(shape, lambda b, c: (0,) * len(shape))

    in_specs = [zspec(hw, 0), zspec(hw, 1), zspec(hw, 2), zspec(hw, 3), zspec(qkw, 2), zspec(hw, 6), zspec(hw, 7),
                zspec(LANES, (AB_IN_PAD - LANES) // LANES),
                pl.BlockSpec((CHUNK, D_MODEL), lambda b, c: (b * n_chunk + c, 0)),
                pl.BlockSpec((CHUNK, HEAD_D), lambda b, c: (c, 0)),
                pl.BlockSpec((CHUNK, HEAD_D), lambda b, c: (c, 0)),
                const((RET_HEADS, CHUNK, CHUNK)), const((RET_HEADS, CHUNK, 1)), const((RET_HEADS, CHUNK, 1)),
                const((CONV_W, qkw)), const((1, qkw)), const((1, LANES)),
                const((1, hw)), const((1, hw)), const((1, hw)), const((2 * hw, D_MODEL))]
    out_shape = (jax.ShapeDtypeStruct((batch * seq, D_MODEL), F32),
                 jax.ShapeDtypeStruct((batch, RET_HEADS, HEAD_D, HEAD_D), F32),
                 jax.ShapeDtypeStruct((batch, ML_HEADS, HEAD_D, HEAD_D), F32),
                 jax.ShapeDtypeStruct((batch, ML_HEADS, HEAD_D), F32),
                 jax.ShapeDtypeStruct((batch, SUBLANES, LANES), F32),
                 jax.ShapeDtypeStruct((batch, CONV_W - 1, qkw), F32))
    out_specs = (pl.BlockSpec((CHUNK, D_MODEL), lambda b, c: (b * n_chunk + c, 0)),
                 pl.BlockSpec((1, RET_HEADS, HEAD_D, HEAD_D), lambda b, c: (b, 0, 0, 0)),
                 pl.BlockSpec((1, ML_HEADS, HEAD_D, HEAD_D), lambda b, c: (b, 0, 0, 0)),
                 pl.BlockSpec((1, ML_HEADS, HEAD_D), lambda b, c: (b, 0, 0)),
                 pl.BlockSpec((1, SUBLANES, LANES), lambda b, c: (b, 0, 0)),
                 pl.BlockSpec((1, CONV_W - 1, qkw), lambda b, c: (b, 0, 0)))
    y, s, cc, n, m, conv = pl.pallas_call(
        functools.partial(_ab_prompt_body, s_dec=s_dec),
        out_shape=out_shape, grid=(batch, n_chunk), in_specs=in_specs, out_specs=out_specs,
        scratch_shapes=[pltpu.VMEM((CHUNK + SUBLANES, qkw), F32), pltpu.VMEM((CHUNK, 2 * hw), F32)],
        compiler_params=_params(2), name="ab_prompt",
    )(z, z, z, z, z, z, z, z, x, cosf, sinf, decay, q_dec, k_dec, conv_w, conv_b.reshape(1, qkw), gbias,
      gn_g.reshape(1, hw), gn_b.reshape(1, hw), hn_g.reshape(1, hw), w_out.astype(BF16))
    return y, s, cc, n, m[:, :ML_HEADS, 0], conv


def _matmul_residual_body(a_ref, w_ref, x_ref, o_ref):
    o_ref[...] = x_ref[...] + jnp.dot(a_ref[...].astype(BF16), w_ref[...], preferred_element_type=F32)


def matmul_residual(a, w, x, row_tile=256):
    n, kk = a.shape
    d = w.shape[1]
    tm = min(row_tile, n)
    return pl.pallas_call(
        _matmul_residual_body, out_shape=jax.ShapeDtypeStruct((n, d), F32), grid=(n // tm,),
        in_specs=[pl.BlockSpec((tm, kk), lambda i: (i, 0)), pl.BlockSpec((kk, d), lambda i: (0, 0)),
                  pl.BlockSpec((tm, d), lambda i: (i, 0))],
        out_specs=pl.BlockSpec((tm, d), lambda i: (i, 0)),
        compiler_params=_params(1), name="matmul_residual",
    )(a, w, x)


def _ab_sample_body(m0_ref, z_ref, cos_ref, sin_ref, s0_ref, c0_ref, n0_ref, conv0_ref,
                    convw_ref, convb_ref, gbias_ref, gng_ref, gnb_ref, hng_ref,
                    y_ref, s_ref, c_ref, n_ref, m_ref, conv_ref, *, g_dec):
    b = pl.program_id(0)
    hw = RET_HEADS * HEAD_D
    qkw = 2 * ML_HEADS * HEAD_D
    tail = CONV_W - 1
    cosf = cos_ref[...]
    sinf = sin_ref[...]
    row = lax.broadcasted_iota(jnp.int32, (HEAD_D, HEAD_D), 0)
    col = lax.broadcasted_iota(jnp.int32, (HEAD_D, HEAD_D), 1)
    eye = row == col

    def to_col(r):
        return jnp.sum(jnp.where(eye, r, 0.0), axis=1, keepdims=True)

    def to_row(cv):
        return jnp.sum(jnp.where(eye, cv, 0.0), axis=0, keepdims=True)

    mqk = z_ref[0, :, 4 * hw:4 * hw + qkw]
    conv = convb_ref[...] + mqk * convw_ref[tail:CONV_W, :]
    for w in range(tail):
        conv = conv + conv0_ref[0, w:w + 1, :] * convw_ref[w:w + 1, :]
    qk = _silu(conv)
    conv_ref[0, 0:tail - 1, :] = conv0_ref[0, 1:tail, :]
    conv_ref[0, tail - 1:tail, :] = mqk
    gz = z_ref[0, :, AB_IN_PAD - LANES:AB_IN_PAD] + gbias_ref[...]

    for h in range(RET_HEADS):
        sl = slice(h * HEAD_D, (h + 1) * HEAD_D)
        q = _rope(z_ref[0, :, sl], cosf, sinf)
        k = _rope(z_ref[0, :, hw + h * HEAD_D:hw + (h + 1) * HEAD_D], cosf, sinf) * (HEAD_D ** -0.5)
        v = z_ref[0, :, 2 * hw + h * HEAD_D:2 * hw + (h + 1) * HEAD_D]
        rg = z_ref[0, :, 3 * hw + h * HEAD_D:3 * hw + (h + 1) * HEAD_D]
        s_old = s0_ref[0, h]
        qk_s = jnp.sum(q * k, axis=1, keepdims=True)
        o = qk_s * v + g_dec[h] * jnp.sum(_r16(to_col(q)) * _r16(s_old), axis=0, keepdims=True)
        s_ref[0, h] = g_dec[h] * s_old + to_col(k) * v
        mu = jnp.mean(o, axis=-1, keepdims=True)
        var = jnp.mean(jnp.square(o - mu), axis=-1, keepdims=True)
        o = (o - mu) * lax.rsqrt(var + EPS) * gng_ref[:, sl] + gnb_ref[:, sl]
        y_ref[0, :, sl] = _silu(rg) * o

        mq = qk[:, sl]
        mk = qk[:, ML_HEADS * HEAD_D + h * HEAD_D:ML_HEADS * HEAD_D + (h + 1) * HEAD_D] * (HEAD_D ** -0.5)
        mv = z_ref[0, :, 4 * hw + qkw + h * HEAD_D:4 * hw + qkw + (h + 1) * HEAD_D]
        mo = z_ref[0, :, 5 * hw + qkw + h * HEAD_D:5 * hw + qkw + (h + 1) * HEAD_D]
        ig = gz[:, h:h + 1]
        lf = _log_sigmoid(gz[:, ML_HEADS + h:ML_HEADS + h + 1])
        m_old = m0_ref[b, h]
        inter = lf + m_old
        m_t = jnp.maximum(inter, ig)
        wgt = jnp.sum(mq * mk, axis=1, keepdims=True) * jnp.exp(ig - m_t)
        e_inter = jnp.exp(inter - m_t)
        c_old = c0_ref[0, h]
        n_old = n0_ref[0, h:h + 1, :]
        cq = to_row(jnp.sum(_r16(c_old) * _r16(mq), axis=1, keepdims=True))
        num = wgt * mv + e_inter * cq
        den = wgt + e_inter * jnp.sum(n_old * mq, axis=1, keepdims=True)
        hc = num / jnp.maximum(jnp.abs(den), jnp.exp(-m_t))
        ws = jnp.exp(ig - m_t)
        c_ref[0, h] = e_inter * c_old + (ws * to_col(mv)) * mk
        n_ref[0, h:h + 1, :] = e_inter * n_old + ws * mk
        m_ref[0, h:h + 1, :] = jnp.broadcast_to(m_t, (1, LANES))
        hm = _sigmoid(mo) * hc
        hm = hm * lax.rsqrt(jnp.mean(hm * hm, axis=-1, keepdims=True) + EPS) * hng_ref[:, sl]
        y_ref[0, :, hw + h * HEAD_D:hw + (h + 1) * HEAD_D] = hm
    m_ref[0, ML_HEADS:SUBLANES, :] = jnp.zeros((SUBLANES - ML_HEADS, LANES), F32)


def ab_sample(z, pos, s0, c0, n0, m0, conv0, conv_w, conv_b, b_ig, b_fg, gn_g, gn_b, hn_g):
    batch = z.shape[0]
    h = np.arange(RET_HEADS, dtype=np.float64)
    g_dec = [float(v) for v in np.exp(np.log1p(-np.exp2(-5.0 - h)))]
    cosf, sinf = _rope_tables(pos)
    gbias = jnp.zeros((1, LANES), F32).at[0, :ML_HEADS].set(b_ig).at[0, ML_HEADS:2 * ML_HEADS].set(b_fg)
    hw = RET_HEADS * HEAD_D
    qkw = 2 * ML_HEADS * HEAD_D

    def per_b(shape):
        return pl.BlockSpec((1,) + shape, lambda b: (b,) + (0,) * len(shape))

    def const(shape):
        return pl.BlockSpec(shape, lambda b: (0,) * len(shape))

    in_specs = [pl.BlockSpec(memory_space=pltpu.SMEM), per_b((1, AB_IN_PAD)), const((1, HEAD_D)), const((1, HEAD_D)),
                per_b((RET_HEADS, HEAD_D, HEAD_D)), per_b((ML_HEADS, HEAD_D, HEAD_D)), per_b((ML_HEADS, HEAD_D)),
                per_b((CONV_W - 1, qkw)), const((CONV_W, qkw)), const((1, qkw)), const((1, LANES)),
                const((1, hw)), const((1, hw)), const((1, hw))]
    out_shape = (jax.ShapeDtypeStruct((batch, 1, 2 * hw), F32),
                 jax.ShapeDtypeStruct((batch, RET_HEADS, HEAD_D, HEAD_D), F32),
                 jax.ShapeDtypeStruct((batch, ML_HEADS, HEAD_D, HEAD_D), F32),
                 jax.ShapeDtypeStruct((batch, ML_HEADS, HEAD_D), F32),
                 jax.ShapeDtypeStruct((batch, SUBLANES, LANES), F32),
                 jax.ShapeDtypeStruct((batch, CONV_W - 1, qkw), F32))
    out_specs = (per_b((1, 2 * hw)), per_b((RET_HEADS, HEAD_D, HEAD_D)), per_b((ML_HEADS, HEAD_D, HEAD_D)),
                 per_b((ML_HEADS, HEAD_D)), per_b((SUBLANES, LANES)), per_b((CONV_W - 1, qkw)))
    y, s, cc, n, m, conv = pl.pallas_call(
        functools.partial(_ab_sample_body, g_dec=g_dec),
        out_shape=out_shape, grid=(batch,), in_specs=in_specs, out_specs=out_specs,
        compiler_params=_params(1), name="ab_sample",
    )(m0, z.reshape(batch, 1, AB_IN_PAD), cosf, sinf, s0, c0, n0, conv0, conv_w, conv_b.reshape(1, qkw), gbias,
      gn_g.reshape(1, hw), gn_b.reshape(1, hw), hn_g.reshape(1, hw))
    return y.reshape(batch, 2 * hw), s, cc, n, m[:, :ML_HEADS, 0], conv


MOE_TILE = 256


def _moe_router_body(x_ref, g_ref, wr_ref, br_ref, hn_ref, route_ref):
    x = x_ref[...]
    hn = x * lax.rsqrt(jnp.mean(x * x, axis=-1, keepdims=True) + EPS) * g_ref[...]
    hn_ref[...] = hn.astype(BF16)
    z = _dot(hn, wr_ref[...]) + br_ref[...]
    lane = lax.broadcasted_iota(jnp.int32, z.shape, 1)
    lanef = lane.astype(F32)
    is_group = lane < MOE_GROUPS
    gl = jnp.where(is_group, z, -jnp.inf)
    gmax = jnp.max(gl, axis=1, keepdims=True)
    g_top = jnp.min(jnp.where(gl == gmax, lanef, float(LANES)), axis=1, keepdims=True)
    pg_top = 1.0 / jnp.sum(jnp.where(is_group, jnp.exp(z - gmax), 0.0), axis=1, keepdims=True)
    grp = ((lane - MOE_GROUPS) // MOE_EXP_PER_GROUP).astype(F32)
    in_group = (lane >= MOE_GROUPS) & (lane < MOE_GROUPS + MOE_EXPERTS) & (grp == g_top)
    el = jnp.where(in_group, z, -jnp.inf)
    v1 = jnp.max(el, axis=1, keepdims=True)
    i1 = jnp.min(jnp.where(el == v1, lanef, float(LANES)), axis=1, keepdims=True)
    el2 = jnp.where(lanef == i1, -jnp.inf, el)
    v2 = jnp.max(el2, axis=1, keepdims=True)
    i2 = jnp.min(jnp.where(el2 == v2, lanef, float(LANES)), axis=1, keepdims=True)
    t = jnp.exp(v2 - v1)
    p1 = 1.0 / (1.0 + t)
    out = jnp.where(lane == 0, i1 - MOE_GROUPS,
                    jnp.where(lane == 1, i2 - MOE_GROUPS,
                              jnp.where(lane == 2, pg_top * p1,
                                        jnp.where(lane == 3, pg_top * (t * p1), 0.0))))
    route_ref[...] = out


def moe_router(x, g, w_group, b_group, w_expert, b_expert, row_tile=256):
    n, d = x.shape
    tm = min(row_tile, n)
    used = MOE_GROUPS + MOE_EXPERTS
    wr = jnp.pad(jnp.concatenate([w_group, w_expert], axis=1), ((0, 0), (0, LANES - used)))
    br = jnp.pad(jnp.concatenate([b_group, b_expert]), (0, LANES - used)).reshape(1, LANES)
    hn, route = pl.pallas_call(
        _moe_router_body,
        out_shape=(jax.ShapeDtypeStruct((n, d), BF16), jax.ShapeDtypeStruct((n, LANES), F32)),
        grid=(n // tm,),
        in_specs=[pl.BlockSpec((tm, d), lambda i: (i, 0)), pl.BlockSpec((1, d), lambda i: (0, 0)),
                  pl.BlockSpec((d, LANES), lambda i: (0, 0)), pl.BlockSpec((1, LANES), lambda i: (0, 0))],
        out_specs=(pl.BlockSpec((tm, d), lambda i: (i, 0)), pl.BlockSpec((tm, LANES), lambda i: (i, 0))),
        compiler_params=_params(1), name="moe_router",
    )(x, g.reshape(1, d), wr, br)
    return hn, route[:, 0:2].astype(jnp.int32), route[:, 2:4]


def _moe_ffn_body(blk_e_ref, x_ref, rw_ref, wg_ref, wu_ref, wd_ref, o_ref, wg_s, wu_s, wd_s):
    i = pl.program_id(0)
    prev = blk_e_ref[jnp.maximum(i - 1, 0)]

    @pl.when((i == 0) | (blk_e_ref[i] != prev))
    def _():
        wg_s[...] = wg_ref[0].astype(BF16)
        wu_s[...] = wu_ref[0].astype(BF16)
        wd_s[...] = wd_ref[0].astype(BF16)

    x = x_ref[...]
    hg = jnp.dot(x, wg_s[...], preferred_element_type=F32)
    hu = jnp.dot(x, wu_s[...], preferred_element_type=F32)
    hb = (_silu(hg) * hu).astype(BF16)
    o_ref[...] = jnp.dot(hb, wd_s[...], preferred_element_type=F32) * rw_ref[...]


def moe_ffn(x_rows, row_w, blk_e, layer, w_g, w_u, w_d):
    r, d = x_rows.shape
    ff = w_g.shape[3]
    n_blk = r // MOE_TILE
    grid_spec = pltpu.PrefetchScalarGridSpec(
        num_scalar_prefetch=1, grid=(n_blk,),
        in_specs=[pl.BlockSpec((MOE_TILE, d), lambda i, e: (i, 0)),
                  pl.BlockSpec((MOE_TILE, 1), lambda i, e: (i, 0)),
                  pl.BlockSpec((None, 1, d, ff), lambda i, e: (layer, e[i], 0, 0)),
                  pl.BlockSpec((None, 1, d, ff), lambda i, e: (layer, e[i], 0, 0)),
                  pl.BlockSpec((None, 1, ff, d), lambda i, e: (layer, e[i], 0, 0))],
        out_specs=pl.BlockSpec((MOE_TILE, d), lambda i, e: (i, 0)),
        scratch_shapes=[pltpu.VMEM((d, ff), BF16), pltpu.VMEM((d, ff), BF16), pltpu.VMEM((ff, d), BF16)])
    return pl.pallas_call(
        _moe_ffn_body, out_shape=jax.ShapeDtypeStruct((r, d), F32), grid_spec=grid_spec,
        compiler_params=_params(1), name="moe_ffn",
    )(blk_e, x_rows, row_w.reshape(r, 1), w_g, w_u, w_d)


def hier_moe_residual(x, g, w_group, b_group, w_expert, b_expert, layer, w_g, w_u, w_d):
    n, d = x.shape
    hn, expert, gate = moe_router(x, g, w_group, b_group, w_expert, b_expert)
    n_exp = w_g.shape[1]
    kk = expert.shape[1]
    a = n * kk
    e_flat = expert.reshape(-1)
    counts = jnp.sum((e_flat[:, None] == jnp.arange(n_exp, dtype=jnp.int32)[None, :]).astype(jnp.int32), axis=0)
    starts = jnp.cumsum(counts) - counts
    padded = (counts + MOE_TILE - 1) // MOE_TILE * MOE_TILE
    pend = jnp.cumsum(padded)
    pstart = pend - padded
    order = jnp.argsort(e_flat).astype(jnp.int32)
    rank = jnp.argsort(order).astype(jnp.int32)
    dest = (pstart[e_flat] + rank - starts[e_flat]).reshape(n, kk)
    r = -(-(a + n_exp * (MOE_TILE - 1)) // MOE_TILE) * MOE_TILE
    n_blk = r // MOE_TILE
    tile_start = jnp.arange(n_blk, dtype=jnp.int32) * MOE_TILE
    blk_e = jnp.minimum(jnp.sum((pend[None, :] <= tile_start[:, None]).astype(jnp.int32), axis=1), n_exp - 1)
    row_e = jnp.repeat(blk_e, MOE_TILE)
    row_off = jnp.arange(r, dtype=jnp.int32) - pstart[row_e]
    row_real = row_off < counts[row_e]
    row_src = order[jnp.clip(starts[row_e] + row_off, 0, a - 1)]
    row_tok = jnp.where(row_real, row_src // kk, 0)
    row_w = jnp.where(row_real, gate.reshape(-1)[row_src], 0.0)
    y_rows = moe_ffn(hn[row_tok], row_w, blk_e, layer, w_g, w_u, w_d)
    return x + jnp.sum(y_rows[dest], axis=1)


KV_W = 2 * NSA_KV_HEADS * NSA_HD
Q_W = NSA_HEADS * NSA_HD
MASK_BUCKET = REL_BUCKETS
KEY_TILE = 512


def _t5_thresholds():
    exact = REL_BUCKETS // 2
    dist = np.arange(0, 4 * REL_MAX_DIST, dtype=np.int64)
    nf = np.maximum(dist, 1).astype(np.float64)
    large = exact + np.floor(np.log(nf / exact) / math.log(REL_MAX_DIST / exact) * (REL_BUCKETS - exact) + 1e-9)
    bucket = np.where(dist < exact, dist, np.minimum(large, REL_BUCKETS - 1)).astype(np.int64)
    return [int(np.argmax(bucket >= b)) for b in range(1, REL_BUCKETS)]


def _bucket_index(dist, valid):
    idx = jnp.zeros(dist.shape, jnp.int32)
    for thr in _t5_thresholds():
        idx = idx + (dist >= thr).astype(jnp.int32)
    return jnp.where(valid, idx, MASK_BUCKET)


def _bias_table(rel_bias):
    t = jnp.zeros((NSA_HEADS, LANES), F32).at[:, :REL_BUCKETS].set(rel_bias.T.astype(F32))
    return t.at[:, MASK_BUCKET].set(NEG)


def _group_mean_matrix():
    i = np.arange(LANES)
    return jnp.asarray((i[:, None] // NSA_HD == i[None, :] // NSA_HD) / NSA_HD, F32)


def _nsa_prep_body(zq_ref, zc_ref, zs_ref, zw_ref, zg_ref, bd_ref, qn_ref, kns_ref, knw_ref,
                   q_ref, cmp_ref, sel_ref, win_ref, selb_ref, winb_ref, gate_ref):
    bd = bd_ref[...]

    def head_norm(x, gain):
        ms = _dot_f32(x * x, bd)
        return x * lax.rsqrt(ms + EPS) * gain

    for c in range(Q_W // LANES):
        sl = slice(c * LANES, (c + 1) * LANES)
        q_ref[:, sl] = (head_norm(zq_ref[:, sl], qn_ref[...]) * (NSA_HD ** -0.5)).astype(BF16)
    cmp_ref[...] = zc_ref[...]
    half = KV_W // 2
    for z_ref, kn_ref, o_ref, ob_ref in ((zs_ref, kns_ref, sel_ref, selb_ref), (zw_ref, knw_ref, win_ref, winb_ref)):
        for c in range(half // LANES):
            sl = slice(c * LANES, (c + 1) * LANES)
            kn = head_norm(z_ref[:, sl], kn_ref[...])
            o_ref[:, sl] = kn
            ob_ref[:, sl] = kn.astype(BF16)
        v = z_ref[:, half:KV_W]
        o_ref[:, half:KV_W] = v
        ob_ref[:, half:KV_W] = v.astype(BF16)
    gate_ref[...] = _sigmoid(zg_ref[...])


def nsa_prep(z, q_norm, k_norm, row_tile=256):
    n = z.shape[0]
    tm = min(row_tile, n)

    def zspec(width, blk):
        return pl.BlockSpec((tm, width), lambda i, blk=blk: (i, blk))

    def const(shape):
        return pl.BlockSpec(shape, lambda i: (0,) * len(shape))

    def tile2(v):
        return jnp.concatenate([v, v]).reshape(1, LANES).astype(F32)

    def rows(w, dt):
        return jax.ShapeDtypeStruct((n, w), dt)

    def out_spec(w):
        return pl.BlockSpec((tm, w), lambda i: (i, 0))

    return pl.pallas_call(
        _nsa_prep_body,
        out_shape=(rows(Q_W, BF16), rows(KV_W, F32), rows(KV_W, F32), rows(KV_W, F32), rows(KV_W, BF16),
                   rows(KV_W, BF16), rows(LANES, F32)),
        grid=(n // tm,),
        in_specs=[zspec(Q_W, 0), zspec(KV_W, 2), zspec(KV_W, 3), zspec(KV_W, 4),
                  zspec(LANES, (Q_W + 3 * KV_W) // LANES), const((LANES, LANES)),
                  const((1, LANES)), const((1, LANES)), const((1, LANES))],
        out_specs=(out_spec(Q_W), out_spec(KV_W), out_spec(KV_W), out_spec(KV_W), out_spec(KV_W), out_spec(KV_W),
                   out_spec(LANES)),
        compiler_params=_params(1), name="nsa_prep",
    )(z, z, z, z, z, _group_mean_matrix(), tile2(q_norm), tile2(k_norm[1]), tile2(k_norm[2]))


PAGES_PER_STEP = 8
SUBS_PER_PAGE = 8
P_W = 2 * NSA_KV_HEADS * 2 * CMP_HID


def _gelu_tanh(x):
    return 0.5 * x * (1.0 + jnp.tanh(math.sqrt(2.0 / math.pi) * (x + 0.044715 * x * x * x)))


def _compress_body(pt_ref, *refs, n_steps):
    page_refs = refs[:PAGES_PER_STEP]
    w1_ref, b1_ref, w2_ref, b2_ref, kn_ref, kc_ref, vc_ref, p_ref = refs[PAGES_PER_STEP:]
    j = pl.program_id(1)
    rows = PAGES_PER_STEP * SUBS_PER_PAGE
    r0 = pl.multiple_of(j * rows, rows)
    for v in range(2):
        for g in range(NSA_KV_HEADS):
            acc = jnp.zeros((rows, 2 * CMP_HID), F32)
            for s in range(CMP_STRIDE):
                x = jnp.concatenate([pr[0, pl.ds(s, SUBS_PER_PAGE, stride=CMP_STRIDE), v, g, :] for pr in page_refs],
                                    axis=0)
                acc = acc + jnp.dot(x.astype(BF16), w1_ref[v, s], preferred_element_type=F32)
            p0 = (v * NSA_KV_HEADS + g) * 2 * CMP_HID
            p_ref[pl.ds(r0, rows), p0:p0 + 2 * CMP_HID] = acc

    @pl.when(j == n_steps - 1)
    def _():
        n_sub = p_ref.shape[0]
        for v in range(2):
            for g in range(NSA_KV_HEADS):
                p0 = (v * NSA_KV_HEADS + g) * 2 * CMP_HID
                hs = p_ref[:, p0:p0 + CMP_HID] + pltpu.roll(p_ref[:, p0 + CMP_HID:p0 + 2 * CMP_HID], n_sub - 1, 0)
                hid = _gelu_tanh(hs + b1_ref[v:v + 1, :])
                out = _dot(hid, w2_ref[v]) + b2_ref[v:v + 1, :]
                if v == 0:
                    out = out * lax.rsqrt(jnp.mean(out * out, axis=-1, keepdims=True) + EPS) * kn_ref[...]
                    kc_ref[0, :, g * NSA_HD:(g + 1) * NSA_HD] = out
                else:
                    vc_ref[0, :, g * NSA_HD:(g + 1) * NSA_HD] = out


def nsa_compress(pages, page_table, w1, b1, w2, b2, k_norm0):
    batch, n_pp = page_table.shape
    n_steps = n_pp // PAGES_PER_STEP
    n_sub = n_pp * SUBS_PER_PAGE
    page_shape = pages.shape[1:]
    w = w1.reshape(2, 2, CMP_STRIDE, NSA_HD, CMP_HID)
    w = jnp.transpose(w, (0, 2, 3, 1, 4)).reshape(2, CMP_STRIDE, NSA_HD, 2 * CMP_HID).astype(BF16)

    def page_spec(t):
        return pl.BlockSpec((1,) + page_shape,
                            lambda b, j, pt, t=t: (pt[b, j * PAGES_PER_STEP + t],) + (0,) * len(page_shape))

    def const(shape):
        return pl.BlockSpec(shape, lambda b, j, pt: (0,) * len(shape))

    grid_spec = pltpu.PrefetchScalarGridSpec(
        num_scalar_prefetch=1, grid=(batch, n_steps),
        in_specs=[page_spec(t) for t in range(PAGES_PER_STEP)] + [
            const((2, CMP_STRIDE, NSA_HD, 2 * CMP_HID)), const((2, CMP_HID)), const((2, CMP_HID, NSA_HD)),
            const((2, NSA_HD)), const((1, NSA_HD))],
        out_specs=(pl.BlockSpec((1, n_sub, KV_W // 2), lambda b, j, pt: (b, 0, 0)),
                   pl.BlockSpec((1, n_sub, KV_W // 2), lambda b, j, pt: (b, 0, 0))),
        scratch_shapes=[pltpu.VMEM((n_sub, P_W), F32)])
    return pl.pallas_call(
        functools.partial(_compress_body, n_steps=n_steps),
        out_shape=(jax.ShapeDtypeStruct((batch, n_sub, KV_W // 2), F32),
                   jax.ShapeDtypeStruct((batch, n_sub, KV_W // 2), F32)),
        grid_spec=grid_spec, compiler_params=_params(2), name="nsa_compress",
    )(page_table, *([pages] * PAGES_PER_STEP), w, b1, w2.astype(BF16), b2, k_norm0.reshape(1, NSA_HD))


Q_TILE = 128
WIN_BLOCKS = (WINDOW + Q_TILE) // Q_TILE
NEAR_TILES = 3
M_INIT = -1e29


def _gather_bias(tbh, idx_slices):
    return jnp.concatenate([jnp.take_along_axis(tbh, idx, axis=1) for idx in idx_slices], axis=1)


def _nsa_attend_body(q_ref, gate_ref, x_ref, kc_ref, vc_ref, kst_ref, vs_ref, w0_ref, w1_ref, w2_ref, w3_ref, w4_ref,
                     idxc_ref, idxw_ref, nb_ref, mmat_ref, tb_ref, wout_ref, y_ref,
                     pc_s, oc_s, sel_s, m_s, l_s, a_s, acc_s, ps_s, pw_s, o_s, *, n_sel):
    win_refs = (w0_ref, w1_ref, w2_ref, w3_ref, w4_ref)
    t = pl.program_id(1)
    s0 = t * Q_TILE
    n_sub = kc_ref.shape[1]
    rep_rows = NSA_REP * Q_TILE
    q_pos = s0 + lax.broadcasted_iota(jnp.int32, (Q_TILE, LANES), 0)
    blk = lax.broadcasted_iota(jnp.int32, (Q_TILE, LANES), 1)
    cur = q_pos // SEL_BLOCK
    forced = (blk == 0) | (blk == cur) | (blk == cur - 1)
    future = blk * SEL_BLOCK > q_pos

    def group_q(g):
        return jnp.concatenate([q_ref[:, (g * NSA_REP + r) * NSA_HD:(g * NSA_REP + r + 1) * NSA_HD]
                                for r in range(NSA_REP)], axis=0)

    def head_table(h):
        return jnp.broadcast_to(tb_ref[h:h + 1, :], (Q_TILE, LANES))

    score_t = []
    for g in range(NSA_KV_HEADS):
        gs = slice(g * NSA_HD, (g + 1) * NSA_HD)
        sc = _dot_nt(group_q(g), kc_ref[0, :, gs])
        imp = jnp.zeros((Q_TILE, n_sub), F32)
        for r in range(NSA_REP):
            rs = slice(r * Q_TILE, (r + 1) * Q_TILE)
            bias = _gather_bias(head_table(g * NSA_REP + r),
                                [idxc_ref[0, :, c * LANES:(c + 1) * LANES] for c in range(n_sub // LANES)])
            s_r = sc[rs] + bias
            m = jnp.maximum(jnp.max(s_r, axis=1, keepdims=True), M_INIT)
            e = jnp.exp(s_r - m)
            p = e / jnp.maximum(jnp.sum(e, axis=1, keepdims=True), 1e-30)
            imp = imp + p
            pc_s[rs, :] = p.astype(BF16)
        oc_s[g] = jnp.dot(pc_s[...], vc_ref[0, :, gs], preferred_element_type=F32)
        score = _dot_f32(imp, mmat_ref[...])
        score = jnp.where(forced, FORCE_SCORE, score)
        score = jnp.where(future, NEG, score)
        score = jnp.where(blk >= n_sel, -jnp.inf, score)
        score_t.append(score.T)

    blk_t = lax.broadcasted_iota(jnp.int32, (LANES, Q_TILE), 0).astype(F32)
    sel_t = [jnp.zeros((LANES, Q_TILE), F32) for _ in range(NSA_KV_HEADS)]
    for _ in range(min(SEL_TOPK, n_sel)):
        for g in range(NSA_KV_HEADS):
            mx = jnp.max(score_t[g], axis=0, keepdims=True)
            first = jnp.min(jnp.where(score_t[g] == mx, blk_t, float(LANES)), axis=0, keepdims=True)
            pick = blk_t == first
            sel_t[g] = jnp.where(pick, 1.0, sel_t[g])
            score_t[g] = jnp.where(pick, -jnp.inf, score_t[g])
    for g in range(NSA_KV_HEADS):
        sel_s[g] = sel_t[g].T.astype(BF16)

    n_kt = (s0 + Q_TILE + KEY_TILE - 1) // KEY_TILE
    n_far = jnp.maximum(n_kt - NEAR_TILES, 0)
    blk_row = lax.broadcasted_iota(jnp.int32, (LANES, KEY_TILE), 0)
    key_col = lax.broadcasted_iota(jnp.int32, (LANES, KEY_TILE), 1)

    for g in range(NSA_KV_HEADS):
        gs = slice(g * NSA_HD, (g + 1) * NSA_HD)
        qg = group_q(g)

        m_s[...] = jnp.full((rep_rows, 1), M_INIT, F32)
        l_s[...] = jnp.zeros((rep_rows, 1), F32)
        acc_s[...] = jnp.zeros((rep_rows, NSA_HD), F32)

        def key_tile(kt, near, g=g, gs=gs, qg=qg):
            k0 = pl.multiple_of(kt * KEY_TILE, KEY_TILE)
            s = jnp.dot(qg, kst_ref[0, gs, pl.ds(k0, KEY_TILE)], preferred_element_type=F32)
            expand = jnp.where((k0 + key_col) // SEL_BLOCK == blk_row, 1.0, 0.0).astype(BF16)
            chosen = jnp.dot(sel_s[g], expand, preferred_element_type=F32)
            negm = (chosen - 1.0) * (-NEG)
            for r in range(NSA_REP):
                rs = slice(r * Q_TILE, (r + 1) * Q_TILE)
                h = g * NSA_REP + r
                if near:
                    d_idx = (s0 - k0) // Q_TILE
                    bias = _gather_bias(head_table(h), [nb_ref[d_idx, :, c * LANES:(c + 1) * LANES]
                                                        for c in range(KEY_TILE // LANES)])
                else:
                    bias = tb_ref[h:h + 1, REL_BUCKETS - 1:REL_BUCKETS]
                s_r = s[rs] + (negm + bias)
                m_old = m_s[rs]
                m_new = jnp.maximum(m_old, jnp.max(s_r, axis=1, keepdims=True))
                alpha = jnp.exp(m_old - m_new)
                p = jnp.exp(s_r - m_new)
                l_s[rs] = alpha * l_s[rs] + jnp.sum(p, axis=1, keepdims=True)
                m_s[rs] = m_new
                a_s[rs] = alpha
                ps_s[rs, :] = p.astype(BF16)
            acc_s[...] = a_s[...] * acc_s[...] + jnp.dot(ps_s[...], vs_ref[pl.ds(k0, KEY_TILE), gs],
                                                         preferred_element_type=F32)

        def far_body(kt, carry):
            key_tile(kt, False)
            return carry

        def near_body(kt, carry):
            key_tile(kt, True)
            return carry

        lax.fori_loop(0, n_far, far_body, 0)
        lax.fori_loop(n_far, n_kt, near_body, 0)

        kw = jnp.concatenate([wr[0, :, gs] for wr in win_refs], axis=0)
        vw = jnp.concatenate([wr[0, :, KV_W // 2 + g * NSA_HD:KV_W // 2 + (g + 1) * NSA_HD] for wr in win_refs], axis=0)
        sw = _dot_nt(qg, kw)
        for r in range(NSA_REP):
            rs = slice(r * Q_TILE, (r + 1) * Q_TILE)
            bias = _gather_bias(head_table(g * NSA_REP + r),
                                [idxw_ref[:, c * LANES:(c + 1) * LANES] for c in range(WIN_BLOCKS)])
            s_r = sw[rs] + bias
            e = jnp.exp(s_r - jnp.max(s_r, axis=1, keepdims=True))
            pw_s[rs, :] = (e / jnp.sum(e, axis=1, keepdims=True)).astype(BF16)
        o_w = jnp.dot(pw_s[...], vw, preferred_element_type=F32)

        for r in range(NSA_REP):
            rs = slice(r * Q_TILE, (r + 1) * Q_TILE)
            h = g * NSA_REP + r
            o_h = (gate_ref[:, 3 * h:3 * h + 1] * oc_s[g, rs, :]
                   + gate_ref[:, 3 * h + 1:3 * h + 2] * (acc_s[rs, :] / l_s[rs])
                   + gate_ref[:, 3 * h + 2:3 * h + 3] * o_w[rs])
            o_s[:, h * NSA_HD:(h + 1) * NSA_HD] = o_h.astype(BF16)

    y_ref[...] = x_ref[...] + jnp.dot(o_s[...], wout_ref[...], preferred_element_type=F32)


def nsa_attend_prompt(q, gates, x, k_c, v_c, selb, winb, rel_bias, w_out, batch, seq):
    n_qt = seq // Q_TILE
    n_sub = k_c.shape[1]
    n_sel = seq // SEL_BLOCK
    assert CMP_LEN == 2 * CMP_STRIDE and SEL_BLOCK == 4 * CMP_STRIDE and SEL_TOPK <= n_sel <= LANES
    kst = jnp.swapaxes(selb[:, :KV_W // 2].reshape(batch, seq, KV_W // 2), 1, 2)
    win_pad = jnp.pad(winb.reshape(batch, seq, KV_W), ((0, 0), (WINDOW, 0), (0, 0)))
    iq = jnp.arange(Q_TILE, dtype=jnp.int32)
    dist_c = (jnp.arange(n_qt, dtype=jnp.int32)[:, None, None] * Q_TILE + iq[None, :, None]
              - (jnp.arange(n_sub, dtype=jnp.int32)[None, None, :] * CMP_STRIDE + CMP_LEN - 1))
    idx_c = _bucket_index(dist_c, dist_c >= 0)
    dist_w = iq[:, None] - jnp.arange(WINDOW + Q_TILE, dtype=jnp.int32)[None, :] + WINDOW
    idx_w = _bucket_index(dist_w, (dist_w >= 0) & (dist_w < WINDOW))
    n_near = NEAR_TILES * KEY_TILE // Q_TILE
    dist_n = (jnp.arange(n_near, dtype=jnp.int32)[:, None, None] * Q_TILE + iq[None, :, None]
              - jnp.arange(KEY_TILE, dtype=jnp.int32)[None, None, :])
    nb = _bucket_index(dist_n, dist_n >= 0)
    ci = np.arange(n_sub)[:, None]
    bj = np.arange(LANES)[None, :]
    mmat = ((ci // 4 == bj).astype(np.float32) + ((ci + 1) // 4 == bj).astype(np.float32)) * (ci < n_sub - 1)

    def rows(width):
        return pl.BlockSpec((Q_TILE, width), lambda b, t: (b * n_qt + t, 0))

    def per_b(shape):
        return pl.BlockSpec((1,) + shape, lambda b, t: (b,) + (0,) * len(shape))

    def const(shape):
        return pl.BlockSpec(shape, lambda b, t: (0,) * len(shape))

    in_specs = [rows(Q_W), rows(LANES), rows(D_MODEL), per_b((n_sub, KV_W // 2)), per_b((n_sub, KV_W // 2)),
                per_b((KV_W // 2, seq)), pl.BlockSpec((seq, KV_W // 2), lambda b, t: (b, 1))]
    in_specs += [pl.BlockSpec((1, Q_TILE, KV_W), lambda b, t, j=j: (b, t + j, 0)) for j in range(WIN_BLOCKS)]
    in_specs += [pl.BlockSpec((1, Q_TILE, n_sub), lambda b, t: (t, 0, 0)), const((Q_TILE, WINDOW + Q_TILE)),
                 const((n_near, Q_TILE, KEY_TILE)), const((n_sub, LANES)), const((NSA_HEADS, LANES)),
                 const((Q_W, D_MODEL))]
    rep_rows = NSA_REP * Q_TILE
    scratch = [pltpu.VMEM((rep_rows, n_sub), BF16), pltpu.VMEM((NSA_KV_HEADS, rep_rows, NSA_HD), F32),
               pltpu.VMEM((NSA_KV_HEADS, Q_TILE, LANES), BF16), pltpu.VMEM((rep_rows, 1), F32),
               pltpu.VMEM((rep_rows, 1), F32), pltpu.VMEM((rep_rows, 1), F32), pltpu.VMEM((rep_rows, NSA_HD), F32),
               pltpu.VMEM((rep_rows, KEY_TILE), BF16), pltpu.VMEM((rep_rows, WINDOW + Q_TILE), BF16),
               pltpu.VMEM((Q_TILE, Q_W), BF16)]
    return pl.pallas_call(
        functools.partial(_nsa_attend_body, n_sel=n_sel),
        out_shape=jax.ShapeDtypeStruct((batch * seq, D_MODEL), F32), grid=(batch, n_qt),
        in_specs=in_specs, out_specs=rows(D_MODEL), scratch_shapes=scratch,
        compiler_params=_params(2), name="nsa_attend_prompt",
    )(q, gates, x, k_c.astype(BF16), v_c.astype(BF16), kst, selb, *([win_pad] * WIN_BLOCKS),
      idx_c, idx_w, nb, jnp.asarray(mmat), _bias_table(rel_bias), w_out.astype(BF16))


SCORE_W = 384


def _nsa_sample_cmp_body(q_ref, kc_ref, vc_ref, idx_ref, mmat_ref, gsum_ref, tb_ref, oc_ref, top_ref, *,
                         n_sel, q_pos):
    n_sub = kc_ref.shape[1]
    q = q_ref[0]
    row_g = lax.broadcasted_iota(jnp.int32, (NSA_HEADS, 1), 0) // NSA_REP
    s = jnp.zeros((NSA_HEADS, n_sub), F32)
    for g in range(NSA_KV_HEADS):
        s = jnp.where(row_g == g, _dot_nt(q, kc_ref[0, :, g * NSA_HD:(g + 1) * NSA_HD]), s)
    tb = tb_ref[...]
    bias = jnp.concatenate([jnp.take_along_axis(
        tb, jnp.broadcast_to(idx_ref[:, c * LANES:(c + 1) * LANES], (NSA_HEADS, LANES)), axis=1)
        for c in range(n_sub // LANES)], axis=1)
    s = s + bias
    m = jnp.maximum(jnp.max(s, axis=1, keepdims=True), M_INIT)
    e = jnp.exp(s - m)
    p = e / jnp.maximum(jnp.sum(e, axis=1, keepdims=True), 1e-30)
    pb = p.astype(BF16)
    o = jnp.zeros((NSA_HEADS, NSA_HD), F32)
    for g in range(NSA_KV_HEADS):
        o = jnp.where(row_g == g, jnp.dot(pb, vc_ref[0, :, g * NSA_HD:(g + 1) * NSA_HD],
                                          preferred_element_type=F32), o)
    oc_ref[0] = o
    imp = _dot_f32(gsum_ref[...], p)
    score = _dot_f32(imp, mmat_ref[...])
    blk = lax.broadcasted_iota(jnp.int32, score.shape, 1)
    cur = q_pos // SEL_BLOCK
    score = jnp.where((blk == 0) | (blk == cur) | (blk == cur - 1), FORCE_SCORE, score)
    score = jnp.where(blk * SEL_BLOCK > q_pos, NEG, score)
    score = jnp.where(blk >= n_sel, -jnp.inf, score)
    blkf = blk.astype(F32)
    lane = lax.broadcasted_iota(jnp.int32, (SUBLANES, LANES), 1)
    top = jnp.zeros((SUBLANES, LANES), F32)
    for it in range(min(SEL_TOPK, n_sel)):
        mx = jnp.max(score, axis=1, keepdims=True)
        first = jnp.min(jnp.where(score == mx, blkf, float(SCORE_W)), axis=1, keepdims=True)
        top = jnp.where(lane == it, first, top)
        score = jnp.where(blkf == first, -jnp.inf, score)
    top_ref[0] = top.astype(jnp.int32)


def nsa_sample_cmp(q, k_c, v_c, rel_bias, past_len):
    batch = q.shape[0]
    n_sub = k_c.shape[1]
    n_sel = past_len // SEL_BLOCK + 1
    assert SEL_TOPK <= n_sel <= SCORE_W
    dist = past_len - (jnp.arange(n_sub, dtype=jnp.int32) * CMP_STRIDE + CMP_LEN - 1)
    idx = _bucket_index(dist, dist >= 0).reshape(1, n_sub)
    ci = np.arange(n_sub)[:, None]
    bj = np.arange(SCORE_W)[None, :]
    mmat = ((ci // 4 == bj).astype(np.float32) + ((ci + 1) // 4 == bj).astype(np.float32)) * (ci < n_sub - 1)
    gsum = (np.arange(SUBLANES)[:, None] == np.arange(NSA_HEADS)[None, :] // NSA_REP).astype(np.float32)

    def per_b(shape):
        return pl.BlockSpec((1,) + shape, lambda b: (b,) + (0,) * len(shape))

    def const(shape):
        return pl.BlockSpec(shape, lambda b: (0,) * len(shape))

    o_c, top = pl.pallas_call(
        functools.partial(_nsa_sample_cmp_body, n_sel=n_sel, q_pos=past_len),
        out_shape=(jax.ShapeDtypeStruct((batch, NSA_HEADS, NSA_HD), F32),
                   jax.ShapeDtypeStruct((batch, SUBLANES, LANES), jnp.int32)),
        grid=(batch,),
        in_specs=[per_b((NSA_HEADS, NSA_HD)), per_b((n_sub, KV_W // 2)), per_b((n_sub, KV_W // 2)),
                  const((1, n_sub)), const((n_sub, SCORE_W)), const((SUBLANES, NSA_HEADS)), const((NSA_HEADS, LANES))],
        out_specs=(per_b((NSA_HEADS, NSA_HD)), per_b((SUBLANES, LANES))),
        compiler_params=_params(1), name="nsa_sample_cmp",
    )(q, k_c.astype(BF16), v_c.astype(BF16), idx, jnp.asarray(mmat), jnp.asarray(gsum), _bias_table(rel_bias))
    return o_c, top[:, :NSA_KV_HEADS, :SEL_TOPK]


SEL_PER_STEP = 2


def _nsa_sample_attend_body(row_ref, sidx_ref, *refs, q_pos, n_past_blk, n_steps):
    n_blk_refs = NSA_KV_HEADS * SEL_PER_STEP
    blk_refs = refs[:n_blk_refs]
    (q_ref, knew_ref, win_ref, wnew_ref, gate_ref, oc_ref, idxw_ref, tb_ref, o_ref,
     s_s, v_s, ow_s) = refs[n_blk_refs:]
    b = pl.program_id(0)
    step = pl.program_id(1)
    q = q_ref[0]
    row_g = lax.broadcasted_iota(jnp.int32, (NSA_HEADS, 1), 0) // NSA_REP
    tb = tb_ref[...]
    half = KV_W // 2

    def by_group(fn):
        out = fn(0)
        for g in range(1, NSA_KV_HEADS):
            out = jnp.where(row_g == g, fn(g), out)
        return out

    @pl.when(step == 0)
    def _():
        s_w = by_group(lambda g: _dot_nt(q, win_ref[0, :, 0, g, :]))
        bias = jnp.concatenate([jnp.take_along_axis(
            tb, jnp.broadcast_to(idxw_ref[:, c * LANES:(c + 1) * LANES], (NSA_HEADS, LANES)), axis=1)
            for c in range(s_w.shape[1] // LANES)], axis=1)
        s_w = s_w + bias
        k_new = by_group(lambda g: jnp.broadcast_to(wnew_ref[0, :, g * NSA_HD:(g + 1) * NSA_HD], (NSA_HEADS, NSA_HD)))
        v_new = by_group(lambda g: jnp.broadcast_to(wnew_ref[0, :, half + g * NSA_HD:half + (g + 1) * NSA_HD],
                                                    (NSA_HEADS, NSA_HD)))
        s_n = jnp.sum(q.astype(F32) * _r16(k_new), axis=1, keepdims=True) + tb[:, 0:1]
        m = jnp.maximum(jnp.max(s_w, axis=1, keepdims=True), s_n)
        e_w = jnp.exp(s_w - m)
        e_n = jnp.exp(s_n - m)
        total = jnp.sum(e_w, axis=1, keepdims=True) + e_n
        pb = (e_w / total).astype(BF16)
        pv = by_group(lambda g: jnp.dot(pb, win_ref[0, :, 1, g, :].astype(BF16), preferred_element_type=F32))
        ow_s[...] = pv + _r16(e_n / total) * _r16(v_new)

    keys = SEL_PER_STEP * SEL_BLOCK
    key_row = lax.broadcasted_iota(jnp.int32, (SEL_BLOCK, NSA_HD), 0)
    lane = lax.broadcasted_iota(jnp.int32, (1, keys), 1)
    thresholds = _t5_thresholds()

    def tile_kv(g, kv):
        parts = []
        for j in range(SEL_PER_STEP):
            is_new = sidx_ref[b, g, step * SEL_PER_STEP + j] >= n_past_blk
            cached = blk_refs[g * SEL_PER_STEP + j][0, :, kv, g, :]
            fresh = jnp.where(key_row == 0, knew_ref[0, :, kv * half + g * NSA_HD:kv * half + (g + 1) * NSA_HD], 0.0)
            parts.append(jnp.where(is_new, fresh, cached))
        return jnp.concatenate(parts, axis=0).astype(BF16)

    def tile_bias(g):
        pos = jnp.zeros((1, keys), jnp.int32)
        for j in range(SEL_PER_STEP):
            blk = sidx_ref[b, g, step * SEL_PER_STEP + j]
            pos = jnp.where(lane // SEL_BLOCK == j, blk * SEL_BLOCK + lane % SEL_BLOCK, pos)
        dist = q_pos - pos
        idx = jnp.zeros((1, keys), jnp.int32)
        for thr in thresholds:
            idx = idx + (dist >= thr).astype(jnp.int32)
        idx = jnp.where(dist >= 0, idx, MASK_BUCKET)
        return jnp.take_along_axis(tb, jnp.broadcast_to(idx, (NSA_HEADS, keys)), axis=1)

    k0 = pl.multiple_of(step * keys, keys)
    s_s[:, pl.ds(k0, keys)] = by_group(lambda g: _dot_nt(q, tile_kv(g, 0)) + tile_bias(g))
    for g in range(NSA_KV_HEADS):
        v_s[g, pl.ds(k0, keys), :] = tile_kv(g, 1)

    @pl.when(step == n_steps - 1)
    def _():
        s = s_s[...]
        e = jnp.exp(s - jnp.max(s, axis=1, keepdims=True))
        pb = (e / jnp.sum(e, axis=1, keepdims=True)).astype(BF16)
        o_sel = by_group(lambda g: jnp.dot(pb, v_s[g], preferred_element_type=F32))
        gate = gate_ref[0]
        o_ref[0] = gate[:, 0:1] * oc_ref[0] + gate[:, 1:2] * o_sel + gate[:, 2:3] * ow_s[...]


def nsa_sample_attend(q, kv_sel_new, kv_win_new, gates, o_c, top, cache_sel, win_buf, page_table, rel_bias,
                      past_len):
    batch = q.shape[0]
    n_past_blk = past_len // SEL_BLOCK
    bpp = cache_sel.shape[1] // SEL_BLOCK
    n_steps = SEL_TOPK // SEL_PER_STEP
    kv_shape = cache_sel.shape[2:]
    blocks = cache_sel.reshape((cache_sel.shape[0] * bpp, SEL_BLOCK) + kv_shape)
    jp = jnp.minimum(top, n_past_blk - 1)
    phys = jnp.take_along_axis(page_table, (jp // bpp).reshape(batch, -1), axis=1).reshape(top.shape)
    blk_row = (phys * bpp + jp % bpp).astype(jnp.int32)
    wb = win_buf.shape[1]
    dist_w = past_len - (past_len - wb + jnp.arange(wb, dtype=jnp.int32))
    idx_w = _bucket_index(dist_w, (dist_w >= 0) & (dist_w < WINDOW)).reshape(1, wb)
    gate3 = gates[:, :3 * NSA_HEADS].reshape(batch, NSA_HEADS, 3)

    def blk_spec(g, j):
        return pl.BlockSpec((1, SEL_BLOCK) + kv_shape,
                            lambda b, s, rows, sidx, g=g, j=j: (rows[b, g, s * SEL_PER_STEP + j], 0, 0, 0, 0))

    def per_b(shape):
        return pl.BlockSpec((1,) + shape, lambda b, s, rows, sidx: (b,) + (0,) * len(shape))

    def const(shape):
        return pl.BlockSpec(shape, lambda b, s, rows, sidx: (0,) * len(shape))

    grid_spec = pltpu.PrefetchScalarGridSpec(
        num_scalar_prefetch=2, grid=(batch, n_steps),
        in_specs=[blk_spec(g, j) for g in range(NSA_KV_HEADS) for j in range(SEL_PER_STEP)] + [
            per_b((NSA_HEADS, NSA_HD)), per_b((1, KV_W)), per_b((wb,) + kv_shape), per_b((1, KV_W)),
            per_b((NSA_HEADS, 3)), per_b((NSA_HEADS, NSA_HD)), const((1, wb)), const((NSA_HEADS, LANES))],
        out_specs=per_b((NSA_HEADS, NSA_HD)),
        scratch_shapes=[pltpu.VMEM((NSA_HEADS, SEL_TOPK * SEL_BLOCK), F32),
                        pltpu.VMEM((NSA_KV_HEADS, SEL_TOPK * SEL_BLOCK, NSA_HD), BF16),
                        pltpu.VMEM((NSA_HEADS, NSA_HD), F32)])
    o = pl.pallas_call(
        functools.partial(_nsa_sample_attend_body, q_pos=past_len, n_past_blk=n_past_blk, n_steps=n_steps),
        out_shape=jax.ShapeDtypeStruct((batch, NSA_HEADS, NSA_HD), F32), grid_spec=grid_spec,
        compiler_params=_params(2), name="nsa_sample_attend",
    )(blk_row, top.astype(jnp.int32), *([blocks] * (NSA_KV_HEADS * SEL_PER_STEP)), q,
      kv_sel_new.reshape(batch, 1, KV_W), win_buf, kv_win_new.reshape(batch, 1, KV_W), gate3, o_c, idx_w,
      _bias_table(rel_bias))
    return o.reshape(batch, Q_W)


def _pad_cols(w, width):
    return jnp.pad(w, ((0, 0), (0, width - w.shape[1]))).astype(BF16)


def kernel(x_prompt, x_sample, state_ret, state_mlstm_C, state_mlstm_n, state_mlstm_m, state_conv, cache_nsa_cmp, cache_nsa_sel, state_nsa_win, page_table, rel_bias, norm_mix, norm_ffn, ab_w_in, ab_conv_w, ab_conv_b, ab_b_igate, ab_b_fgate, ab_gn_g, ab_gn_b, ab_hn_g, ab_w_out, nsa_w_in, nsa_q_norm, nsa_k_norm, nsa_cmp_w1, nsa_cmp_b1, nsa_cmp_w2, nsa_cmp_b2, nsa_w_out, moe_w_group, moe_b_group, moe_w_expert, moe_b_expert, moe_w_gate, moe_w_up, moe_w_down):
    bp, lp, d = x_prompt.shape
    bs, ls, _ = x_sample.shape
    page_size = cache_nsa_cmp.shape[2]
    past_len = page_table.shape[1] * page_size
    assert norm_mix.shape[0] == 2 and ls == 1 and d == D_MODEL and lp % KEY_TILE == 0
    xp = x_prompt.reshape(bp * lp, d)
    xs = x_sample.reshape(bs, d)

    def moe(x, layer):
        return hier_moe_residual(x, norm_ffn[layer], moe_w_group[layer], moe_b_group[layer], moe_w_expert[layer],
                                 moe_b_expert[layer], layer, moe_w_gate, moe_w_up, moe_w_down)

    w_in = _pad_cols(ab_w_in[0], AB_IN_PAD)
    ab = (ab_conv_w[0], ab_conv_b[0], ab_b_igate[0], ab_b_fgate[0], ab_gn_g[0], ab_gn_b[0], ab_hn_g[0])
    zp = norm_matmul(xp, norm_mix[0], w_in)
    xp, ret_p, mc_p, mn_p, mm_p, conv_p = ab_prompt(zp, xp, bp, lp, *ab, ab_w_out[0])
    zs = norm_matmul(xs, norm_mix[0], w_in)
    pos_s = past_len + jnp.arange(ls, dtype=jnp.int32)
    ys, ret_s, mc_s, mn_s, mm_s, conv_s = ab_sample(zs, pos_s, state_ret[0], state_mlstm_C[0], state_mlstm_n[0],
                                                    state_mlstm_m[0], state_conv[0], *ab)
    xs = matmul_residual(ys, ab_w_out[0].astype(BF16), xs)
    xp = moe(xp, 0)
    xs = moe(xs, 0)

    w_in = _pad_cols(nsa_w_in[0], NSA_IN_PAD)
    cmp_w = (nsa_cmp_w1[0], nsa_cmp_b1[0], nsa_cmp_w2[0], nsa_cmp_b2[0], nsa_k_norm[0, 0])
    kv_shape = (2, NSA_KV_HEADS, NSA_HD)
    zp = norm_matmul(xp, norm_mix[1], w_in)
    q, cmp_p, sel_p, win_p, selb, winb, gates = nsa_prep(zp, nsa_q_norm[0], nsa_k_norm[0])
    own_pages = jnp.arange(bp * lp // page_size, dtype=jnp.int32).reshape(bp, lp // page_size)
    cmp_p = cmp_p.reshape((1, bp, lp) + kv_shape)
    k_c, v_c = nsa_compress(cmp_p.reshape((bp * lp // page_size, page_size) + kv_shape), own_pages, *cmp_w)
    xp = nsa_attend_prompt(q, gates, xp, k_c, v_c, selb, winb, rel_bias, nsa_w_out[0], bp, lp)
    win_keep = min(WINDOW, lp)
    sel_p = sel_p.reshape((1, bp, lp) + kv_shape)
    win_p = win_p.reshape((bp, lp) + kv_shape)[None, :, lp - win_keep:]

    zs = norm_matmul(xs, norm_mix[1], w_in)
    q, cmp_s, sel_s, win_s, _, _, gates = nsa_prep(zs, nsa_q_norm[0], nsa_k_norm[0])
    k_c, v_c = nsa_compress(cache_nsa_cmp[0], page_table, *cmp_w)
    q3 = q.reshape(bs, NSA_HEADS, NSA_HD)
    o_c, top = nsa_sample_cmp(q3, k_c, v_c, rel_bias, past_len)
    win_buf = state_nsa_win[0]
    o = nsa_sample_attend(q3, sel_s, win_s, gates, o_c, top, cache_nsa_sel[0], win_buf, page_table, rel_bias,
                          past_len)
    xs = matmul_residual(o, nsa_w_out[0].astype(BF16), xs)
    win_s = jnp.concatenate([win_buf, win_s.reshape((bs, ls) + kv_shape)], axis=1)[None, :, ls:]
    cmp_s = cmp_s.reshape((1, bs, ls) + kv_shape)
    sel_s = sel_s.reshape((1, bs, ls) + kv_shape)
    xp = moe(xp, 1)
    xs = moe(xs, 1)

    return (xp.reshape(bp, lp, d), xs.reshape(bs, ls, d), ret_p[None], ret_s[None], mc_p[None], mc_s[None],
            mn_p[None], mn_s[None], mm_p[None], mm_s[None], conv_p[None], conv_s[None],
            cmp_p, cmp_s, sel_p, sel_s, win_p, win_s)
```

```python
import functools
import math

import jax
import jax.numpy as jnp
import numpy as np
from jax import lax
from jax.experimental import pallas as pl
from jax.experimental.pallas import tpu as pltpu

F32 = jnp.float32
BF16 = jnp.bfloat16
LANES = 128
SUBLANES = 8
VMEM_LIMIT = 56 * 1024 * 1024

D_MODEL = 1024
RET_HEADS = 4
ML_HEADS = 4
HEAD_D = 128
CONV_W = 4
CHUNK = 128
ROPE_BASE = 10000.0
AB_IN = 4104
AB_IN_PAD = 4224
NSA_HEADS = 16
NSA_KV_HEADS = 4
NSA_REP = 4
NSA_HD = 64
NSA_IN_PAD = 2688
CMP_LEN = 32
CMP_STRIDE = 16
CMP_HID = 128
SEL_BLOCK = 64
SEL_TOPK = 16
WINDOW = 512
REL_BUCKETS = 32
REL_MAX_DIST = 1024
FORCE_SCORE = 1e4
MOE_GROUPS = 4
MOE_EXP_PER_GROUP = 8
MOE_EXPERTS = 32
NEG = -1e30
EPS = 1e-6


def _params(n_grid):
    return pltpu.CompilerParams(dimension_semantics=("arbitrary",) * n_grid, vmem_limit_bytes=VMEM_LIMIT)


def _dot(a, b):
    return jnp.dot(a.astype(BF16), b.astype(BF16), preferred_element_type=F32)


def _dot_nt(a, b):
    return lax.dot_general(a.astype(BF16), b.astype(BF16), (((1,), (1,)), ((), ())), preferred_element_type=F32)


def _dot_tn(a, b):
    return lax.dot_general(a.astype(BF16), b.astype(BF16), (((0,), (0,)), ((), ())), preferred_element_type=F32)


def _dot_f32(a, b):
    return jnp.dot(a, b, preferred_element_type=F32, precision=lax.Precision.HIGHEST)


def _r16(x):
    return x.astype(BF16).astype(F32)


def _sigmoid(x):
    return 1.0 / (1.0 + jnp.exp(-x))


def _silu(x):
    return x * _sigmoid(x)


def _log_sigmoid(x):
    return -(jnp.maximum(-x, 0.0) + jnp.log1p(jnp.exp(-jnp.abs(x))))


def _norm_matmul_body(x_ref, g_ref, w_ref, o_ref, *, col_tile):
    x = x_ref[...]
    y = x * lax.rsqrt(jnp.mean(x * x, axis=-1, keepdims=True) + EPS) * g_ref[...]
    yb = y.astype(BF16)
    for c0 in range(0, o_ref.shape[1], col_tile):
        o_ref[:, c0:c0 + col_tile] = jnp.dot(yb, w_ref[:, c0:c0 + col_tile], preferred_element_type=F32)


def norm_matmul(x, g, w, row_tile=256):
    n, d = x.shape
    c = w.shape[1]
    tm = min(row_tile, n)
    col_tile = 384 if c % 384 == 0 else LANES
    return pl.pallas_call(
        functools.partial(_norm_matmul_body, col_tile=col_tile),
        out_shape=jax.ShapeDtypeStruct((n, c), F32),
        grid=(n // tm,),
        in_specs=[pl.BlockSpec((tm, d), lambda i: (i, 0)),
                  pl.BlockSpec((1, d), lambda i: (0, 0)),
                  pl.BlockSpec((d, c), lambda i: (0, 0))],
        out_specs=pl.BlockSpec((tm, c), lambda i: (i, 0)),
        compiler_params=_params(1),
        name="norm_matmul",
    )(x, g.reshape(1, d), w)


def _retention_constants(c):
    h = np.arange(RET_HEADS, dtype=np.float64)
    log_g = np.log1p(-np.exp2(-5.0 - h))
    i = np.arange(c, dtype=np.float64)
    diff = i[:, None] - i[None, :]
    decay = np.where(diff >= 0, np.exp(np.maximum(diff, 0.0)[None] * log_g[:, None, None]), 0.0)
    q_dec = np.exp((i + 1.0)[None, :] * log_g[:, None])[:, :, None]
    k_dec = np.exp((c - 1.0 - i)[None, :] * log_g[:, None])[:, :, None]
    s_dec = np.exp(c * log_g)
    return (jnp.asarray(decay, F32), jnp.asarray(q_dec, F32), jnp.asarray(k_dec, F32),
            [float(v) for v in s_dec])


def _rope_tables(pos):
    half = HEAD_D // 2
    freqs = ROPE_BASE ** (-jnp.arange(half, dtype=F32) / half)
    ang = pos.astype(F32)[:, None] * freqs[None, :]
    cos, sin = jnp.cos(ang), jnp.sin(ang)
    return jnp.concatenate([cos, cos], axis=-1), jnp.concatenate([-sin, sin], axis=-1)


def _rope(x, cosf, sinf):
    return x * cosf + pltpu.roll(x, HEAD_D // 2, 1) * sinf


def _ab_prompt_body(rq_ref, rk_ref, rv_ref, rg_ref, mqk_ref, mv_ref, mo_ref, gz_ref, x_ref, cos_ref, sin_ref,
                    decay_ref, qdec_ref, kdec_ref, convw_ref, convb_ref, gbias_ref, gng_ref, gnb_ref, hng_ref,
                    wout_ref,
                    y_ref, s_ref, c_ref, n_ref, m_ref, conv_ref,
                    cbuf_ref, ycat_ref, *, s_dec):
    c = pl.program_id(1)
    tail = CONV_W - 1

    @pl.when(c == 0)
    def _():
        s_ref[...] = jnp.zeros_like(s_ref)
        c_ref[...] = jnp.zeros_like(c_ref)
        n_ref[...] = jnp.zeros_like(n_ref)
        m_ref[...] = jnp.zeros_like(m_ref)
        cbuf_ref[0:SUBLANES, :] = jnp.zeros((SUBLANES, cbuf_ref.shape[1]), F32)

    cosf = cos_ref[...]
    sinf = sin_ref[...]
    row = lax.broadcasted_iota(jnp.int32, (CHUNK, CHUNK), 0)
    col = lax.broadcasted_iota(jnp.int32, (CHUNK, CHUNK), 1)
    eye = row == col
    tril = row >= col
    triu = row <= col

    cbuf_ref[SUBLANES:SUBLANES + CHUNK, :] = mqk_ref[...]
    conv = convb_ref[...]
    for w in range(CONV_W):
        conv = conv + (_r16(cbuf_ref[SUBLANES - tail + w:SUBLANES - tail + w + CHUNK, :])
                       * _r16(convw_ref[w:w + 1, :]))
    qk = _silu(conv)
    last = cbuf_ref[CHUNK + SUBLANES - tail:CHUNK + SUBLANES, :]
    cbuf_ref[SUBLANES - tail:SUBLANES, :] = last
    conv_ref[0] = last

    gz = gz_ref[...] + gbias_ref[...]
    for h in range(RET_HEADS):
        sl = slice(h * HEAD_D, (h + 1) * HEAD_D)
        q = _rope(rq_ref[:, sl], cosf, sinf)
        k = _rope(rk_ref[:, sl], cosf, sinf) * (HEAD_D ** -0.5)
        v = rv_ref[:, sl]
        a = _dot_nt(q, k) * decay_ref[h]
        s_old = s_ref[0, h]
        o = _dot(a, v) + qdec_ref[h] * _dot(q, s_old)
        s_ref[0, h] = s_dec[h] * s_old + _dot_tn(k * kdec_ref[h], v)
        mu = jnp.mean(o, axis=-1, keepdims=True)
        var = jnp.mean(jnp.square(o - mu), axis=-1, keepdims=True)
        o = (o - mu) * lax.rsqrt(var + EPS) * gng_ref[:, sl] + gnb_ref[:, sl]
        ycat_ref[:, sl] = _silu(rg_ref[:, sl]) * o

        mq = qk[:, sl]
        mk = qk[:, ML_HEADS * HEAD_D + h * HEAD_D:ML_HEADS * HEAD_D + (h + 1) * HEAD_D] * (HEAD_D ** -0.5)
        mv = mv_ref[:, sl]
        i_col = gz[:, h:h + 1]
        f_col = _log_sigmoid(gz[:, ML_HEADS + h:ML_HEADS + h + 1])
        i_row = jnp.sum(jnp.where(eye, i_col, 0.0), axis=0, keepdims=True)
        f_row = jnp.sum(jnp.where(eye, f_col, 0.0), axis=0, keepdims=True)
        b_col = jnp.sum(jnp.where(tril, f_row, 0.0), axis=1, keepdims=True)
        b_row = jnp.sum(jnp.where(triu, f_col, 0.0), axis=0, keepdims=True)
        m_old = m_ref[0, h:h + 1, 0:1]
        dlog = jnp.where(tril, b_col - b_row + i_row, -jnp.inf)
        inter = b_col + m_old
        m_t = jnp.maximum(inter, jnp.max(dlog, axis=1, keepdims=True))
        wgt = _dot_nt(mq, mk) * jnp.exp(dlog - m_t)
        e_inter = jnp.exp(inter - m_t)
        c_old = c_ref[0, h]
        n_old = n_ref[0, h:h + 1, :]
        num = _dot(wgt, mv) + e_inter * _dot_nt(mq, c_old)
        den = (jnp.sum(wgt, axis=1, keepdims=True)
               + e_inter * jnp.sum(_r16(mq) * _r16(n_old), axis=1, keepdims=True))
        hc = num / jnp.maximum(jnp.abs(den), jnp.exp(-m_t))
        b_last = b_col[CHUNK - 1:CHUNK, :]
        u_row = b_last - b_row + i_row
        u_col = b_last - b_col + i_col
        m_new = jnp.maximum(b_last + m_old, jnp.max(u_row, axis=1, keepdims=True))
        ws_col = jnp.exp(u_col - m_new)
        f_state = jnp.exp(b_last + m_old - m_new)
        c_ref[0, h] = f_state * c_old + _dot_tn(mv * ws_col, mk)
        n_ref[0, h:h + 1, :] = f_state * n_old + jnp.sum(_r16(ws_col) * _r16(mk), axis=0, keepdims=True)
        m_ref[0, h:h + 1, :] = jnp.broadcast_to(m_new, (1, LANES))
        hm = _sigmoid(mo_ref[:, sl]) * hc
        hm = hm * lax.rsqrt(jnp.mean(hm * hm, axis=-1, keepdims=True) + EPS) * hng_ref[:, sl]
        ycat_ref[:, RET_HEADS * HEAD_D + h * HEAD_D:RET_HEADS * HEAD_D + (h + 1) * HEAD_D] = hm

    y_ref[...] = x_ref[...] + jnp.dot(ycat_ref[...].astype(BF16), wout_ref[...], preferred_element_type=F32)


def ab_prompt(z, x, batch, seq, conv_w, conv_b, b_ig, b_fg, gn_g, gn_b, hn_g, w_out):
    n_chunk = seq // CHUNK
    decay, q_dec, k_dec, s_dec = _retention_constants(CHUNK)
    cosf, sinf = _rope_tables(jnp.arange(seq, dtype=jnp.int32))
    gbias = jnp.zeros((1, LANES), F32).at[0, :ML_HEADS].set(b_ig).at[0, ML_HEADS:2 * ML_HEADS].set(b_fg)
    hw = RET_HEADS * HEAD_D
    qkw = 2 * ML_HEADS * HEAD_D

    def zspec(width, blk):
        return pl.BlockSpec((CHUNK, width), lambda b, c, blk=blk: (b * n_chunk + c, blk))

    def const(shape):
        return pl.BlockSpec(shape, lambda b, c: (0,) * len(shape))

    in_specs = [zspec(hw, 0), zspec(hw, 1), zspec(hw, 2), zspec(hw, 3), zspec(qkw, 2), zspec(hw, 6), zspec(hw, 7),
                zspec(LANES, (AB_IN_PAD - LANES) // LANES),
                pl.BlockSpec((CHUNK, D_MODEL), lambda b, c: (b * n_chunk + c, 0)),
                pl.BlockSpec((CHUNK, HEAD_D), lambda b, c: (c, 0)),
                pl.BlockSpec((CHUNK, HEAD_D), lambda b, c: (c, 0)),
                const((RET_HEADS, CHUNK, CHUNK)), const((RET_HEADS, CHUNK, 1)), const((RET_HEADS, CHUNK, 1)),
                const((CONV_W, qkw)), const((1, qkw)), const((1, LANES)),
                const((1, hw)), const((1, hw)), const((1, hw)), const((2 * hw, D_MODEL))]
    out_shape = (jax.ShapeDtypeStruct((batch * seq, D_MODEL), F32),
                 jax.ShapeDtypeStruct((batch, RET_HEADS, HEAD_D, HEAD_D), F32),
                 jax.ShapeDtypeStruct((batch, ML_HEADS, HEAD_D, HEAD_D), F32),
                 jax.ShapeDtypeStruct((batch, ML_HEADS, HEAD_D), F32),
                 jax.ShapeDtypeStruct((batch, SUBLANES, LANES), F32),
                 jax.ShapeDtypeStruct((batch, CONV_W - 1, qkw), F32))
    out_specs = (pl.BlockSpec((CHUNK, D_MODEL), lambda b, c: (b * n_chunk + c, 0)),
                 pl.BlockSpec((1, RET_HEADS, HEAD_D, HEAD_D), lambda b, c: (b, 0, 0, 0)),
                 pl.BlockSpec((1, ML_HEADS, HEAD_D, HEAD_D), lambda b, c: (b, 0, 0, 0)),
                 pl.BlockSpec((1, ML_HEADS, HEAD_D), lambda b, c: (b, 0, 0)),
                 pl.BlockSpec((1, SUBLANES, LANES), lambda b, c: (b, 0, 0)),
                 pl.BlockSpec((1, CONV_W - 1, qkw), lambda b, c: (b, 0, 0)))
    y, s, cc, n, m, conv = pl.pallas_call(
        functools.partial(_ab_prompt_body, s_dec=s_dec),
        out_shape=out_shape, grid=(batch, n_chunk), in_specs=in_specs, out_specs=out_specs,
        scratch_shapes=[pltpu.VMEM((CHUNK + SUBLANES, qkw), F32), pltpu.VMEM((CHUNK, 2 * hw), F32)],
        compiler_params=_params(2), name="ab_prompt",
    )(z, z, z, z, z, z, z, z, x, cosf, sinf, decay, q_dec, k_dec, conv_w, conv_b.reshape(1, qkw), gbias,
      gn_g.reshape(1, hw), gn_b.reshape(1, hw), hn_g.reshape(1, hw), w_out.astype(BF16))
    return y, s, cc, n, m[:, :ML_HEADS, 0], conv


def _matmul_residual_body(a_ref, w_ref, x_ref, o_ref):
    o_ref[...] = x_ref[...] + jnp.dot(a_ref[...].astype(BF16), w_ref[...], preferred_element_type=F32)


def matmul_residual(a, w, x, row_tile=256):
    n, kk = a.shape
    d = w.shape[1]
    tm = min(row_tile, n)
    return pl.pallas_call(
        _matmul_residual_body, out_shape=jax.ShapeDtypeStruct((n, d), F32), grid=(n // tm,),
        in_specs=[pl.BlockSpec((tm, kk), lambda i: (i, 0)), pl.BlockSpec((kk, d), lambda i: (0, 0)),
                  pl.BlockSpec((tm, d), lambda i: (i, 0))],
        out_specs=pl.BlockSpec((tm, d), lambda i: (i, 0)),
        compiler_params=_params(1), name="matmul_residual",
    )(a, w, x)


def _ab_sample_body(m0_ref, z_ref, cos_ref, sin_ref, s0_ref, c0_ref, n0_ref, conv0_ref,
                    convw_ref, convb_ref, gbias_ref, gng_ref, gnb_ref, hng_ref,
                    y_ref, s_ref, c_ref, n_ref, m_ref, conv_ref, *, g_dec):
    b = pl.program_id(0)
    hw = RET_HEADS * HEAD_D
    qkw = 2 * ML_HEADS * HEAD_D
    tail = CONV_W - 1
    cosf = cos_ref[...]
    sinf = sin_ref[...]
    row = lax.broadcasted_iota(jnp.int32, (HEAD_D, HEAD_D), 0)
    col = lax.broadcasted_iota(jnp.int32, (HEAD_D, HEAD_D), 1)
    eye = row == col

    def to_col(r):
        return jnp.sum(jnp.where(eye, r, 0.0), axis=1, keepdims=True)

    def to_row(cv):
        return jnp.sum(jnp.where(eye, cv, 0.0), axis=0, keepdims=True)

    mqk = z_ref[0, :, 4 * hw:4 * hw + qkw]
    conv = convb_ref[...] + mqk * convw_ref[tail:CONV_W, :]
    for w in range(tail):
        conv = conv + conv0_ref[0, w:w + 1, :] * convw_ref[w:w + 1, :]
    qk = _silu(conv)
    conv_ref[0, 0:tail - 1, :] = conv0_ref[0, 1:tail, :]
    conv_ref[0, tail - 1:tail, :] = mqk
    gz = z_ref[0, :, AB_IN_PAD - LANES:AB_IN_PAD] + gbias_ref[...]

    for h in range(RET_HEADS):
        sl = slice(h * HEAD_D, (h + 1) * HEAD_D)
        q = _rope(z_ref[0, :, sl], cosf, sinf)
        k = _rope(z_ref[0, :, hw + h * HEAD_D:hw + (h + 1) * HEAD_D], cosf, sinf) * (HEAD_D ** -0.5)
        v = z_ref[0, :, 2 * hw + h * HEAD_D:2 * hw + (h + 1) * HEAD_D]
        rg = z_ref[0, :, 3 * hw + h * HEAD_D:3 * hw + (h + 1) * HEAD_D]
        s_old = s0_ref[0, h]
        qk_s = jnp.sum(q * k, axis=1, keepdims=True)
        o = qk_s * v + g_dec[h] * jnp.sum(_r16(to_col(q)) * _r16(s_old), axis=0, keepdims=True)
        s_ref[0, h] = g_dec[h] * s_old + to_col(k) * v
        mu = jnp.mean(o, axis=-1, keepdims=True)
        var = jnp.mean(jnp.square(o - mu), axis=-1, keepdims=True)
        o = (o - mu) * lax.rsqrt(var + EPS) * gng_ref[:, sl] + gnb_ref[:, sl]
        y_ref[0, :, sl] = _silu(rg) * o

        mq = qk[:, sl]
        mk = qk[:, ML_HEADS * HEAD_D + h * HEAD_D:ML_HEADS * HEAD_D + (h + 1) * HEAD_D] * (HEAD_D ** -0.5)
        mv = z_ref[0, :, 4 * hw + qkw + h * HEAD_D:4 * hw + qkw + (h + 1) * HEAD_D]
        mo = z_ref[0, :, 5 * hw + qkw + h * HEAD_D:5 * hw + qkw + (h + 1) * HEAD_D]
        ig = gz[:, h:h + 1]
        lf = _log_sigmoid(gz[:, ML_HEADS + h:ML_HEADS + h + 1])
        m_old = m0_ref[b, h]
        inter = lf + m_old
        m_t = jnp.maximum(inter, ig)
        wgt = jnp.sum(mq * mk, axis=1, keepdims=True) * jnp.exp(ig - m_t)
        e_inter = jnp.exp(inter - m_t)
        c_old = c0_ref[0, h]
        n_old = n0_ref[0, h:h + 1, :]
        cq = to_row(jnp.sum(_r16(c_old) * _r16(mq), axis=1, keepdims=True))
        num = wgt * mv + e_inter * cq
        den = wgt + e_inter * jnp.sum(n_old * mq, axis=1, keepdims=True)
        hc = num / jnp.maximum(jnp.abs(den), jnp.exp(-m_t))
        ws = jnp.exp(ig - m_t)
        c_ref[0, h] = e_inter * c_old + (ws * to_col(mv)) * mk
        n_ref[0, h:h + 1, :] = e_inter * n_old + ws * mk
        m_ref[0, h:h + 1, :] = jnp.broadcast_to(m_t, (1, LANES))
        hm = _sigmoid(mo) * hc
        hm = hm * lax.rsqrt(jnp.mean(hm * hm, axis=-1, keepdims=True) + EPS) * hng_ref[:, sl]
        y_ref[0, :, hw + h * HEAD_D:hw + (h + 1) * HEAD_D] = hm
    m_ref[0, ML_HEADS:SUBLANES, :] = jnp.zeros((SUBLANES - ML_HEADS, LANES), F32)


def ab_sample(z, pos, s0, c0, n0, m0, conv0, conv_w, conv_b, b_ig, b_fg, gn_g, gn_b, hn_g):
    batch = z.shape[0]
    h = np.arange(RET_HEADS, dtype=np.float64)
    g_dec = [float(v) for v in np.exp(np.log1p(-np.exp2(-5.0 - h)))]
    cosf, sinf = _rope_tables(pos)
    gbias = jnp.zeros((1, LANES), F32).at[0, :ML_HEADS].set(b_ig).at[0, ML_HEADS:2 * ML_HEADS].set(b_fg)
    hw = RET_HEADS * HEAD_D
    qkw = 2 * ML_HEADS * HEAD_D

    def per_b(shape):
        return pl.BlockSpec((1,) + shape, lambda b: (b,) + (0,) * len(shape))

    def const(shape):
        return pl.BlockSpec(shape, lambda b: (0,) * len(shape))

    in_specs = [pl.BlockSpec(memory_space=pltpu.SMEM), per_b((1, AB_IN_PAD)), const((1, HEAD_D)), const((1, HEAD_D)),
                per_b((RET_HEADS, HEAD_D, HEAD_D)), per_b((ML_HEADS, HEAD_D, HEAD_D)), per_b((ML_HEADS, HEAD_D)),
                per_b((CONV_W - 1, qkw)), const((CONV_W, qkw)), const((1, qkw)), const((1, LANES)),
                const((1, hw)), const((1, hw)), const((1, hw))]
    out_shape = (jax.ShapeDtypeStruct((batch, 1, 2 * hw), F32),
                 jax.ShapeDtypeStruct((batch, RET_HEADS, HEAD_D, HEAD_D), F32),
                 jax.ShapeDtypeStruct((batch, ML_HEADS, HEAD_D, HEAD_D), F32),
                 jax.ShapeDtypeStruct((batch, ML_HEADS, HEAD_D), F32),
                 jax.ShapeDtypeStruct((batch, SUBLANES, LANES), F32),
                 jax.ShapeDtypeStruct((batch, CONV_W - 1, qkw), F32))
    out_specs = (per_b((1, 2 * hw)), per_b((RET_HEADS, HEAD_D, HEAD_D)), per_b((ML_HEADS, HEAD_D, HEAD_D)),
                 per_b((ML_HEADS, HEAD_D)), per_b((SUBLANES, LANES)), per_b((CONV_W - 1, qkw)))
    y, s, cc, n, m, conv = pl.pallas_call(
        functools.partial(_ab_sample_body, g_dec=g_dec),
        out_shape=out_shape, grid=(batch,), in_specs=in_specs, out_specs=out_specs,
        compiler_params=_params(1), name="ab_sample",
    )(m0, z.reshape(batch, 1, AB_IN_PAD), cosf, sinf, s0, c0, n0, conv0, conv_w, conv_b.reshape(1, qkw), gbias,
      gn_g.reshape(1, hw), gn_b.reshape(1, hw), hn_g.reshape(1, hw))
    return y.reshape(batch, 2 * hw), s, cc, n, m[:, :ML_HEADS, 0], conv


MOE_TILE = 256


def _moe_router_body(x_ref, g_ref, wr_ref, br_ref, hn_ref, route_ref):
    x = x_ref[...]
    hn = x * lax.rsqrt(jnp.mean(x * x, axis=-1, keepdims=True) + EPS) * g_ref[...]
    hn_ref[...] = hn.astype(BF16)
    z = _dot(hn, wr_ref[...]) + br_ref[...]
    lane = lax.broadcasted_iota(jnp.int32, z.shape, 1)
    lanef = lane.astype(F32)
    is_group = lane < MOE_GROUPS
    gl = jnp.where(is_group, z, -jnp.inf)
    gmax = jnp.max(gl, axis=1, keepdims=True)
    g_top = jnp.min(jnp.where(gl == gmax, lanef, float(LANES)), axis=1, keepdims=True)
    pg_top = 1.0 / jnp.sum(jnp.where(is_group, jnp.exp(z - gmax), 0.0), axis=1, keepdims=True)
    grp = ((lane - MOE_GROUPS) // MOE_EXP_PER_GROUP).astype(F32)
    in_group = (lane >= MOE_GROUPS) & (lane < MOE_GROUPS + MOE_EXPERTS) & (grp == g_top)
    el = jnp.where(in_group, z, -jnp.inf)
    v1 = jnp.max(el, axis=1, keepdims=True)
    i1 = jnp.min(jnp.where(el == v1, lanef, float(LANES)), axis=1, keepdims=True)
    el2 = jnp.where(lanef == i1, -jnp.inf, el)
    v2 = jnp.max(el2, axis=1, keepdims=True)
    i2 = jnp.min(jnp.where(el2 == v2, lanef, float(LANES)), axis=1, keepdims=True)
    t = jnp.exp(v2 - v1)
    p1 = 1.0 / (1.0 + t)
    out = jnp.where(lane == 0, i1 - MOE_GROUPS,
                    jnp.where(lane == 1, i2 - MOE_GROUPS,
                              jnp.where(lane == 2, pg_top * p1,
                                        jnp.where(lane == 3, pg_top * (t * p1), 0.0))))
    route_ref[...] = out


def moe_router(x, g, w_group, b_group, w_expert, b_expert, row_tile=256):
    n, d = x.shape
    tm = min(row_tile, n)
    used = MOE_GROUPS + MOE_EXPERTS
    wr = jnp.pad(jnp.concatenate([w_group, w_expert], axis=1), ((0, 0), (0, LANES - used)))
    br = jnp.pad(jnp.concatenate([b_group, b_expert]), (0, LANES - used)).reshape(1, LANES)
    hn, route = pl.pallas_call(
        _moe_router_body,
        out_shape=(jax.ShapeDtypeStruct((n, d), BF16), jax.ShapeDtypeStruct((n, LANES), F32)),
        grid=(n // tm,),
        in_specs=[pl.BlockSpec((tm, d), lambda i: (i, 0)), pl.BlockSpec((1, d), lambda i: (0, 0)),
                  pl.BlockSpec((d, LANES), lambda i: (0, 0)), pl.BlockSpec((1, LANES), lambda i: (0, 0))],
        out_specs=(pl.BlockSpec((tm, d), lambda i: (i, 0)), pl.BlockSpec((tm, LANES), lambda i: (i, 0))),
        compiler_params=_params(1), name="moe_router",
    )(x, g.reshape(1, d), wr, br)
    return hn, route[:, 0:2].astype(jnp.int32), route[:, 2:4]


def _moe_ffn_body(blk_e_ref, x_ref, rw_ref, wg_ref, wu_ref, wd_ref, o_ref, wg_s, wu_s, wd_s):
    i = pl.program_id(0)
    prev = blk_e_ref[jnp.maximum(i - 1, 0)]

    @pl.when((i == 0) | (blk_e_ref[i] != prev))
    def _():
        wg_s[...] = wg_ref[0].astype(BF16)
        wu_s[...] = wu_ref[0].astype(BF16)
        wd_s[...] = wd_ref[0].astype(BF16)

    x = x_ref[...]
    hg = jnp.dot(x, wg_s[...], preferred_element_type=F32)
    hu = jnp.dot(x, wu_s[...], preferred_element_type=F32)
    hb = (_silu(hg) * hu).astype(BF16)
    o_ref[...] = jnp.dot(hb, wd_s[...], preferred_element_type=F32) * rw_ref[...]


def moe_ffn(x_rows, row_w, blk_e, layer, w_g, w_u, w_d):
    r, d = x_rows.shape
    ff = w_g.shape[3]
    n_blk = r // MOE_TILE
    grid_spec = pltpu.PrefetchScalarGridSpec(
        num_scalar_prefetch=1, grid=(n_blk,),
        in_specs=[pl.BlockSpec((MOE_TILE, d), lambda i, e: (i, 0)),
                  pl.BlockSpec((MOE_TILE, 1), lambda i, e: (i, 0)),
                  pl.BlockSpec((None, 1, d, ff), lambda i, e: (layer, e[i], 0, 0)),
                  pl.BlockSpec((None, 1, d, ff), lambda i, e: (layer, e[i], 0, 0)),
                  pl.BlockSpec((None, 1, ff, d), lambda i, e: (layer, e[i], 0, 0))],
        out_specs=pl.BlockSpec((MOE_TILE, d), lambda i, e: (i, 0)),
        scratch_shapes=[pltpu.VMEM((d, ff), BF16), pltpu.VMEM((d, ff), BF16), pltpu.VMEM((ff, d), BF16)])
    return pl.pallas_call(
        _moe_ffn_body, out_shape=jax.ShapeDtypeStruct((r, d), F32), grid_spec=grid_spec,
        compiler_params=_params(1), name="moe_ffn",
    )(blk_e, x_rows, row_w.reshape(r, 1), w_g, w_u, w_d)


def hier_moe_residual(x, g, w_group, b_group, w_expert, b_expert, layer, w_g, w_u, w_d):
    n, d = x.shape
    hn, expert, gate = moe_router(x, g, w_group, b_group, w_expert, b_expert)
    n_exp = w_g.shape[1]
    kk = expert.shape[1]
    a = n * kk
    e_flat = expert.reshape(-1)
    counts = jnp.sum((e_flat[:, None] == jnp.arange(n_exp, dtype=jnp.int32)[None, :]).astype(jnp.int32), axis=0)
    starts = jnp.cumsum(counts) - counts
    padded = (counts + MOE_TILE - 1) // MOE_TILE * MOE_TILE
    pend = jnp.cumsum(padded)
    pstart = pend - padded
    order = jnp.argsort(e_flat).astype(jnp.int32)
    rank = jnp.argsort(order).astype(jnp.int32)
    dest = (pstart[e_flat] + rank - starts[e_flat]).reshape(n, kk)
    r = -(-(a + n_exp * (MOE_TILE - 1)) // MOE_TILE) * MOE_TILE
    n_blk = r // MOE_TILE
    tile_start = jnp.arange(n_blk, dtype=jnp.int32) * MOE_TILE
    blk_e = jnp.minimum(jnp.sum((pend[None, :] <= tile_start[:, None]).astype(jnp.int32), axis=1), n_exp - 1)
    row_e = jnp.repeat(blk_e, MOE_TILE)
    row_off = jnp.arange(r, dtype=jnp.int32) - pstart[row_e]
    row_real = row_off < counts[row_e]
    row_src = order[jnp.clip(starts[row_e] + row_off, 0, a - 1)]
    row_tok = jnp.where(row_real, row_src // kk, 0)
    row_w = jnp.where(row_real, gate.reshape(-1)[row_src], 0.0)
    y_rows = moe_ffn(hn[row_tok], row_w, blk_e, layer, w_g, w_u, w_d)
    return x + jnp.sum(y_rows[dest], axis=1)


KV_W = 2 * NSA_KV_HEADS * NSA_HD
Q_W = NSA_HEADS * NSA_HD
MASK_BUCKET = REL_BUCKETS
KEY_TILE = 512


def _t5_thresholds():
    exact = REL_BUCKETS // 2
    dist = np.arange(0, 4 * REL_MAX_DIST, dtype=np.int64)
    nf = np.maximum(dist, 1).astype(np.float64)
    large = exact + np.floor(np.log(nf / exact) / math.log(REL_MAX_DIST / exact) * (REL_BUCKETS - exact) + 1e-9)
    bucket = np.where(dist < exact, dist, np.minimum(large, REL_BUCKETS - 1)).astype(np.int64)
    return [int(np.argmax(bucket >= b)) for b in range(1, REL_BUCKETS)]


def _bucket_index(dist, valid):
    idx = jnp.zeros(dist.shape, jnp.int32)
    for thr in _t5_thresholds():
        idx = idx + (dist >= thr).astype(jnp.int32)
    return jnp.where(valid, idx, MASK_BUCKET)


def _bias_table(rel_bias):
    t = jnp.zeros((NSA_HEADS, LANES), F32).at[:, :REL_BUCKETS].set(rel_bias.T.astype(F32))
    return t.at[:, MASK_BUCKET].set(NEG)


def _group_mean_matrix():
    i = np.arange(LANES)
    return jnp.asarray((i[:, None] // NSA_HD == i[None, :] // NSA_HD) / NSA_HD, F32)


def _nsa_prep_body(zq_ref, zc_ref, zs_ref, zw_ref, zg_ref, bd_ref, qn_ref, kns_ref, knw_ref,
                   q_ref, cmp_ref, sel_ref, win_ref, selb_ref, winb_ref, gate_ref, *kt_ref, transposed):
    bd = bd_ref[...]

    def head_norm(x, gain):
        ms = _dot_f32(x * x, bd)
        return x * lax.rsqrt(ms + EPS) * gain

    def emit(o_ref, rows):
        if transposed:
            o_ref[0] = rows.T
        else:
            o_ref[...] = rows

    for c in range(Q_W // LANES):
        sl = slice(c * LANES, (c + 1) * LANES)
        q_ref[:, sl] = (head_norm(zq_ref[:, sl], qn_ref[...]) * (NSA_HD ** -0.5)).astype(BF16)
    emit(cmp_ref, zc_ref[...])
    half = KV_W // 2
    for z_ref, kn_ref, o_ref, ob_ref in ((zs_ref, kns_ref, sel_ref, selb_ref), (zw_ref, knw_ref, win_ref, winb_ref)):
        kn = jnp.concatenate([head_norm(z_ref[:, c * LANES:(c + 1) * LANES], kn_ref[...])
                              for c in range(half // LANES)], axis=1)
        rows = jnp.concatenate([kn, z_ref[:, half:KV_W]], axis=1)
        emit(o_ref, rows)
        ob_ref[...] = rows.astype(BF16)
        if transposed and o_ref is sel_ref:
            kt_ref[0][0] = kn.T.astype(BF16)
    gate_ref[...] = _sigmoid(zg_ref[...])


def nsa_prep(z, q_norm, k_norm, batch=None, row_tile=256):
    n = z.shape[0]
    tm = min(row_tile, n)
    transposed = batch is not None

    def zspec(width, blk):
        return pl.BlockSpec((tm, width), lambda i, blk=blk: (i, blk))

    def const(shape):
        return pl.BlockSpec(shape, lambda i: (0,) * len(shape))

    def tile2(v):
        return jnp.concatenate([v, v]).reshape(1, LANES).astype(F32)

    def rows(w, dt):
        return jax.ShapeDtypeStruct((n, w), dt)

    def out_spec(w):
        return pl.BlockSpec((tm, w), lambda i: (i, 0))

    if transposed:
        seq = n // batch
        tps = seq // tm
        kv_shape = jax.ShapeDtypeStruct((batch, KV_W, seq), F32)
        kv_spec = pl.BlockSpec((1, KV_W, tm), lambda i: (i // tps, 0, i % tps))
        extra_shape = (jax.ShapeDtypeStruct((batch, KV_W // 2, seq), BF16),)
        extra_spec = (pl.BlockSpec((1, KV_W // 2, tm), lambda i: (i // tps, 0, i % tps)),)
    else:
        kv_shape, kv_spec, extra_shape, extra_spec = rows(KV_W, F32), out_spec(KV_W), (), ()
    return pl.pallas_call(
        functools.partial(_nsa_prep_body, transposed=transposed),
        out_shape=(rows(Q_W, BF16), kv_shape, kv_shape, kv_shape, rows(KV_W, BF16), rows(KV_W, BF16),
                   rows(LANES, F32)) + extra_shape,
        grid=(n // tm,),
        in_specs=[zspec(Q_W, 0), zspec(KV_W, 2), zspec(KV_W, 3), zspec(KV_W, 4),
                  zspec(LANES, (Q_W + 3 * KV_W) // LANES), const((LANES, LANES)),
                  const((1, LANES)), const((1, LANES)), const((1, LANES))],
        out_specs=(out_spec(Q_W), kv_spec, kv_spec, kv_spec, out_spec(KV_W), out_spec(KV_W),
                   out_spec(LANES)) + extra_spec,
        compiler_params=_params(1), name="nsa_prep",
    )(z, z, z, z, z, _group_mean_matrix(), tile2(q_norm), tile2(k_norm[1]), tile2(k_norm[2]))


PAGES_PER_STEP = 16
PAGE_ROWS = 128
SUBS_PER_PAGE = PAGE_ROWS // CMP_STRIDE
P_W = 2 * NSA_KV_HEADS * 2 * CMP_HID
KV_PAIRS = NSA_KV_HEADS // 2


def _gelu_tanh(x):
    return 0.5 * x * (1.0 + jnp.tanh(math.sqrt(2.0 / math.pi) * (x + 0.044715 * x * x * x)))


def _compress_body(pt_ref, *refs, n_steps):
    page_refs = refs[:PAGES_PER_STEP]
    w1_ref, b1_ref, w2_ref, b2_ref, kn_ref, kc_ref, vc_ref, p_ref, x_s = refs[PAGES_PER_STEP:]
    j = pl.program_id(1)
    rows = PAGES_PER_STEP * SUBS_PER_PAGE
    r0 = pl.multiple_of(j * rows, rows)
    for t, pr in enumerate(page_refs):
        for v in range(2):
            for gp in range(KV_PAIRS):
                x_s[t, v * KV_PAIRS + gp] = pr[0, v, 2 * gp:2 * gp + 2].reshape(2 * NSA_HD, PAGE_ROWS).T
    for v in range(2):
        for gp in range(KV_PAIRS):
            acc = jnp.zeros((rows, 4 * CMP_HID), F32)
            for s in range(CMP_STRIDE):
                x = jnp.concatenate([x_s[t, v * KV_PAIRS + gp, pl.ds(s, SUBS_PER_PAGE, stride=CMP_STRIDE), :]
                                     for t in range(PAGES_PER_STEP)], axis=0)
                acc = acc + jnp.dot(x.astype(BF16), w1_ref[v, s], preferred_element_type=F32)
            p0 = (v * NSA_KV_HEADS + 2 * gp) * 2 * CMP_HID
            p_ref[pl.ds(r0, rows), p0:p0 + 4 * CMP_HID] = acc

    @pl.when(j == n_steps - 1)
    def _():
        n_sub = p_ref.shape[0]
        for v in range(2):
            for g in range(NSA_KV_HEADS):
                p0 = (v * NSA_KV_HEADS + g) * 2 * CMP_HID
                hs = p_ref[:, p0:p0 + CMP_HID] + pltpu.roll(p_ref[:, p0 + CMP_HID:p0 + 2 * CMP_HID], n_sub - 1, 0)
                hid = _gelu_tanh(hs + b1_ref[v:v + 1, :])
                out = _dot(hid, w2_ref[v]) + b2_ref[v:v + 1, :]
                if v == 0:
                    out = out * lax.rsqrt(jnp.mean(out * out, axis=-1, keepdims=True) + EPS) * kn_ref[...]
                    kc_ref[0, :, g * NSA_HD:(g + 1) * NSA_HD] = out
                else:
                    vc_ref[0, :, g * NSA_HD:(g + 1) * NSA_HD] = out


def nsa_compress(rows_t, page_table, w1, b1, w2, b2, k_norm0):
    if page_table is None:
        batch, n_pp = rows_t.shape[0], rows_t.shape[-1] // PAGE_ROWS
        page_table = jnp.zeros((1, 1), jnp.int32)

        def page_index(b, p, pt):
            return (b, 0, 0, 0, p)
    else:
        batch, n_pp = page_table.shape

        def page_index(b, p, pt):
            return (pt[b, p], 0, 0, 0, 0)
    n_steps = n_pp // PAGES_PER_STEP
    n_sub = n_pp * SUBS_PER_PAGE
    w = w1.reshape(2, 2, CMP_STRIDE, NSA_HD, CMP_HID)
    w = jnp.transpose(w, (0, 2, 3, 1, 4)).reshape(2, CMP_STRIDE, NSA_HD, 2 * CMP_HID)
    zero = jnp.zeros_like(w)
    wpair = jnp.concatenate([jnp.concatenate([w, zero], axis=-1), jnp.concatenate([zero, w], axis=-1)], axis=2)
    wpair = wpair.astype(BF16)

    def page_spec(t):
        return pl.BlockSpec((1, 2, NSA_KV_HEADS, NSA_HD, PAGE_ROWS),
                            lambda b, j, pt, t=t: page_index(b, j * PAGES_PER_STEP + t, pt))

    def const(shape):
        return pl.BlockSpec(shape, lambda b, j, pt: (0,) * len(shape))

    grid_spec = pltpu.PrefetchScalarGridSpec(
        num_scalar_prefetch=1, grid=(batch, n_steps),
        in_specs=[page_spec(t) for t in range(PAGES_PER_STEP)] + [
            const((2, CMP_STRIDE, LANES, 4 * CMP_HID)), const((2, CMP_HID)), const((2, CMP_HID, NSA_HD)),
            const((2, NSA_HD)), const((1, NSA_HD))],
        out_specs=(pl.BlockSpec((1, n_sub, KV_W // 2), lambda b, j, pt: (b, 0, 0)),
                   pl.BlockSpec((1, n_sub, KV_W // 2), lambda b, j, pt: (b, 0, 0))),
        scratch_shapes=[pltpu.VMEM((n_sub, P_W), F32),
                        pltpu.VMEM((PAGES_PER_STEP, 2 * KV_PAIRS, PAGE_ROWS, 2 * NSA_HD), F32)])
    return pl.pallas_call(
        functools.partial(_compress_body, n_steps=n_steps),
        out_shape=(jax.ShapeDtypeStruct((batch, n_sub, KV_W // 2), F32),
                   jax.ShapeDtypeStruct((batch, n_sub, KV_W // 2), F32)),
        grid_spec=grid_spec, compiler_params=_params(2), name="nsa_compress",
    )(page_table, *([rows_t] * PAGES_PER_STEP), wpair, b1, w2.astype(BF16), b2, k_norm0.reshape(1, NSA_HD))


Q_TILE = 128
WIN_BLOCKS = (WINDOW + Q_TILE) // Q_TILE
NEAR_TILES = 3
M_INIT = -1e29


def _gather_bias(tbh, idx_slices):
    return jnp.concatenate([jnp.take_along_axis(tbh, idx, axis=1) for idx in idx_slices], axis=1)


def _nsa_attend_body(q_ref, gate_ref, x_ref, kc_ref, vc_ref, kst_ref, vs_ref, w0_ref, w1_ref, w2_ref, w3_ref, w4_ref,
                     idxc_ref, idxw_ref, nb_ref, mmat_ref, tb_ref, wout_ref, y_ref,
                     pc_s, oc_s, sel_s, m_s, l_s, a_s, acc_s, ps_s, pw_s, o_s, *, n_sel):
    win_refs = (w0_ref, w1_ref, w2_ref, w3_ref, w4_ref)
    t = pl.program_id(1)
    s0 = t * Q_TILE
    n_sub = kc_ref.shape[1]
    rep_rows = NSA_REP * Q_TILE
    q_pos = s0 + lax.broadcasted_iota(jnp.int32, (Q_TILE, LANES), 0)
    blk = lax.broadcasted_iota(jnp.int32, (Q_TILE, LANES), 1)
    cur = q_pos // SEL_BLOCK
    forced = (blk == 0) | (blk == cur) | (blk == cur - 1)
    future = blk * SEL_BLOCK > q_pos

    def group_q(g):
        return jnp.concatenate([q_ref[:, (g * NSA_REP + r) * NSA_HD:(g * NSA_REP + r + 1) * NSA_HD]
                                for r in range(NSA_REP)], axis=0)

    def head_table(h):
        return jnp.broadcast_to(tb_ref[h:h + 1, :], (Q_TILE, LANES))

    score_t = []
    for g in range(NSA_KV_HEADS):
        gs = slice(g * NSA_HD, (g + 1) * NSA_HD)
        sc = _dot_nt(group_q(g), kc_ref[0, :, gs])
        imp = jnp.zeros((Q_TILE, n_sub), F32)
        for r in range(NSA_REP):
            rs = slice(r * Q_TILE, (r + 1) * Q_TILE)
            bias = _gather_bias(head_table(g * NSA_REP + r),
                                [idxc_ref[0, :, c * LANES:(c + 1) * LANES] for c in range(n_sub // LANES)])
            s_r = sc[rs] + bias
            m = jnp.maximum(jnp.max(s_r, axis=1, keepdims=True), M_INIT)
            e = jnp.exp(s_r - m)
            p = e / jnp.maximum(jnp.sum(e, axis=1, keepdims=True), 1e-30)
            imp = imp + p
            pc_s[rs, :] = p.astype(BF16)
        oc_s[g] = jnp.dot(pc_s[...], vc_ref[0, :, gs], preferred_element_type=F32)
        score = _dot_f32(imp, mmat_ref[...])
        score = jnp.where(forced, FORCE_SCORE, score)
        score = jnp.where(future, NEG, score)
        score = jnp.where(blk >= n_sel, -jnp.inf, score)
        score_t.append(score.T)

    blk_t = lax.broadcasted_iota(jnp.int32, (LANES, Q_TILE), 0).astype(F32)
    sel_t = [jnp.zeros((LANES, Q_TILE), F32) for _ in range(NSA_KV_HEADS)]
    for _ in range(min(SEL_TOPK, n_sel)):
        for g in range(NSA_KV_HEADS):
            mx = jnp.max(score_t[g], axis=0, keepdims=True)
            first = jnp.min(jnp.where(score_t[g] == mx, blk_t, float(LANES)), axis=0, keepdims=True)
            pick = blk_t == first
            sel_t[g] = jnp.where(pick, 1.0, sel_t[g])
            score_t[g] = jnp.where(pick, -jnp.inf, score_t[g])
    for g in range(NSA_KV_HEADS):
        sel_s[g] = sel_t[g].T.astype(BF16)

    n_kt = (s0 + Q_TILE + KEY_TILE - 1) // KEY_TILE
    n_far = jnp.maximum(n_kt - NEAR_TILES, 0)
    blk_row = lax.broadcasted_iota(jnp.int32, (LANES, KEY_TILE), 0)
    key_col = lax.broadcasted_iota(jnp.int32, (LANES, KEY_TILE), 1)

    for g in range(NSA_KV_HEADS):
        gs = slice(g * NSA_HD, (g + 1) * NSA_HD)
        qg = group_q(g)

        m_s[...] = jnp.full((rep_rows, 1), M_INIT, F32)
        l_s[...] = jnp.zeros((rep_rows, 1), F32)
        acc_s[...] = jnp.zeros((rep_rows, NSA_HD), F32)

        def key_tile(kt, near, g=g, gs=gs, qg=qg):
            k0 = pl.multiple_of(kt * KEY_TILE, KEY_TILE)
            s = jnp.dot(qg, kst_ref[0, gs, pl.ds(k0, KEY_TILE)], preferred_element_type=F32)
            expand = jnp.where((k0 + key_col) // SEL_BLOCK == blk_row, 1.0, 0.0).astype(BF16)
            chosen = jnp.dot(sel_s[g], expand, preferred_element_type=F32)
            negm = (chosen - 1.0) * (-NEG)
            for r in range(NSA_REP):
                rs = slice(r * Q_TILE, (r + 1) * Q_TILE)
                h = g * NSA_REP + r
                if near:
                    d_idx = (s0 - k0) // Q_TILE
                    bias = _gather_bias(head_table(h), [nb_ref[d_idx, :, c * LANES:(c + 1) * LANES]
                                                        for c in range(KEY_TILE // LANES)])
                else:
                    bias = tb_ref[h:h + 1, REL_BUCKETS - 1:REL_BUCKETS]
                s_r = s[rs] + (negm + bias)
                m_old = m_s[rs]
                m_new = jnp.maximum(m_old, jnp.max(s_r, axis=1, keepdims=True))
                alpha = jnp.exp(m_old - m_new)
                p = jnp.exp(s_r - m_new)
                l_s[rs] = alpha * l_s[rs] + jnp.sum(p, axis=1, keepdims=True)
                m_s[rs] = m_new
                a_s[rs] = alpha
                ps_s[rs, :] = p.astype(BF16)
            acc_s[...] = a_s[...] * acc_s[...] + jnp.dot(ps_s[...], vs_ref[pl.ds(k0, KEY_TILE), gs],
                                                         preferred_element_type=F32)

        def far_body(kt, carry):
            key_tile(kt, False)
            return carry

        def near_body(kt, carry):
            key_tile(kt, True)
            return carry

        lax.fori_loop(0, n_far, far_body, 0)
        lax.fori_loop(n_far, n_kt, near_body, 0)

        kw = jnp.concatenate([wr[0, :, gs] for wr in win_refs], axis=0)
        vw = jnp.concatenate([wr[0, :, KV_W // 2 + g * NSA_HD:KV_W // 2 + (g + 1) * NSA_HD] for wr in win_refs], axis=0)
        sw = _dot_nt(qg, kw)
        for r in range(NSA_REP):
            rs = slice(r * Q_TILE, (r + 1) * Q_TILE)
            bias = _gather_bias(head_table(g * NSA_REP + r),
                                [idxw_ref[:, c * LANES:(c + 1) * LANES] for c in range(WIN_BLOCKS)])
            s_r = sw[rs] + bias
            e = jnp.exp(s_r - jnp.max(s_r, axis=1, keepdims=True))
            pw_s[rs, :] = (e / jnp.sum(e, axis=1, keepdims=True)).astype(BF16)
        o_w = jnp.dot(pw_s[...], vw, preferred_element_type=F32)

        for r in range(NSA_REP):
            rs = slice(r * Q_TILE, (r + 1) * Q_TILE)
            h = g * NSA_REP + r
            o_h = (gate_ref[:, 3 * h:3 * h + 1] * oc_s[g, rs, :]
                   + gate_ref[:, 3 * h + 1:3 * h + 2] * (acc_s[rs, :] / l_s[rs])
                   + gate_ref[:, 3 * h + 2:3 * h + 3] * o_w[rs])
            o_s[:, h * NSA_HD:(h + 1) * NSA_HD] = o_h.astype(BF16)

    y_ref[...] = x_ref[...] + jnp.dot(o_s[...], wout_ref[...], preferred_element_type=F32)


def nsa_attend_prompt(q, gates, x, k_c, v_c, kst, selb, winb, rel_bias, w_out, batch, seq):
    n_qt = seq // Q_TILE
    n_sub = k_c.shape[1]
    n_sel = seq // SEL_BLOCK
    assert CMP_LEN == 2 * CMP_STRIDE and SEL_BLOCK == 4 * CMP_STRIDE and SEL_TOPK <= n_sel <= LANES
    win_pad = jnp.pad(winb.reshape(batch, seq, KV_W), ((0, 0), (WINDOW, 0), (0, 0)))
    iq = jnp.arange(Q_TILE, dtype=jnp.int32)
    dist_c = (jnp.arange(n_qt, dtype=jnp.int32)[:, None, None] * Q_TILE + iq[None, :, None]
              - (jnp.arange(n_sub, dtype=jnp.int32)[None, None, :] * CMP_STRIDE + CMP_LEN - 1))
    idx_c = _bucket_index(dist_c, dist_c >= 0)
    dist_w = iq[:, None] - jnp.arange(WINDOW + Q_TILE, dtype=jnp.int32)[None, :] + WINDOW
    idx_w = _bucket_index(dist_w, (dist_w >= 0) & (dist_w < WINDOW))
    n_near = NEAR_TILES * KEY_TILE // Q_TILE
    dist_n = (jnp.arange(n_near, dtype=jnp.int32)[:, None, None] * Q_TILE + iq[None, :, None]
              - jnp.arange(KEY_TILE, dtype=jnp.int32)[None, None, :])
    nb = _bucket_index(dist_n, dist_n >= 0)
    ci = np.arange(n_sub)[:, None]
    bj = np.arange(LANES)[None, :]
    mmat = ((ci // 4 == bj).astype(np.float32) + ((ci + 1) // 4 == bj).astype(np.float32)) * (ci < n_sub - 1)

    def rows(width):
        return pl.BlockSpec((Q_TILE, width), lambda b, t: (b * n_qt + t, 0))

    def per_b(shape):
        return pl.BlockSpec((1,) + shape, lambda b, t: (b,) + (0,) * len(shape))

    def const(shape):
        return pl.BlockSpec(shape, lambda b, t: (0,) * len(shape))

    in_specs = [rows(Q_W), rows(LANES), rows(D_MODEL), per_b((n_sub, KV_W // 2)), per_b((n_sub, KV_W // 2)),
                per_b((KV_W // 2, seq)), pl.BlockSpec((seq, KV_W // 2), lambda b, t: (b, 1))]
    in_specs += [pl.BlockSpec((1, Q_TILE, KV_W), lambda b, t, j=j: (b, t + j, 0)) for j in range(WIN_BLOCKS)]
    in_specs += [pl.BlockSpec((1, Q_TILE, n_sub), lambda b, t: (t, 0, 0)), const((Q_TILE, WINDOW + Q_TILE)),
                 const((n_near, Q_TILE, KEY_TILE)), const((n_sub, LANES)), const((NSA_HEADS, LANES)),
                 const((Q_W, D_MODEL))]
    rep_rows = NSA_REP * Q_TILE
    scratch = [pltpu.VMEM((rep_rows, n_sub), BF16), pltpu.VMEM((NSA_KV_HEADS, rep_rows, NSA_HD), F32),
               pltpu.VMEM((NSA_KV_HEADS, Q_TILE, LANES), BF16), pltpu.VMEM((rep_rows, 1), F32),
               pltpu.VMEM((rep_rows, 1), F32), pltpu.VMEM((rep_rows, 1), F32), pltpu.VMEM((rep_rows, NSA_HD), F32),
               pltpu.VMEM((rep_rows, KEY_TILE), BF16), pltpu.VMEM((rep_rows, WINDOW + Q_TILE), BF16),
               pltpu.VMEM((Q_TILE, Q_W), BF16)]
    return pl.pallas_call(
        functools.partial(_nsa_attend_body, n_sel=n_sel),
        out_shape=jax.ShapeDtypeStruct((batch * seq, D_MODEL), F32), grid=(batch, n_qt),
        in_specs=in_specs, out_specs=rows(D_MODEL), scratch_shapes=scratch,
        compiler_params=_params(2), name="nsa_attend_prompt",
    )(q, gates, x, k_c.astype(BF16), v_c.astype(BF16), kst, selb, *([win_pad] * WIN_BLOCKS),
      idx_c, idx_w, nb, jnp.asarray(mmat), _bias_table(rel_bias), w_out.astype(BF16))


SCORE_W = 384


def _nsa_sample_cmp_body(q_ref, kc_ref, vc_ref, idx_ref, mmat_ref, gsum_ref, tb_ref, oc_ref, top_ref, *,
                         n_sel, q_pos):
    n_sub = kc_ref.shape[1]
    q = q_ref[0]
    row_g = lax.broadcasted_iota(jnp.int32, (NSA_HEADS, 1), 0) // NSA_REP
    s = jnp.zeros((NSA_HEADS, n_sub), F32)
    for g in range(NSA_KV_HEADS):
        s = jnp.where(row_g == g, _dot_nt(q, kc_ref[0, :, g * NSA_HD:(g + 1) * NSA_HD]), s)
    tb = tb_ref[...]
    bias = jnp.concatenate([jnp.take_along_axis(
        tb, jnp.broadcast_to(idx_ref[:, c * LANES:(c + 1) * LANES], (NSA_HEADS, LANES)), axis=1)
        for c in range(n_sub // LANES)], axis=1)
    s = s + bias
    m = jnp.maximum(jnp.max(s, axis=1, keepdims=True), M_INIT)
    e = jnp.exp(s - m)
    p = e / jnp.maximum(jnp.sum(e, axis=1, keepdims=True), 1e-30)
    pb = p.astype(BF16)
    o = jnp.zeros((NSA_HEADS, NSA_HD), F32)
    for g in range(NSA_KV_HEADS):
        o = jnp.where(row_g == g, jnp.dot(pb, vc_ref[0, :, g * NSA_HD:(g + 1) * NSA_HD],
                                          preferred_element_type=F32), o)
    oc_ref[0] = o
    imp = _dot_f32(gsum_ref[...], p)
    score = _dot_f32(imp, mmat_ref[...])
    blk = lax.broadcasted_iota(jnp.int32, score.shape, 1)
    cur = q_pos // SEL_BLOCK
    score = jnp.where((blk == 0) | (blk == cur) | (blk == cur - 1), FORCE_SCORE, score)
    score = jnp.where(blk * SEL_BLOCK > q_pos, NEG, score)
    score = jnp.where(blk >= n_sel, -jnp.inf, score)
    blkf = blk.astype(F32)
    lane = lax.broadcasted_iota(jnp.int32, (SUBLANES, LANES), 1)
    top = jnp.zeros((SUBLANES, LANES), F32)
    for it in range(min(SEL_TOPK, n_sel)):
        mx = jnp.max(score, axis=1, keepdims=True)
        first = jnp.min(jnp.where(score == mx, blkf, float(SCORE_W)), axis=1, keepdims=True)
        top = jnp.where(lane == it, first, top)
        score = jnp.where(blkf == first, -jnp.inf, score)
    top_ref[0] = top.astype(jnp.int32)


def nsa_sample_cmp(q, k_c, v_c, rel_bias, past_len):
    batch = q.shape[0]
    n_sub = k_c.shape[1]
    n_sel = past_len // SEL_BLOCK + 1
    assert SEL_TOPK <= n_sel <= SCORE_W
    dist = past_len - (jnp.arange(n_sub, dtype=jnp.int32) * CMP_STRIDE + CMP_LEN - 1)
    idx = _bucket_index(dist, dist >= 0).reshape(1, n_sub)
    ci = np.arange(n_sub)[:, None]
    bj = np.arange(SCORE_W)[None, :]
    mmat = ((ci // 4 == bj).astype(np.float32) + ((ci + 1) // 4 == bj).astype(np.float32)) * (ci < n_sub - 1)
    gsum = (np.arange(SUBLANES)[:, None] == np.arange(NSA_HEADS)[None, :] // NSA_REP).astype(np.float32)

    def per_b(shape):
        return pl.BlockSpec((1,) + shape, lambda b: (b,) + (0,) * len(shape))

    def const(shape):
        return pl.BlockSpec(shape, lambda b: (0,) * len(shape))

    o_c, top = pl.pallas_call(
        functools.partial(_nsa_sample_cmp_body, n_sel=n_sel, q_pos=past_len),
        out_shape=(jax.ShapeDtypeStruct((batch, NSA_HEADS, NSA_HD), F32),
                   jax.ShapeDtypeStruct((batch, SUBLANES, LANES), jnp.int32)),
        grid=(batch,),
        in_specs=[per_b((NSA_HEADS, NSA_HD)), per_b((n_sub, KV_W // 2)), per_b((n_sub, KV_W // 2)),
                  const((1, n_sub)), const((n_sub, SCORE_W)), const((SUBLANES, NSA_HEADS)), const((NSA_HEADS, LANES))],
        out_specs=(per_b((NSA_HEADS, NSA_HD)), per_b((SUBLANES, LANES))),
        compiler_params=_params(1), name="nsa_sample_cmp",
    )(q, k_c.astype(BF16), v_c.astype(BF16), idx, jnp.asarray(mmat), jnp.asarray(gsum), _bias_table(rel_bias))
    return o_c, top[:, :NSA_KV_HEADS, :SEL_TOPK]


SEL_PER_STEP = 2


def _nsa_sample_attend_body(row_ref, sidx_ref, *refs, q_pos, n_past_blk, n_steps):
    n_blk_refs = NSA_KV_HEADS * SEL_PER_STEP
    blk_refs = refs[:n_blk_refs]
    (q_ref, knew_ref, win_ref, wnew_ref, gate_ref, oc_ref, idxw_ref, tb_ref, o_ref,
     s_s, v_s, ow_s) = refs[n_blk_refs:]
    b = pl.program_id(0)
    step = pl.program_id(1)
    q = q_ref[0]
    row_g = lax.broadcasted_iota(jnp.int32, (NSA_HEADS, 1), 0) // NSA_REP
    tb = tb_ref[...]
    half = KV_W // 2

    def by_group(fn):
        out = fn(0)
        for g in range(1, NSA_KV_HEADS):
            out = jnp.where(row_g == g, fn(g), out)
        return out

    @pl.when(step == 0)
    def _():
        s_w = by_group(lambda g: _dot(q, win_ref[0, 0, g]))
        bias = jnp.concatenate([jnp.take_along_axis(
            tb, jnp.broadcast_to(idxw_ref[:, c * LANES:(c + 1) * LANES], (NSA_HEADS, LANES)), axis=1)
            for c in range(s_w.shape[1] // LANES)], axis=1)
        s_w = s_w + bias
        k_new = by_group(lambda g: jnp.broadcast_to(wnew_ref[0, :, g * NSA_HD:(g + 1) * NSA_HD], (NSA_HEADS, NSA_HD)))
        v_new = by_group(lambda g: jnp.broadcast_to(wnew_ref[0, :, half + g * NSA_HD:half + (g + 1) * NSA_HD],
                                                    (NSA_HEADS, NSA_HD)))
        s_n = jnp.sum(q.astype(F32) * _r16(k_new), axis=1, keepdims=True) + tb[:, 0:1]
        m = jnp.maximum(jnp.max(s_w, axis=1, keepdims=True), s_n)
        e_w = jnp.exp(s_w - m)
        e_n = jnp.exp(s_n - m)
        total = jnp.sum(e_w, axis=1, keepdims=True) + e_n
        pb = (e_w / total).astype(BF16)
        pv = by_group(lambda g: _dot_nt(pb, win_ref[0, 1, g]))
        ow_s[...] = pv + _r16(e_n / total) * _r16(v_new)

    keys = SEL_PER_STEP * PAGE_ROWS
    key_lane = lax.broadcasted_iota(jnp.int32, (NSA_HD, PAGE_ROWS), 1)
    lane = lax.broadcasted_iota(jnp.int32, (1, PAGE_ROWS), 1)
    thresholds = _t5_thresholds()

    def tile_kv(g, kv):
        parts = []
        for j in range(SEL_PER_STEP):
            is_new = sidx_ref[b, g, step * SEL_PER_STEP + j] >= n_past_blk
            cached = blk_refs[g * SEL_PER_STEP + j][0, kv, g]
            fresh = jnp.where(key_lane == 0, knew_ref[0, kv * half + g * NSA_HD:kv * half + (g + 1) * NSA_HD, :], 0.0)
            parts.append(jnp.where(is_new, fresh, cached))
        return jnp.concatenate(parts, axis=1).astype(BF16)

    def tile_bias(g):
        parts = []
        for j in range(SEL_PER_STEP):
            blk = sidx_ref[b, g, step * SEL_PER_STEP + j]
            is_new = blk >= n_past_blk
            base = jnp.where(is_new, blk * SEL_BLOCK, (blk * SEL_BLOCK) // PAGE_ROWS * PAGE_ROWS)
            pos = base + lane
            dist = q_pos - pos
            idx = jnp.zeros((1, PAGE_ROWS), jnp.int32)
            for thr in thresholds:
                idx = idx + (dist >= thr).astype(jnp.int32)
            parts.append(jnp.where((dist >= 0) & (pos // SEL_BLOCK == blk), idx, MASK_BUCKET))
        idx = jnp.concatenate(parts, axis=1)
        return jnp.concatenate([jnp.take_along_axis(
            tb, jnp.broadcast_to(idx[:, c * LANES:(c + 1) * LANES], (NSA_HEADS, LANES)), axis=1)
            for c in range(keys // LANES)], axis=1)

    k0 = pl.multiple_of(step * keys, keys)
    s_s[:, pl.ds(k0, keys)] = by_group(lambda g: jnp.dot(q, tile_kv(g, 0), preferred_element_type=F32)
                                       + tile_bias(g))
    for g in range(NSA_KV_HEADS):
        v_s[g, :, pl.ds(k0, keys)] = tile_kv(g, 1)

    @pl.when(step == n_steps - 1)
    def _():
        s = s_s[...]
        e = jnp.exp(s - jnp.max(s, axis=1, keepdims=True))
        pb = (e / jnp.sum(e, axis=1, keepdims=True)).astype(BF16)
        o_sel = by_group(lambda g: _dot_nt(pb, v_s[g]))
        gate = gate_ref[0]
        o_ref[0] = gate[:, 0:1] * oc_ref[0] + gate[:, 1:2] * o_sel + gate[:, 2:3] * ow_s[...]


def nsa_sample_attend(q, kv_sel_new, kv_win_new, gates, o_c, top, cache_sel, win_buf, page_table, rel_bias,
                      past_len):
    batch = q.shape[0]
    n_past_blk = past_len // SEL_BLOCK
    bpp = PAGE_ROWS // SEL_BLOCK
    n_steps = SEL_TOPK // SEL_PER_STEP
    page_shape = cache_sel.shape[1:]
    jp = jnp.minimum(top, n_past_blk - 1)
    phys = jnp.take_along_axis(page_table, (jp // bpp).reshape(batch, -1), axis=1).reshape(top.shape)
    phys = phys.astype(jnp.int32)
    wb = win_buf.shape[-1]
    dist_w = past_len - (past_len - wb + jnp.arange(wb, dtype=jnp.int32))
    idx_w = _bucket_index(dist_w, (dist_w >= 0) & (dist_w < WINDOW)).reshape(1, wb)
    gate3 = gates[:, :3 * NSA_HEADS].reshape(batch, NSA_HEADS, 3)

    def blk_spec(g, j):
        return pl.BlockSpec((1,) + page_shape,
                            lambda b, s, rows, sidx, g=g, j=j: (rows[b, g, s * SEL_PER_STEP + j], 0, 0, 0, 0))

    def per_b(shape):
        return pl.BlockSpec((1,) + shape, lambda b, s, rows, sidx: (b,) + (0,) * len(shape))

    def const(shape):
        return pl.BlockSpec(shape, lambda b, s, rows, sidx: (0,) * len(shape))

    grid_spec = pltpu.PrefetchScalarGridSpec(
        num_scalar_prefetch=2, grid=(batch, n_steps),
        in_specs=[blk_spec(g, j) for g in range(NSA_KV_HEADS) for j in range(SEL_PER_STEP)] + [
            per_b((NSA_HEADS, NSA_HD)), per_b((KV_W, 1)), per_b(win_buf.shape[1:]), per_b((1, KV_W)),
            per_b((NSA_HEADS, 3)), per_b((NSA_HEADS, NSA_HD)), const((1, wb)), const((NSA_HEADS, LANES))],
        out_specs=per_b((NSA_HEADS, NSA_HD)),
        scratch_shapes=[pltpu.VMEM((NSA_HEADS, SEL_TOPK * PAGE_ROWS), F32),
                        pltpu.VMEM((NSA_KV_HEADS, NSA_HD, SEL_TOPK * PAGE_ROWS), BF16),
                        pltpu.VMEM((NSA_HEADS, NSA_HD), F32)])
    o = pl.pallas_call(
        functools.partial(_nsa_sample_attend_body, q_pos=past_len, n_past_blk=n_past_blk, n_steps=n_steps),
        out_shape=jax.ShapeDtypeStruct((batch, NSA_HEADS, NSA_HD), F32), grid_spec=grid_spec,
        compiler_params=_params(2), name="nsa_sample_attend",
    )(phys, top.astype(jnp.int32), *([cache_sel] * (NSA_KV_HEADS * SEL_PER_STEP)), q,
      kv_sel_new.reshape(batch, KV_W, 1), win_buf, kv_win_new.reshape(batch, 1, KV_W), gate3, o_c, idx_w,
      _bias_table(rel_bias))
    return o.reshape(batch, Q_W)


def _pad_cols(w, width):
    return jnp.pad(w, ((0, 0), (0, width - w.shape[1]))).astype(BF16)


def kernel(x_prompt, x_sample, state_ret, state_mlstm_C, state_mlstm_n, state_mlstm_m, state_conv, cache_nsa_cmp, cache_nsa_sel, state_nsa_win, page_table, rel_bias, norm_mix, norm_ffn, ab_w_in, ab_conv_w, ab_conv_b, ab_b_igate, ab_b_fgate, ab_gn_g, ab_gn_b, ab_hn_g, ab_w_out, nsa_w_in, nsa_q_norm, nsa_k_norm, nsa_cmp_w1, nsa_cmp_b1, nsa_cmp_w2, nsa_cmp_b2, nsa_w_out, moe_w_group, moe_b_group, moe_w_expert, moe_b_expert, moe_w_gate, moe_w_up, moe_w_down):
    bp, lp, d = x_prompt.shape
    bs, ls, _ = x_sample.shape
    page_size = cache_nsa_cmp.shape[2]
    past_len = page_table.shape[1] * page_size
    assert norm_mix.shape[0] == 2 and ls == 1 and d == D_MODEL and lp % KEY_TILE == 0
    xp = x_prompt.reshape(bp * lp, d)
    xs = x_sample.reshape(bs, d)

    def moe(x, layer):
        return hier_moe_residual(x, norm_ffn[layer], moe_w_group[layer], moe_b_group[layer], moe_w_expert[layer],
                                 moe_b_expert[layer], layer, moe_w_gate, moe_w_up, moe_w_down)

    w_in = _pad_cols(ab_w_in[0], AB_IN_PAD)
    ab = (ab_conv_w[0], ab_conv_b[0], ab_b_igate[0], ab_b_fgate[0], ab_gn_g[0], ab_gn_b[0], ab_hn_g[0])
    zp = norm_matmul(xp, norm_mix[0], w_in)
    xp, ret_p, mc_p, mn_p, mm_p, conv_p = ab_prompt(zp, xp, bp, lp, *ab, ab_w_out[0])
    zs = norm_matmul(xs, norm_mix[0], w_in)
    pos_s = past_len + jnp.arange(ls, dtype=jnp.int32)
    ys, ret_s, mc_s, mn_s, mm_s, conv_s = ab_sample(zs, pos_s, state_ret[0], state_mlstm_C[0], state_mlstm_n[0],
                                                    state_mlstm_m[0], state_conv[0], *ab)
    xs = matmul_residual(ys, ab_w_out[0].astype(BF16), xs)
    xp = moe(xp, 0)
    xs = moe(xs, 0)

    w_in = _pad_cols(nsa_w_in[0], NSA_IN_PAD)
    cmp_w = (nsa_cmp_w1[0], nsa_cmp_b1[0], nsa_cmp_w2[0], nsa_cmp_b2[0], nsa_k_norm[0, 0])
    kv_shape = (2, NSA_KV_HEADS, NSA_HD)
    zp = norm_matmul(xp, norm_mix[1], w_in)
    q, cmp_t, sel_t, win_t, selb, winb, gates, kst = nsa_prep(zp, nsa_q_norm[0], nsa_k_norm[0], batch=bp)
    k_c, v_c = nsa_compress(cmp_t.reshape((bp,) + kv_shape + (lp,)), None, *cmp_w)
    xp = nsa_attend_prompt(q, gates, xp, k_c, v_c, kst, selb, winb, rel_bias, nsa_w_out[0], bp, lp)
    win_keep = min(WINDOW, lp)

    def rows_major(t):
        return jnp.transpose(t.reshape((1, bp) + kv_shape + (t.shape[-1],)), (0, 1, 5, 2, 3, 4))

    cmp_p = rows_major(cmp_t)
    sel_p = rows_major(sel_t)
    win_p = rows_major(win_t[:, :, lp - win_keep:])

    def rows_minor(c):
        return jnp.moveaxis(c, -4, -1)

    zs = norm_matmul(xs, norm_mix[1], w_in)
    q, cmp_s, sel_s, win_s, _, _, gates = nsa_prep(zs, nsa_q_norm[0], nsa_k_norm[0])
    k_c, v_c = nsa_compress(rows_minor(cache_nsa_cmp[0]), page_table, *cmp_w)
    q3 = q.reshape(bs, NSA_HEADS, NSA_HD)
    o_c, top = nsa_sample_cmp(q3, k_c, v_c, rel_bias, past_len)
    win_buf = state_nsa_win[0]
    o = nsa_sample_attend(q3, sel_s, win_s, gates, o_c, top, rows_minor(cache_nsa_sel[0]), rows_minor(win_buf),
                          page_table, rel_bias, past_len)
    xs = matmul_residual(o, nsa_w_out[0].astype(BF16), xs)
    win_s = jnp.concatenate([win_buf, win_s.reshape((bs, ls) + kv_shape)], axis=1)[None, :, ls:]
    cmp_s = cmp_s.reshape((1, bs, ls) + kv_shape)
    sel_s = sel_s.reshape((1, bs, ls) + kv_shape)
    xp = moe(xp, 1)
    xs = moe(xs, 1)

    return (xp.reshape(bp, lp, d), xs.reshape(bs, ls, d), ret_p[None], ret_s[None], mc_p[None], mc_s[None],
            mn_p[None], mn_s[None], mm_p[None], mm_s[None], conv_p[None], conv_s[None],
            cmp_p, cmp_s, sel_p, sel_s, win_p, win_s)
```

```python
import functools
import math

import jax
import jax.numpy as jnp
import numpy as np
from jax import lax
from jax.experimental import pallas as pl
from jax.experimental.pallas import tpu as pltpu

F32 = jnp.float32
BF16 = jnp.bfloat16
LANES = 128
SUBLANES = 8
VMEM_LIMIT = 56 * 1024 * 1024

D_MODEL = 1024
RET_HEADS = 4
ML_HEADS = 4
HEAD_D = 128
CONV_W = 4
CHUNK = 128
ROPE_BASE = 10000.0
AB_IN = 4104
AB_IN_PAD = 4224
NSA_HEADS = 16
NSA_KV_HEADS = 4
NSA_REP = 4
NSA_HD = 64
NSA_IN_PAD = 2688
CMP_LEN = 32
CMP_STRIDE = 16
CMP_HID = 128
SEL_BLOCK = 64
SEL_TOPK = 16
WINDOW = 512
REL_BUCKETS = 32
REL_MAX_DIST = 1024
FORCE_SCORE = 1e4
MOE_GROUPS = 4
MOE_EXP_PER_GROUP = 8
MOE_EXPERTS = 32
NEG = -1e30
EPS = 1e-6


def _params(n_grid):
    return pltpu.CompilerParams(dimension_semantics=("arbitrary",) * n_grid, vmem_limit_bytes=VMEM_LIMIT)


def _dot(a, b):
    return jnp.dot(a.astype(BF16), b.astype(BF16), preferred_element_type=F32)


def _dot_nt(a, b):
    return lax.dot_general(a.astype(BF16), b.astype(BF16), (((1,), (1,)), ((), ())), preferred_element_type=F32)


def _dot_tn(a, b):
    return lax.dot_general(a.astype(BF16), b.astype(BF16), (((0,), (0,)), ((), ())), preferred_element_type=F32)


def _dot_f32(a, b):
    return jnp.dot(a, b, preferred_element_type=F32, precision=lax.Precision.HIGHEST)


def _r16(x):
    return x.astype(BF16).astype(F32)


def _sigmoid(x):
    return 1.0 / (1.0 + jnp.exp(-x))


def _silu(x):
    return x * _sigmoid(x)


def _log_sigmoid(x):
    return -(jnp.maximum(-x, 0.0) + jnp.log1p(jnp.exp(-jnp.abs(x))))


def _norm_matmul_body(x_ref, g_ref, w_ref, o_ref, *, col_tile):
    x = x_ref[...]
    y = x * lax.rsqrt(jnp.mean(x * x, axis=-1, keepdims=True) + EPS) * g_ref[...]
    yb = y.astype(BF16)
    for c0 in range(0, o_ref.shape[1], col_tile):
        o_ref[:, c0:c0 + col_tile] = jnp.dot(yb, w_ref[:, c0:c0 + col_tile], preferred_element_type=F32)


def norm_matmul(x, g, w, row_tile=256):
    n, d = x.shape
    c = w.shape[1]
    tm = min(row_tile, n)
    col_tile = 384 if c % 384 == 0 else LANES
    return pl.pallas_call(
        functools.partial(_norm_matmul_body, col_tile=col_tile),
        out_shape=jax.ShapeDtypeStruct((n, c), F32),
        grid=(n // tm,),
        in_specs=[pl.BlockSpec((tm, d), lambda i: (i, 0)),
                  pl.BlockSpec((1, d), lambda i: (0, 0)),
                  pl.BlockSpec((d, c), lambda i: (0, 0))],
        out_specs=pl.BlockSpec((tm, c), lambda i: (i, 0)),
        compiler_params=_params(1),
        name="norm_matmul",
    )(x, g.reshape(1, d), w)


def _retention_constants(c):
    h = np.arange(RET_HEADS, dtype=np.float64)
    log_g = np.log1p(-np.exp2(-5.0 - h))
    i = np.arange(c, dtype=np.float64)
    diff = i[:, None] - i[None, :]
    decay = np.where(diff >= 0, np.exp(np.maximum(diff, 0.0)[None] * log_g[:, None, None]), 0.0)
    q_dec = np.exp((i + 1.0)[None, :] * log_g[:, None])[:, :, None]
    k_dec = np.exp((c - 1.0 - i)[None, :] * log_g[:, None])[:, :, None]
    s_dec = np.exp(c * log_g)
    return (jnp.asarray(decay, F32), jnp.asarray(q_dec, F32), jnp.asarray(k_dec, F32),
            [float(v) for v in s_dec])


def _rope_tables(pos):
    half = HEAD_D // 2
    freqs = ROPE_BASE ** (-jnp.arange(half, dtype=F32) / half)
    ang = pos.astype(F32)[:, None] * freqs[None, :]
    cos, sin = jnp.cos(ang), jnp.sin(ang)
    return jnp.concatenate([cos, cos], axis=-1), jnp.concatenate([-sin, sin], axis=-1)


def _rope(x, cosf, sinf):
    return x * cosf + pltpu.roll(x, HEAD_D // 2, 1) * sinf


def _ab_prompt_body(rq_ref, rk_ref, rv_ref, rg_ref, mqk_ref, mv_ref, mo_ref, gz_ref, x_ref, cos_ref, sin_ref,
                    decay_ref, qdec_ref, kdec_ref, convw_ref, convb_ref, gbias_ref, gng_ref, gnb_ref, hng_ref,
                    wout_ref,
                    y_ref, s_ref, c_ref, n_ref, m_ref, conv_ref,
                    cbuf_ref, ycat_ref, *, s_dec):
    c = pl.program_id(1)
    tail = CONV_W - 1

    @pl.when(c == 0)
    def _():
        s_ref[...] = jnp.zeros_like(s_ref)
        c_ref[...] = jnp.zeros_like(c_ref)
        n_ref[...] = jnp.zeros_like(n_ref)
        m_ref[...] = jnp.zeros_like(m_ref)
        cbuf_ref[0:SUBLANES, :] = jnp.zeros((SUBLANES, cbuf_ref.shape[1]), F32)

    cosf = cos_ref[...]
    sinf = sin_ref[...]
    row = lax.broadcasted_iota(jnp.int32, (CHUNK, CHUNK), 0)
    col = lax.broadcasted_iota(jnp.int32, (CHUNK, CHUNK), 1)
    eye = row == col
    tril = row >= col
    triu = row <= col

    cbuf_ref[SUBLANES:SUBLANES + CHUNK, :] = mqk_ref[...]
    conv = convb_ref[...]
    for w in range(CONV_W):
        conv = conv + (_r16(cbuf_ref[SUBLANES - tail + w:SUBLANES - tail + w + CHUNK, :])
                       * _r16(convw_ref[w:w + 1, :]))
    qk = _silu(conv)
    last = cbuf_ref[CHUNK + SUBLANES - tail:CHUNK + SUBLANES, :]
    cbuf_ref[SUBLANES - tail:SUBLANES, :] = last
    conv_ref[0] = last

    gz = gz_ref[...] + gbias_ref[...]
    for h in range(RET_HEADS):
        sl = slice(h * HEAD_D, (h + 1) * HEAD_D)
        q = _rope(rq_ref[:, sl], cosf, sinf)
        k = _rope(rk_ref[:, sl], cosf, sinf) * (HEAD_D ** -0.5)
        v = rv_ref[:, sl]
        a = _dot_nt(q, k) * decay_ref[h]
        s_old = s_ref[0, h]
        o = _dot(a, v) + qdec_ref[h] * _dot(q, s_old)
        s_ref[0, h] = s_dec[h] * s_old + _dot_tn(k * kdec_ref[h], v)
        mu = jnp.mean(o, axis=-1, keepdims=True)
        var = jnp.mean(jnp.square(o - mu), axis=-1, keepdims=True)
        o = (o - mu) * lax.rsqrt(var + EPS) * gng_ref[:, sl] + gnb_ref[:, sl]
        ycat_ref[:, sl] = _silu(rg_ref[:, sl]) * o

        mq = qk[:, sl]
        mk = qk[:, ML_HEADS * HEAD_D + h * HEAD_D:ML_HEADS * HEAD_D + (h + 1) * HEAD_D] * (HEAD_D ** -0.5)
        mv = mv_ref[:, sl]
        i_col = gz[:, h:h + 1]
        f_col = _log_sigmoid(gz[:, ML_HEADS + h:ML_HEADS + h + 1])
        i_row = jnp.sum(jnp.where(eye, i_col, 0.0), axis=0, keepdims=True)
        f_row = jnp.sum(jnp.where(eye, f_col, 0.0), axis=0, keepdims=True)
        b_col = jnp.sum(jnp.where(tril, f_row, 0.0), axis=1, keepdims=True)
        b_row = jnp.sum(jnp.where(triu, f_col, 0.0), axis=0, keepdims=True)
        m_old = m_ref[0, h:h + 1, 0:1]
        dlog = jnp.where(tril, b_col - b_row + i_row, -jnp.inf)
        inter = b_col + m_old
        m_t = jnp.maximum(inter, jnp.max(dlog, axis=1, keepdims=True))
        wgt = _dot_nt(mq, mk) * jnp.exp(dlog - m_t)
        e_inter = jnp.exp(inter - m_t)
        c_old = c_ref[0, h]
        n_old = n_ref[0, h:h + 1, :]
        num = _dot(wgt, mv) + e_inter * _dot_nt(mq, c_old)
        den = (jnp.sum(wgt, axis=1, keepdims=True)
               + e_inter * jnp.sum(_r16(mq) * _r16(n_old), axis=1, keepdims=True))
        hc = num / jnp.maximum(jnp.abs(den), jnp.exp(-m_t))
        b_last = b_col[CHUNK - 1:CHUNK, :]
        u_row = b_last - b_row + i_row
        u_col = b_last - b_col + i_col
        m_new = jnp.maximum(b_last + m_old, jnp.max(u_row, axis=1, keepdims=True))
        ws_col = jnp.exp(u_col - m_new)
        f_state = jnp.exp(b_last + m_old - m_new)
        c_ref[0, h] = f_state * c_old + _dot_tn(mv * ws_col, mk)
        n_ref[0, h:h + 1, :] = f_state * n_old + jnp.sum(_r16(ws_col) * _r16(mk), axis=0, keepdims=True)
        m_ref[0, h:h + 1, :] = jnp.broadcast_to(m_new, (1, LANES))
        hm = _sigmoid(mo_ref[:, sl]) * hc
        hm = hm * lax.rsqrt(jnp.mean(hm * hm, axis=-1, keepdims=True) + EPS) * hng_ref[:, sl]
        ycat_ref[:, RET_HEADS * HEAD_D + h * HEAD_D:RET_HEADS * HEAD_D + (h + 1) * HEAD_D] = hm

    y_ref[...] = x_ref[...] + jnp.dot(ycat_ref[...].astype(BF16), wout_ref[...], preferred_element_type=F32)


def ab_prompt(z, x, batch, seq, conv_w, conv_b, b_ig, b_fg, gn_g, gn_b, hn_g, w_out):
    n_chunk = seq // CHUNK
    decay, q_dec, k_dec, s_dec = _retention_constants(CHUNK)
    cosf, sinf = _rope_tables(jnp.arange(seq, dtype=jnp.int32))
    gbias = jnp.zeros((1, LANES), F32).at[0, :ML_HEADS].set(b_ig).at[0, ML_HEADS:2 * ML_HEADS].set(b_fg)
    hw = RET_HEADS * HEAD_D
    qkw = 2 * ML_HEADS * HEAD_D

    def zspec(width, blk):
        return pl.BlockSpec((CHUNK, width), lambda b, c, blk=blk: (b * n_chunk + c, blk))

    def const(shape):
        return pl.BlockSpec(shape, lambda b, c: (0,) * len(shape))

    in_specs = [zspec(hw, 0), zspec(hw, 1), zspec(hw, 2), zspec(hw, 3), zspec(qkw, 2), zspec(hw, 6), zspec(hw, 7),
                zspec(LANES, (AB_IN_PAD - LANES) // LANES),
                pl.BlockSpec((CHUNK, D_MODEL), lambda b, c: (b * n_chunk + c, 0)),
                pl.BlockSpec((CHUNK, HEAD_D), lambda b, c: (c, 0)),
                pl.BlockSpec((CHUNK, HEAD_D), lambda b, c: (c, 0)),
                const((RET_HEADS, CHUNK, CHUNK)), const((RET_HEADS, CHUNK, 1)), const((RET_HEADS, CHUNK, 1)),
                const((CONV_W, qkw)), const((1, qkw)), const((1, LANES)),
                const((1, hw)), const((1, hw)), const((1, hw)), const((2 * hw, D_MODEL))]
    out_shape = (jax.ShapeDtypeStruct((batch * seq, D_MODEL), F32),
                 jax.ShapeDtypeStruct((batch, RET_HEADS, HEAD_D, HEAD_D), F32),
                 jax.ShapeDtypeStruct((batch, ML_HEADS, HEAD_D, HEAD_D), F32),
                 jax.ShapeDtypeStruct((batch, ML_HEADS, HEAD_D), F32),
                 jax.ShapeDtypeStruct((batch, SUBLANES, LANES), F32),
                 jax.ShapeDtypeStruct((batch, CONV_W - 1, qkw), F32))
    out_specs = (pl.BlockSpec((CHUNK, D_MODEL), lambda b, c: (b * n_chunk + c, 0)),
                 pl.BlockSpec((1, RET_HEADS, HEAD_D, HEAD_D), lambda b, c: (b, 0, 0, 0)),
                 pl.BlockSpec((1, ML_HEADS, HEAD_D, HEAD_D), lambda b, c: (b, 0, 0, 0)),
                 pl.BlockSpec((1, ML_HEADS, HEAD_D), lambda b, c: (b, 0, 0)),
                 pl.BlockSpec((1, SUBLANES, LANES), lambda b, c: (b, 0, 0)),
                 pl.BlockSpec((1, CONV_W - 1, qkw), lambda b, c: (b, 0, 0)))
    y, s, cc, n, m, conv = pl.pallas_call(
        functools.partial(_ab_prompt_body, s_dec=s_dec),
        out_shape=out_shape, grid=(batch, n_chunk), in_specs=in_specs, out_specs=out_specs,
        scratch_shapes=[pltpu.VMEM((CHUNK + SUBLANES, qkw), F32), pltpu.VMEM((CHUNK, 2 * hw), F32)],
        compiler_params=_params(2), name="ab_prompt",
    )(z, z, z, z, z, z, z, z, x, cosf, sinf, decay, q_dec, k_dec, conv_w, conv_b.reshape(1, qkw), gbias,
      gn_g.reshape(1, hw), gn_b.reshape(1, hw), hn_g.reshape(1, hw), w_out.astype(BF16))
    return y, s, cc, n, m[:, :ML_HEADS, 0], conv


def _matmul_residual_body(a_ref, w_ref, x_ref, o_ref):
    o_ref[...] = x_ref[...] + jnp.dot(a_ref[...].astype(BF16), w_ref[...], preferred_element_type=F32)


def matmul_residual(a, w, x, row_tile=256):
    n, kk = a.shape
    d = w.shape[1]
    tm = min(row_tile, n)
    return pl.pallas_call(
        _matmul_residual_body, out_shape=jax.ShapeDtypeStruct((n, d), F32), grid=(n // tm,),
        in_specs=[pl.BlockSpec((tm, kk), lambda i: (i, 0)), pl.BlockSpec((kk, d), lambda i: (0, 0)),
                  pl.BlockSpec((tm, d), lambda i: (i, 0))],
        out_specs=pl.BlockSpec((tm, d), lambda i: (i, 0)),
        compiler_params=_params(1), name="matmul_residual",
    )(a, w, x)


def _ab_sample_body(m0_ref, z_ref, cos_ref, sin_ref, s0_ref, c0_ref, n0_ref, conv0_ref,
                    convw_ref, convb_ref, gbias_ref, gng_ref, gnb_ref, hng_ref,
                    y_ref, s_ref, c_ref, n_ref, m_ref, conv_ref, *, g_dec):
    b = pl.program_id(0)
    hw = RET_HEADS * HEAD_D
    qkw = 2 * ML_HEADS * HEAD_D
    tail = CONV_W - 1
    cosf = cos_ref[...]
    sinf = sin_ref[...]
    row = lax.broadcasted_iota(jnp.int32, (HEAD_D, HEAD_D), 0)
    col = lax.broadcasted_iota(jnp.int32, (HEAD_D, HEAD_D), 1)
    eye = row == col

    def to_col(r):
        return jnp.sum(jnp.where(eye, r, 0.0), axis=1, keepdims=True)

    def to_row(cv):
        return jnp.sum(jnp.where(eye, cv, 0.0), axis=0, keepdims=True)

    mqk = z_ref[0, :, 4 * hw:4 * hw + qkw]
    conv = convb_ref[...] + mqk * convw_ref[tail:CONV_W, :]
    for w in range(tail):
        conv = conv + conv0_ref[0, w:w + 1, :] * convw_ref[w:w + 1, :]
    qk = _silu(conv)
    conv_ref[0, 0:tail - 1, :] = conv0_ref[0, 1:tail, :]
    conv_ref[0, tail - 1:tail, :] = mqk
    gz = z_ref[0, :, AB_IN_PAD - LANES:AB_IN_PAD] + gbias_ref[...]

    for h in range(RET_HEADS):
        sl = slice(h * HEAD_D, (h + 1) * HEAD_D)
        q = _rope(z_ref[0, :, sl], cosf, sinf)
        k = _rope(z_ref[0, :, hw + h * HEAD_D:hw + (h + 1) * HEAD_D], cosf, sinf) * (HEAD_D ** -0.5)
        v = z_ref[0, :, 2 * hw + h * HEAD_D:2 * hw + (h + 1) * HEAD_D]
        rg = z_ref[0, :, 3 * hw + h * HEAD_D:3 * hw + (h + 1) * HEAD_D]
        s_old = s0_ref[0, h]
        qk_s = jnp.sum(q * k, axis=1, keepdims=True)
        o = qk_s * v + g_dec[h] * jnp.sum(_r16(to_col(q)) * _r16(s_old), axis=0, keepdims=True)
        s_ref[0, h] = g_dec[h] * s_old + to_col(k) * v
        mu = jnp.mean(o, axis=-1, keepdims=True)
        var = jnp.mean(jnp.square(o - mu), axis=-1, keepdims=True)
        o = (o - mu) * lax.rsqrt(var + EPS) * gng_ref[:, sl] + gnb_ref[:, sl]
        y_ref[0, :, sl] = _silu(rg) * o

        mq = qk[:, sl]
        mk = qk[:, ML_HEADS * HEAD_D + h * HEAD_D:ML_HEADS * HEAD_D + (h + 1) * HEAD_D] * (HEAD_D ** -0.5)
        mv = z_ref[0, :, 4 * hw + qkw + h * HEAD_D:4 * hw + qkw + (h + 1) * HEAD_D]
        mo = z_ref[0, :, 5 * hw + qkw + h * HEAD_D:5 * hw + qkw + (h + 1) * HEAD_D]
        ig = gz[:, h:h + 1]
        lf = _log_sigmoid(gz[:, ML_HEADS + h:ML_HEADS + h + 1])
        m_old = m0_ref[b, h]
        inter = lf + m_old
        m_t = jnp.maximum(inter, ig)
        wgt = jnp.sum(mq * mk, axis=1, keepdims=True) * jnp.exp(ig - m_t)
        e_inter = jnp.exp(inter - m_t)
        c_old = c0_ref[0, h]
        n_old = n0_ref[0, h:h + 1, :]
        cq = to_row(jnp.sum(_r16(c_old) * _r16(mq), axis=1, keepdims=True))
        num = wgt * mv + e_inter * cq
        den = wgt + e_inter * jnp.sum(n_old * mq, axis=1, keepdims=True)
        hc = num / jnp.maximum(jnp.abs(den), jnp.exp(-m_t))
        ws = jnp.exp(ig - m_t)
        c_ref[0, h] = e_inter * c_old + (ws * to_col(mv)) * mk
        n_ref[0, h:h + 1, :] = e_inter * n_old + ws * mk
        m_ref[0, h:h + 1, :] = jnp.broadcast_to(m_t, (1, LANES))
        hm = _sigmoid(mo) * hc
        hm = hm * lax.rsqrt(jnp.mean(hm * hm, axis=-1, keepdims=True) + EPS) * hng_ref[:, sl]
        y_ref[0, :, hw + h * HEAD_D:hw + (h + 1) * HEAD_D] = hm
    m_ref[0, ML_HEADS:SUBLANES, :] = jnp.zeros((SUBLANES - ML_HEADS, LANES), F32)


def ab_sample(z, pos, s0, c0, n0, m0, conv0, conv_w, conv_b, b_ig, b_fg, gn_g, gn_b, hn_g):
    batch = z.shape[0]
    h = np.arange(RET_HEADS, dtype=np.float64)
    g_dec = [float(v) for v in np.exp(np.log1p(-np.exp2(-5.0 - h)))]
    cosf, sinf = _rope_tables(pos)
    gbias = jnp.zeros((1, LANES), F32).at[0, :ML_HEADS].set(b_ig).at[0, ML_HEADS:2 * ML_HEADS].set(b_fg)
    hw = RET_HEADS * HEAD_D
    qkw = 2 * ML_HEADS * HEAD_D

    def per_b(shape):
        return pl.BlockSpec((1,) + shape, lambda b: (b,) + (0,) * len(shape))

    def const(shape):
        return pl.BlockSpec(shape, lambda b: (0,) * len(shape))

    in_specs = [pl.BlockSpec(memory_space=pltpu.SMEM), per_b((1, AB_IN_PAD)), const((1, HEAD_D)), const((1, HEAD_D)),
                per_b((RET_HEADS, HEAD_D, HEAD_D)), per_b((ML_HEADS, HEAD_D, HEAD_D)), per_b((ML_HEADS, HEAD_D)),
                per_b((CONV_W - 1, qkw)), const((CONV_W, qkw)), const((1, qkw)), const((1, LANES)),
                const((1, hw)), const((1, hw)), const((1, hw))]
    out_shape = (jax.ShapeDtypeStruct((batch, 1, 2 * hw), F32),
                 jax.ShapeDtypeStruct((batch, RET_HEADS, HEAD_D, HEAD_D), F32),
                 jax.ShapeDtypeStruct((batch, ML_HEADS, HEAD_D, HEAD_D), F32),
                 jax.ShapeDtypeStruct((batch, ML_HEADS, HEAD_D), F32),
                 jax.ShapeDtypeStruct((batch, SUBLANES, LANES), F32),
                 jax.ShapeDtypeStruct((batch, CONV_W - 1, qkw), F32))
    out_specs = (per_b((1, 2 * hw)), per_b((RET_HEADS, HEAD_D, HEAD_D)), per_b((ML_HEADS, HEAD_D, HEAD_D)),
                 per_b((ML_HEADS, HEAD_D)), per_b((SUBLANES, LANES)), per_b((CONV_W - 1, qkw)))
    y, s, cc, n, m, conv = pl.pallas_call(
        functools.partial(_ab_sample_body, g_dec=g_dec),
        out_shape=out_shape, grid=(batch,), in_specs=in_specs, out_specs=out_specs,
        compiler_params=_params(1), name="ab_sample",
    )(m0, z.reshape(batch, 1, AB_IN_PAD), cosf, sinf, s0, c0, n0, conv0, conv_w, conv_b.reshape(1, qkw), gbias,
      gn_g.reshape(1, hw), gn_b.reshape(1, hw), hn_g.reshape(1, hw))
    return y.reshape(batch, 2 * hw), s, cc, n, m[:, :ML_HEADS, 0], conv


MOE_TILE = 256


def _moe_router_body(x_ref, g_ref, wr_ref, br_ref, hn_ref, route_ref):
    x = x_ref[...]
    hn = x * lax.rsqrt(jnp.mean(x * x, axis=-1, keepdims=True) + EPS) * g_ref[...]
    hn_ref[...] = hn.astype(BF16)
    z = _dot(hn, wr_ref[...]) + br_ref[...]
    lane = lax.broadcasted_iota(jnp.int32, z.shape, 1)
    lanef = lane.astype(F32)
    is_group = lane < MOE_GROUPS
    gl = jnp.where(is_group, z, -jnp.inf)
    gmax = jnp.max(gl, axis=1, keepdims=True)
    g_top = jnp.min(jnp.where(gl == gmax, lanef, float(LANES)), axis=1, keepdims=True)
    pg_top = 1.0 / jnp.sum(jnp.where(is_group, jnp.exp(z - gmax), 0.0), axis=1, keepdims=True)
    grp = ((lane - MOE_GROUPS) // MOE_EXP_PER_GROUP).astype(F32)
    in_group = (lane >= MOE_GROUPS) & (lane < MOE_GROUPS + MOE_EXPERTS) & (grp == g_top)
    el = jnp.where(in_group, z, -jnp.inf)
    v1 = jnp.max(el, axis=1, keepdims=True)
    i1 = jnp.min(jnp.where(el == v1, lanef, float(LANES)), axis=1, keepdims=True)
    el2 = jnp.where(lanef == i1, -jnp.inf, el)
    v2 = jnp.max(el2, axis=1, keepdims=True)
    i2 = jnp.min(jnp.where(el2 == v2, lanef, float(LANES)), axis=1, keepdims=True)
    t = jnp.exp(v2 - v1)
    p1 = 1.0 / (1.0 + t)
    out = jnp.where(lane == 0, i1 - MOE_GROUPS,
                    jnp.where(lane == 1, i2 - MOE_GROUPS,
                              jnp.where(lane == 2, pg_top * p1,
                                        jnp.where(lane == 3, pg_top * (t * p1), 0.0))))
    route_ref[...] = out


def moe_router(x, g, w_group, b_group, w_expert, b_expert, row_tile=256):
    n, d = x.shape
    tm = min(row_tile, n)
    used = MOE_GROUPS + MOE_EXPERTS
    wr = jnp.pad(jnp.concatenate([w_group, w_expert], axis=1), ((0, 0), (0, LANES - used)))
    br = jnp.pad(jnp.concatenate([b_group, b_expert]), (0, LANES - used)).reshape(1, LANES)
    hn, route = pl.pallas_call(
        _moe_router_body,
        out_shape=(jax.ShapeDtypeStruct((n, d), BF16), jax.ShapeDtypeStruct((n, LANES), F32)),
        grid=(n // tm,),
        in_specs=[pl.BlockSpec((tm, d), lambda i: (i, 0)), pl.BlockSpec((1, d), lambda i: (0, 0)),
                  pl.BlockSpec((d, LANES), lambda i: (0, 0)), pl.BlockSpec((1, LANES), lambda i: (0, 0))],
        out_specs=(pl.BlockSpec((tm, d), lambda i: (i, 0)), pl.BlockSpec((tm, LANES), lambda i: (i, 0))),
        compiler_params=_params(1), name="moe_router",
    )(x, g.reshape(1, d), wr, br)
    return hn, route[:, 0:2].astype(jnp.int32), route[:, 2:4]


def _moe_ffn_body(blk_e_ref, x_ref, rw_ref, wg_ref, wu_ref, wd_ref, o_ref, wg_s, wu_s, wd_s):
    i = pl.program_id(0)
    prev = blk_e_ref[jnp.maximum(i - 1, 0)]

    @pl.when((i == 0) | (blk_e_ref[i] != prev))
    def _():
        wg_s[...] = wg_ref[0].astype(BF16)
        wu_s[...] = wu_ref[0].astype(BF16)
        wd_s[...] = wd_ref[0].astype(BF16)

    x = x_ref[...]
    hg = jnp.dot(x, wg_s[...], preferred_element_type=F32)
    hu = jnp.dot(x, wu_s[...], preferred_element_type=F32)
    hb = (_silu(hg) * hu).astype(BF16)
    o_ref[...] = jnp.dot(hb, wd_s[...], preferred_element_type=F32) * rw_ref[...]


def moe_ffn(x_rows, row_w, blk_e, layer, w_g, w_u, w_d):
    r, d = x_rows.shape
    ff = w_g.shape[3]
    n_blk = r // MOE_TILE
    grid_spec = pltpu.PrefetchScalarGridSpec(
        num_scalar_prefetch=1, grid=(n_blk,),
        in_specs=[pl.BlockSpec((MOE_TILE, d), lambda i, e: (i, 0)),
                  pl.BlockSpec((MOE_TILE, 1), lambda i, e: (i, 0)),
                  pl.BlockSpec((None, 1, d, ff), lambda i, e: (layer, e[i], 0, 0)),
                  pl.BlockSpec((None, 1, d, ff), lambda i, e: (layer, e[i], 0, 0)),
                  pl.BlockSpec((None, 1, ff, d), lambda i, e: (layer, e[i], 0, 0))],
        out_specs=pl.BlockSpec((MOE_TILE, d), lambda i, e: (i, 0)),
        scratch_shapes=[pltpu.VMEM((d, ff), BF16), pltpu.VMEM((d, ff), BF16), pltpu.VMEM((ff, d), BF16)])
    return pl.pallas_call(
        _moe_ffn_body, out_shape=jax.ShapeDtypeStruct((r, d), F32), grid_spec=grid_spec,
        compiler_params=_params(1), name="moe_ffn",
    )(blk_e, x_rows, row_w.reshape(r, 1), w_g, w_u, w_d)


def hier_moe_residual(x, g, w_group, b_group, w_expert, b_expert, layer, w_g, w_u, w_d):
    n, d = x.shape
    hn, expert, gate = moe_router(x, g, w_group, b_group, w_expert, b_expert)
    n_exp = w_g.shape[1]
    kk = expert.shape[1]
    a = n * kk
    e_flat = expert.reshape(-1)
    counts = jnp.sum((e_flat[:, None] == jnp.arange(n_exp, dtype=jnp.int32)[None, :]).astype(jnp.int32), axis=0)
    starts = jnp.cumsum(counts) - counts
    padded = (counts + MOE_TILE - 1) // MOE_TILE * MOE_TILE
    pend = jnp.cumsum(padded)
    pstart = pend - padded
    order = jnp.argsort(e_flat).astype(jnp.int32)
    rank = jnp.argsort(order).astype(jnp.int32)
    dest = (pstart[e_flat] + rank - starts[e_flat]).reshape(n, kk)
    r = -(-(a + n_exp * (MOE_TILE - 1)) // MOE_TILE) * MOE_TILE
    n_blk = r // MOE_TILE
    tile_start = jnp.arange(n_blk, dtype=jnp.int32) * MOE_TILE
    blk_e = jnp.minimum(jnp.sum((pend[None, :] <= tile_start[:, None]).astype(jnp.int32), axis=1), n_exp - 1)
    row_e = jnp.repeat(blk_e, MOE_TILE)
    row_off = jnp.arange(r, dtype=jnp.int32) - pstart[row_e]
    row_real = row_off < counts[row_e]
    row_src = order[jnp.clip(starts[row_e] + row_off, 0, a - 1)]
    row_tok = jnp.where(row_real, row_src // kk, 0)
    row_w = jnp.where(row_real, gate.reshape(-1)[row_src], 0.0)
    y_rows = moe_ffn(hn[row_tok], row_w, blk_e, layer, w_g, w_u, w_d)
    return x + jnp.sum(y_rows[dest], axis=1)


KV_W = 2 * NSA_KV_HEADS * NSA_HD
Q_W = NSA_HEADS * NSA_HD
MASK_BUCKET = REL_BUCKETS
KEY_TILE = 512


def _t5_thresholds():
    exact = REL_BUCKETS // 2
    dist = np.arange(0, 4 * REL_MAX_DIST, dtype=np.int64)
    nf = np.maximum(dist, 1).astype(np.float64)
    large = exact + np.floor(np.log(nf / exact) / math.log(REL_MAX_DIST / exact) * (REL_BUCKETS - exact) + 1e-9)
    bucket = np.where(dist < exact, dist, np.minimum(large, REL_BUCKETS - 1)).astype(np.int64)
    return [int(np.argmax(bucket >= b)) for b in range(1, REL_BUCKETS)]


def _bucket_index(dist, valid):
    idx = jnp.zeros(dist.shape, jnp.int32)
    for thr in _t5_thresholds():
        idx = idx + (dist >= thr).astype(jnp.int32)
    return jnp.where(valid, idx, MASK_BUCKET)


def _bias_table(rel_bias):
    t = jnp.zeros((NSA_HEADS, LANES), F32).at[:, :REL_BUCKETS].set(rel_bias.T.astype(F32))
    return t.at[:, MASK_BUCKET].set(NEG)


def _group_mean_matrix():
    i = np.arange(LANES)
    return jnp.asarray((i[:, None] // NSA_HD == i[None, :] // NSA_HD) / NSA_HD, F32)


def _nsa_prep_body(zq_ref, zc_ref, zs_ref, zw_ref, zg_ref, bd_ref, qn_ref, kns_ref, knw_ref,
                   q_ref, cmp_ref, sel_ref, win_ref, selb_ref, winb_ref, gate_ref, *kt_ref, transposed):
    bd = bd_ref[...]

    def head_norm(x, gain):
        ms = _dot_f32(x * x, bd)
        return x * lax.rsqrt(ms + EPS) * gain

    def emit(o_ref, rows):
        if transposed:
            o_ref[0] = rows.T
        else:
            o_ref[...] = rows

    for c in range(Q_W // LANES):
        sl = slice(c * LANES, (c + 1) * LANES)
        q_ref[:, sl] = (head_norm(zq_ref[:, sl], qn_ref[...]) * (NSA_HD ** -0.5)).astype(BF16)
    emit(cmp_ref, zc_ref[...])
    half = KV_W // 2
    for z_ref, kn_ref, o_ref, ob_ref in ((zs_ref, kns_ref, sel_ref, selb_ref), (zw_ref, knw_ref, win_ref, winb_ref)):
        kn = jnp.concatenate([head_norm(z_ref[:, c * LANES:(c + 1) * LANES], kn_ref[...])
                              for c in range(half // LANES)], axis=1)
        rows = jnp.concatenate([kn, z_ref[:, half:KV_W]], axis=1)
        emit(o_ref, rows)
        ob_ref[...] = rows.astype(BF16)
        if transposed and o_ref is sel_ref:
            kt_ref[0][0] = kn.T.astype(BF16)
    gate_ref[...] = _sigmoid(zg_ref[...])


def nsa_prep(z, q_norm, k_norm, batch=None, row_tile=256):
    n = z.shape[0]
    tm = min(row_tile, n)
    transposed = batch is not None

    def zspec(width, blk):
        return pl.BlockSpec((tm, width), lambda i, blk=blk: (i, blk))

    def const(shape):
        return pl.BlockSpec(shape, lambda i: (0,) * len(shape))

    def tile2(v):
        return jnp.concatenate([v, v]).reshape(1, LANES).astype(F32)

    def rows(w, dt):
        return jax.ShapeDtypeStruct((n, w), dt)

    def out_spec(w):
        return pl.BlockSpec((tm, w), lambda i: (i, 0))

    if transposed:
        seq = n // batch
        tps = seq // tm
        kv_shape = jax.ShapeDtypeStruct((batch, KV_W, seq), F32)
        kv_spec = pl.BlockSpec((1, KV_W, tm), lambda i: (i // tps, 0, i % tps))
        extra_shape = (jax.ShapeDtypeStruct((batch, KV_W // 2, seq), BF16),)
        extra_spec = (pl.BlockSpec((1, KV_W // 2, tm), lambda i: (i // tps, 0, i % tps)),)
    else:
        kv_shape, kv_spec, extra_shape, extra_spec = rows(KV_W, F32), out_spec(KV_W), (), ()
    return pl.pallas_call(
        functools.partial(_nsa_prep_body, transposed=transposed),
        out_shape=(rows(Q_W, BF16), kv_shape, kv_shape, kv_shape, rows(KV_W, BF16), rows(KV_W, BF16),
                   rows(LANES, F32)) + extra_shape,
        grid=(n // tm,),
        in_specs=[zspec(Q_W, 0), zspec(KV_W, 2), zspec(KV_W, 3), zspec(KV_W, 4),
                  zspec(LANES, (Q_W + 3 * KV_W) // LANES), const((LANES, LANES)),
                  const((1, LANES)), const((1, LANES)), const((1, LANES))],
        out_specs=(out_spec(Q_W), kv_spec, kv_spec, kv_spec, out_spec(KV_W), out_spec(KV_W),
                   out_spec(LANES)) + extra_spec,
        compiler_params=_params(1), name="nsa_prep",
    )(z, z, z, z, z, _group_mean_matrix(), tile2(q_norm), tile2(k_norm[1]), tile2(k_norm[2]))


PAGES_PER_STEP = 16
PAGE_ROWS = 128
SUBS_PER_PAGE = PAGE_ROWS // CMP_STRIDE
P_W = 2 * NSA_KV_HEADS * 2 * CMP_HID
KV_PAIRS = NSA_KV_HEADS // 2


def _gelu_tanh(x):
    return 0.5 * x * (1.0 + jnp.tanh(math.sqrt(2.0 / math.pi) * (x + 0.044715 * x * x * x)))


def _compress_body(pt_ref, *refs, n_steps):
    page_refs = refs[:PAGES_PER_STEP]
    w1_ref, b1_ref, w2_ref, b2_ref, kn_ref, kc_ref, vc_ref, p_ref, x_s = refs[PAGES_PER_STEP:]
    j = pl.program_id(1)
    rows = PAGES_PER_STEP * SUBS_PER_PAGE
    r0 = pl.multiple_of(j * rows, rows)
    for t, pr in enumerate(page_refs):
        for v in range(2):
            for gp in range(KV_PAIRS):
                x_s[t, v * KV_PAIRS + gp] = pr[0, v, 2 * gp:2 * gp + 2].reshape(2 * NSA_HD, PAGE_ROWS).T
    for v in range(2):
        for gp in range(KV_PAIRS):
            acc = jnp.zeros((rows, 4 * CMP_HID), F32)
            for s in range(CMP_STRIDE):
                x = jnp.concatenate([x_s[t, v * KV_PAIRS + gp, pl.ds(s, SUBS_PER_PAGE, stride=CMP_STRIDE), :]
                                     for t in range(PAGES_PER_STEP)], axis=0)
                acc = acc + jnp.dot(x.astype(BF16), w1_ref[v, s], preferred_element_type=F32)
            p0 = (v * NSA_KV_HEADS + 2 * gp) * 2 * CMP_HID
            p_ref[pl.ds(r0, rows), p0:p0 + 4 * CMP_HID] = acc

    @pl.when(j == n_steps - 1)
    def _():
        n_sub = p_ref.shape[0]
        for v in range(2):
            for g in range(NSA_KV_HEADS):
                p0 = (v * NSA_KV_HEADS + g) * 2 * CMP_HID
                hs = p_ref[:, p0:p0 + CMP_HID] + pltpu.roll(p_ref[:, p0 + CMP_HID:p0 + 2 * CMP_HID], n_sub - 1, 0)
                hid = _gelu_tanh(hs + b1_ref[v:v + 1, :])
                out = _dot(hid, w2_ref[v]) + b2_ref[v:v + 1, :]
                if v == 0:
                    out = out * lax.rsqrt(jnp.mean(out * out, axis=-1, keepdims=True) + EPS) * kn_ref[...]
                    kc_ref[0, :, g * NSA_HD:(g + 1) * NSA_HD] = out
                else:
                    vc_ref[0, :, g * NSA_HD:(g + 1) * NSA_HD] = out


def nsa_compress(rows_t, page_table, w1, b1, w2, b2, k_norm0):
    if page_table is None:
        batch, n_pp = rows_t.shape[0], rows_t.shape[-1] // PAGE_ROWS
        page_table = jnp.zeros((1, 1), jnp.int32)

        def page_index(b, p, pt):
            return (b, 0, 0, 0, p)
    else:
        batch, n_pp = page_table.shape

        def page_index(b, p, pt):
            return (pt[b, p], 0, 0, 0, 0)
    n_steps = n_pp // PAGES_PER_STEP
    n_sub = n_pp * SUBS_PER_PAGE
    w = w1.reshape(2, 2, CMP_STRIDE, NSA_HD, CMP_HID)
    w = jnp.transpose(w, (0, 2, 3, 1, 4)).reshape(2, CMP_STRIDE, NSA_HD, 2 * CMP_HID)
    zero = jnp.zeros_like(w)
    wpair = jnp.concatenate([jnp.concatenate([w, zero], axis=-1), jnp.concatenate([zero, w], axis=-1)], axis=2)
    wpair = wpair.astype(BF16)

    def page_spec(t):
        return pl.BlockSpec((1, 2, NSA_KV_HEADS, NSA_HD, PAGE_ROWS),
                            lambda b, j, pt, t=t: page_index(b, j * PAGES_PER_STEP + t, pt))

    def const(shape):
        return pl.BlockSpec(shape, lambda b, j, pt: (0,) * len(shape))

    grid_spec = pltpu.PrefetchScalarGridSpec(
        num_scalar_prefetch=1, grid=(batch, n_steps),
        in_specs=[page_spec(t) for t in range(PAGES_PER_STEP)] + [
            const((2, CMP_STRIDE, LANES, 4 * CMP_HID)), const((2, CMP_HID)), const((2, CMP_HID, NSA_HD)),
            const((2, NSA_HD)), const((1, NSA_HD))],
        out_specs=(pl.BlockSpec((1, n_sub, KV_W // 2), lambda b, j, pt: (b, 0, 0)),
                   pl.BlockSpec((1, n_sub, KV_W // 2), lambda b, j, pt: (b, 0, 0))),
        scratch_shapes=[pltpu.VMEM((n_sub, P_W), F32),
                        pltpu.VMEM((PAGES_PER_STEP, 2 * KV_PAIRS, PAGE_ROWS, 2 * NSA_HD), F32)])
    return pl.pallas_call(
        functools.partial(_compress_body, n_steps=n_steps),
        out_shape=(jax.ShapeDtypeStruct((batch, n_sub, KV_W // 2), F32),
                   jax.ShapeDtypeStruct((batch, n_sub, KV_W // 2), F32)),
        grid_spec=grid_spec, compiler_params=_params(2), name="nsa_compress",
    )(page_table, *([rows_t] * PAGES_PER_STEP), wpair, b1, w2.astype(BF16), b2, k_norm0.reshape(1, NSA_HD))


Q_TILE = 128
WIN_BLOCKS = (WINDOW + Q_TILE) // Q_TILE
NEAR_TILES = 3
NEAR_CHUNKS = 8
M_INIT = -1e29


def _gather_bias(tbh, idx_slices):
    return jnp.concatenate([jnp.take_along_axis(tbh, idx, axis=1) for idx in idx_slices], axis=1)


def _nsa_attend_body(q_ref, gate_ref, x_ref, kc_ref, vc_ref, kst_ref, vs_ref, w0_ref, w1_ref, w2_ref, w3_ref, w4_ref,
                     idxc_ref, idxw_ref, nb_ref, mmat_ref, tb_ref, wout_ref, y_ref,
                     pc_s, oc_s, sel_s, m_s, l_s, a_s, acc_s, ps_s, pw_s, o_s, bsel_s, bwin_s, *, n_sel):
    win_refs = (w0_ref, w1_ref, w2_ref, w3_ref, w4_ref)
    t = pl.program_id(1)
    s0 = t * Q_TILE
    n_sub = kc_ref.shape[1]
    rep_rows = NSA_REP * Q_TILE
    q_pos = s0 + lax.broadcasted_iota(jnp.int32, (Q_TILE, LANES), 0)
    blk = lax.broadcasted_iota(jnp.int32, (Q_TILE, LANES), 1)
    cur = q_pos // SEL_BLOCK
    forced = (blk == 0) | (blk == cur) | (blk == cur - 1)
    future = blk * SEL_BLOCK > q_pos

    def group_q(g):
        return jnp.concatenate([q_ref[:, (g * NSA_REP + r) * NSA_HD:(g * NSA_REP + r + 1) * NSA_HD]
                                for r in range(NSA_REP)], axis=0)

    def head_table(h):
        return jnp.broadcast_to(tb_ref[h:h + 1, :], (Q_TILE, LANES))

    @pl.when((pl.program_id(0) == 0) & (t == 0))
    def _():
        for h in range(NSA_HEADS):
            tbh = head_table(h)
            far_bias = tb_ref[h:h + 1, REL_BUCKETS - 1:REL_BUCKETS]
            for k in range(NEAR_CHUNKS):
                bsel_s[h, k] = jnp.take_along_axis(tbh, nb_ref[k], axis=1) - far_bias
            bsel_s[h, NEAR_CHUNKS] = jnp.zeros((Q_TILE, LANES), F32)
            bsel_s[h, NEAR_CHUNKS + 1] = jnp.full((Q_TILE, LANES), NEG, F32)
            bwin_s[h] = _gather_bias(tbh, [idxw_ref[:, c * LANES:(c + 1) * LANES] for c in range(WIN_BLOCKS)])

    score_t = []
    for g in range(NSA_KV_HEADS):
        gs = slice(g * NSA_HD, (g + 1) * NSA_HD)
        sc = _dot_nt(group_q(g), kc_ref[0, :, gs])
        imp = jnp.zeros((Q_TILE, n_sub), F32)
        for r in range(NSA_REP):
            rs = slice(r * Q_TILE, (r + 1) * Q_TILE)
            bias = _gather_bias(head_table(g * NSA_REP + r),
                                [idxc_ref[0, :, c * LANES:(c + 1) * LANES] for c in range(n_sub // LANES)])
            s_r = sc[rs] + bias
            m = jnp.maximum(jnp.max(s_r, axis=1, keepdims=True), M_INIT)
            e = jnp.exp(s_r - m)
            p = e / jnp.maximum(jnp.sum(e, axis=1, keepdims=True), 1e-30)
            imp = imp + p
            pc_s[rs, :] = p.astype(BF16)
        oc_s[g] = jnp.dot(pc_s[...], vc_ref[0, :, gs], preferred_element_type=F32)
        score = _dot_f32(imp, mmat_ref[...])
        score = jnp.where(forced, FORCE_SCORE, score)
        score = jnp.where(future, NEG, score)
        score = jnp.where(blk >= n_sel, -jnp.inf, score)
        score_t.append(score.T)

    blk_t = lax.broadcasted_iota(jnp.int32, (LANES, Q_TILE), 0).astype(F32)
    sel_t = [jnp.zeros((LANES, Q_TILE), F32) for _ in range(NSA_KV_HEADS)]
    for _ in range(min(SEL_TOPK, n_sel)):
        for g in range(NSA_KV_HEADS):
            mx = jnp.max(score_t[g], axis=0, keepdims=True)
            first = jnp.min(jnp.where(score_t[g] == mx, blk_t, float(LANES)), axis=0, keepdims=True)
            pick = blk_t == first
            sel_t[g] = jnp.where(pick, 1.0, sel_t[g])
            score_t[g] = jnp.where(pick, -jnp.inf, score_t[g])
    for g in range(NSA_KV_HEADS):
        sel_s[g] = sel_t[g].T.astype(BF16)

    n_kt = (s0 + Q_TILE + KEY_TILE - 1) // KEY_TILE
    n_far = jnp.maximum(n_kt - NEAR_TILES, 0)
    blk_row = lax.broadcasted_iota(jnp.int32, (LANES, KEY_TILE), 0)
    key_col = lax.broadcasted_iota(jnp.int32, (LANES, KEY_TILE), 1)

    for g in range(NSA_KV_HEADS):
        gs = slice(g * NSA_HD, (g + 1) * NSA_HD)
        qg = group_q(g)

        m_s[...] = jnp.full((rep_rows, 1), M_INIT, F32)
        l_s[...] = jnp.zeros((rep_rows, 1), F32)
        acc_s[...] = jnp.zeros((rep_rows, NSA_HD), F32)

        def key_tile(kt, near, g=g, gs=gs, qg=qg):
            k0 = pl.multiple_of(kt * KEY_TILE, KEY_TILE)
            s = jnp.dot(qg, kst_ref[0, gs, pl.ds(k0, KEY_TILE)], preferred_element_type=F32)
            expand = jnp.where((k0 + key_col) // SEL_BLOCK == blk_row, 1.0, 0.0).astype(BF16)
            chosen = jnp.dot(sel_s[g], expand, preferred_element_type=F32)
            negm = (chosen - 1.0) * (-NEG)
            for r in range(NSA_REP):
                rs = slice(r * Q_TILE, (r + 1) * Q_TILE)
                h = g * NSA_REP + r
                if near:
                    chunks = []
                    for c in range(KEY_TILE // LANES):
                        k = (s0 - k0) // LANES - c
                        chunks.append(bsel_s[h, jnp.where(k < 0, NEAR_CHUNKS + 1, jnp.minimum(k, NEAR_CHUNKS))])
                    s_r = s[rs] + (negm + jnp.concatenate(chunks, axis=1))
                else:
                    s_r = s[rs] + negm
                m_old = m_s[rs]
                m_new = jnp.maximum(m_old, jnp.max(s_r, axis=1, keepdims=True))
                alpha = jnp.exp(m_old - m_new)
                p = jnp.exp(s_r - m_new)
                l_s[rs] = alpha * l_s[rs] + jnp.sum(p, axis=1, keepdims=True)
                m_s[rs] = m_new
                a_s[rs] = alpha
                ps_s[rs, :] = p.astype(BF16)
            acc_s[...] = a_s[...] * acc_s[...] + jnp.dot(ps_s[...], vs_ref[pl.ds(k0, KEY_TILE), gs],
                                                         preferred_element_type=F32)

        def far_body(kt, carry):
            key_tile(kt, False)
            return carry

        def near_body(kt, carry):
            key_tile(kt, True)
            return carry

        lax.fori_loop(0, n_far, far_body, 0)
        lax.fori_loop(n_far, n_kt, near_body, 0)

        kw = jnp.concatenate([wr[0, :, gs] for wr in win_refs], axis=0)
        vw = jnp.concatenate([wr[0, :, KV_W // 2 + g * NSA_HD:KV_W // 2 + (g + 1) * NSA_HD] for wr in win_refs], axis=0)
        sw = _dot_nt(qg, kw)
        for r in range(NSA_REP):
            rs = slice(r * Q_TILE, (r + 1) * Q_TILE)
            s_r = sw[rs] + bwin_s[g * NSA_REP + r]
            e = jnp.exp(s_r - jnp.max(s_r, axis=1, keepdims=True))
            pw_s[rs, :] = (e / jnp.sum(e, axis=1, keepdims=True)).astype(BF16)
        o_w = jnp.dot(pw_s[...], vw, preferred_element_type=F32)

        for r in range(NSA_REP):
            rs = slice(r * Q_TILE, (r + 1) * Q_TILE)
            h = g * NSA_REP + r
            o_h = (gate_ref[:, 3 * h:3 * h + 1] * oc_s[g, rs, :]
                   + gate_ref[:, 3 * h + 1:3 * h + 2] * (acc_s[rs, :] / l_s[rs])
                   + gate_ref[:, 3 * h + 2:3 * h + 3] * o_w[rs])
            o_s[:, h * NSA_HD:(h + 1) * NSA_HD] = o_h.astype(BF16)

    y_ref[...] = x_ref[...] + jnp.dot(o_s[...], wout_ref[...], preferred_element_type=F32)


def nsa_attend_prompt(q, gates, x, k_c, v_c, kst, selb, winb, rel_bias, w_out, batch, seq):
    n_qt = seq // Q_TILE
    n_sub = k_c.shape[1]
    n_sel = seq // SEL_BLOCK
    assert CMP_LEN == 2 * CMP_STRIDE and SEL_BLOCK == 4 * CMP_STRIDE and SEL_TOPK <= n_sel <= LANES
    win_pad = jnp.pad(winb.reshape(batch, seq, KV_W), ((0, 0), (WINDOW, 0), (0, 0)))
    iq = jnp.arange(Q_TILE, dtype=jnp.int32)
    dist_c = (jnp.arange(n_qt, dtype=jnp.int32)[:, None, None] * Q_TILE + iq[None, :, None]
              - (jnp.arange(n_sub, dtype=jnp.int32)[None, None, :] * CMP_STRIDE + CMP_LEN - 1))
    idx_c = _bucket_index(dist_c, dist_c >= 0)
    dist_w = iq[:, None] - jnp.arange(WINDOW + Q_TILE, dtype=jnp.int32)[None, :] + WINDOW
    idx_w = _bucket_index(dist_w, (dist_w >= 0) & (dist_w < WINDOW))
    dist_n = (jnp.arange(NEAR_CHUNKS, dtype=jnp.int32)[:, None, None] * LANES + iq[None, :, None]
              - jnp.arange(LANES, dtype=jnp.int32)[None, None, :])
    nb = _bucket_index(dist_n, dist_n >= 0)
    ci = np.arange(n_sub)[:, None]
    bj = np.arange(LANES)[None, :]
    mmat = ((ci // 4 == bj).astype(np.float32) + ((ci + 1) // 4 == bj).astype(np.float32)) * (ci < n_sub - 1)

    def rows(width):
        return pl.BlockSpec((Q_TILE, width), lambda b, t: (b * n_qt + t, 0))

    def per_b(shape):
        return pl.BlockSpec((1,) + shape, lambda b, t: (b,) + (0,) * len(shape))

    def const(shape):
        return pl.BlockSpec(shape, lambda b, t: (0,) * len(shape))

    in_specs = [rows(Q_W), rows(LANES), rows(D_MODEL), per_b((n_sub, KV_W // 2)), per_b((n_sub, KV_W // 2)),
                per_b((KV_W // 2, seq)), pl.BlockSpec((seq, KV_W // 2), lambda b, t: (b, 1))]
    in_specs += [pl.BlockSpec((1, Q_TILE, KV_W), lambda b, t, j=j: (b, t + j, 0)) for j in range(WIN_BLOCKS)]
    in_specs += [pl.BlockSpec((1, Q_TILE, n_sub), lambda b, t: (t, 0, 0)), const((Q_TILE, WINDOW + Q_TILE)),
                 const((NEAR_CHUNKS, Q_TILE, LANES)), const((n_sub, LANES)), const((NSA_HEADS, LANES)),
                 const((Q_W, D_MODEL))]
    rep_rows = NSA_REP * Q_TILE
    scratch = [pltpu.VMEM((rep_rows, n_sub), BF16), pltpu.VMEM((NSA_KV_HEADS, rep_rows, NSA_HD), F32),
               pltpu.VMEM((NSA_KV_HEADS, Q_TILE, LANES), BF16), pltpu.VMEM((rep_rows, 1), F32),
               pltpu.VMEM((rep_rows, 1), F32), pltpu.VMEM((rep_rows, 1), F32), pltpu.VMEM((rep_rows, NSA_HD), F32),
               pltpu.VMEM((rep_rows, KEY_TILE), BF16), pltpu.VMEM((rep_rows, WINDOW + Q_TILE), BF16),
               pltpu.VMEM((Q_TILE, Q_W), BF16),
               pltpu.VMEM((NSA_HEADS, NEAR_CHUNKS + 2, Q_TILE, LANES), F32),
               pltpu.VMEM((NSA_HEADS, Q_TILE, WINDOW + Q_TILE), F32)]
    return pl.pallas_call(
        functools.partial(_nsa_attend_body, n_sel=n_sel),
        out_shape=jax.ShapeDtypeStruct((batch * seq, D_MODEL), F32), grid=(batch, n_qt),
        in_specs=in_specs, out_specs=rows(D_MODEL), scratch_shapes=scratch,
        compiler_params=_params(2), name="nsa_attend_prompt",
    )(q, gates, x, k_c.astype(BF16), v_c.astype(BF16), kst, selb, *([win_pad] * WIN_BLOCKS),
      idx_c, idx_w, nb, jnp.asarray(mmat), _bias_table(rel_bias), w_out.astype(BF16))


SCORE_W = 384


def _nsa_sample_cmp_body(q_ref, kc_ref, vc_ref, idx_ref, mmat_ref, gsum_ref, tb_ref, oc_ref, top_ref, *,
                         n_sel, q_pos):
    n_sub = kc_ref.shape[1]
    q = q_ref[0]
    row_g = lax.broadcasted_iota(jnp.int32, (NSA_HEADS, 1), 0) // NSA_REP
    s = jnp.zeros((NSA_HEADS, n_sub), F32)
    for g in range(NSA_KV_HEADS):
        s = jnp.where(row_g == g, _dot_nt(q, kc_ref[0, :, g * NSA_HD:(g + 1) * NSA_HD]), s)
    tb = tb_ref[...]
    bias = jnp.concatenate([jnp.take_along_axis(
        tb, jnp.broadcast_to(idx_ref[:, c * LANES:(c + 1) * LANES], (NSA_HEADS, LANES)), axis=1)
        for c in range(n_sub // LANES)], axis=1)
    s = s + bias
    m = jnp.maximum(jnp.max(s, axis=1, keepdims=True), M_INIT)
    e = jnp.exp(s - m)
    p = e / jnp.maximum(jnp.sum(e, axis=1, keepdims=True), 1e-30)
    pb = p.astype(BF16)
    o = jnp.zeros((NSA_HEADS, NSA_HD), F32)
    for g in range(NSA_KV_HEADS):
        o = jnp.where(row_g == g, jnp.dot(pb, vc_ref[0, :, g * NSA_HD:(g + 1) * NSA_HD],
                                          preferred_element_type=F32), o)
    oc_ref[0] = o
    imp = _dot_f32(gsum_ref[...], p)
    score = _dot_f32(imp, mmat_ref[...])
    blk = lax.broadcasted_iota(jnp.int32, score.shape, 1)
    cur = q_pos // SEL_BLOCK
    score = jnp.where((blk == 0) | (blk == cur) | (blk == cur - 1), FORCE_SCORE, score)
    score = jnp.where(blk * SEL_BLOCK > q_pos, NEG, score)
    score = jnp.where(blk >= n_sel, -jnp.inf, score)
    blkf = blk.astype(F32)
    lane = lax.broadcasted_iota(jnp.int32, (SUBLANES, LANES), 1)
    top = jnp.zeros((SUBLANES, LANES), F32)
    for it in range(min(SEL_TOPK, n_sel)):
        mx = jnp.max(score, axis=1, keepdims=True)
        first = jnp.min(jnp.where(score == mx, blkf, float(SCORE_W)), axis=1, keepdims=True)
        top = jnp.where(lane == it, first, top)
        score = jnp.where(blkf == first, -jnp.inf, score)
    top_ref[0] = top.astype(jnp.int32)


def nsa_sample_cmp(q, k_c, v_c, rel_bias, past_len):
    batch = q.shape[0]
    n_sub = k_c.shape[1]
    n_sel = past_len // SEL_BLOCK + 1
    assert SEL_TOPK <= n_sel <= SCORE_W
    dist = past_len - (jnp.arange(n_sub, dtype=jnp.int32) * CMP_STRIDE + CMP_LEN - 1)
    idx = _bucket_index(dist, dist >= 0).reshape(1, n_sub)
    ci = np.arange(n_sub)[:, None]
    bj = np.arange(SCORE_W)[None, :]
    mmat = ((ci // 4 == bj).astype(np.float32) + ((ci + 1) // 4 == bj).astype(np.float32)) * (ci < n_sub - 1)
    gsum = (np.arange(SUBLANES)[:, None] == np.arange(NSA_HEADS)[None, :] // NSA_REP).astype(np.float32)

    def per_b(shape):
        return pl.BlockSpec((1,) + shape, lambda b: (b,) + (0,) * len(shape))

    def const(shape):
        return pl.BlockSpec(shape, lambda b: (0,) * len(shape))

    o_c, top = pl.pallas_call(
        functools.partial(_nsa_sample_cmp_body, n_sel=n_sel, q_pos=past_len),
        out_shape=(jax.ShapeDtypeStruct((batch, NSA_HEADS, NSA_HD), F32),
                   jax.ShapeDtypeStruct((batch, SUBLANES, LANES), jnp.int32)),
        grid=(batch,),
        in_specs=[per_b((NSA_HEADS, NSA_HD)), per_b((n_sub, KV_W // 2)), per_b((n_sub, KV_W // 2)),
                  const((1, n_sub)), const((n_sub, SCORE_W)), const((SUBLANES, NSA_HEADS)), const((NSA_HEADS, LANES))],
        out_specs=(per_b((NSA_HEADS, NSA_HD)), per_b((SUBLANES, LANES))),
        compiler_params=_params(1), name="nsa_sample_cmp",
    )(q, k_c.astype(BF16), v_c.astype(BF16), idx, jnp.asarray(mmat), jnp.asarray(gsum), _bias_table(rel_bias))
    return o_c, top[:, :NSA_KV_HEADS, :SEL_TOPK]


SEL_PER_STEP = 2


def _nsa_sample_attend_body(row_ref, sidx_ref, *refs, q_pos, n_past_blk, n_steps):
    n_blk_refs = NSA_KV_HEADS * SEL_PER_STEP
    blk_refs = refs[:n_blk_refs]
    (q_ref, knew_ref, win_ref, wnew_ref, gate_ref, oc_ref, idxw_ref, tb_ref, o_ref,
     s_s, v_s, ow_s) = refs[n_blk_refs:]
    b = pl.program_id(0)
    step = pl.program_id(1)
    q = q_ref[0]
    row_g = lax.broadcasted_iota(jnp.int32, (NSA_HEADS, 1), 0) // NSA_REP
    tb = tb_ref[...]
    half = KV_W // 2

    def by_group(fn):
        out = fn(0)
        for g in range(1, NSA_KV_HEADS):
            out = jnp.where(row_g == g, fn(g), out)
        return out

    @pl.when(step == 0)
    def _():
        s_w = by_group(lambda g: _dot(q, win_ref[0, 0, g]))
        bias = jnp.concatenate([jnp.take_along_axis(
            tb, jnp.broadcast_to(idxw_ref[:, c * LANES:(c + 1) * LANES], (NSA_HEADS, LANES)), axis=1)
            for c in range(s_w.shape[1] // LANES)], axis=1)
        s_w = s_w + bias
        k_new = by_group(lambda g: jnp.broadcast_to(wnew_ref[0, :, g * NSA_HD:(g + 1) * NSA_HD], (NSA_HEADS, NSA_HD)))
        v_new = by_group(lambda g: jnp.broadcast_to(wnew_ref[0, :, half + g * NSA_HD:half + (g + 1) * NSA_HD],
                                                    (NSA_HEADS, NSA_HD)))
        s_n = jnp.sum(q.astype(F32) * _r16(k_new), axis=1, keepdims=True) + tb[:, 0:1]
        m = jnp.maximum(jnp.max(s_w, axis=1, keepdims=True), s_n)
        e_w = jnp.exp(s_w - m)
        e_n = jnp.exp(s_n - m)
        total = jnp.sum(e_w, axis=1, keepdims=True) + e_n
        pb = (e_w / total).astype(BF16)
        pv = by_group(lambda g: _dot_nt(pb, win_ref[0, 1, g]))
        ow_s[...] = pv + _r16(e_n / total) * _r16(v_new)

    keys = SEL_PER_STEP * PAGE_ROWS
    key_lane = lax.broadcasted_iota(jnp.int32, (NSA_HD, PAGE_ROWS), 1)
    lane = lax.broadcasted_iota(jnp.int32, (1, PAGE_ROWS), 1)
    thresholds = _t5_thresholds()

    def tile_kv(g, kv):
        parts = []
        for j in range(SEL_PER_STEP):
            is_new = sidx_ref[b, g, step * SEL_PER_STEP + j] >= n_past_blk
            cached = blk_refs[g * SEL_PER_STEP + j][0, kv, g]
            fresh = jnp.where(key_lane == 0, knew_ref[0, kv * half + g * NSA_HD:kv * half + (g + 1) * NSA_HD, :], 0.0)
            parts.append(jnp.where(is_new, fresh, cached))
        return jnp.concatenate(parts, axis=1).astype(BF16)

    def tile_bias(g):
        parts = []
        for j in range(SEL_PER_STEP):
            blk = sidx_ref[b, g, step * SEL_PER_STEP + j]
            is_new = blk >= n_past_blk
            base = jnp.where(is_new, blk * SEL_BLOCK, (blk * SEL_BLOCK) // PAGE_ROWS * PAGE_ROWS)
            pos = base + lane
            dist = q_pos - pos
            idx = jnp.zeros((1, PAGE_ROWS), jnp.int32)
            for thr in thresholds:
                idx = idx + (dist >= thr).astype(jnp.int32)
            parts.append(jnp.where((dist >= 0) & (pos // SEL_BLOCK == blk), idx, MASK_BUCKET))
        idx = jnp.concatenate(parts, axis=1)
        return jnp.concatenate([jnp.take_along_axis(
            tb, jnp.broadcast_to(idx[:, c * LANES:(c + 1) * LANES], (NSA_HEADS, LANES)), axis=1)
            for c in range(keys // LANES)], axis=1)

    k0 = pl.multiple_of(step * keys, keys)
    s_s[:, pl.ds(k0, keys)] = by_group(lambda g: jnp.dot(q, tile_kv(g, 0), preferred_element_type=F32)
                                       + tile_bias(g))
    for g in range(NSA_KV_HEADS):
        v_s[g, :, pl.ds(k0, keys)] = tile_kv(g, 1)

    @pl.when(step == n_steps - 1)
    def _():
        s = s_s[...]
        e = jnp.exp(s - jnp.max(s, axis=1, keepdims=True))
        pb = (e / jnp.sum(e, axis=1, keepdims=True)).astype(BF16)
        o_sel = by_group(lambda g: _dot_nt(pb, v_s[g]))
        gate = gate_ref[0]
        o_ref[0] = gate[:, 0:1] * oc_ref[0] + gate[:, 1:2] * o_sel + gate[:, 2:3] * ow_s[...]


def nsa_sample_attend(q, kv_sel_new, kv_win_new, gates, o_c, top, cache_sel, win_buf, page_table, rel_bias,
                      past_len):
    batch = q.shape[0]
    n_past_blk = past_len // SEL_BLOCK
    bpp = PAGE_ROWS // SEL_BLOCK
    n_steps = SEL_TOPK // SEL_PER_STEP
    page_shape = cache_sel.shape[1:]
    jp = jnp.minimum(top, n_past_blk - 1)
    phys = jnp.take_along_axis(page_table, (jp // bpp).reshape(batch, -1), axis=1).reshape(top.shape)
    phys = phys.astype(jnp.int32)
    wb = win_buf.shape[-1]
    dist_w = past_len - (past_len - wb + jnp.arange(wb, dtype=jnp.int32))
    idx_w = _bucket_index(dist_w, (dist_w >= 0) & (dist_w < WINDOW)).reshape(1, wb)
    gate3 = gates[:, :3 * NSA_HEADS].reshape(batch, NSA_HEADS, 3)

    def blk_spec(g, j):
        return pl.BlockSpec((1,) + page_shape,
                            lambda b, s, rows, sidx, g=g, j=j: (rows[b, g, s * SEL_PER_STEP + j], 0, 0, 0, 0))

    def per_b(shape):
        return pl.BlockSpec((1,) + shape, lambda b, s, rows, sidx: (b,) + (0,) * len(shape))

    def const(shape):
        return pl.BlockSpec(shape, lambda b, s, rows, sidx: (0,) * len(shape))

    grid_spec = pltpu.PrefetchScalarGridSpec(
        num_scalar_prefetch=2, grid=(batch, n_steps),
        in_specs=[blk_spec(g, j) for g in range(NSA_KV_HEADS) for j in range(SEL_PER_STEP)] + [
            per_b((NSA_HEADS, NSA_HD)), per_b((KV_W, 1)), per_b(win_buf.shape[1:]), per_b((1, KV_W)),
            per_b((NSA_HEADS, 3)), per_b((NSA_HEADS, NSA_HD)), const((1, wb)), const((NSA_HEADS, LANES))],
        out_specs=per_b((NSA_HEADS, NSA_HD)),
        scratch_shapes=[pltpu.VMEM((NSA_HEADS, SEL_TOPK * PAGE_ROWS), F32),
                        pltpu.VMEM((NSA_KV_HEADS, NSA_HD, SEL_TOPK * PAGE_ROWS), BF16),
                        pltpu.VMEM((NSA_HEADS, NSA_HD), F32)])
    o = pl.pallas_call(
        functools.partial(_nsa_sample_attend_body, q_pos=past_len, n_past_blk=n_past_blk, n_steps=n_steps),
        out_shape=jax.ShapeDtypeStruct((batch, NSA_HEADS, NSA_HD), F32), grid_spec=grid_spec,
        compiler_params=_params(2), name="nsa_sample_attend",
    )(phys, top.astype(jnp.int32), *([cache_sel] * (NSA_KV_HEADS * SEL_PER_STEP)), q,
      kv_sel_new.reshape(batch, KV_W, 1), win_buf, kv_win_new.reshape(batch, 1, KV_W), gate3, o_c, idx_w,
      _bias_table(rel_bias))
    return o.reshape(batch, Q_W)


def _pad_cols(w, width):
    return jnp.pad(w, ((0, 0), (0, width - w.shape[1]))).astype(BF16)


def kernel(x_prompt, x_sample, state_ret, state_mlstm_C, state_mlstm_n, state_mlstm_m, state_conv, cache_nsa_cmp, cache_nsa_sel, state_nsa_win, page_table, rel_bias, norm_mix, norm_ffn, ab_w_in, ab_conv_w, ab_conv_b, ab_b_igate, ab_b_fgate, ab_gn_g, ab_gn_b, ab_hn_g, ab_w_out, nsa_w_in, nsa_q_norm, nsa_k_norm, nsa_cmp_w1, nsa_cmp_b1, nsa_cmp_w2, nsa_cmp_b2, nsa_w_out, moe_w_group, moe_b_group, moe_w_expert, moe_b_expert, moe_w_gate, moe_w_up, moe_w_down):
    bp, lp, d = x_prompt.shape
    bs, ls, _ = x_sample.shape
    page_size = cache_nsa_cmp.shape[2]
    past_len = page_table.shape[1] * page_size
    assert norm_mix.shape[0] == 2 and ls == 1 and d == D_MODEL and lp % KEY_TILE == 0
    xp = x_prompt.reshape(bp * lp, d)
    xs = x_sample.reshape(bs, d)

    def moe(x, layer):
        return hier_moe_residual(x, norm_ffn[layer], moe_w_group[layer], moe_b_group[layer], moe_w_expert[layer],
                                 moe_b_expert[layer], layer, moe_w_gate, moe_w_up, moe_w_down)

    w_in = _pad_cols(ab_w_in[0], AB_IN_PAD)
    ab = (ab_conv_w[0], ab_conv_b[0], ab_b_igate[0], ab_b_fgate[0], ab_gn_g[0], ab_gn_b[0], ab_hn_g[0])
    zp = norm_matmul(xp, norm_mix[0], w_in)
    xp, ret_p, mc_p, mn_p, mm_p, conv_p = ab_prompt(zp, xp, bp, lp, *ab, ab_w_out[0])
    zs = norm_matmul(xs, norm_mix[0], w_in)
    pos_s = past_len + jnp.arange(ls, dtype=jnp.int32)
    ys, ret_s, mc_s, mn_s, mm_s, conv_s = ab_sample(zs, pos_s, state_ret[0], state_mlstm_C[0], state_mlstm_n[0],
                                                    state_mlstm_m[0], state_conv[0], *ab)
    xs = matmul_residual(ys, ab_w_out[0].astype(BF16), xs)
    xp = moe(xp, 0)
    xs = moe(xs, 0)

    w_in = _pad_cols(nsa_w_in[0], NSA_IN_PAD)
    cmp_w = (nsa_cmp_w1[0], nsa_cmp_b1[0], nsa_cmp_w2[0], nsa_cmp_b2[0], nsa_k_norm[0, 0])
    kv_shape = (2, NSA_KV_HEADS, NSA_HD)
    zp = norm_matmul(xp, norm_mix[1], w_in)
    q, cmp_t, sel_t, win_t, selb, winb, gates, kst = nsa_prep(zp, nsa_q_norm[0], nsa_k_norm[0], batch=bp)
    k_c, v_c = nsa_compress(cmp_t.reshape((bp,) + kv_shape + (lp,)), None, *cmp_w)
    xp = nsa_attend_prompt(q, gates, xp, k_c, v_c, kst, selb, winb, rel_bias, nsa_w_out[0], bp, lp)
    win_keep = min(WINDOW, lp)

    def rows_major(t):
        return jnp.transpose(t.reshape((1, bp) + kv_shape + (t.shape[-1],)), (0, 1, 5, 2, 3, 4))

    cmp_p = rows_major(cmp_t)
    sel_p = rows_major(sel_t)
    win_p = rows_major(win_t[:, :, lp - win_keep:])

    def rows_minor(c):
        return jnp.moveaxis(c, -4, -1)

    zs = norm_matmul(xs, norm_mix[1], w_in)
    q, cmp_s, sel_s, win_s, _, _, gates = nsa_prep(zs, nsa_q_norm[0], nsa_k_norm[0])
    k_c, v_c = nsa_compress(rows_minor(cache_nsa_cmp[0]), page_table, *cmp_w)
    q3 = q.reshape(bs, NSA_HEADS, NSA_HD)
    o_c, top = nsa_sample_cmp(q3, k_c, v_c, rel_bias, past_len)
    win_buf = state_nsa_win[0]
    o = nsa_sample_attend(q3, sel_s, win_s, gates, o_c, top, rows_minor(cache_nsa_sel[0]), rows_minor(win_buf),
                          page_table, rel_bias, past_len)
    xs = matmul_residual(o, nsa_w_out[0].astype(BF16), xs)
    win_s = jnp.concatenate([win_buf, win_s.reshape((bs, ls) + kv_shape)], axis=1)[None, :, ls:]
    cmp_s = cmp_s.reshape((1, bs, ls) + kv_shape)
    sel_s = sel_s.reshape((1, bs, ls) + kv_shape)
    xp = moe(xp, 1)
    xs = moe(xs, 1)

    return (xp.reshape(bp, lp, d), xs.reshape(bs, ls, d), ret_p[None], ret_s[None], mc_p[None], mc_s[None],
            mn_p[None], mn_s[None], mm_p[None], mm_s[None], conv_p[None], conv_s[None],
            cmp_p, cmp_s, sel_p, sel_s, win_p, win_s)
```

```python
import functools
import math

import jax
import jax.numpy as jnp
import numpy as np
from jax import lax
from jax.experimental import pallas as pl
from jax.experimental.pallas import tpu as pltpu

F32 = jnp.float32
BF16 = jnp.bfloat16
LANES = 128
SUBLANES = 8
VMEM_LIMIT = 56 * 1024 * 1024

D_MODEL = 1024
RET_HEADS = 4
ML_HEADS = 4
HEAD_D = 128
CONV_W = 4
CHUNK = 128
ROPE_BASE = 10000.0
AB_IN = 4104
AB_IN_PAD = 4224
NSA_HEADS = 16
NSA_KV_HEADS = 4
NSA_REP = 4
NSA_HD = 64
NSA_IN_PAD = 2688
CMP_LEN = 32
CMP_STRIDE = 16
CMP_HID = 128
SEL_BLOCK = 64
SEL_TOPK = 16
WINDOW = 512
REL_BUCKETS = 32
REL_MAX_DIST = 1024
FORCE_SCORE = 1e4
MOE_GROUPS = 4
MOE_EXP_PER_GROUP = 8
MOE_EXPERTS = 32
NEG = -1e30
EPS = 1e-6


def _params(n_grid):
    return pltpu.CompilerParams(dimension_semantics=("arbitrary",) * n_grid, vmem_limit_bytes=VMEM_LIMIT)


def _dot(a, b):
    return jnp.dot(a.astype(BF16), b.astype(BF16), preferred_element_type=F32)


def _dot_nt(a, b):
    return lax.dot_general(a.astype(BF16), b.astype(BF16), (((1,), (1,)), ((), ())), preferred_element_type=F32)


def _dot_tn(a, b):
    return lax.dot_general(a.astype(BF16), b.astype(BF16), (((0,), (0,)), ((), ())), preferred_element_type=F32)


def _dot_f32(a, b):
    return jnp.dot(a, b, preferred_element_type=F32, precision=lax.Precision.HIGHEST)


def _r16(x):
    return x.astype(BF16).astype(F32)


def _sigmoid(x):
    return 1.0 / (1.0 + jnp.exp(-x))


def _silu(x):
    return x * _sigmoid(x)


def _log_sigmoid(x):
    return -(jnp.maximum(-x, 0.0) + jnp.log1p(jnp.exp(-jnp.abs(x))))


def _norm_matmul_body(x_ref, g_ref, w_ref, o_ref, *, col_tile):
    x = x_ref[...]
    y = x * lax.rsqrt(jnp.mean(x * x, axis=-1, keepdims=True) + EPS) * g_ref[...]
    yb = y.astype(BF16)
    for c0 in range(0, o_ref.shape[1], col_tile):
        o_ref[:, c0:c0 + col_tile] = jnp.dot(yb, w_ref[:, c0:c0 + col_tile], preferred_element_type=F32)


def norm_matmul(x, g, w, row_tile=256):
    n, d = x.shape
    c = w.shape[1]
    tm = min(row_tile, n)
    col_tile = 384 if c % 384 == 0 else LANES
    return pl.pallas_call(
        functools.partial(_norm_matmul_body, col_tile=col_tile),
        out_shape=jax.ShapeDtypeStruct((n, c), F32),
        grid=(n // tm,),
        in_specs=[pl.BlockSpec((tm, d), lambda i: (i, 0)),
                  pl.BlockSpec((1, d), lambda i: (0, 0)),
                  pl.BlockSpec((d, c), lambda i: (0, 0))],
        out_specs=pl.BlockSpec((tm, c), lambda i: (i, 0)),
        compiler_params=_params(1),
        name="norm_matmul",
    )(x, g.reshape(1, d), w)


def _retention_constants(c):
    h = np.arange(RET_HEADS, dtype=np.float64)
    log_g = np.log1p(-np.exp2(-5.0 - h))
    i = np.arange(c, dtype=np.float64)
    diff = i[:, None] - i[None, :]
    decay = np.where(diff >= 0, np.exp(np.maximum(diff, 0.0)[None] * log_g[:, None, None]), 0.0)
    q_dec = np.exp((i + 1.0)[None, :] * log_g[:, None])[:, :, None]
    k_dec = np.exp((c - 1.0 - i)[None, :] * log_g[:, None])[:, :, None]
    s_dec = np.exp(c * log_g)
    return (jnp.asarray(decay, F32), jnp.asarray(q_dec, F32), jnp.asarray(k_dec, F32),
            [float(v) for v in s_dec])


def _rope_tables(pos):
    half = HEAD_D // 2
    freqs = ROPE_BASE ** (-jnp.arange(half, dtype=F32) / half)
    ang = pos.astype(F32)[:, None] * freqs[None, :]
    cos, sin = jnp.cos(ang), jnp.sin(ang)
    return jnp.concatenate([cos, cos], axis=-1), jnp.concatenate([-sin, sin], axis=-1)


def _rope(x, cosf, sinf):
    return x * cosf + pltpu.roll(x, HEAD_D // 2, 1) * sinf


def _ab_prompt_body(rq_ref, rk_ref, rv_ref, rg_ref, mqk_ref, mv_ref, mo_ref, gz_ref, x_ref, cos_ref, sin_ref,
                    decay_ref, qdec_ref, kdec_ref, convw_ref, convb_ref, gbias_ref, gng_ref, gnb_ref, hng_ref,
                    wout_ref,
                    y_ref, s_ref, c_ref, n_ref, m_ref, conv_ref,
                    cbuf_ref, ycat_ref, *, s_dec):
    c = pl.program_id(1)
    tail = CONV_W - 1

    @pl.when(c == 0)
    def _():
        s_ref[...] = jnp.zeros_like(s_ref)
        c_ref[...] = jnp.zeros_like(c_ref)
        n_ref[...] = jnp.zeros_like(n_ref)
        m_ref[...] = jnp.zeros_like(m_ref)
        cbuf_ref[0:SUBLANES, :] = jnp.zeros((SUBLANES, cbuf_ref.shape[1]), F32)

    cosf = cos_ref[...]
    sinf = sin_ref[...]
    row = lax.broadcasted_iota(jnp.int32, (CHUNK, CHUNK), 0)
    col = lax.broadcasted_iota(jnp.int32, (CHUNK, CHUNK), 1)
    eye = row == col
    tril = row >= col
    triu = row <= col

    cbuf_ref[SUBLANES:SUBLANES + CHUNK, :] = mqk_ref[...]
    conv = convb_ref[...]
    for w in range(CONV_W):
        conv = conv + (_r16(cbuf_ref[SUBLANES - tail + w:SUBLANES - tail + w + CHUNK, :])
                       * _r16(convw_ref[w:w + 1, :]))
    qk = _silu(conv)
    last = cbuf_ref[CHUNK + SUBLANES - tail:CHUNK + SUBLANES, :]
    cbuf_ref[SUBLANES - tail:SUBLANES, :] = last
    conv_ref[0] = last

    gz = gz_ref[...] + gbias_ref[...]
    for h in range(RET_HEADS):
        sl = slice(h * HEAD_D, (h + 1) * HEAD_D)
        q = _rope(rq_ref[:, sl], cosf, sinf)
        k = _rope(rk_ref[:, sl], cosf, sinf) * (HEAD_D ** -0.5)
        v = rv_ref[:, sl]
        a = _dot_nt(q, k) * decay_ref[h]
        s_old = s_ref[0, h]
        o = _dot(a, v) + qdec_ref[h] * _dot(q, s_old)
        s_ref[0, h] = s_dec[h] * s_old + _dot_tn(k * kdec_ref[h], v)
        mu = jnp.mean(o, axis=-1, keepdims=True)
        var = jnp.mean(jnp.square(o - mu), axis=-1, keepdims=True)
        o = (o - mu) * lax.rsqrt(var + EPS) * gng_ref[:, sl] + gnb_ref[:, sl]
        ycat_ref[:, sl] = _silu(rg_ref[:, sl]) * o

        mq = qk[:, sl]
        mk = qk[:, ML_HEADS * HEAD_D + h * HEAD_D:ML_HEADS * HEAD_D + (h + 1) * HEAD_D] * (HEAD_D ** -0.5)
        mv = mv_ref[:, sl]
        i_col = gz[:, h:h + 1]
        f_col = _log_sigmoid(gz[:, ML_HEADS + h:ML_HEADS + h + 1])
        i_row = jnp.sum(jnp.where(eye, i_col, 0.0), axis=0, keepdims=True)
        f_row = jnp.sum(jnp.where(eye, f_col, 0.0), axis=0, keepdims=True)
        b_col = jnp.sum(jnp.where(tril, f_row, 0.0), axis=1, keepdims=True)
        b_row = jnp.sum(jnp.where(triu, f_col, 0.0), axis=0, keepdims=True)
        m_old = m_ref[0, h:h + 1, 0:1]
        dlog = jnp.where(tril, b_col - b_row + i_row, -jnp.inf)
        inter = b_col + m_old
        m_t = jnp.maximum(inter, jnp.max(dlog, axis=1, keepdims=True))
        wgt = _dot_nt(mq, mk) * jnp.exp(dlog - m_t)
        e_inter = jnp.exp(inter - m_t)
        c_old = c_ref[0, h]
        n_old = n_ref[0, h:h + 1, :]
        num = _dot(wgt, mv) + e_inter * _dot_nt(mq, c_old)
        den = (jnp.sum(wgt, axis=1, keepdims=True)
               + e_inter * jnp.sum(_r16(mq) * _r16(n_old), axis=1, keepdims=True))
        hc = num / jnp.maximum(jnp.abs(den), jnp.exp(-m_t))
        b_last = b_col[CHUNK - 1:CHUNK, :]
        u_row = b_last - b_row + i_row
        u_col = b_last - b_col + i_col
        m_new = jnp.maximum(b_last + m_old, jnp.max(u_row, axis=1, keepdims=True))
        ws_col = jnp.exp(u_col - m_new)
        f_state = jnp.exp(b_last + m_old - m_new)
        c_ref[0, h] = f_state * c_old + _dot_tn(mv * ws_col, mk)
        n_ref[0, h:h + 1, :] = f_state * n_old + jnp.sum(_r16(ws_col) * _r16(mk), axis=0, keepdims=True)
        m_ref[0, h:h + 1, :] = jnp.broadcast_to(m_new, (1, LANES))
        hm = _sigmoid(mo_ref[:, sl]) * hc
        hm = hm * lax.rsqrt(jnp.mean(hm * hm, axis=-1, keepdims=True) + EPS) * hng_ref[:, sl]
        ycat_ref[:, RET_HEADS * HEAD_D + h * HEAD_D:RET_HEADS * HEAD_D + (h + 1) * HEAD_D] = hm

    y_ref[...] = x_ref[...] + jnp.dot(ycat_ref[...].astype(BF16), wout_ref[...], preferred_element_type=F32)


def ab_prompt(z, x, batch, seq, conv_w, conv_b, b_ig, b_fg, gn_g, gn_b, hn_g, w_out):
    n_chunk = seq // CHUNK
    decay, q_dec, k_dec, s_dec = _retention_constants(CHUNK)
    cosf, sinf = _rope_tables(jnp.arange(seq, dtype=jnp.int32))
    gbias = jnp.zeros((1, LANES), F32).at[0, :ML_HEADS].set(b_ig).at[0, ML_HEADS:2 * ML_HEADS].set(b_fg)
    hw = RET_HEADS * HEAD_D
    qkw = 2 * ML_HEADS * HEAD_D

    def zspec(width, blk):
        return pl.BlockSpec((CHUNK, width), lambda b, c, blk=blk: (b * n_chunk + c, blk))

    def const(shape):
        return pl.BlockSpec(shape, lambda b, c: (0,) * len(shape))

    in_specs = [zspec(hw, 0), zspec(hw, 1), zspec(hw, 2), zspec(hw, 3), zspec(qkw, 2), zspec(hw, 6), zspec(hw, 7),
                zspec(LANES, (AB_IN_PAD - LANES) // LANES),
                pl.BlockSpec((CHUNK, D_MODEL), lambda b, c: (b * n_chunk + c, 0)),
                pl.BlockSpec((CHUNK, HEAD_D), lambda b, c: (c, 0)),
                pl.BlockSpec((CHUNK, HEAD_D), lambda b, c: (c, 0)),
                const((RET_HEADS, CHUNK, CHUNK)), const((RET_HEADS, CHUNK, 1)), const((RET_HEADS, CHUNK, 1)),
                const((CONV_W, qkw)), const((1, qkw)), const((1, LANES)),
                const((1, hw)), const((1, hw)), const((1, hw)), const((2 * hw, D_MODEL))]
    out_shape = (jax.ShapeDtypeStruct((batch * seq, D_MODEL), F32),
                 jax.ShapeDtypeStruct((batch, RET_HEADS, HEAD_D, HEAD_D), F32),
                 jax.ShapeDtypeStruct((batch, ML_HEADS, HEAD_D, HEAD_D), F32),
                 jax.ShapeDtypeStruct((batch, ML_HEADS, HEAD_D), F32),
                 jax.ShapeDtypeStruct((batch, SUBLANES, LANES), F32),
                 jax.ShapeDtypeStruct((batch, CONV_W - 1, qkw), F32))
    out_specs = (pl.BlockSpec((CHUNK, D_MODEL), lambda b, c: (b * n_chunk + c, 0)),
                 pl.BlockSpec((1, RET_HEADS, HEAD_D, HEAD_D), lambda b, c: (b, 0, 0, 0)),
                 pl.BlockSpec((1, ML_HEADS, HEAD_D, HEAD_D), lambda b, c: (b, 0, 0, 0)),
                 pl.BlockSpec((1, ML_HEADS, HEAD_D), lambda b, c: (b, 0, 0)),
                 pl.BlockSpec((1, SUBLANES, LANES), lambda b, c: (b, 0, 0)),
                 pl.BlockSpec((1, CONV_W - 1, qkw), lambda b, c: (b, 0, 0)))
    y, s, cc, n, m, conv = pl.pallas_call(
        functools.partial(_ab_prompt_body, s_dec=s_dec),
        out_shape=out_shape, grid=(batch, n_chunk), in_specs=in_specs, out_specs=out_specs,
        scratch_shapes=[pltpu.VMEM((CHUNK + SUBLANES, qkw), F32), pltpu.VMEM((CHUNK, 2 * hw), F32)],
        compiler_params=_params(2), name="ab_prompt",
    )(z, z, z, z, z, z, z, z, x, cosf, sinf, decay, q_dec, k_dec, conv_w, conv_b.reshape(1, qkw), gbias,
      gn_g.reshape(1, hw), gn_b.reshape(1, hw), hn_g.reshape(1, hw), w_out.astype(BF16))
    return y, s, cc, n, m[:, :ML_HEADS, 0], conv


def _matmul_residual_body(a_ref, w_ref, x_ref, o_ref):
    o_ref[...] = x_ref[...] + jnp.dot(a_ref[...].astype(BF16), w_ref[...], preferred_element_type=F32)


def matmul_residual(a, w, x, row_tile=256):
    n, kk = a.shape
    d = w.shape[1]
    tm = min(row_tile, n)
    return pl.pallas_call(
        _matmul_residual_body, out_shape=jax.ShapeDtypeStruct((n, d), F32), grid=(n // tm,),
        in_specs=[pl.BlockSpec((tm, kk), lambda i: (i, 0)), pl.BlockSpec((kk, d), lambda i: (0, 0)),
                  pl.BlockSpec((tm, d), lambda i: (i, 0))],
        out_specs=pl.BlockSpec((tm, d), lambda i: (i, 0)),
        compiler_params=_params(1), name="matmul_residual",
    )(a, w, x)


def _ab_sample_body(m0_ref, z_ref, cos_ref, sin_ref, s0_ref, c0_ref, n0_ref, conv0_ref,
                    convw_ref, convb_ref, gbias_ref, gng_ref, gnb_ref, hng_ref,
                    y_ref, s_ref, c_ref, n_ref, m_ref, conv_ref, *, g_dec):
    b = pl.program_id(0)
    hw = RET_HEADS * HEAD_D
    qkw = 2 * ML_HEADS * HEAD_D
    tail = CONV_W - 1
    cosf = cos_ref[...]
    sinf = sin_ref[...]
    row = lax.broadcasted_iota(jnp.int32, (HEAD_D, HEAD_D), 0)
    col = lax.broadcasted_iota(jnp.int32, (HEAD_D, HEAD_D), 1)
    eye = row == col

    def to_col(r):
        return jnp.sum(jnp.where(eye, r, 0.0), axis=1, keepdims=True)

    def to_row(cv):
        return jnp.sum(jnp.where(eye, cv, 0.0), axis=0, keepdims=True)

    mqk = z_ref[0, :, 4 * hw:4 * hw + qkw]
    conv = convb_ref[...] + mqk * convw_ref[tail:CONV_W, :]
    for w in range(tail):
        conv = conv + conv0_ref[0, w:w + 1, :] * convw_ref[w:w + 1, :]
    qk = _silu(conv)
    conv_ref[0, 0:tail - 1, :] = conv0_ref[0, 1:tail, :]
    conv_ref[0, tail - 1:tail, :] = mqk
    gz = z_ref[0, :, AB_IN_PAD - LANES:AB_IN_PAD] + gbias_ref[...]

    for h in range(RET_HEADS):
        sl = slice(h * HEAD_D, (h + 1) * HEAD_D)
        q = _rope(z_ref[0, :, sl], cosf, sinf)
        k = _rope(z_ref[0, :, hw + h * HEAD_D:hw + (h + 1) * HEAD_D], cosf, sinf) * (HEAD_D ** -0.5)
        v = z_ref[0, :, 2 * hw + h * HEAD_D:2 * hw + (h + 1) * HEAD_D]
        rg = z_ref[0, :, 3 * hw + h * HEAD_D:3 * hw + (h + 1) * HEAD_D]
        s_old = s0_ref[0, h]
        qk_s = jnp.sum(q * k, axis=1, keepdims=True)
        o = qk_s * v + g_dec[h] * jnp.sum(_r16(to_col(q)) * _r16(s_old), axis=0, keepdims=True)
        s_ref[0, h] = g_dec[h] * s_old + to_col(k) * v
        mu = jnp.mean(o, axis=-1, keepdims=True)
        var = jnp.mean(jnp.square(o - mu), axis=-1, keepdims=True)
        o = (o - mu) * lax.rsqrt(var + EPS) * gng_ref[:, sl] + gnb_ref[:, sl]
        y_ref[0, :, sl] = _silu(rg) * o

        mq = qk[:, sl]
        mk = qk[:, ML_HEADS * HEAD_D + h * HEAD_D:ML_HEADS * HEAD_D + (h + 1) * HEAD_D] * (HEAD_D ** -0.5)
        mv = z_ref[0, :, 4 * hw + qkw + h * HEAD_D:4 * hw + qkw + (h + 1) * HEAD_D]
        mo = z_ref[0, :, 5 * hw + qkw + h * HEAD_D:5 * hw + qkw + (h + 1) * HEAD_D]
        ig = gz[:, h:h + 1]
        lf = _log_sigmoid(gz[:, ML_HEADS + h:ML_HEADS + h + 1])
        m_old = m0_ref[b, h]
        inter = lf + m_old
        m_t = jnp.maximum(inter, ig)
        wgt = jnp.sum(mq * mk, axis=1, keepdims=True) * jnp.exp(ig - m_t)
        e_inter = jnp.exp(inter - m_t)
        c_old = c0_ref[0, h]
        n_old = n0_ref[0, h:h + 1, :]
        cq = to_row(jnp.sum(_r16(c_old) * _r16(mq), axis=1, keepdims=True))
        num = wgt * mv + e_inter * cq
        den = wgt + e_inter * jnp.sum(n_old * mq, axis=1, keepdims=True)
        hc = num / jnp.maximum(jnp.abs(den), jnp.exp(-m_t))
        ws = jnp.exp(ig - m_t)
        c_ref[0, h] = e_inter * c_old + (ws * to_col(mv)) * mk
        n_ref[0, h:h + 1, :] = e_inter * n_old + ws * mk
        m_ref[0, h:h + 1, :] = jnp.broadcast_to(m_t, (1, LANES))
        hm = _sigmoid(mo) * hc
        hm = hm * lax.rsqrt(jnp.mean(hm * hm, axis=-1, keepdims=True) + EPS) * hng_ref[:, sl]
        y_ref[0, :, hw + h * HEAD_D:hw + (h + 1) * HEAD_D] = hm
    m_ref[0, ML_HEADS:SUBLANES, :] = jnp.zeros((SUBLANES - ML_HEADS, LANES), F32)


def ab_sample(z, pos, s0, c0, n0, m0, conv0, conv_w, conv_b, b_ig, b_fg, gn_g, gn_b, hn_g):
    batch = z.shape[0]
    h = np.arange(RET_HEADS, dtype=np.float64)
    g_dec = [float(v) for v in np.exp(np.log1p(-np.exp2(-5.0 - h)))]
    cosf, sinf = _rope_tables(pos)
    gbias = jnp.zeros((1, LANES), F32).at[0, :ML_HEADS].set(b_ig).at[0, ML_HEADS:2 * ML_HEADS].set(b_fg)
    hw = RET_HEADS * HEAD_D
    qkw = 2 * ML_HEADS * HEAD_D

    def per_b(shape):
        return pl.BlockSpec((1,) + shape, lambda b: (b,) + (0,) * len(shape))

    def const(shape):
        return pl.BlockSpec(shape, lambda b: (0,) * len(shape))

    in_specs = [pl.BlockSpec(memory_space=pltpu.SMEM), per_b((1, AB_IN_PAD)), const((1, HEAD_D)), const((1, HEAD_D)),
                per_b((RET_HEADS, HEAD_D, HEAD_D)), per_b((ML_HEADS, HEAD_D, HEAD_D)), per_b((ML_HEADS, HEAD_D)),
                per_b((CONV_W - 1, qkw)), const((CONV_W, qkw)), const((1, qkw)), const((1, LANES)),
                const((1, hw)), const((1, hw)), const((1, hw))]
    out_shape = (jax.ShapeDtypeStruct((batch, 1, 2 * hw), F32),
                 jax.ShapeDtypeStruct((batch, RET_HEADS, HEAD_D, HEAD_D), F32),
                 jax.ShapeDtypeStruct((batch, ML_HEADS, HEAD_D, HEAD_D), F32),
                 jax.ShapeDtypeStruct((batch, ML_HEADS, HEAD_D), F32),
                 jax.ShapeDtypeStruct((batch, SUBLANES, LANES), F32),
                 jax.ShapeDtypeStruct((batch, CONV_W - 1, qkw), F32))
    out_specs = (per_b((1, 2 * hw)), per_b((RET_HEADS, HEAD_D, HEAD_D)), per_b((ML_HEADS, HEAD_D, HEAD_D)),
                 per_b((ML_HEADS, HEAD_D)), per_b((SUBLANES, LANES)), per_b((CONV_W - 1, qkw)))
    y, s, cc, n, m, conv = pl.pallas_call(
        functools.partial(_ab_sample_body, g_dec=g_dec),
        out_shape=out_shape, grid=(batch,), in_specs=in_specs, out_specs=out_specs,
        compiler_params=_params(1), name="ab_sample",
    )(m0, z.reshape(batch, 1, AB_IN_PAD), cosf, sinf, s0, c0, n0, conv0, conv_w, conv_b.reshape(1, qkw), gbias,
      gn_g.reshape(1, hw), gn_b.reshape(1, hw), hn_g.reshape(1, hw))
    return y.reshape(batch, 2 * hw), s, cc, n, m[:, :ML_HEADS, 0], conv


MOE_TILE = 256


def _moe_router_body(x_ref, g_ref, wr_ref, br_ref, hn_ref, route_ref, count_ref):
    x = x_ref[...]
    hn = x * lax.rsqrt(jnp.mean(x * x, axis=-1, keepdims=True) + EPS) * g_ref[...]
    hn_ref[...] = hn
    z = _dot(hn, wr_ref[...]) + br_ref[...]
    lane = lax.broadcasted_iota(jnp.int32, z.shape, 1)
    lanef = lane.astype(F32)
    is_group = lane < MOE_GROUPS
    gl = jnp.where(is_group, z, -jnp.inf)
    gmax = jnp.max(gl, axis=1, keepdims=True)
    g_top = jnp.min(jnp.where(gl == gmax, lanef, float(LANES)), axis=1, keepdims=True)
    pg_top = 1.0 / jnp.sum(jnp.where(is_group, jnp.exp(z - gmax), 0.0), axis=1, keepdims=True)
    grp = ((lane - MOE_GROUPS) // MOE_EXP_PER_GROUP).astype(F32)
    in_group = (lane >= MOE_GROUPS) & (lane < MOE_GROUPS + MOE_EXPERTS) & (grp == g_top)
    el = jnp.where(in_group, z, -jnp.inf)
    v1 = jnp.max(el, axis=1, keepdims=True)
    i1 = jnp.min(jnp.where(el == v1, lanef, float(LANES)), axis=1, keepdims=True)
    el2 = jnp.where(lanef == i1, -jnp.inf, el)
    v2 = jnp.max(el2, axis=1, keepdims=True)
    i2 = jnp.min(jnp.where(el2 == v2, lanef, float(LANES)), axis=1, keepdims=True)
    t = jnp.exp(v2 - v1)
    p1 = 1.0 / (1.0 + t)
    out = jnp.where(lane == 0, i1 - MOE_GROUPS,
                    jnp.where(lane == 1, i2 - MOE_GROUPS,
                              jnp.where(lane == 2, pg_top * p1,
                                        jnp.where(lane == 3, pg_top * (t * p1), 0.0))))
    route_ref[...] = out
    picked = jnp.where((lanef == i1 - MOE_GROUPS) | (lanef == i2 - MOE_GROUPS), 1.0, 0.0)

    @pl.when(pl.program_id(0) == 0)
    def _():
        count_ref[...] = jnp.zeros_like(count_ref)

    count_ref[...] += jnp.sum(picked, axis=0, keepdims=True)


def moe_router(x, g, w_group, b_group, w_expert, b_expert, row_tile=256):
    n, d = x.shape
    tm = min(row_tile, n)
    used = MOE_GROUPS + MOE_EXPERTS
    wr = jnp.pad(jnp.concatenate([w_group, w_expert], axis=1), ((0, 0), (0, LANES - used)))
    br = jnp.pad(jnp.concatenate([b_group, b_expert]), (0, LANES - used)).reshape(1, LANES)
    hn, route, count = pl.pallas_call(
        _moe_router_body,
        out_shape=(jax.ShapeDtypeStruct((n, d), F32), jax.ShapeDtypeStruct((n, LANES), F32),
                   jax.ShapeDtypeStruct((1, LANES), F32)),
        grid=(n // tm,),
        in_specs=[pl.BlockSpec((tm, d), lambda i: (i, 0)), pl.BlockSpec((1, d), lambda i: (0, 0)),
                  pl.BlockSpec((d, LANES), lambda i: (0, 0)), pl.BlockSpec((1, LANES), lambda i: (0, 0))],
        out_specs=(pl.BlockSpec((tm, d), lambda i: (i, 0)), pl.BlockSpec((tm, LANES), lambda i: (i, 0)),
                   pl.BlockSpec((1, LANES), lambda i: (0, 0))),
        compiler_params=_params(1), name="moe_router",
    )(x, g.reshape(1, d), wr, br)
    return hn, route, route[:, 0:2].astype(jnp.int32), count[0, :MOE_EXPERTS].astype(jnp.int32)


def _moe_ffn_body(blk_e_ref, n_real_ref, asg_ref, hn_ref, wg_ref, wu_ref, wd_ref, o_ref,
                  x_s, y_s, wg_s, wu_s, wd_s, gsem, ssem, *, n_blk):
    i = pl.program_id(0)
    slot = i % 2

    def gather_row(tile, r, slot):
        tok = asg_ref[tile * MOE_TILE + r] // 2
        return pltpu.make_async_copy(hn_ref.at[pl.ds(tok, 1), :], x_s.at[slot, pl.ds(r, 1), :], gsem.at[slot])

    def scatter_row(r):
        a = asg_ref[i * MOE_TILE + r]
        return pltpu.make_async_copy(y_s.at[pl.ds(r, 1), :], o_ref.at[a % 2, pl.ds(a // 2, 1), :], ssem.at[0])

    def start_gather(tile, slot):
        def body(r, c):
            gather_row(tile, r, slot).start()
            return c
        lax.fori_loop(0, n_real_ref[tile], body, 0)

    @pl.when(i == 0)
    def _():
        x_s[...] = jnp.zeros_like(x_s)
        start_gather(0, 0)

    @pl.when(i + 1 < n_blk)
    def _():
        start_gather(i + 1, 1 - slot)

    prev = blk_e_ref[jnp.maximum(i - 1, 0)]

    @pl.when((i == 0) | (blk_e_ref[i] != prev))
    def _():
        wg_s[...] = wg_ref[0].astype(BF16)
        wu_s[...] = wu_ref[0].astype(BF16)
        wd_s[...] = wd_ref[0].astype(BF16)

    n_real = n_real_ref[i]

    def wait_gather(r, c):
        gather_row(i, r, slot).wait()
        return c

    lax.fori_loop(0, n_real, wait_gather, 0)

    @pl.when(n_real > 0)
    def _():
        x = x_s[slot].astype(BF16)
        hg = jnp.dot(x, wg_s[...], preferred_element_type=F32)
        hu = jnp.dot(x, wu_s[...], preferred_element_type=F32)
        hb = (_silu(hg) * hu).astype(BF16)
        y_s[...] = jnp.dot(hb, wd_s[...], preferred_element_type=F32)

        def start_scatter(r, c):
            scatter_row(r).start()
            return c

        def wait_scatter(r, c):
            scatter_row(r).wait()
            return c

        lax.fori_loop(0, n_real, start_scatter, 0)
        lax.fori_loop(0, n_real, wait_scatter, 0)


def moe_ffn(hn, asg, n_real, blk_e, layer, w_g, w_u, w_d):
    n_tok, d = hn.shape
    ff = w_g.shape[3]
    n_blk = blk_e.shape[0]
    grid_spec = pltpu.PrefetchScalarGridSpec(
        num_scalar_prefetch=3, grid=(n_blk,),
        in_specs=[pl.BlockSpec(memory_space=pl.ANY),
                  pl.BlockSpec((None, 1, d, ff), lambda i, e, nr, a: (layer, e[i], 0, 0)),
                  pl.BlockSpec((None, 1, d, ff), lambda i, e, nr, a: (layer, e[i], 0, 0)),
                  pl.BlockSpec((None, 1, ff, d), lambda i, e, nr, a: (layer, e[i], 0, 0))],
        out_specs=pl.BlockSpec(memory_space=pl.ANY),
        scratch_shapes=[pltpu.VMEM((2, MOE_TILE, d), F32), pltpu.VMEM((MOE_TILE, d), F32),
                        pltpu.VMEM((d, ff), BF16), pltpu.VMEM((d, ff), BF16), pltpu.VMEM((ff, d), BF16),
                        pltpu.SemaphoreType.DMA((2,)), pltpu.SemaphoreType.DMA((1,))])
    return pl.pallas_call(
        functools.partial(_moe_ffn_body, n_blk=n_blk),
        out_shape=jax.ShapeDtypeStruct((2, n_tok, d), F32), grid_spec=grid_spec,
        compiler_params=_params(1), name="moe_ffn",
    )(blk_e, n_real, asg, hn, w_g, w_u, w_d)


def _moe_combine_body(x_ref, route_ref, y0_ref, y1_ref, o_ref):
    o_ref[...] = x_ref[...] + (y0_ref[...] * route_ref[:, 2:3] + y1_ref[...] * route_ref[:, 3:4])


def moe_combine(x, route, y, row_tile=512):
    n, d = x.shape
    tm = min(row_tile, n)
    return pl.pallas_call(
        _moe_combine_body, out_shape=jax.ShapeDtypeStruct((n, d), F32), grid=(n // tm,),
        in_specs=[pl.BlockSpec((tm, d), lambda i: (i, 0)), pl.BlockSpec((tm, LANES), lambda i: (i, 0)),
                  pl.BlockSpec((None, tm, d), lambda i: (0, i, 0)), pl.BlockSpec((None, tm, d), lambda i: (1, i, 0))],
        out_specs=pl.BlockSpec((tm, d), lambda i: (i, 0)),
        compiler_params=_params(1), name="moe_combine",
    )(x, route, y, y)


def hier_moe_residual(x, g, w_group, b_group, w_expert, b_expert, layer, w_g, w_u, w_d):
    n, d = x.shape
    hn, route, expert, counts = moe_router(x, g, w_group, b_group, w_expert, b_expert)
    n_exp = w_g.shape[1]
    kk = expert.shape[1]
    assert kk == 2
    a = n * kk
    order = jnp.argsort(expert.reshape(-1)).astype(jnp.int32)
    starts = jnp.cumsum(counts) - counts
    padded = (counts + MOE_TILE - 1) // MOE_TILE * MOE_TILE
    pend = jnp.cumsum(padded)
    pstart = pend - padded
    n_blk = -(-(a + n_exp * (MOE_TILE - 1)) // MOE_TILE)
    tile_start = jnp.arange(n_blk, dtype=jnp.int32) * MOE_TILE
    blk_e = jnp.minimum(jnp.sum((pend[None, :] <= tile_start[:, None]).astype(jnp.int32), axis=1), n_exp - 1)
    n_real = jnp.clip(counts[blk_e] - (tile_start - pstart[blk_e]), 0, MOE_TILE).astype(jnp.int32)
    row_off = (tile_start - pstart[blk_e] + starts[blk_e])[:, None] + jnp.arange(MOE_TILE, dtype=jnp.int32)[None, :]
    asg = order[jnp.clip(row_off, 0, a - 1)].reshape(-1)
    y = moe_ffn(hn, asg, n_real, blk_e, layer, w_g, w_u, w_d)
    return moe_combine(x, route, y)


KV_W = 2 * NSA_KV_HEADS * NSA_HD
Q_W = NSA_HEADS * NSA_HD
MASK_BUCKET = REL_BUCKETS
KEY_TILE = 512


def _t5_thresholds():
    exact = REL_BUCKETS // 2
    dist = np.arange(0, 4 * REL_MAX_DIST, dtype=np.int64)
    nf = np.maximum(dist, 1).astype(np.float64)
    large = exact + np.floor(np.log(nf / exact) / math.log(REL_MAX_DIST / exact) * (REL_BUCKETS - exact) + 1e-9)
    bucket = np.where(dist < exact, dist, np.minimum(large, REL_BUCKETS - 1)).astype(np.int64)
    return [int(np.argmax(bucket >= b)) for b in range(1, REL_BUCKETS)]


def _bucket_index(dist, valid):
    idx = jnp.zeros(dist.shape, jnp.int32)
    for thr in _t5_thresholds():
        idx = idx + (dist >= thr).astype(jnp.int32)
    return jnp.where(valid, idx, MASK_BUCKET)


def _bias_table(rel_bias):
    t = jnp.zeros((NSA_HEADS, LANES), F32).at[:, :REL_BUCKETS].set(rel_bias.T.astype(F32))
    return t.at[:, MASK_BUCKET].set(NEG)


def _group_mean_matrix():
    i = np.arange(LANES)
    return jnp.asarray((i[:, None] // NSA_HD == i[None, :] // NSA_HD) / NSA_HD, F32)


def _nsa_prep_body(zq_ref, zc_ref, zs_ref, zw_ref, zg_ref, bd_ref, qn_ref, kns_ref, knw_ref,
                   q_ref, cmp_ref, sel_ref, win_ref, selb_ref, winb_ref, gate_ref, *kt_ref, transposed):
    bd = bd_ref[...]

    def head_norm(x, gain):
        ms = _dot_f32(x * x, bd)
        return x * lax.rsqrt(ms + EPS) * gain

    def emit(o_ref, rows):
        if transposed:
            o_ref[0] = rows.T
        else:
            o_ref[...] = rows

    for c in range(Q_W // LANES):
        sl = slice(c * LANES, (c + 1) * LANES)
        q_ref[:, sl] = (head_norm(zq_ref[:, sl], qn_ref[...]) * (NSA_HD ** -0.5)).astype(BF16)
    emit(cmp_ref, zc_ref[...])
    half = KV_W // 2
    for z_ref, kn_ref, o_ref, ob_ref in ((zs_ref, kns_ref, sel_ref, selb_ref), (zw_ref, knw_ref, win_ref, winb_ref)):
        kn = jnp.concatenate([head_norm(z_ref[:, c * LANES:(c + 1) * LANES], kn_ref[...])
                              for c in range(half // LANES)], axis=1)
        rows = jnp.concatenate([kn, z_ref[:, half:KV_W]], axis=1)
        emit(o_ref, rows)
        ob_ref[...] = rows.astype(BF16)
        if transposed and o_ref is sel_ref:
            kt_ref[0][0] = kn.T.astype(BF16)
    gate_ref[...] = _sigmoid(zg_ref[...])


def nsa_prep(z, q_norm, k_norm, batch=None, row_tile=256):
    n = z.shape[0]
    tm = min(row_tile, n)
    transposed = batch is not None

    def zspec(width, blk):
        return pl.BlockSpec((tm, width), lambda i, blk=blk: (i, blk))

    def const(shape):
        return pl.BlockSpec(shape, lambda i: (0,) * len(shape))

    def tile2(v):
        return jnp.concatenate([v, v]).reshape(1, LANES).astype(F32)

    def rows(w, dt):
        return jax.ShapeDtypeStruct((n, w), dt)

    def out_spec(w):
        return pl.BlockSpec((tm, w), lambda i: (i, 0))

    if transposed:
        seq = n // batch
        tps = seq // tm
        kv_shape = jax.ShapeDtypeStruct((batch, KV_W, seq), F32)
        kv_spec = pl.BlockSpec((1, KV_W, tm), lambda i: (i // tps, 0, i % tps))
        extra_shape = (jax.ShapeDtypeStruct((batch, KV_W // 2, seq), BF16),)
        extra_spec = (pl.BlockSpec((1, KV_W // 2, tm), lambda i: (i // tps, 0, i % tps)),)
    else:
        kv_shape, kv_spec, extra_shape, extra_spec = rows(KV_W, F32), out_spec(KV_W), (), ()
    return pl.pallas_call(
        functools.partial(_nsa_prep_body, transposed=transposed),
        out_shape=(rows(Q_W, BF16), kv_shape, kv_shape, kv_shape, rows(KV_W, BF16), rows(KV_W, BF16),
                   rows(LANES, F32)) + extra_shape,
        grid=(n // tm,),
        in_specs=[zspec(Q_W, 0), zspec(KV_W, 2), zspec(KV_W, 3), zspec(KV_W, 4),
                  zspec(LANES, (Q_W + 3 * KV_W) // LANES), const((LANES, LANES)),
                  const((1, LANES)), const((1, LANES)), const((1, LANES))],
        out_specs=(out_spec(Q_W), kv_spec, kv_spec, kv_spec, out_spec(KV_W), out_spec(KV_W),
                   out_spec(LANES)) + extra_spec,
        compiler_params=_params(1), name="nsa_prep",
    )(z, z, z, z, z, _group_mean_matrix(), tile2(q_norm), tile2(k_norm[1]), tile2(k_norm[2]))


PAGES_PER_STEP = 16
PAGE_ROWS = 128
SUBS_PER_PAGE = PAGE_ROWS // CMP_STRIDE
P_W = 2 * NSA_KV_HEADS * 2 * CMP_HID
KV_PAIRS = NSA_KV_HEADS // 2


def _gelu_tanh(x):
    return 0.5 * x * (1.0 + jnp.tanh(math.sqrt(2.0 / math.pi) * (x + 0.044715 * x * x * x)))


def _compress_body(pt_ref, *refs, n_steps):
    page_refs = refs[:PAGES_PER_STEP]
    w1_ref, b1_ref, w2_ref, b2_ref, kn_ref, kc_ref, vc_ref, p_ref, x_s = refs[PAGES_PER_STEP:]
    j = pl.program_id(1)
    rows = PAGES_PER_STEP * SUBS_PER_PAGE
    r0 = pl.multiple_of(j * rows, rows)
    for t, pr in enumerate(page_refs):
        for v in range(2):
            for gp in range(KV_PAIRS):
                x_s[t, v * KV_PAIRS + gp] = pr[0, v, 2 * gp:2 * gp + 2].reshape(2 * NSA_HD, PAGE_ROWS).T
    for v in range(2):
        for gp in range(KV_PAIRS):
            acc = jnp.zeros((rows, 4 * CMP_HID), F32)
            for s in range(CMP_STRIDE):
                x = jnp.concatenate([x_s[t, v * KV_PAIRS + gp, pl.ds(s, SUBS_PER_PAGE, stride=CMP_STRIDE), :]
                                     for t in range(PAGES_PER_STEP)], axis=0)
                acc = acc + jnp.dot(x.astype(BF16), w1_ref[v, s], preferred_element_type=F32)
            p0 = (v * NSA_KV_HEADS + 2 * gp) * 2 * CMP_HID
            p_ref[pl.ds(r0, rows), p0:p0 + 4 * CMP_HID] = acc

    @pl.when(j == n_steps - 1)
    def _():
        n_sub = p_ref.shape[0]
        for v in range(2):
            for g in range(NSA_KV_HEADS):
                p0 = (v * NSA_KV_HEADS + g) * 2 * CMP_HID
                hs = p_ref[:, p0:p0 + CMP_HID] + pltpu.roll(p_ref[:, p0 + CMP_HID:p0 + 2 * CMP_HID], n_sub - 1, 0)
                hid = _gelu_tanh(hs + b1_ref[v:v + 1, :])
                out = _dot(hid, w2_ref[v]) + b2_ref[v:v + 1, :]
                if v == 0:
                    out = out * lax.rsqrt(jnp.mean(out * out, axis=-1, keepdims=True) + EPS) * kn_ref[...]
                    kc_ref[0, :, g * NSA_HD:(g + 1) * NSA_HD] = out
                else:
                    vc_ref[0, :, g * NSA_HD:(g + 1) * NSA_HD] = out


def nsa_compress(rows_t, page_table, w1, b1, w2, b2, k_norm0):
    if page_table is None:
        batch, n_pp = rows_t.shape[0], rows_t.shape[-1] // PAGE_ROWS
        page_table = jnp.zeros((1, 1), jnp.int32)

        def page_index(b, p, pt):
            return (b, 0, 0, 0, p)
    else:
        batch, n_pp = page_table.shape

        def page_index(b, p, pt):
            return (pt[b, p], 0, 0, 0, 0)
    n_steps = n_pp // PAGES_PER_STEP
    n_sub = n_pp * SUBS_PER_PAGE
    w = w1.reshape(2, 2, CMP_STRIDE, NSA_HD, CMP_HID)
    w = jnp.transpose(w, (0, 2, 3, 1, 4)).reshape(2, CMP_STRIDE, NSA_HD, 2 * CMP_HID)
    zero = jnp.zeros_like(w)
    wpair = jnp.concatenate([jnp.concatenate([w, zero], axis=-1), jnp.concatenate([zero, w], axis=-1)], axis=2)
    wpair = wpair.astype(BF16)

    def page_spec(t):
        return pl.BlockSpec((1, 2, NSA_KV_HEADS, NSA_HD, PAGE_ROWS),
                            lambda b, j, pt, t=t: page_index(b, j * PAGES_PER_STEP + t, pt))

    def const(shape):
        return pl.BlockSpec(shape, lambda b, j, pt: (0,) * len(shape))

    grid_spec = pltpu.PrefetchScalarGridSpec(
        num_scalar_prefetch=1, grid=(batch, n_steps),
        in_specs=[page_spec(t) for t in range(PAGES_PER_STEP)] + [
            const((2, CMP_STRIDE, LANES, 4 * CMP_HID)), const((2, CMP_HID)), const((2, CMP_HID, NSA_HD)),
            const((2, NSA_HD)), const((1, NSA_HD))],
        out_specs=(pl.BlockSpec((1, n_sub, KV_W // 2), lambda b, j, pt: (b, 0, 0)),
                   pl.BlockSpec((1, n_sub, KV_W // 2), lambda b, j, pt: (b, 0, 0))),
        scratch_shapes=[pltpu.VMEM((n_sub, P_W), F32),
                        pltpu.VMEM((PAGES_PER_STEP, 2 * KV_PAIRS, PAGE_ROWS, 2 * NSA_HD), F32)])
    return pl.pallas_call(
        functools.partial(_compress_body, n_steps=n_steps),
        out_shape=(jax.ShapeDtypeStruct((batch, n_sub, KV_W // 2), F32),
                   jax.ShapeDtypeStruct((batch, n_sub, KV_W // 2), F32)),
        grid_spec=grid_spec, compiler_params=_params(2), name="nsa_compress",
    )(page_table, *([rows_t] * PAGES_PER_STEP), wpair, b1, w2.astype(BF16), b2, k_norm0.reshape(1, NSA_HD))


Q_TILE = 128
WIN_BLOCKS = (WINDOW + Q_TILE) // Q_TILE
NEAR_TILES = 3
NEAR_CHUNKS = 8
M_INIT = -1e29


def _gather_bias(tbh, idx_slices):
    return jnp.concatenate([jnp.take_along_axis(tbh, idx, axis=1) for idx in idx_slices], axis=1)


def _nsa_attend_body(q_ref, gate_ref, x_ref, kc_ref, vc_ref, kst_ref, vs_ref, w0_ref, w1_ref, w2_ref, w3_ref, w4_ref,
                     idxc_ref, idxw_ref, nb_ref, mmat_ref, tb_ref, wout_ref, y_ref,
                     pc_s, oc_s, sel_s, m_s, l_s, a_s, acc_s, ps_s, pw_s, o_s, bsel_s, bwin_s, *, n_sel):
    win_refs = (w0_ref, w1_ref, w2_ref, w3_ref, w4_ref)
    t = pl.program_id(1)
    s0 = t * Q_TILE
    n_sub = kc_ref.shape[1]
    rep_rows = NSA_REP * Q_TILE
    q_pos = s0 + lax.broadcasted_iota(jnp.int32, (Q_TILE, LANES), 0)
    blk = lax.broadcasted_iota(jnp.int32, (Q_TILE, LANES), 1)
    cur = q_pos // SEL_BLOCK
    forced = (blk == 0) | (blk == cur) | (blk == cur - 1)
    future = blk * SEL_BLOCK > q_pos

    def group_q(g):
        return jnp.concatenate([q_ref[:, (g * NSA_REP + r) * NSA_HD:(g * NSA_REP + r + 1) * NSA_HD]
                                for r in range(NSA_REP)], axis=0)

    def head_table(h):
        return jnp.broadcast_to(tb_ref[h:h + 1, :], (Q_TILE, LANES))

    @pl.when((pl.program_id(0) == 0) & (t == 0))
    def _():
        for h in range(NSA_HEADS):
            tbh = head_table(h)
            far_bias = tb_ref[h:h + 1, REL_BUCKETS - 1:REL_BUCKETS]
            for k in range(NEAR_CHUNKS):
                bsel_s[h, k] = jnp.take_along_axis(tbh, nb_ref[k], axis=1) - far_bias
            bsel_s[h, NEAR_CHUNKS] = jnp.zeros((Q_TILE, LANES), F32)
            bsel_s[h, NEAR_CHUNKS + 1] = jnp.full((Q_TILE, LANES), NEG, F32)
            bwin_s[h] = _gather_bias(tbh, [idxw_ref[:, c * LANES:(c + 1) * LANES] for c in range(WIN_BLOCKS)])

    score_t = []
    for g in range(NSA_KV_HEADS):
        gs = slice(g * NSA_HD, (g + 1) * NSA_HD)
        sc = _dot_nt(group_q(g), kc_ref[0, :, gs])
        imp = jnp.zeros((Q_TILE, n_sub), F32)
        for r in range(NSA_REP):
            rs = slice(r * Q_TILE, (r + 1) * Q_TILE)
            bias = _gather_bias(head_table(g * NSA_REP + r),
                                [idxc_ref[0, :, c * LANES:(c + 1) * LANES] for c in range(n_sub // LANES)])
            s_r = sc[rs] + bias
            m = jnp.maximum(jnp.max(s_r, axis=1, keepdims=True), M_INIT)
            e = jnp.exp(s_r - m)
            p = e / jnp.maximum(jnp.sum(e, axis=1, keepdims=True), 1e-30)
            imp = imp + p
            pc_s[rs, :] = p.astype(BF16)
        oc_s[g] = jnp.dot(pc_s[...], vc_ref[0, :, gs], preferred_element_type=F32)
        score = _dot_f32(imp, mmat_ref[...])
        score = jnp.where(forced, FORCE_SCORE, score)
        score = jnp.where(future, NEG, score)
        score = jnp.where(blk >= n_sel, -jnp.inf, score)
        score_t.append(score.T)

    blk_t = lax.broadcasted_iota(jnp.int32, (LANES, Q_TILE), 0).astype(F32)
    sel_t = [jnp.zeros((LANES, Q_TILE), F32) for _ in range(NSA_KV_HEADS)]
    for _ in range(min(SEL_TOPK, n_sel)):
        for g in range(NSA_KV_HEADS):
            mx = jnp.max(score_t[g], axis=0, keepdims=True)
            first = jnp.min(jnp.where(score_t[g] == mx, blk_t, float(LANES)), axis=0, keepdims=True)
            pick = blk_t == first
            sel_t[g] = jnp.where(pick, 1.0, sel_t[g])
            score_t[g] = jnp.where(pick, -jnp.inf, score_t[g])
    for g in range(NSA_KV_HEADS):
        sel_s[g] = sel_t[g].T.astype(BF16)

    n_kt = (s0 + Q_TILE + KEY_TILE - 1) // KEY_TILE
    n_far = jnp.maximum(n_kt - NEAR_TILES, 0)
    blk_row = lax.broadcasted_iota(jnp.int32, (LANES, KEY_TILE), 0)
    key_col = lax.broadcasted_iota(jnp.int32, (LANES, KEY_TILE), 1)

    for g in range(NSA_KV_HEADS):
        gs = slice(g * NSA_HD, (g + 1) * NSA_HD)
        qg = group_q(g)

        m_s[...] = jnp.full((rep_rows, 1), M_INIT, F32)
        l_s[...] = jnp.zeros((rep_rows, 1), F32)
        acc_s[...] = jnp.zeros((rep_rows, NSA_HD), F32)

        def key_tile(kt, near, g=g, gs=gs, qg=qg):
            k0 = pl.multiple_of(kt * KEY_TILE, KEY_TILE)
            s = jnp.dot(qg, kst_ref[0, gs, pl.ds(k0, KEY_TILE)], preferred_element_type=F32)
            expand = jnp.where((k0 + key_col) // SEL_BLOCK == blk_row, 1.0, 0.0).astype(BF16)
            chosen = jnp.dot(sel_s[g], expand, preferred_element_type=F32)
            negm = (chosen - 1.0) * (-NEG)
            for r in range(NSA_REP):
                rs = slice(r * Q_TILE, (r + 1) * Q_TILE)
                h = g * NSA_REP + r
                if near:
                    chunks = []
                    for c in range(KEY_TILE // LANES):
                        k = (s0 - k0) // LANES - c
                        chunks.append(bsel_s[h, jnp.where(k < 0, NEAR_CHUNKS + 1, jnp.minimum(k, NEAR_CHUNKS))])
                    s_r = s[rs] + (negm + jnp.concatenate(chunks, axis=1))
                else:
                    s_r = s[rs] + negm
                m_old = m_s[rs]
                m_new = jnp.maximum(m_old, jnp.max(s_r, axis=1, keepdims=True))
                alpha = jnp.exp(m_old - m_new)
                p = jnp.exp(s_r - m_new)
                l_s[rs] = alpha * l_s[rs] + jnp.sum(p, axis=1, keepdims=True)
                m_s[rs] = m_new
                a_s[rs] = alpha
                ps_s[rs, :] = p.astype(BF16)
            acc_s[...] = a_s[...] * acc_s[...] + jnp.dot(ps_s[...], vs_ref[pl.ds(k0, KEY_TILE), gs],
                                                         preferred_element_type=F32)

        def far_body(kt, carry):
            key_tile(kt, False)
            return carry

        def near_body(kt, carry):
            key_tile(kt, True)
            return carry

        lax.fori_loop(0, n_far, far_body, 0)
        lax.fori_loop(n_far, n_kt, near_body, 0)

        kw = jnp.concatenate([wr[0, :, gs] for wr in win_refs], axis=0)
        vw = jnp.concatenate([wr[0, :, KV_W // 2 + g * NSA_HD:KV_W // 2 + (g + 1) * NSA_HD] for wr in win_refs], axis=0)
        sw = _dot_nt(qg, kw)
        for r in range(NSA_REP):
            rs = slice(r * Q_TILE, (r + 1) * Q_TILE)
            s_r = sw[rs] + bwin_s[g * NSA_REP + r]
            e = jnp.exp(s_r - jnp.max(s_r, axis=1, keepdims=True))
            pw_s[rs, :] = (e / jnp.sum(e, axis=1, keepdims=True)).astype(BF16)
        o_w = jnp.dot(pw_s[...], vw, preferred_element_type=F32)

        for r in range(NSA_REP):
            rs = slice(r * Q_TILE, (r + 1) * Q_TILE)
            h = g * NSA_REP + r
            o_h = (gate_ref[:, 3 * h:3 * h + 1] * oc_s[g, rs, :]
                   + gate_ref[:, 3 * h + 1:3 * h + 2] * (acc_s[rs, :] / l_s[rs])
                   + gate_ref[:, 3 * h + 2:3 * h + 3] * o_w[rs])
            o_s[:, h * NSA_HD:(h + 1) * NSA_HD] = o_h.astype(BF16)

    y_ref[...] = x_ref[...] + jnp.dot(o_s[...], wout_ref[...], preferred_element_type=F32)


def nsa_attend_prompt(q, gates, x, k_c, v_c, kst, selb, winb, rel_bias, w_out, batch, seq):
    n_qt = seq // Q_TILE
    n_sub = k_c.shape[1]
    n_sel = seq // SEL_BLOCK
    assert CMP_LEN == 2 * CMP_STRIDE and SEL_BLOCK == 4 * CMP_STRIDE and SEL_TOPK <= n_sel <= LANES
    win_pad = jnp.pad(winb.reshape(batch, seq, KV_W), ((0, 0), (WINDOW, 0), (0, 0)))
    iq = jnp.arange(Q_TILE, dtype=jnp.int32)
    dist_c = (jnp.arange(n_qt, dtype=jnp.int32)[:, None, None] * Q_TILE + iq[None, :, None]
              - (jnp.arange(n_sub, dtype=jnp.int32)[None, None, :] * CMP_STRIDE + CMP_LEN - 1))
    idx_c = _bucket_index(dist_c, dist_c >= 0)
    dist_w = iq[:, None] - jnp.arange(WINDOW + Q_TILE, dtype=jnp.int32)[None, :] + WINDOW
    idx_w = _bucket_index(dist_w, (dist_w >= 0) & (dist_w < WINDOW))
    dist_n = (jnp.arange(NEAR_CHUNKS, dtype=jnp.int32)[:, None, None] * LANES + iq[None, :, None]
              - jnp.arange(LANES, dtype=jnp.int32)[None, None, :])
    nb = _bucket_index(dist_n, dist_n >= 0)
    ci = np.arange(n_sub)[:, None]
    bj = np.arange(LANES)[None, :]
    mmat = ((ci // 4 == bj).astype(np.float32) + ((ci + 1) // 4 == bj).astype(np.float32)) * (ci < n_sub - 1)

    def rows(width):
        return pl.BlockSpec((Q_TILE, width), lambda b, t: (b * n_qt + t, 0))

    def per_b(shape):
        return pl.BlockSpec((1,) + shape, lambda b, t: (b,) + (0,) * len(shape))

    def const(shape):
        return pl.BlockSpec(shape, lambda b, t: (0,) * len(shape))

    in_specs = [rows(Q_W), rows(LANES), rows(D_MODEL), per_b((n_sub, KV_W // 2)), per_b((n_sub, KV_W // 2)),
                per_b((KV_W // 2, seq)), pl.BlockSpec((seq, KV_W // 2), lambda b, t: (b, 1))]
    in_specs += [pl.BlockSpec((1, Q_TILE, KV_W), lambda b, t, j=j: (b, t + j, 0)) for j in range(WIN_BLOCKS)]
    in_specs += [pl.BlockSpec((1, Q_TILE, n_sub), lambda b, t: (t, 0, 0)), const((Q_TILE, WINDOW + Q_TILE)),
                 const((NEAR_CHUNKS, Q_TILE, LANES)), const((n_sub, LANES)), const((NSA_HEADS, LANES)),
                 const((Q_W, D_MODEL))]
    rep_rows = NSA_REP * Q_TILE
    scratch = [pltpu.VMEM((rep_rows, n_sub), BF16), pltpu.VMEM((NSA_KV_HEADS, rep_rows, NSA_HD), F32),
               pltpu.VMEM((NSA_KV_HEADS, Q_TILE, LANES), BF16), pltpu.VMEM((rep_rows, 1), F32),
               pltpu.VMEM((rep_rows, 1), F32), pltpu.VMEM((rep_rows, 1), F32), pltpu.VMEM((rep_rows, NSA_HD), F32),
               pltpu.VMEM((rep_rows, KEY_TILE), BF16), pltpu.VMEM((rep_rows, WINDOW + Q_TILE), BF16),
               pltpu.VMEM((Q_TILE, Q_W), BF16),
               pltpu.VMEM((NSA_HEADS, NEAR_CHUNKS + 2, Q_TILE, LANES), F32),
               pltpu.VMEM((NSA_HEADS, Q_TILE, WINDOW + Q_TILE), F32)]
    return pl.pallas_call(
        functools.partial(_nsa_attend_body, n_sel=n_sel),
        out_shape=jax.ShapeDtypeStruct((batch * seq, D_MODEL), F32), grid=(batch, n_qt),
        in_specs=in_specs, out_specs=rows(D_MODEL), scratch_shapes=scratch,
        compiler_params=_params(2), name="nsa_attend_prompt",
    )(q, gates, x, k_c.astype(BF16), v_c.astype(BF16), kst, selb, *([win_pad] * WIN_BLOCKS),
      idx_c, idx_w, nb, jnp.asarray(mmat), _bias_table(rel_bias), w_out.astype(BF16))


SCORE_W = 384


def _nsa_sample_cmp_body(q_ref, kc_ref, vc_ref, idx_ref, mmat_ref, gsum_ref, tb_ref, oc_ref, top_ref, *,
                         n_sel, q_pos):
    n_sub = kc_ref.shape[1]
    q = q_ref[0]
    row_g = lax.broadcasted_iota(jnp.int32, (NSA_HEADS, 1), 0) // NSA_REP
    s = jnp.zeros((NSA_HEADS, n_sub), F32)
    for g in range(NSA_KV_HEADS):
        s = jnp.where(row_g == g, _dot_nt(q, kc_ref[0, :, g * NSA_HD:(g + 1) * NSA_HD]), s)
    tb = tb_ref[...]
    bias = jnp.concatenate([jnp.take_along_axis(
        tb, jnp.broadcast_to(idx_ref[:, c * LANES:(c + 1) * LANES], (NSA_HEADS, LANES)), axis=1)
        for c in range(n_sub // LANES)], axis=1)
    s = s + bias
    m = jnp.maximum(jnp.max(s, axis=1, keepdims=True), M_INIT)
    e = jnp.exp(s - m)
    p = e / jnp.maximum(jnp.sum(e, axis=1, keepdims=True), 1e-30)
    pb = p.astype(BF16)
    o = jnp.zeros((NSA_HEADS, NSA_HD), F32)
    for g in range(NSA_KV_HEADS):
        o = jnp.where(row_g == g, jnp.dot(pb, vc_ref[0, :, g * NSA_HD:(g + 1) * NSA_HD],
                                          preferred_element_type=F32), o)
    oc_ref[0] = o
    imp = _dot_f32(gsum_ref[...], p)
    score = _dot_f32(imp, mmat_ref[...])
    blk = lax.broadcasted_iota(jnp.int32, score.shape, 1)
    cur = q_pos // SEL_BLOCK
    score = jnp.where((blk == 0) | (blk == cur) | (blk == cur - 1), FORCE_SCORE, score)
    score = jnp.where(blk * SEL_BLOCK > q_pos, NEG, score)
    score = jnp.where(blk >= n_sel, -jnp.inf, score)
    blkf = blk.astype(F32)
    lane = lax.broadcasted_iota(jnp.int32, (SUBLANES, LANES), 1)
    top = jnp.zeros((SUBLANES, LANES), F32)
    for it in range(min(SEL_TOPK, n_sel)):
        mx = jnp.max(score, axis=1, keepdims=True)
        first = jnp.min(jnp.where(score == mx, blkf, float(SCORE_W)), axis=1, keepdims=True)
        top = jnp.where(lane == it, first, top)
        score = jnp.where(blkf == first, -jnp.inf, score)
    top_ref[0] = top.astype(jnp.int32)


def nsa_sample_cmp(q, k_c, v_c, rel_bias, past_len):
    batch = q.shape[0]
    n_sub = k_c.shape[1]
    n_sel = past_len // SEL_BLOCK + 1
    assert SEL_TOPK <= n_sel <= SCORE_W
    dist = past_len - (jnp.arange(n_sub, dtype=jnp.int32) * CMP_STRIDE + CMP_LEN - 1)
    idx = _bucket_index(dist, dist >= 0).reshape(1, n_sub)
    ci = np.arange(n_sub)[:, None]
    bj = np.arange(SCORE_W)[None, :]
    mmat = ((ci // 4 == bj).astype(np.float32) + ((ci + 1) // 4 == bj).astype(np.float32)) * (ci < n_sub - 1)
    gsum = (np.arange(SUBLANES)[:, None] == np.arange(NSA_HEADS)[None, :] // NSA_REP).astype(np.float32)

    def per_b(shape):
        return pl.BlockSpec((1,) + shape, lambda b: (b,) + (0,) * len(shape))

    def const(shape):
        return pl.BlockSpec(shape, lambda b: (0,) * len(shape))

    o_c, top = pl.pallas_call(
        functools.partial(_nsa_sample_cmp_body, n_sel=n_sel, q_pos=past_len),
        out_shape=(jax.ShapeDtypeStruct((batch, NSA_HEADS, NSA_HD), F32),
                   jax.ShapeDtypeStruct((batch, SUBLANES, LANES), jnp.int32)),
        grid=(batch,),
        in_specs=[per_b((NSA_HEADS, NSA_HD)), per_b((n_sub, KV_W // 2)), per_b((n_sub, KV_W // 2)),
                  const((1, n_sub)), const((n_sub, SCORE_W)), const((SUBLANES, NSA_HEADS)), const((NSA_HEADS, LANES))],
        out_specs=(per_b((NSA_HEADS, NSA_HD)), per_b((SUBLANES, LANES))),
        compiler_params=_params(1), name="nsa_sample_cmp",
    )(q, k_c.astype(BF16), v_c.astype(BF16), idx, jnp.asarray(mmat), jnp.asarray(gsum), _bias_table(rel_bias))
    return o_c, top[:, :NSA_KV_HEADS, :SEL_TOPK]


SEL_PER_STEP = 2


def _nsa_sample_attend_body(row_ref, sidx_ref, *refs, q_pos, n_past_blk, n_steps):
    n_blk_refs = NSA_KV_HEADS * SEL_PER_STEP
    blk_refs = refs[:n_blk_refs]
    (q_ref, knew_ref, win_ref, wnew_ref, gate_ref, oc_ref, idxw_ref, tb_ref, o_ref,
     s_s, v_s, ow_s) = refs[n_blk_refs:]
    b = pl.program_id(0)
    step = pl.program_id(1)
    q = q_ref[0]
    row_g = lax.broadcasted_iota(jnp.int32, (NSA_HEADS, 1), 0) // NSA_REP
    tb = tb_ref[...]
    half = KV_W // 2

    def by_group(fn):
        out = fn(0)
        for g in range(1, NSA_KV_HEADS):
            out = jnp.where(row_g == g, fn(g), out)
        return out

    @pl.when(step == 0)
    def _():
        s_w = by_group(lambda g: _dot(q, win_ref[0, 0, g]))
        bias = jnp.concatenate([jnp.take_along_axis(
            tb, jnp.broadcast_to(idxw_ref[:, c * LANES:(c + 1) * LANES], (NSA_HEADS, LANES)), axis=1)
            for c in range(s_w.shape[1] // LANES)], axis=1)
        s_w = s_w + bias
        k_new = by_group(lambda g: jnp.broadcast_to(wnew_ref[0, :, g * NSA_HD:(g + 1) * NSA_HD], (NSA_HEADS, NSA_HD)))
        v_new = by_group(lambda g: jnp.broadcast_to(wnew_ref[0, :, half + g * NSA_HD:half + (g + 1) * NSA_HD],
                                                    (NSA_HEADS, NSA_HD)))
        s_n = jnp.sum(q.astype(F32) * _r16(k_new), axis=1, keepdims=True) + tb[:, 0:1]
        m = jnp.maximum(jnp.max(s_w, axis=1, keepdims=True), s_n)
        e_w = jnp.exp(s_w - m)
        e_n = jnp.exp(s_n - m)
        total = jnp.sum(e_w, axis=1, keepdims=True) + e_n
        pb = (e_w / total).astype(BF16)
        pv = by_group(lambda g: _dot_nt(pb, win_ref[0, 1, g]))
        ow_s[...] = pv + _r16(e_n / total) * _r16(v_new)

    keys = SEL_PER_STEP * PAGE_ROWS
    key_lane = lax.broadcasted_iota(jnp.int32, (NSA_HD, PAGE_ROWS), 1)
    lane = lax.broadcasted_iota(jnp.int32, (1, PAGE_ROWS), 1)
    thresholds = _t5_thresholds()

    def tile_kv(g, kv):
        parts = []
        for j in range(SEL_PER_STEP):
            is_new = sidx_ref[b, g, step * SEL_PER_STEP + j] >= n_past_blk
            cached = blk_refs[g * SEL_PER_STEP + j][0, kv, g]
            fresh = jnp.where(key_lane == 0, knew_ref[0, kv * half + g * NSA_HD:kv * half + (g + 1) * NSA_HD, :], 0.0)
            parts.append(jnp.where(is_new, fresh, cached))
        return jnp.concatenate(parts, axis=1).astype(BF16)

    def tile_bias(g):
        parts = []
        for j in range(SEL_PER_STEP):
            blk = sidx_ref[b, g, step * SEL_PER_STEP + j]
            is_new = blk >= n_past_blk
            base = jnp.where(is_new, blk * SEL_BLOCK, (blk * SEL_BLOCK) // PAGE_ROWS * PAGE_ROWS)
            pos = base + lane
            dist = q_pos - pos
            idx = jnp.zeros((1, PAGE_ROWS), jnp.int32)
            for thr in thresholds:
                idx = idx + (dist >= thr).astype(jnp.int32)
            parts.append(jnp.where((dist >= 0) & (pos // SEL_BLOCK == blk), idx, MASK_BUCKET))
        idx = jnp.concatenate(parts, axis=1)
        return jnp.concatenate([jnp.take_along_axis(
            tb, jnp.broadcast_to(idx[:, c * LANES:(c + 1) * LANES], (NSA_HEADS, LANES)), axis=1)
            for c in range(keys // LANES)], axis=1)

    k0 = pl.multiple_of(step * keys, keys)
    s_s[:, pl.ds(k0, keys)] = by_group(lambda g: jnp.dot(q, tile_kv(g, 0), preferred_element_type=F32)
                                       + tile_bias(g))
    for g in range(NSA_KV_HEADS):
        v_s[g, :, pl.ds(k0, keys)] = tile_kv(g, 1)

    @pl.when(step == n_steps - 1)
    def _():
        s = s_s[...]
        e = jnp.exp(s - jnp.max(s, axis=1, keepdims=True))
        pb = (e / jnp.sum(e, axis=1, keepdims=True)).astype(BF16)
        o_sel = by_group(lambda g: _dot_nt(pb, v_s[g]))
        gate = gate_ref[0]
        o_ref[0] = gate[:, 0:1] * oc_ref[0] + gate[:, 1:2] * o_sel + gate[:, 2:3] * ow_s[...]


def nsa_sample_attend(q, kv_sel_new, kv_win_new, gates, o_c, top, cache_sel, win_buf, page_table, rel_bias,
                      past_len):
    batch = q.shape[0]
    n_past_blk = past_len // SEL_BLOCK
    bpp = PAGE_ROWS // SEL_BLOCK
    n_steps = SEL_TOPK // SEL_PER_STEP
    page_shape = cache_sel.shape[1:]
    jp = jnp.minimum(top, n_past_blk - 1)
    phys = jnp.take_along_axis(page_table, (jp // bpp).reshape(batch, -1), axis=1).reshape(top.shape)
    phys = phys.astype(jnp.int32)
    wb = win_buf.shape[-1]
    dist_w = past_len - (past_len - wb + jnp.arange(wb, dtype=jnp.int32))
    idx_w = _bucket_index(dist_w, (dist_w >= 0) & (dist_w < WINDOW)).reshape(1, wb)
    gate3 = gates[:, :3 * NSA_HEADS].reshape(batch, NSA_HEADS, 3)

    def blk_spec(g, j):
        return pl.BlockSpec((1,) + page_shape,
                            lambda b, s, rows, sidx, g=g, j=j: (rows[b, g, s * SEL_PER_STEP + j], 0, 0, 0, 0))

    def per_b(shape):
        return pl.BlockSpec((1,) + shape, lambda b, s, rows, sidx: (b,) + (0,) * len(shape))

    def const(shape):
        return pl.BlockSpec(shape, lambda b, s, rows, sidx: (0,) * len(shape))

    grid_spec = pltpu.PrefetchScalarGridSpec(
        num_scalar_prefetch=2, grid=(batch, n_steps),
        in_specs=[blk_spec(g, j) for g in range(NSA_KV_HEADS) for j in range(SEL_PER_STEP)] + [
            per_b((NSA_HEADS, NSA_HD)), per_b((KV_W, 1)), per_b(win_buf.shape[1:]), per_b((1, KV_W)),
            per_b((NSA_HEADS, 3)), per_b((NSA_HEADS, NSA_HD)), const((1, wb)), const((NSA_HEADS, LANES))],
        out_specs=per_b((NSA_HEADS, NSA_HD)),
        scratch_shapes=[pltpu.VMEM((NSA_HEADS, SEL_TOPK * PAGE_ROWS), F32),
                        pltpu.VMEM((NSA_KV_HEADS, NSA_HD, SEL_TOPK * PAGE_ROWS), BF16),
                        pltpu.VMEM((NSA_HEADS, NSA_HD), F32)])
    o = pl.pallas_call(
        functools.partial(_nsa_sample_attend_body, q_pos=past_len, n_past_blk=n_past_blk, n_steps=n_steps),
        out_shape=jax.ShapeDtypeStruct((batch, NSA_HEADS, NSA_HD), F32), grid_spec=grid_spec,
        compiler_params=_params(2), name="nsa_sample_attend",
    )(phys, top.astype(jnp.int32), *([cache_sel] * (NSA_KV_HEADS * SEL_PER_STEP)), q,
      kv_sel_new.reshape(batch, KV_W, 1), win_buf, kv_win_new.reshape(batch, 1, KV_W), gate3, o_c, idx_w,
      _bias_table(rel_bias))
    return o.reshape(batch, Q_W)


def _pad_cols(w, width):
    return jnp.pad(w, ((0, 0), (0, width - w.shape[1]))).astype(BF16)


def kernel(x_prompt, x_sample, state_ret, state_mlstm_C, state_mlstm_n, state_mlstm_m, state_conv, cache_nsa_cmp, cache_nsa_sel, state_nsa_win, page_table, rel_bias, norm_mix, norm_ffn, ab_w_in, ab_conv_w, ab_conv_b, ab_b_igate, ab_b_fgate, ab_gn_g, ab_gn_b, ab_hn_g, ab_w_out, nsa_w_in, nsa_q_norm, nsa_k_norm, nsa_cmp_w1, nsa_cmp_b1, nsa_cmp_w2, nsa_cmp_b2, nsa_w_out, moe_w_group, moe_b_group, moe_w_expert, moe_b_expert, moe_w_gate, moe_w_up, moe_w_down):
    bp, lp, d = x_prompt.shape
    bs, ls, _ = x_sample.shape
    page_size = cache_nsa_cmp.shape[2]
    past_len = page_table.shape[1] * page_size
    assert norm_mix.shape[0] == 2 and ls == 1 and d == D_MODEL and lp % KEY_TILE == 0
    xp = x_prompt.reshape(bp * lp, d)
    xs = x_sample.reshape(bs, d)

    def moe(x, layer):
        return hier_moe_residual(x, norm_ffn[layer], moe_w_group[layer], moe_b_group[layer], moe_w_expert[layer],
                                 moe_b_expert[layer], layer, moe_w_gate, moe_w_up, moe_w_down)

    w_in = _pad_cols(ab_w_in[0], AB_IN_PAD)
    ab = (ab_conv_w[0], ab_conv_b[0], ab_b_igate[0], ab_b_fgate[0], ab_gn_g[0], ab_gn_b[0], ab_hn_g[0])
    zp = norm_matmul(xp, norm_mix[0], w_in)
    xp, ret_p, mc_p, mn_p, mm_p, conv_p = ab_prompt(zp, xp, bp, lp, *ab, ab_w_out[0])
    zs = norm_matmul(xs, norm_mix[0], w_in)
    pos_s = past_len + jnp.arange(ls, dtype=jnp.int32)
    ys, ret_s, mc_s, mn_s, mm_s, conv_s = ab_sample(zs, pos_s, state_ret[0], state_mlstm_C[0], state_mlstm_n[0],
                                                    state_mlstm_m[0], state_conv[0], *ab)
    xs = matmul_residual(ys, ab_w_out[0].astype(BF16), xs)
    xp = moe(xp, 0)
    xs = moe(xs, 0)

    w_in = _pad_cols(nsa_w_in[0], NSA_IN_PAD)
    cmp_w = (nsa_cmp_w1[0], nsa_cmp_b1[0], nsa_cmp_w2[0], nsa_cmp_b2[0], nsa_k_norm[0, 0])
    kv_shape = (2, NSA_KV_HEADS, NSA_HD)
    zp = norm_matmul(xp, norm_mix[1], w_in)
    q, cmp_t, sel_t, win_t, selb, winb, gates, kst = nsa_prep(zp, nsa_q_norm[0], nsa_k_norm[0], batch=bp)
    k_c, v_c = nsa_compress(cmp_t.reshape((bp,) + kv_shape + (lp,)), None, *cmp_w)
    xp = nsa_attend_prompt(q, gates, xp, k_c, v_c, kst, selb, winb, rel_bias, nsa_w_out[0], bp, lp)
    win_keep = min(WINDOW, lp)

    def rows_major(t):
        return jnp.transpose(t.reshape((1, bp) + kv_shape + (t.shape[-1],)), (0, 1, 5, 2, 3, 4))

    cmp_p = rows_major(cmp_t)
    sel_p = rows_major(sel_t)
    win_p = rows_major(win_t[:, :, lp - win_keep:])

    def rows_minor(c):
        return jnp.moveaxis(c, -4, -1)

    zs = norm_matmul(xs, norm_mix[1], w_in)
    q, cmp_s, sel_s, win_s, _, _, gates = nsa_prep(zs, nsa_q_norm[0], nsa_k_norm[0])
    k_c, v_c = nsa_compress(rows_minor(cache_nsa_cmp[0]), page_table, *cmp_w)
    q3 = q.reshape(bs, NSA_HEADS, NSA_HD)
    o_c, top = nsa_sample_cmp(q3, k_c, v_c, rel_bias, past_len)
    win_buf = state_nsa_win[0]
    o = nsa_sample_attend(q3, sel_s, win_s, gates, o_c, top, rows_minor(cache_nsa_sel[0]), rows_minor(win_buf),
                          page_table, rel_bias, past_len)
    xs = matmul_residual(o, nsa_w_out[0].astype(BF16), xs)
    win_s = jnp.concatenate([win_buf, win_s.reshape((bs, ls) + kv_shape)], axis=1)[None, :, ls:]
    cmp_s = cmp_s.reshape((1, bs, ls) + kv_shape)
    sel_s = sel_s.reshape((1, bs, ls) + kv_shape)
    xp = moe(xp, 1)
    xs = moe(xs, 1)

    return (xp.reshape(bp, lp, d), xs.reshape(bs, ls, d), ret_p[None], ret_s[None], mc_p[None], mc_s[None],
            mn_p[None], mn_s[None], mm_p[None], mm_s[None], conv_p[None], conv_s[None],
            cmp_p, cmp_s, sel_p, sel_s, win_p, win_s)
```

```python
import functools
import math

import jax
import jax.numpy as jnp
import numpy as np
from jax import lax
from jax.experimental import pallas as pl
from jax.experimental.pallas import tpu as pltpu

F32 = jnp.float32
BF16 = jnp.bfloat16
LANES = 128
SUBLANES = 8
VMEM_LIMIT = 56 * 1024 * 1024

D_MODEL = 1024
RET_HEADS = 4
ML_HEADS = 4
HEAD_D = 128
CONV_W = 4
CHUNK = 128
ROPE_BASE = 10000.0
AB_IN = 4104
AB_IN_PAD = 4224
NSA_HEADS = 16
NSA_KV_HEADS = 4
NSA_REP = 4
NSA_HD = 64
NSA_IN_PAD = 2688
CMP_LEN = 32
CMP_STRIDE = 16
CMP_HID = 128
SEL_BLOCK = 64
SEL_TOPK = 16
WINDOW = 512
REL_BUCKETS = 32
REL_MAX_DIST = 1024
FORCE_SCORE = 1e4
MOE_GROUPS = 4
MOE_EXP_PER_GROUP = 8
MOE_EXPERTS = 32
NEG = -1e30
EPS = 1e-6


def _params(n_grid):
    return pltpu.CompilerParams(dimension_semantics=("arbitrary",) * n_grid, vmem_limit_bytes=VMEM_LIMIT)


def _dot(a, b):
    return jnp.dot(a.astype(BF16), b.astype(BF16), preferred_element_type=F32)


def _dot_nt(a, b):
    return lax.dot_general(a.astype(BF16), b.astype(BF16), (((1,), (1,)), ((), ())), preferred_element_type=F32)


def _dot_tn(a, b):
    return lax.dot_general(a.astype(BF16), b.astype(BF16), (((0,), (0,)), ((), ())), preferred_element_type=F32)


def _dot_f32(a, b):
    return jnp.dot(a, b, preferred_element_type=F32, precision=lax.Precision.HIGHEST)


def _r16(x):
    return x.astype(BF16).astype(F32)


def _sigmoid(x):
    return 1.0 / (1.0 + jnp.exp(-x))


def _silu(x):
    return x * _sigmoid(x)


def _log_sigmoid(x):
    return -(jnp.maximum(-x, 0.0) + jnp.log1p(jnp.exp(-jnp.abs(x))))


def _norm_matmul_body(x_ref, g_ref, w_ref, o_ref, *, col_tile):
    x = x_ref[...]
    y = x * lax.rsqrt(jnp.mean(x * x, axis=-1, keepdims=True) + EPS) * g_ref[...]
    yb = y.astype(BF16)
    for c0 in range(0, o_ref.shape[1], col_tile):
        o_ref[:, c0:c0 + col_tile] = jnp.dot(yb, w_ref[:, c0:c0 + col_tile], preferred_element_type=F32)


def norm_matmul(x, g, w, row_tile=256):
    n, d = x.shape
    c = w.shape[1]
    tm = min(row_tile, n)
    col_tile = 384 if c % 384 == 0 else LANES
    return pl.pallas_call(
        functools.partial(_norm_matmul_body, col_tile=col_tile),
        out_shape=jax.ShapeDtypeStruct((n, c), F32),
        grid=(n // tm,),
        in_specs=[pl.BlockSpec((tm, d), lambda i: (i, 0)),
                  pl.BlockSpec((1, d), lambda i: (0, 0)),
                  pl.BlockSpec((d, c), lambda i: (0, 0))],
        out_specs=pl.BlockSpec((tm, c), lambda i: (i, 0)),
        compiler_params=_params(1),
        name="norm_matmul",
    )(x, g.reshape(1, d), w)


def _retention_constants(c):
    h = np.arange(RET_HEADS, dtype=np.float64)
    log_g = np.log1p(-np.exp2(-5.0 - h))
    i = np.arange(c, dtype=np.float64)
    diff = i[:, None] - i[None, :]
    decay = np.where(diff >= 0, np.exp(np.maximum(diff, 0.0)[None] * log_g[:, None, None]), 0.0)
    q_dec = np.exp((i + 1.0)[None, :] * log_g[:, None])[:, :, None]
    k_dec = np.exp((c - 1.0 - i)[None, :] * log_g[:, None])[:, :, None]
    s_dec = np.exp(c * log_g)
    return (jnp.asarray(decay, F32), jnp.asarray(q_dec, F32), jnp.asarray(k_dec, F32),
            [float(v) for v in s_dec])


def _rope_tables(pos):
    half = HEAD_D // 2
    freqs = ROPE_BASE ** (-jnp.arange(half, dtype=F32) / half)
    ang = pos.astype(F32)[:, None] * freqs[None, :]
    cos, sin = jnp.cos(ang), jnp.sin(ang)
    return jnp.concatenate([cos, cos], axis=-1), jnp.concatenate([-sin, sin], axis=-1)


def _rope(x, cosf, sinf):
    return x * cosf + pltpu.roll(x, HEAD_D // 2, 1) * sinf


def _ab_prompt_body(rq_ref, rk_ref, rv_ref, rg_ref, mqk_ref, mv_ref, mo_ref, gz_ref, x_ref, cos_ref, sin_ref,
                    decay_ref, qdec_ref, kdec_ref, convw_ref, convb_ref, gbias_ref, gng_ref, gnb_ref, hng_ref,
                    wout_ref,
                    y_ref, s_ref, c_ref, n_ref, m_ref, conv_ref,
                    cbuf_ref, ycat_ref, *, s_dec):
    c = pl.program_id(1)
    tail = CONV_W - 1

    @pl.when(c == 0)
    def _():
        s_ref[...] = jnp.zeros_like(s_ref)
        c_ref[...] = jnp.zeros_like(c_ref)
        n_ref[...] = jnp.zeros_like(n_ref)
        m_ref[...] = jnp.zeros_like(m_ref)
        cbuf_ref[0:SUBLANES, :] = jnp.zeros((SUBLANES, cbuf_ref.shape[1]), F32)

    cosf = cos_ref[...]
    sinf = sin_ref[...]
    row = lax.broadcasted_iota(jnp.int32, (CHUNK, CHUNK), 0)
    col = lax.broadcasted_iota(jnp.int32, (CHUNK, CHUNK), 1)
    eye = row == col
    tril = row >= col
    triu = row <= col

    cbuf_ref[SUBLANES:SUBLANES + CHUNK, :] = mqk_ref[...]
    conv = convb_ref[...]
    for w in range(CONV_W):
        conv = conv + (_r16(cbuf_ref[SUBLANES - tail + w:SUBLANES - tail + w + CHUNK, :])
                       * _r16(convw_ref[w:w + 1, :]))
    qk = _silu(conv)
    last = cbuf_ref[CHUNK + SUBLANES - tail:CHUNK + SUBLANES, :]
    cbuf_ref[SUBLANES - tail:SUBLANES, :] = last
    conv_ref[0] = last

    gz = gz_ref[...] + gbias_ref[...]
    for h in range(RET_HEADS):
        sl = slice(h * HEAD_D, (h + 1) * HEAD_D)
        q = _rope(rq_ref[:, sl], cosf, sinf)
        k = _rope(rk_ref[:, sl], cosf, sinf) * (HEAD_D ** -0.5)
        v = rv_ref[:, sl]
        a = _dot_nt(q, k) * decay_ref[h]
        s_old = s_ref[0, h]
        o = _dot(a, v) + qdec_ref[h] * _dot(q, s_old)
        s_ref[0, h] = s_dec[h] * s_old + _dot_tn(k * kdec_ref[h], v)
        mu = jnp.mean(o, axis=-1, keepdims=True)
        var = jnp.mean(jnp.square(o - mu), axis=-1, keepdims=True)
        o = (o - mu) * lax.rsqrt(var + EPS) * gng_ref[:, sl] + gnb_ref[:, sl]
        ycat_ref[:, sl] = _silu(rg_ref[:, sl]) * o

        mq = qk[:, sl]
        mk = qk[:, ML_HEADS * HEAD_D + h * HEAD_D:ML_HEADS * HEAD_D + (h + 1) * HEAD_D] * (HEAD_D ** -0.5)
        mv = mv_ref[:, sl]
        i_col = gz[:, h:h + 1]
        f_col = _log_sigmoid(gz[:, ML_HEADS + h:ML_HEADS + h + 1])
        i_row = jnp.sum(jnp.where(eye, i_col, 0.0), axis=0, keepdims=True)
        f_row = jnp.sum(jnp.where(eye, f_col, 0.0), axis=0, keepdims=True)
        b_col = jnp.sum(jnp.where(tril, f_row, 0.0), axis=1, keepdims=True)
        b_row = jnp.sum(jnp.where(triu, f_col, 0.0), axis=0, keepdims=True)
        m_old = m_ref[0, h:h + 1, 0:1]
        dlog = jnp.where(tril, b_col - b_row + i_row, -jnp.inf)
        inter = b_col + m_old
        m_t = jnp.maximum(inter, jnp.max(dlog, axis=1, keepdims=True))
        wgt = _dot_nt(mq, mk) * jnp.exp(dlog - m_t)
        e_inter = jnp.exp(inter - m_t)
        c_old = c_ref[0, h]
        n_old = n_ref[0, h:h + 1, :]
        num = _dot(wgt, mv) + e_inter * _dot_nt(mq, c_old)
        den = (jnp.sum(wgt, axis=1, keepdims=True)
               + e_inter * jnp.sum(_r16(mq) * _r16(n_old), axis=1, keepdims=True))
        hc = num / jnp.maximum(jnp.abs(den), jnp.exp(-m_t))
        b_last = b_col[CHUNK - 1:CHUNK, :]
        u_row = b_last - b_row + i_row
        u_col = b_last - b_col + i_col
        m_new = jnp.maximum(b_last + m_old, jnp.max(u_row, axis=1, keepdims=True))
        ws_col = jnp.exp(u_col - m_new)
        f_state = jnp.exp(b_last + m_old - m_new)
        c_ref[0, h] = f_state * c_old + _dot_tn(mv * ws_col, mk)
        n_ref[0, h:h + 1, :] = f_state * n_old + jnp.sum(_r16(ws_col) * _r16(mk), axis=0, keepdims=True)
        m_ref[0, h:h + 1, :] = jnp.broadcast_to(m_new, (1, LANES))
        hm = _sigmoid(mo_ref[:, sl]) * hc
        hm = hm * lax.rsqrt(jnp.mean(hm * hm, axis=-1, keepdims=True) + EPS) * hng_ref[:, sl]
        ycat_ref[:, RET_HEADS * HEAD_D + h * HEAD_D:RET_HEADS * HEAD_D + (h + 1) * HEAD_D] = hm

    y_ref[...] = x_ref[...] + jnp.dot(ycat_ref[...].astype(BF16), wout_ref[...], preferred_element_type=F32)


def ab_prompt(z, x, batch, seq, conv_w, conv_b, b_ig, b_fg, gn_g, gn_b, hn_g, w_out):
    n_chunk = seq // CHUNK
    decay, q_dec, k_dec, s_dec = _retention_constants(CHUNK)
    cosf, sinf = _rope_tables(jnp.arange(seq, dtype=jnp.int32))
    gbias = jnp.zeros((1, LANES), F32).at[0, :ML_HEADS].set(b_ig).at[0, ML_HEADS:2 * ML_HEADS].set(b_fg)
    hw = RET_HEADS * HEAD_D
    qkw = 2 * ML_HEADS * HEAD_D

    def zspec(width, blk):
        return pl.BlockSpec((CHUNK, width), lambda b, c, blk=blk: (b * n_chunk + c, blk))

    def const(shape):
        return pl.BlockSpec(shape, lambda b, c: (0,) * len(shape))

    in_specs = [zspec(hw, 0), zspec(hw, 1), zspec(hw, 2), zspec(hw, 3), zspec(qkw, 2), zspec(hw, 6), zspec(hw, 7),
                zspec(LANES, (AB_IN_PAD - LANES) // LANES),
                pl.BlockSpec((CHUNK, D_MODEL), lambda b, c: (b * n_chunk + c, 0)),
                pl.BlockSpec((CHUNK, HEAD_D), lambda b, c: (c, 0)),
                pl.BlockSpec((CHUNK, HEAD_D), lambda b, c: (c, 0)),
                const((RET_HEADS, CHUNK, CHUNK)), const((RET_HEADS, CHUNK, 1)), const((RET_HEADS, CHUNK, 1)),
                const((CONV_W, qkw)), const((1, qkw)), const((1, LANES)),
                const((1, hw)), const((1, hw)), const((1, hw)), const((2 * hw, D_MODEL))]
    out_shape = (jax.ShapeDtypeStruct((batch * seq, D_MODEL), F32),
                 jax.ShapeDtypeStruct((batch, RET_HEADS, HEAD_D, HEAD_D), F32),
                 jax.ShapeDtypeStruct((batch, ML_HEADS, HEAD_D, HEAD_D), F32),
                 jax.ShapeDtypeStruct((batch, ML_HEADS, HEAD_D), F32),
                 jax.ShapeDtypeStruct((batch, SUBLANES, LANES), F32),
                 jax.ShapeDtypeStruct((batch, CONV_W - 1, qkw), F32))
    out_specs = (pl.BlockSpec((CHUNK, D_MODEL), lambda b, c: (b * n_chunk + c, 0)),
                 pl.BlockSpec((1, RET_HEADS, HEAD_D, HEAD_D), lambda b, c: (b, 0, 0, 0)),
                 pl.BlockSpec((1, ML_HEADS, HEAD_D, HEAD_D), lambda b, c: (b, 0, 0, 0)),
                 pl.BlockSpec((1, ML_HEADS, HEAD_D), lambda b, c: (b, 0, 0)),
                 pl.BlockSpec((1, SUBLANES, LANES), lambda b, c: (b, 0, 0)),
                 pl.BlockSpec((1, CONV_W - 1, qkw), lambda b, c: (b, 0, 0)))
    y, s, cc, n, m, conv = pl.pallas_call(
        functools.partial(_ab_prompt_body, s_dec=s_dec),
        out_shape=out_shape, grid=(batch, n_chunk), in_specs=in_specs, out_specs=out_specs,
        scratch_shapes=[pltpu.VMEM((CHUNK + SUBLANES, qkw), F32), pltpu.VMEM((CHUNK, 2 * hw), F32)],
        compiler_params=_params(2), name="ab_prompt",
    )(z, z, z, z, z, z, z, z, x, cosf, sinf, decay, q_dec, k_dec, conv_w, conv_b.reshape(1, qkw), gbias,
      gn_g.reshape(1, hw), gn_b.reshape(1, hw), hn_g.reshape(1, hw), w_out.astype(BF16))
    return y, s, cc, n, m[:, :ML_HEADS, 0], conv


def _matmul_residual_body(a_ref, w_ref, x_ref, o_ref):
    o_ref[...] = x_ref[...] + jnp.dot(a_ref[...].astype(BF16), w_ref[...], preferred_element_type=F32)


def matmul_residual(a, w, x, row_tile=256):
    n, kk = a.shape
    d = w.shape[1]
    tm = min(row_tile, n)
    return pl.pallas_call(
        _matmul_residual_body, out_shape=jax.ShapeDtypeStruct((n, d), F32), grid=(n // tm,),
        in_specs=[pl.BlockSpec((tm, kk), lambda i: (i, 0)), pl.BlockSpec((kk, d), lambda i: (0, 0)),
                  pl.BlockSpec((tm, d), lambda i: (i, 0))],
        out_specs=pl.BlockSpec((tm, d), lambda i: (i, 0)),
        compiler_params=_params(1), name="matmul_residual",
    )(a, w, x)


def _ab_sample_body(m0_ref, z_ref, cos_ref, sin_ref, s0_ref, c0_ref, n0_ref, conv0_ref,
                    convw_ref, convb_ref, gbias_ref, gng_ref, gnb_ref, hng_ref,
                    y_ref, s_ref, c_ref, n_ref, m_ref, conv_ref, *, g_dec):
    b = pl.program_id(0)
    hw = RET_HEADS * HEAD_D
    qkw = 2 * ML_HEADS * HEAD_D
    tail = CONV_W - 1
    cosf = cos_ref[...]
    sinf = sin_ref[...]
    row = lax.broadcasted_iota(jnp.int32, (HEAD_D, HEAD_D), 0)
    col = lax.broadcasted_iota(jnp.int32, (HEAD_D, HEAD_D), 1)
    eye = row == col

    def to_col(r):
        return jnp.sum(jnp.where(eye, r, 0.0), axis=1, keepdims=True)

    def to_row(cv):
        return jnp.sum(jnp.where(eye, cv, 0.0), axis=0, keepdims=True)

    mqk = z_ref[0, :, 4 * hw:4 * hw + qkw]
    conv = convb_ref[...] + mqk * convw_ref[tail:CONV_W, :]
    for w in range(tail):
        conv = conv + conv0_ref[0, w:w + 1, :] * convw_ref[w:w + 1, :]
    qk = _silu(conv)
    conv_ref[0, 0:tail - 1, :] = conv0_ref[0, 1:tail, :]
    conv_ref[0, tail - 1:tail, :] = mqk
    gz = z_ref[0, :, AB_IN_PAD - LANES:AB_IN_PAD] + gbias_ref[...]

    for h in range(RET_HEADS):
        sl = slice(h * HEAD_D, (h + 1) * HEAD_D)
        q = _rope(z_ref[0, :, sl], cosf, sinf)
        k = _rope(z_ref[0, :, hw + h * HEAD_D:hw + (h + 1) * HEAD_D], cosf, sinf) * (HEAD_D ** -0.5)
        v = z_ref[0, :, 2 * hw + h * HEAD_D:2 * hw + (h + 1) * HEAD_D]
        rg = z_ref[0, :, 3 * hw + h * HEAD_D:3 * hw + (h + 1) * HEAD_D]
        s_old = s0_ref[0, h]
        qk_s = jnp.sum(q * k, axis=1, keepdims=True)
        o = qk_s * v + g_dec[h] * jnp.sum(_r16(to_col(q)) * _r16(s_old), axis=0, keepdims=True)
        s_ref[0, h] = g_dec[h] * s_old + to_col(k) * v
        mu = jnp.mean(o, axis=-1, keepdims=True)
        var = jnp.mean(jnp.square(o - mu), axis=-1, keepdims=True)
        o = (o - mu) * lax.rsqrt(var + EPS) * gng_ref[:, sl] + gnb_ref[:, sl]
        y_ref[0, :, sl] = _silu(rg) * o

        mq = qk[:, sl]
        mk = qk[:, ML_HEADS * HEAD_D + h * HEAD_D:ML_HEADS * HEAD_D + (h + 1) * HEAD_D] * (HEAD_D ** -0.5)
        mv = z_ref[0, :, 4 * hw + qkw + h * HEAD_D:4 * hw + qkw + (h + 1) * HEAD_D]
        mo = z_ref[0, :, 5 * hw + qkw + h * HEAD_D:5 * hw + qkw + (h + 1) * HEAD_D]
        ig = gz[:, h:h + 1]
        lf = _log_sigmoid(gz[:, ML_HEADS + h:ML_HEADS + h + 1])
        m_old = m0_ref[b, h]
        inter = lf + m_old
        m_t = jnp.maximum(inter, ig)
        wgt = jnp.sum(mq * mk, axis=1, keepdims=True) * jnp.exp(ig - m_t)
        e_inter = jnp.exp(inter - m_t)
        c_old = c0_ref[0, h]
        n_old = n0_ref[0, h:h + 1, :]
        cq = to_row(jnp.sum(_r16(c_old) * _r16(mq), axis=1, keepdims=True))
        num = wgt * mv + e_inter * cq
        den = wgt + e_inter * jnp.sum(n_old * mq, axis=1, keepdims=True)
        hc = num / jnp.maximum(jnp.abs(den), jnp.exp(-m_t))
        ws = jnp.exp(ig - m_t)
        c_ref[0, h] = e_inter * c_old + (ws * to_col(mv)) * mk
        n_ref[0, h:h + 1, :] = e_inter * n_old + ws * mk
        m_ref[0, h:h + 1, :] = jnp.broadcast_to(m_t, (1, LANES))
        hm = _sigmoid(mo) * hc
        hm = hm * lax.rsqrt(jnp.mean(hm * hm, axis=-1, keepdims=True) + EPS) * hng_ref[:, sl]
        y_ref[0, :, hw + h * HEAD_D:hw + (h + 1) * HEAD_D] = hm
    m_ref[0, ML_HEADS:SUBLANES, :] = jnp.zeros((SUBLANES - ML_HEADS, LANES), F32)


def ab_sample(z, pos, s0, c0, n0, m0, conv0, conv_w, conv_b, b_ig, b_fg, gn_g, gn_b, hn_g):
    batch = z.shape[0]
    h = np.arange(RET_HEADS, dtype=np.float64)
    g_dec = [float(v) for v in np.exp(np.log1p(-np.exp2(-5.0 - h)))]
    cosf, sinf = _rope_tables(pos)
    gbias = jnp.zeros((1, LANES), F32).at[0, :ML_HEADS].set(b_ig).at[0, ML_HEADS:2 * ML_HEADS].set(b_fg)
    hw = RET_HEADS * HEAD_D
    qkw = 2 * ML_HEADS * HEAD_D

    def per_b(shape):
        return pl.BlockSpec((1,) + shape, lambda b: (b,) + (0,) * len(shape))

    def const(shape):
        return pl.BlockSpec(shape, lambda b: (0,) * len(shape))

    in_specs = [pl.BlockSpec(memory_space=pltpu.SMEM), per_b((1, AB_IN_PAD)), const((1, HEAD_D)), const((1, HEAD_D)),
                per_b((RET_HEADS, HEAD_D, HEAD_D)), per_b((ML_HEADS, HEAD_D, HEAD_D)), per_b((ML_HEADS, HEAD_D)),
                per_b((CONV_W - 1, qkw)), const((CONV_W, qkw)), const((1, qkw)), const((1, LANES)),
                const((1, hw)), const((1, hw)), const((1, hw))]
    out_shape = (jax.ShapeDtypeStruct((batch, 1, 2 * hw), F32),
                 jax.ShapeDtypeStruct((batch, RET_HEADS, HEAD_D, HEAD_D), F32),
                 jax.ShapeDtypeStruct((batch, ML_HEADS, HEAD_D, HEAD_D), F32),
                 jax.ShapeDtypeStruct((batch, ML_HEADS, HEAD_D), F32),
                 jax.ShapeDtypeStruct((batch, SUBLANES, LANES), F32),
                 jax.ShapeDtypeStruct((batch, CONV_W - 1, qkw), F32))
    out_specs = (per_b((1, 2 * hw)), per_b((RET_HEADS, HEAD_D, HEAD_D)), per_b((ML_HEADS, HEAD_D, HEAD_D)),
                 per_b((ML_HEADS, HEAD_D)), per_b((SUBLANES, LANES)), per_b((CONV_W - 1, qkw)))
    y, s, cc, n, m, conv = pl.pallas_call(
        functools.partial(_ab_sample_body, g_dec=g_dec),
        out_shape=out_shape, grid=(batch,), in_specs=in_specs, out_specs=out_specs,
        compiler_params=_params(1), name="ab_sample",
    )(m0, z.reshape(batch, 1, AB_IN_PAD), cosf, sinf, s0, c0, n0, conv0, conv_w, conv_b.reshape(1, qkw), gbias,
      gn_g.reshape(1, hw), gn_b.reshape(1, hw), hn_g.reshape(1, hw))
    return y.reshape(batch, 2 * hw), s, cc, n, m[:, :ML_HEADS, 0], conv


MOE_TILE = 256


def _moe_router_body(x_ref, g_ref, wr_ref, br_ref, hn_ref, route_ref, count_ref):
    x = x_ref[...]
    hn = x * lax.rsqrt(jnp.mean(x * x, axis=-1, keepdims=True) + EPS) * g_ref[...]
    hn_ref[...] = hn
    z = _dot(hn, wr_ref[...]) + br_ref[...]
    lane = lax.broadcasted_iota(jnp.int32, z.shape, 1)
    lanef = lane.astype(F32)
    is_group = lane < MOE_GROUPS
    gl = jnp.where(is_group, z, -jnp.inf)
    gmax = jnp.max(gl, axis=1, keepdims=True)
    g_top = jnp.min(jnp.where(gl == gmax, lanef, float(LANES)), axis=1, keepdims=True)
    pg_top = 1.0 / jnp.sum(jnp.where(is_group, jnp.exp(z - gmax), 0.0), axis=1, keepdims=True)
    grp = ((lane - MOE_GROUPS) // MOE_EXP_PER_GROUP).astype(F32)
    in_group = (lane >= MOE_GROUPS) & (lane < MOE_GROUPS + MOE_EXPERTS) & (grp == g_top)
    el = jnp.where(in_group, z, -jnp.inf)
    v1 = jnp.max(el, axis=1, keepdims=True)
    i1 = jnp.min(jnp.where(el == v1, lanef, float(LANES)), axis=1, keepdims=True)
    el2 = jnp.where(lanef == i1, -jnp.inf, el)
    v2 = jnp.max(el2, axis=1, keepdims=True)
    i2 = jnp.min(jnp.where(el2 == v2, lanef, float(LANES)), axis=1, keepdims=True)
    t = jnp.exp(v2 - v1)
    p1 = 1.0 / (1.0 + t)
    out = jnp.where(lane == 0, i1 - MOE_GROUPS,
                    jnp.where(lane == 1, i2 - MOE_GROUPS,
                              jnp.where(lane == 2, pg_top * p1,
                                        jnp.where(lane == 3, pg_top * (t * p1), 0.0))))
    route_ref[...] = out
    picked = jnp.where((lanef == i1 - MOE_GROUPS) | (lanef == i2 - MOE_GROUPS), 1.0, 0.0)

    @pl.when(pl.program_id(0) == 0)
    def _():
        count_ref[...] = jnp.zeros_like(count_ref)

    count_ref[...] += jnp.sum(picked, axis=0, keepdims=True)


def moe_router(x, g, w_group, b_group, w_expert, b_expert, row_tile=256):
    n, d = x.shape
    tm = min(row_tile, n)
    used = MOE_GROUPS + MOE_EXPERTS
    wr = jnp.pad(jnp.concatenate([w_group, w_expert], axis=1), ((0, 0), (0, LANES - used)))
    br = jnp.pad(jnp.concatenate([b_group, b_expert]), (0, LANES - used)).reshape(1, LANES)
    hn, route, count = pl.pallas_call(
        _moe_router_body,
        out_shape=(jax.ShapeDtypeStruct((n, d), F32), jax.ShapeDtypeStruct((n, LANES), F32),
                   jax.ShapeDtypeStruct((1, LANES), F32)),
        grid=(n // tm,),
        in_specs=[pl.BlockSpec((tm, d), lambda i: (i, 0)), pl.BlockSpec((1, d), lambda i: (0, 0)),
                  pl.BlockSpec((d, LANES), lambda i: (0, 0)), pl.BlockSpec((1, LANES), lambda i: (0, 0))],
        out_specs=(pl.BlockSpec((tm, d), lambda i: (i, 0)), pl.BlockSpec((tm, LANES), lambda i: (i, 0)),
                   pl.BlockSpec((1, LANES), lambda i: (0, 0))),
        compiler_params=_params(1), name="moe_router",
    )(x, g.reshape(1, d), wr, br)
    return hn, route, route[:, 0:2].astype(jnp.int32), count[0, :MOE_EXPERTS].astype(jnp.int32)


def _moe_ffn_body(blk_e_ref, n_real_ref, asg_ref, hn_ref, wg_ref, wu_ref, wd_ref, o_ref,
                  x_s, y_s, wg_s, wu_s, wd_s, gsem, ssem, *, n_blk):
    i = pl.program_id(0)
    slot = i % 2

    def gather_row(tile, r, slot):
        tok = lax.shift_right_logical(asg_ref[tile * MOE_TILE + r], 1)
        return pltpu.make_async_copy(hn_ref.at[pl.ds(tok, 1), :], x_s.at[slot, pl.ds(r, 1), :], gsem.at[slot])

    def scatter_row(r):
        a = asg_ref[i * MOE_TILE + r]
        return pltpu.make_async_copy(y_s.at[pl.ds(r, 1), :],
                                     o_ref.at[a & 1, pl.ds(lax.shift_right_logical(a, 1), 1), :], ssem.at[0])

    def for_rows(n_rows, fn):
        @pl.when(n_rows == MOE_TILE)
        def _():
            for r in range(MOE_TILE):
                fn(r)

        @pl.when(n_rows < MOE_TILE)
        def _():
            def body(r, c):
                fn(r)
                return c
            lax.fori_loop(0, n_rows, body, 0)

    def start_gather(tile, slot):
        for_rows(n_real_ref[tile], lambda r: gather_row(tile, r, slot).start())

    @pl.when(i == 0)
    def _():
        x_s[...] = jnp.zeros_like(x_s)
        start_gather(0, 0)

    for s in range(2):
        @pl.when((i + 1 < n_blk) & (slot == 1 - s))
        def _(s=s):
            start_gather(i + 1, s)

    prev = blk_e_ref[jnp.maximum(i - 1, 0)]

    @pl.when((i == 0) | (blk_e_ref[i] != prev))
    def _():
        wg_s[...] = wg_ref[0].astype(BF16)
        wu_s[...] = wu_ref[0].astype(BF16)
        wd_s[...] = wd_ref[0].astype(BF16)

    n_real = n_real_ref[i]
    for_rows(n_real, lambda r: gather_row(i, 0, slot).wait())

    @pl.when(n_real > 0)
    def _():
        x = x_s[slot].astype(BF16)
        hg = jnp.dot(x, wg_s[...], preferred_element_type=F32)
        hu = jnp.dot(x, wu_s[...], preferred_element_type=F32)
        hb = (_silu(hg) * hu).astype(BF16)
        y_s[...] = jnp.dot(hb, wd_s[...], preferred_element_type=F32)
        for_rows(n_real, lambda r: scatter_row(r).start())
        for_rows(n_real, lambda r: scatter_row(0).wait())


def moe_ffn(hn, asg, n_real, blk_e, layer, w_g, w_u, w_d):
    n_tok, d = hn.shape
    ff = w_g.shape[3]
    n_blk = blk_e.shape[0]
    grid_spec = pltpu.PrefetchScalarGridSpec(
        num_scalar_prefetch=3, grid=(n_blk,),
        in_specs=[pl.BlockSpec(memory_space=pl.ANY),
                  pl.BlockSpec((None, 1, d, ff), lambda i, e, nr, a: (layer, e[i], 0, 0)),
                  pl.BlockSpec((None, 1, d, ff), lambda i, e, nr, a: (layer, e[i], 0, 0)),
                  pl.BlockSpec((None, 1, ff, d), lambda i, e, nr, a: (layer, e[i], 0, 0))],
        out_specs=pl.BlockSpec(memory_space=pl.ANY),
        scratch_shapes=[pltpu.VMEM((2, MOE_TILE, d), F32), pltpu.VMEM((MOE_TILE, d), F32),
                        pltpu.VMEM((d, ff), BF16), pltpu.VMEM((d, ff), BF16), pltpu.VMEM((ff, d), BF16),
                        pltpu.SemaphoreType.DMA((2,)), pltpu.SemaphoreType.DMA((1,))])
    return pl.pallas_call(
        functools.partial(_moe_ffn_body, n_blk=n_blk),
        out_shape=jax.ShapeDtypeStruct((2, n_tok, d), F32), grid_spec=grid_spec,
        compiler_params=_params(1), name="moe_ffn",
    )(blk_e, n_real, asg, hn, w_g, w_u, w_d)


def _moe_combine_body(x_ref, route_ref, y0_ref, y1_ref, o_ref):
    o_ref[...] = x_ref[...] + (y0_ref[...] * route_ref[:, 2:3] + y1_ref[...] * route_ref[:, 3:4])


def moe_combine(x, route, y, row_tile=512):
    n, d = x.shape
    tm = min(row_tile, n)
    return pl.pallas_call(
        _moe_combine_body, out_shape=jax.ShapeDtypeStruct((n, d), F32), grid=(n // tm,),
        in_specs=[pl.BlockSpec((tm, d), lambda i: (i, 0)), pl.BlockSpec((tm, LANES), lambda i: (i, 0)),
                  pl.BlockSpec((None, tm, d), lambda i: (0, i, 0)), pl.BlockSpec((None, tm, d), lambda i: (1, i, 0))],
        out_specs=pl.BlockSpec((tm, d), lambda i: (i, 0)),
        compiler_params=_params(1), name="moe_combine",
    )(x, route, y, y)


def hier_moe_residual(x, g, w_group, b_group, w_expert, b_expert, layer, w_g, w_u, w_d):
    n, d = x.shape
    hn, route, expert, counts = moe_router(x, g, w_group, b_group, w_expert, b_expert)
    n_exp = w_g.shape[1]
    kk = expert.shape[1]
    assert kk == 2
    a = n * kk
    order = jnp.argsort(expert.reshape(-1)).astype(jnp.int32)
    starts = jnp.cumsum(counts) - counts
    padded = (counts + MOE_TILE - 1) // MOE_TILE * MOE_TILE
    pend = jnp.cumsum(padded)
    pstart = pend - padded
    n_blk = -(-(a + n_exp * (MOE_TILE - 1)) // MOE_TILE)
    tile_start = jnp.arange(n_blk, dtype=jnp.int32) * MOE_TILE
    blk_e = jnp.minimum(jnp.sum((pend[None, :] <= tile_start[:, None]).astype(jnp.int32), axis=1), n_exp - 1)
    n_real = jnp.clip(counts[blk_e] - (tile_start - pstart[blk_e]), 0, MOE_TILE).astype(jnp.int32)
    row_off = (tile_start - pstart[blk_e] + starts[blk_e])[:, None] + jnp.arange(MOE_TILE, dtype=jnp.int32)[None, :]
    asg = order[jnp.clip(row_off, 0, a - 1)].reshape(-1)
    y = moe_ffn(hn, asg, n_real, blk_e, layer, w_g, w_u, w_d)
    return moe_combine(x, route, y)


KV_W = 2 * NSA_KV_HEADS * NSA_HD
Q_W = NSA_HEADS * NSA_HD
MASK_BUCKET = REL_BUCKETS
KEY_TILE = 512


def _t5_thresholds():
    exact = REL_BUCKETS // 2
    dist = np.arange(0, 4 * REL_MAX_DIST, dtype=np.int64)
    nf = np.maximum(dist, 1).astype(np.float64)
    large = exact + np.floor(np.log(nf / exact) / math.log(REL_MAX_DIST / exact) * (REL_BUCKETS - exact) + 1e-9)
    bucket = np.where(dist < exact, dist, np.minimum(large, REL_BUCKETS - 1)).astype(np.int64)
    return [int(np.argmax(bucket >= b)) for b in range(1, REL_BUCKETS)]


def _bucket_index(dist, valid):
    idx = jnp.zeros(dist.shape, jnp.int32)
    for thr in _t5_thresholds():
        idx = idx + (dist >= thr).astype(jnp.int32)
    return jnp.where(valid, idx, MASK_BUCKET)


def _bias_table(rel_bias):
    t = jnp.zeros((NSA_HEADS, LANES), F32).at[:, :REL_BUCKETS].set(rel_bias.T.astype(F32))
    return t.at[:, MASK_BUCKET].set(NEG)


def _group_mean_matrix():
    i = np.arange(LANES)
    return jnp.asarray((i[:, None] // NSA_HD == i[None, :] // NSA_HD) / NSA_HD, F32)


def _nsa_prep_body(zq_ref, zc_ref, zs_ref, zw_ref, zg_ref, bd_ref, qn_ref, kns_ref, knw_ref,
                   q_ref, cmp_ref, sel_ref, win_ref, selb_ref, winb_ref, gate_ref, *kt_ref, transposed):
    bd = bd_ref[...]

    def head_norm(x, gain):
        ms = _dot_f32(x * x, bd)
        return x * lax.rsqrt(ms + EPS) * gain

    def emit(o_ref, rows):
        if transposed:
            o_ref[0] = rows.T
        else:
            o_ref[...] = rows

    for c in range(Q_W // LANES):
        sl = slice(c * LANES, (c + 1) * LANES)
        q_ref[:, sl] = (head_norm(zq_ref[:, sl], qn_ref[...]) * (NSA_HD ** -0.5)).astype(BF16)
    emit(cmp_ref, zc_ref[...])
    half = KV_W // 2
    for z_ref, kn_ref, o_ref, ob_ref in ((zs_ref, kns_ref, sel_ref, selb_ref), (zw_ref, knw_ref, win_ref, winb_ref)):
        kn = jnp.concatenate([head_norm(z_ref[:, c * LANES:(c + 1) * LANES], kn_ref[...])
                              for c in range(half // LANES)], axis=1)
        rows = jnp.concatenate([kn, z_ref[:, half:KV_W]], axis=1)
        emit(o_ref, rows)
        ob_ref[...] = rows.astype(BF16)
        if transposed and o_ref is sel_ref:
            kt_ref[0][0] = kn.T.astype(BF16)
    gate_ref[...] = _sigmoid(zg_ref[...])


def nsa_prep(z, q_norm, k_norm, batch=None, row_tile=256):
    n = z.shape[0]
    tm = min(row_tile, n)
    transposed = batch is not None

    def zspec(width, blk):
        return pl.BlockSpec((tm, width), lambda i, blk=blk: (i, blk))

    def const(shape):
        return pl.BlockSpec(shape, lambda i: (0,) * len(shape))

    def tile2(v):
        return jnp.concatenate([v, v]).reshape(1, LANES).astype(F32)

    def rows(w, dt):
        return jax.ShapeDtypeStruct((n, w), dt)

    def out_spec(w):
        return pl.BlockSpec((tm, w), lambda i: (i, 0))

    if transposed:
        seq = n // batch
        tps = seq // tm
        kv_shape = jax.ShapeDtypeStruct((batch, KV_W, seq), F32)
        kv_spec = pl.BlockSpec((1, KV_W, tm), lambda i: (i // tps, 0, i % tps))
        extra_shape = (jax.ShapeDtypeStruct((batch, KV_W // 2, seq), BF16),)
        extra_spec = (pl.BlockSpec((1, KV_W // 2, tm), lambda i: (i // tps, 0, i % tps)),)
    else:
        kv_shape, kv_spec, extra_shape, extra_spec = rows(KV_W, F32), out_spec(KV_W), (), ()
    return pl.pallas_call(
        functools.partial(_nsa_prep_body, transposed=transposed),
        out_shape=(rows(Q_W, BF16), kv_shape, kv_shape, kv_shape, rows(KV_W, BF16), rows(KV_W, BF16),
                   rows(LANES, F32)) + extra_shape,
        grid=(n // tm,),
        in_specs=[zspec(Q_W, 0), zspec(KV_W, 2), zspec(KV_W, 3), zspec(KV_W, 4),
                  zspec(LANES, (Q_W + 3 * KV_W) // LANES), const((LANES, LANES)),
                  const((1, LANES)), const((1, LANES)), const((1, LANES))],
        out_specs=(out_spec(Q_W), kv_spec, kv_spec, kv_spec, out_spec(KV_W), out_spec(KV_W),
                   out_spec(LANES)) + extra_spec,
        compiler_params=_params(1), name="nsa_prep",
    )(z, z, z, z, z, _group_mean_matrix(), tile2(q_norm), tile2(k_norm[1]), tile2(k_norm[2]))


PAGES_PER_STEP = 16
PAGE_ROWS = 128
SUBS_PER_PAGE = PAGE_ROWS // CMP_STRIDE
P_W = 2 * NSA_KV_HEADS * 2 * CMP_HID
KV_PAIRS = NSA_KV_HEADS // 2


def _gelu_tanh(x):
    return 0.5 * x * (1.0 + jnp.tanh(math.sqrt(2.0 / math.pi) * (x + 0.044715 * x * x * x)))


def _compress_body(pt_ref, *refs, n_steps):
    page_refs = refs[:PAGES_PER_STEP]
    w1_ref, b1_ref, w2_ref, b2_ref, kn_ref, kc_ref, vc_ref, p_ref, x_s = refs[PAGES_PER_STEP:]
    j = pl.program_id(1)
    rows = PAGES_PER_STEP * SUBS_PER_PAGE
    r0 = pl.multiple_of(j * rows, rows)
    for t, pr in enumerate(page_refs):
        for v in range(2):
            for gp in range(KV_PAIRS):
                x_s[t, v * KV_PAIRS + gp] = pr[0, v, 2 * gp:2 * gp + 2].reshape(2 * NSA_HD, PAGE_ROWS).T
    for v in range(2):
        for gp in range(KV_PAIRS):
            acc = jnp.zeros((rows, 4 * CMP_HID), F32)
            for s in range(CMP_STRIDE):
                x = jnp.concatenate([x_s[t, v * KV_PAIRS + gp, pl.ds(s, SUBS_PER_PAGE, stride=CMP_STRIDE), :]
                                     for t in range(PAGES_PER_STEP)], axis=0)
                acc = acc + jnp.dot(x.astype(BF16), w1_ref[v, s], preferred_element_type=F32)
            p0 = (v * NSA_KV_HEADS + 2 * gp) * 2 * CMP_HID
            p_ref[pl.ds(r0, rows), p0:p0 + 4 * CMP_HID] = acc

    @pl.when(j == n_steps - 1)
    def _():
        n_sub = p_ref.shape[0]
        for v in range(2):
            for g in range(NSA_KV_HEADS):
                p0 = (v * NSA_KV_HEADS + g) * 2 * CMP_HID
                hs = p_ref[:, p0:p0 + CMP_HID] + pltpu.roll(p_ref[:, p0 + CMP_HID:p0 + 2 * CMP_HID], n_sub - 1, 0)
                hid = _gelu_tanh(hs + b1_ref[v:v + 1, :])
                out = _dot(hid, w2_ref[v]) + b2_ref[v:v + 1, :]
                if v == 0:
                    out = out * lax.rsqrt(jnp.mean(out * out, axis=-1, keepdims=True) + EPS) * kn_ref[...]
                    kc_ref[0, :, g * NSA_HD:(g + 1) * NSA_HD] = out
                else:
                    vc_ref[0, :, g * NSA_HD:(g + 1) * NSA_HD] = out


def nsa_compress(rows_t, page_table, w1, b1, w2, b2, k_norm0):
    if page_table is None:
        batch, n_pp = rows_t.shape[0], rows_t.shape[-1] // PAGE_ROWS
        page_table = jnp.zeros((1, 1), jnp.int32)

        def page_index(b, p, pt):
            return (b, 0, 0, 0, p)
    else:
        batch, n_pp = page_table.shape

        def page_index(b, p, pt):
            return (pt[b, p], 0, 0, 0, 0)
    n_steps = n_pp // PAGES_PER_STEP
    n_sub = n_pp * SUBS_PER_PAGE
    w = w1.reshape(2, 2, CMP_STRIDE, NSA_HD, CMP_HID)
    w = jnp.transpose(w, (0, 2, 3, 1, 4)).reshape(2, CMP_STRIDE, NSA_HD, 2 * CMP_HID)
    zero = jnp.zeros_like(w)
    wpair = jnp.concatenate([jnp.concatenate([w, zero], axis=-1), jnp.concatenate([zero, w], axis=-1)], axis=2)
    wpair = wpair.astype(BF16)

    def page_spec(t):
        return pl.BlockSpec((1, 2, NSA_KV_HEADS, NSA_HD, PAGE_ROWS),
                            lambda b, j, pt, t=t: page_index(b, j * PAGES_PER_STEP + t, pt))

    def const(shape):
        return pl.BlockSpec(shape, lambda b, j, pt: (0,) * len(shape))

    grid_spec = pltpu.PrefetchScalarGridSpec(
        num_scalar_prefetch=1, grid=(batch, n_steps),
        in_specs=[page_spec(t) for t in range(PAGES_PER_STEP)] + [
            const((2, CMP_STRIDE, LANES, 4 * CMP_HID)), const((2, CMP_HID)), const((2, CMP_HID, NSA_HD)),
            const((2, NSA_HD)), const((1, NSA_HD))],
        out_specs=(pl.BlockSpec((1, n_sub, KV_W // 2), lambda b, j, pt: (b, 0, 0)),
                   pl.BlockSpec((1, n_sub, KV_W // 2), lambda b, j, pt: (b, 0, 0))),
        scratch_shapes=[pltpu.VMEM((n_sub, P_W), F32),
                        pltpu.VMEM((PAGES_PER_STEP, 2 * KV_PAIRS, PAGE_ROWS, 2 * NSA_HD), F32)])
    return pl.pallas_call(
        functools.partial(_compress_body, n_steps=n_steps),
        out_shape=(jax.ShapeDtypeStruct((batch, n_sub, KV_W // 2), F32),
                   jax.ShapeDtypeStruct((batch, n_sub, KV_W // 2), F32)),
        grid_spec=grid_spec, compiler_params=_params(2), name="nsa_compress",
    )(page_table, *([rows_t] * PAGES_PER_STEP), wpair, b1, w2.astype(BF16), b2, k_norm0.reshape(1, NSA_HD))


Q_TILE = 128
WIN_BLOCKS = (WINDOW + Q_TILE) // Q_TILE
NEAR_TILES = 3
NEAR_CHUNKS = 8
M_INIT = -1e29


def _gather_bias(tbh, idx_slices):
    return jnp.concatenate([jnp.take_along_axis(tbh, idx, axis=1) for idx in idx_slices], axis=1)


def _nsa_attend_body(q_ref, gate_ref, x_ref, kc_ref, vc_ref, kst_ref, vs_ref, w0_ref, w1_ref, w2_ref, w3_ref, w4_ref,
                     idxc_ref, idxw_ref, nb_ref, mmat_ref, tb_ref, wout_ref, y_ref,
                     pc_s, oc_s, sel_s, m_s, l_s, a_s, acc_s, ps_s, pw_s, o_s, bsel_s, bwin_s, *, n_sel):
    win_refs = (w0_ref, w1_ref, w2_ref, w3_ref, w4_ref)
    t = pl.program_id(1)
    s0 = t * Q_TILE
    n_sub = kc_ref.shape[1]
    rep_rows = NSA_REP * Q_TILE
    q_pos = s0 + lax.broadcasted_iota(jnp.int32, (Q_TILE, LANES), 0)
    blk = lax.broadcasted_iota(jnp.int32, (Q_TILE, LANES), 1)
    cur = q_pos // SEL_BLOCK
    forced = (blk == 0) | (blk == cur) | (blk == cur - 1)
    future = blk * SEL_BLOCK > q_pos

    def group_q(g):
        return jnp.concatenate([q_ref[:, (g * NSA_REP + r) * NSA_HD:(g * NSA_REP + r + 1) * NSA_HD]
                                for r in range(NSA_REP)], axis=0)

    def head_table(h):
        return jnp.broadcast_to(tb_ref[h:h + 1, :], (Q_TILE, LANES))

    @pl.when((pl.program_id(0) == 0) & (t == 0))
    def _():
        for h in range(NSA_HEADS):
            tbh = head_table(h)
            far_bias = tb_ref[h:h + 1, REL_BUCKETS - 1:REL_BUCKETS]
            for k in range(NEAR_CHUNKS):
                bsel_s[h, k] = jnp.take_along_axis(tbh, nb_ref[k], axis=1) - far_bias
            bsel_s[h, NEAR_CHUNKS] = jnp.zeros((Q_TILE, LANES), F32)
            bsel_s[h, NEAR_CHUNKS + 1] = jnp.full((Q_TILE, LANES), NEG, F32)
            bwin_s[h] = _gather_bias(tbh, [idxw_ref[:, c * LANES:(c + 1) * LANES] for c in range(WIN_BLOCKS)])

    score_t = []
    for g in range(NSA_KV_HEADS):
        gs = slice(g * NSA_HD, (g + 1) * NSA_HD)
        sc = _dot_nt(group_q(g), kc_ref[0, :, gs])
        imp = jnp.zeros((Q_TILE, n_sub), F32)
        for r in range(NSA_REP):
            rs = slice(r * Q_TILE, (r + 1) * Q_TILE)
            bias = _gather_bias(head_table(g * NSA_REP + r),
                                [idxc_ref[0, :, c * LANES:(c + 1) * LANES] for c in range(n_sub // LANES)])
            s_r = sc[rs] + bias
            m = jnp.maximum(jnp.max(s_r, axis=1, keepdims=True), M_INIT)
            e = jnp.exp(s_r - m)
            p = e / jnp.maximum(jnp.sum(e, axis=1, keepdims=True), 1e-30)
            imp = imp + p
            pc_s[rs, :] = p.astype(BF16)
        oc_s[g] = jnp.dot(pc_s[...], vc_ref[0, :, gs], preferred_element_type=F32)
        score = _dot_f32(imp, mmat_ref[...])
        score = jnp.where(forced, FORCE_SCORE, score)
        score = jnp.where(future, NEG, score)
        score = jnp.where(blk >= n_sel, -jnp.inf, score)
        score_t.append(score.T)

    blk_t = lax.broadcasted_iota(jnp.int32, (LANES, Q_TILE), 0).astype(F32)
    sel_t = [jnp.zeros((LANES, Q_TILE), F32) for _ in range(NSA_KV_HEADS)]
    for _ in range(min(SEL_TOPK, n_sel)):
        for g in range(NSA_KV_HEADS):
            mx = jnp.max(score_t[g], axis=0, keepdims=True)
            first = jnp.min(jnp.where(score_t[g] == mx, blk_t, float(LANES)), axis=0, keepdims=True)
            pick = blk_t == first
            sel_t[g] = jnp.where(pick, 1.0, sel_t[g])
            score_t[g] = jnp.where(pick, -jnp.inf, score_t[g])
    for g in range(NSA_KV_HEADS):
        sel_s[g] = sel_t[g].T.astype(BF16)

    n_kt = (s0 + Q_TILE + KEY_TILE - 1) // KEY_TILE
    n_far = jnp.maximum(n_kt - NEAR_TILES, 0)
    blk_row = lax.broadcasted_iota(jnp.int32, (LANES, KEY_TILE), 0)
    key_col = lax.broadcasted_iota(jnp.int32, (LANES, KEY_TILE), 1)

    for g in range(NSA_KV_HEADS):
        gs = slice(g * NSA_HD, (g + 1) * NSA_HD)
        qg = group_q(g)

        m_s[...] = jnp.full((rep_rows, 1), M_INIT, F32)
        l_s[...] = jnp.zeros((rep_rows, 1), F32)
        acc_s[...] = jnp.zeros((rep_rows, NSA_HD), F32)

        def key_tile(kt, near, g=g, gs=gs, qg=qg):
            k0 = pl.multiple_of(kt * KEY_TILE, KEY_TILE)
            s = jnp.dot(qg, kst_ref[0, gs, pl.ds(k0, KEY_TILE)], preferred_element_type=F32)
            expand = jnp.where((k0 + key_col) // SEL_BLOCK == blk_row, 1.0, 0.0).astype(BF16)
            chosen = jnp.dot(sel_s[g], expand, preferred_element_type=F32)
            negm = (chosen - 1.0) * (-NEG)
            for r in range(NSA_REP):
                rs = slice(r * Q_TILE, (r + 1) * Q_TILE)
                h = g * NSA_REP + r
                if near:
                    chunks = []
                    for c in range(KEY_TILE // LANES):
                        k = (s0 - k0) // LANES - c
                        chunks.append(bsel_s[h, jnp.where(k < 0, NEAR_CHUNKS + 1, jnp.minimum(k, NEAR_CHUNKS))])
                    s_r = s[rs] + (negm + jnp.concatenate(chunks, axis=1))
                else:
                    s_r = s[rs] + negm
                m_old = m_s[rs]
                m_new = jnp.maximum(m_old, jnp.max(s_r, axis=1, keepdims=True))
                alpha = jnp.exp(m_old - m_new)
                p = jnp.exp(s_r - m_new)
                l_s[rs] = alpha * l_s[rs] + jnp.sum(p, axis=1, keepdims=True)
                m_s[rs] = m_new
                a_s[rs] = alpha
                ps_s[rs, :] = p.astype(BF16)
            acc_s[...] = a_s[...] * acc_s[...] + jnp.dot(ps_s[...], vs_ref[pl.ds(k0, KEY_TILE), gs],
                                                         preferred_element_type=F32)

        def far_body(kt, carry):
            key_tile(kt, False)
            return carry

        def near_body(kt, carry):
            key_tile(kt, True)
            return carry

        lax.fori_loop(0, n_far, far_body, 0)
        lax.fori_loop(n_far, n_kt, near_body, 0)

        kw = jnp.concatenate([wr[0, :, gs] for wr in win_refs], axis=0)
        vw = jnp.concatenate([wr[0, :, KV_W // 2 + g * NSA_HD:KV_W // 2 + (g + 1) * NSA_HD] for wr in win_refs], axis=0)
        sw = _dot_nt(qg, kw)
        for r in range(NSA_REP):
            rs = slice(r * Q_TILE, (r + 1) * Q_TILE)
            s_r = sw[rs] + bwin_s[g * NSA_REP + r]
            e = jnp.exp(s_r - jnp.max(s_r, axis=1, keepdims=True))
            pw_s[rs, :] = (e / jnp.sum(e, axis=1, keepdims=True)).astype(BF16)
        o_w = jnp.dot(pw_s[...], vw, preferred_element_type=F32)

        for r in range(NSA_REP):
            rs = slice(r * Q_TILE, (r + 1) * Q_TILE)
            h = g * NSA_REP + r
            o_h = (gate_ref[:, 3 * h:3 * h + 1] * oc_s[g, rs, :]
                   + gate_ref[:, 3 * h + 1:3 * h + 2] * (acc_s[rs, :] / l_s[rs])
                   + gate_ref[:, 3 * h + 2:3 * h + 3] * o_w[rs])
            o_s[:, h * NSA_HD:(h + 1) * NSA_HD] = o_h.astype(BF16)

    y_ref[...] = x_ref[...] + jnp.dot(o_s[...], wout_ref[...], preferred_element_type=F32)


def nsa_attend_prompt(q, gates, x, k_c, v_c, kst, selb, winb, rel_bias, w_out, batch, seq):
    n_qt = seq // Q_TILE
    n_sub = k_c.shape[1]
    n_sel = seq // SEL_BLOCK
    assert CMP_LEN == 2 * CMP_STRIDE and SEL_BLOCK == 4 * CMP_STRIDE and SEL_TOPK <= n_sel <= LANES
    win_pad = jnp.pad(winb.reshape(batch, seq, KV_W), ((0, 0), (WINDOW, 0), (0, 0)))
    iq = jnp.arange(Q_TILE, dtype=jnp.int32)
    dist_c = (jnp.arange(n_qt, dtype=jnp.int32)[:, None, None] * Q_TILE + iq[None, :, None]
              - (jnp.arange(n_sub, dtype=jnp.int32)[None, None, :] * CMP_STRIDE + CMP_LEN - 1))
    idx_c = _bucket_index(dist_c, dist_c >= 0)
    dist_w = iq[:, None] - jnp.arange(WINDOW + Q_TILE, dtype=jnp.int32)[None, :] + WINDOW
    idx_w = _bucket_index(dist_w, (dist_w >= 0) & (dist_w < WINDOW))
    dist_n = (jnp.arange(NEAR_CHUNKS, dtype=jnp.int32)[:, None, None] * LANES + iq[None, :, None]
              - jnp.arange(LANES, dtype=jnp.int32)[None, None, :])
    nb = _bucket_index(dist_n, dist_n >= 0)
    ci = np.arange(n_sub)[:, None]
    bj = np.arange(LANES)[None, :]
    mmat = ((ci // 4 == bj).astype(np.float32) + ((ci + 1) // 4 == bj).astype(np.float32)) * (ci < n_sub - 1)

    def rows(width):
        return pl.BlockSpec((Q_TILE, width), lambda b, t: (b * n_qt + t, 0))

    def per_b(shape):
        return pl.BlockSpec((1,) + shape, lambda b, t: (b,) + (0,) * len(shape))

    def const(shape):
        return pl.BlockSpec(shape, lambda b, t: (0,) * len(shape))

    in_specs = [rows(Q_W), rows(LANES), rows(D_MODEL), per_b((n_sub, KV_W // 2)), per_b((n_sub, KV_W // 2)),
                per_b((KV_W // 2, seq)), pl.BlockSpec((seq, KV_W // 2), lambda b, t: (b, 1))]
    in_specs += [pl.BlockSpec((1, Q_TILE, KV_W), lambda b, t, j=j: (b, t + j, 0)) for j in range(WIN_BLOCKS)]
    in_specs += [pl.BlockSpec((1, Q_TILE, n_sub), lambda b, t: (t, 0, 0)), const((Q_TILE, WINDOW + Q_TILE)),
                 const((NEAR_CHUNKS, Q_TILE, LANES)), const((n_sub, LANES)), const((NSA_HEADS, LANES)),
                 const((Q_W, D_MODEL))]
    rep_rows = NSA_REP * Q_TILE
    scratch = [pltpu.VMEM((rep_rows, n_sub), BF16), pltpu.VMEM((NSA_KV_HEADS, rep_rows, NSA_HD), F32),
               pltpu.VMEM((NSA_KV_HEADS, Q_TILE, LANES), BF16), pltpu.VMEM((rep_rows, 1), F32),
               pltpu.VMEM((rep_rows, 1), F32), pltpu.VMEM((rep_rows, 1), F32), pltpu.VMEM((rep_rows, NSA_HD), F32),
               pltpu.VMEM((rep_rows, KEY_TILE), BF16), pltpu.VMEM((rep_rows, WINDOW + Q_TILE), BF16),
               pltpu.VMEM((Q_TILE, Q_W), BF16),
               pltpu.VMEM((NSA_HEADS, NEAR_CHUNKS + 2, Q_TILE, LANES), F32),
               pltpu.VMEM((NSA_HEADS, Q_TILE, WINDOW + Q_TILE), F32)]
    return pl.pallas_call(
        functools.partial(_nsa_attend_body, n_sel=n_sel),
        out_shape=jax.ShapeDtypeStruct((batch * seq, D_MODEL), F32), grid=(batch, n_qt),
        in_specs=in_specs, out_specs=rows(D_MODEL), scratch_shapes=scratch,
        compiler_params=_params(2), name="nsa_attend_prompt",
    )(q, gates, x, k_c.astype(BF16), v_c.astype(BF16), kst, selb, *([win_pad] * WIN_BLOCKS),
      idx_c, idx_w, nb, jnp.asarray(mmat), _bias_table(rel_bias), w_out.astype(BF16))


SCORE_W = 384


def _nsa_sample_cmp_body(q_ref, kc_ref, vc_ref, idx_ref, mmat_ref, gsum_ref, tb_ref, oc_ref, top_ref, *,
                         n_sel, q_pos):
    n_sub = kc_ref.shape[1]
    q = q_ref[0]
    row_g = lax.broadcasted_iota(jnp.int32, (NSA_HEADS, 1), 0) // NSA_REP
    s = jnp.zeros((NSA_HEADS, n_sub), F32)
    for g in range(NSA_KV_HEADS):
        s = jnp.where(row_g == g, _dot_nt(q, kc_ref[0, :, g * NSA_HD:(g + 1) * NSA_HD]), s)
    tb = tb_ref[...]
    bias = jnp.concatenate([jnp.take_along_axis(
        tb, jnp.broadcast_to(idx_ref[:, c * LANES:(c + 1) * LANES], (NSA_HEADS, LANES)), axis=1)
        for c in range(n_sub // LANES)], axis=1)
    s = s + bias
    m = jnp.maximum(jnp.max(s, axis=1, keepdims=True), M_INIT)
    e = jnp.exp(s - m)
    p = e / jnp.maximum(jnp.sum(e, axis=1, keepdims=True), 1e-30)
    pb = p.astype(BF16)
    o = jnp.zeros((NSA_HEADS, NSA_HD), F32)
    for g in range(NSA_KV_HEADS):
        o = jnp.where(row_g == g, jnp.dot(pb, vc_ref[0, :, g * NSA_HD:(g + 1) * NSA_HD],
                                          preferred_element_type=F32), o)
    oc_ref[0] = o
    imp = _dot_f32(gsum_ref[...], p)
    score = _dot_f32(imp, mmat_ref[...])
    blk = lax.broadcasted_iota(jnp.int32, score.shape, 1)
    cur = q_pos // SEL_BLOCK
    score = jnp.where((blk == 0) | (blk == cur) | (blk == cur - 1), FORCE_SCORE, score)
    score = jnp.where(blk * SEL_BLOCK > q_pos, NEG, score)
    score = jnp.where(blk >= n_sel, -jnp.inf, score)
    blkf = blk.astype(F32)
    lane = lax.broadcasted_iota(jnp.int32, (SUBLANES, LANES), 1)
    top = jnp.zeros((SUBLANES, LANES), F32)
    for it in range(min(SEL_TOPK, n_sel)):
        mx = jnp.max(score, axis=1, keepdims=True)
        first = jnp.min(jnp.where(score == mx, blkf, float(SCORE_W)), axis=1, keepdims=True)
        top = jnp.where(lane == it, first, top)
        score = jnp.where(blkf == first, -jnp.inf, score)
    top_ref[0] = top.astype(jnp.int32)


def nsa_sample_cmp(q, k_c, v_c, rel_bias, past_len):
    batch = q.shape[0]
    n_sub = k_c.shape[1]
    n_sel = past_len // SEL_BLOCK + 1
    assert SEL_TOPK <= n_sel <= SCORE_W
    dist = past_len - (jnp.arange(n_sub, dtype=jnp.int32) * CMP_STRIDE + CMP_LEN - 1)
    idx = _bucket_index(dist, dist >= 0).reshape(1, n_sub)
    ci = np.arange(n_sub)[:, None]
    bj = np.arange(SCORE_W)[None, :]
    mmat = ((ci // 4 == bj).astype(np.float32) + ((ci + 1) // 4 == bj).astype(np.float32)) * (ci < n_sub - 1)
    gsum = (np.arange(SUBLANES)[:, None] == np.arange(NSA_HEADS)[None, :] // NSA_REP).astype(np.float32)

    def per_b(shape):
        return pl.BlockSpec((1,) + shape, lambda b: (b,) + (0,) * len(shape))

    def const(shape):
        return pl.BlockSpec(shape, lambda b: (0,) * len(shape))

    o_c, top = pl.pallas_call(
        functools.partial(_nsa_sample_cmp_body, n_sel=n_sel, q_pos=past_len),
        out_shape=(jax.ShapeDtypeStruct((batch, NSA_HEADS, NSA_HD), F32),
                   jax.ShapeDtypeStruct((batch, SUBLANES, LANES), jnp.int32)),
        grid=(batch,),
        in_specs=[per_b((NSA_HEADS, NSA_HD)), per_b((n_sub, KV_W // 2)), per_b((n_sub, KV_W // 2)),
                  const((1, n_sub)), const((n_sub, SCORE_W)), const((SUBLANES, NSA_HEADS)), const((NSA_HEADS, LANES))],
        out_specs=(per_b((NSA_HEADS, NSA_HD)), per_b((SUBLANES, LANES))),
        compiler_params=_params(1), name="nsa_sample_cmp",
    )(q, k_c.astype(BF16), v_c.astype(BF16), idx, jnp.asarray(mmat), jnp.asarray(gsum), _bias_table(rel_bias))
    return o_c, top[:, :NSA_KV_HEADS, :SEL_TOPK]


SEL_PER_STEP = 2


def _nsa_sample_attend_body(row_ref, sidx_ref, *refs, q_pos, n_past_blk, n_steps):
    n_blk_refs = NSA_KV_HEADS * SEL_PER_STEP
    blk_refs = refs[:n_blk_refs]
    (q_ref, knew_ref, win_ref, wnew_ref, gate_ref, oc_ref, idxw_ref, tb_ref, o_ref,
     s_s, v_s, ow_s) = refs[n_blk_refs:]
    b = pl.program_id(0)
    step = pl.program_id(1)
    q = q_ref[0]
    row_g = lax.broadcasted_iota(jnp.int32, (NSA_HEADS, 1), 0) // NSA_REP
    tb = tb_ref[...]
    half = KV_W // 2

    def by_group(fn):
        out = fn(0)
        for g in range(1, NSA_KV_HEADS):
            out = jnp.where(row_g == g, fn(g), out)
        return out

    @pl.when(step == 0)
    def _():
        s_w = by_group(lambda g: _dot(q, win_ref[0, 0, g]))
        bias = jnp.concatenate([jnp.take_along_axis(
            tb, jnp.broadcast_to(idxw_ref[:, c * LANES:(c + 1) * LANES], (NSA_HEADS, LANES)), axis=1)
            for c in range(s_w.shape[1] // LANES)], axis=1)
        s_w = s_w + bias
        k_new = by_group(lambda g: jnp.broadcast_to(wnew_ref[0, :, g * NSA_HD:(g + 1) * NSA_HD], (NSA_HEADS, NSA_HD)))
        v_new = by_group(lambda g: jnp.broadcast_to(wnew_ref[0, :, half + g * NSA_HD:half + (g + 1) * NSA_HD],
                                                    (NSA_HEADS, NSA_HD)))
        s_n = jnp.sum(q.astype(F32) * _r16(k_new), axis=1, keepdims=True) + tb[:, 0:1]
        m = jnp.maximum(jnp.max(s_w, axis=1, keepdims=True), s_n)
        e_w = jnp.exp(s_w - m)
        e_n = jnp.exp(s_n - m)
        total = jnp.sum(e_w, axis=1, keepdims=True) + e_n
        pb = (e_w / total).astype(BF16)
        pv = by_group(lambda g: _dot_nt(pb, win_ref[0, 1, g]))
        ow_s[...] = pv + _r16(e_n / total) * _r16(v_new)

    keys = SEL_PER_STEP * PAGE_ROWS
    key_lane = lax.broadcasted_iota(jnp.int32, (NSA_HD, PAGE_ROWS), 1)
    lane = lax.broadcasted_iota(jnp.int32, (1, PAGE_ROWS), 1)
    thresholds = _t5_thresholds()

    def tile_kv(g, kv):
        parts = []
        for j in range(SEL_PER_STEP):
            is_new = sidx_ref[b, g, step * SEL_PER_STEP + j] >= n_past_blk
            cached = blk_refs[g * SEL_PER_STEP + j][0, kv, g]
            fresh = jnp.where(key_lane == 0, knew_ref[0, kv * half + g * NSA_HD:kv * half + (g + 1) * NSA_HD, :], 0.0)
            parts.append(jnp.where(is_new, fresh, cached))
        return jnp.concatenate(parts, axis=1).astype(BF16)

    def tile_bias(g):
        parts = []
        for j in range(SEL_PER_STEP):
            blk = sidx_ref[b, g, step * SEL_PER_STEP + j]
            is_new = blk >= n_past_blk
            base = jnp.where(is_new, blk * SEL_BLOCK, (blk * SEL_BLOCK) // PAGE_ROWS * PAGE_ROWS)
            pos = base + lane
            dist = q_pos - pos
            idx = jnp.zeros((1, PAGE_ROWS), jnp.int32)
            for thr in thresholds:
                idx = idx + (dist >= thr).astype(jnp.int32)
            parts.append(jnp.where((dist >= 0) & (pos // SEL_BLOCK == blk), idx, MASK_BUCKET))
        idx = jnp.concatenate(parts, axis=1)
        return jnp.concatenate([jnp.take_along_axis(
            tb, jnp.broadcast_to(idx[:, c * LANES:(c + 1) * LANES], (NSA_HEADS, LANES)), axis=1)
            for c in range(keys // LANES)], axis=1)

    k0 = pl.multiple_of(step * keys, keys)
    s_s[:, pl.ds(k0, keys)] = by_group(lambda g: jnp.dot(q, tile_kv(g, 0), preferred_element_type=F32)
                                       + tile_bias(g))
    for g in range(NSA_KV_HEADS):
        v_s[g, :, pl.ds(k0, keys)] = tile_kv(g, 1)

    @pl.when(step == n_steps - 1)
    def _():
        s = s_s[...]
        e = jnp.exp(s - jnp.max(s, axis=1, keepdims=True))
        pb = (e / jnp.sum(e, axis=1, keepdims=True)).astype(BF16)
        o_sel = by_group(lambda g: _dot_nt(pb, v_s[g]))
        gate = gate_ref[0]
        o_ref[0] = gate[:, 0:1] * oc_ref[0] + gate[:, 1:2] * o_sel + gate[:, 2:3] * ow_s[...]


def nsa_sample_attend(q, kv_sel_new, kv_win_new, gates, o_c, top, cache_sel, win_buf, page_table, rel_bias,
                      past_len):
    batch = q.shape[0]
    n_past_blk = past_len // SEL_BLOCK
    bpp = PAGE_ROWS // SEL_BLOCK
    n_steps = SEL_TOPK // SEL_PER_STEP
    page_shape = cache_sel.shape[1:]
    jp = jnp.minimum(top, n_past_blk - 1)
    phys = jnp.take_along_axis(page_table, (jp // bpp).reshape(batch, -1), axis=1).reshape(top.shape)
    phys = phys.astype(jnp.int32)
    wb = win_buf.shape[-1]
    dist_w = past_len - (past_len - wb + jnp.arange(wb, dtype=jnp.int32))
    idx_w = _bucket_index(dist_w, (dist_w >= 0) & (dist_w < WINDOW)).reshape(1, wb)
    gate3 = gates[:, :3 * NSA_HEADS].reshape(batch, NSA_HEADS, 3)

    def blk_spec(g, j):
        return pl.BlockSpec((1,) + page_shape,
                            lambda b, s, rows, sidx, g=g, j=j: (rows[b, g, s * SEL_PER_STEP + j], 0, 0, 0, 0))

    def per_b(shape):
        return pl.BlockSpec((1,) + shape, lambda b, s, rows, sidx: (b,) + (0,) * len(shape))

    def const(shape):
        return pl.BlockSpec(shape, lambda b, s, rows, sidx: (0,) * len(shape))

    grid_spec = pltpu.PrefetchScalarGridSpec(
        num_scalar_prefetch=2, grid=(batch, n_steps),
        in_specs=[blk_spec(g, j) for g in range(NSA_KV_HEADS) for j in range(SEL_PER_STEP)] + [
            per_b((NSA_HEADS, NSA_HD)), per_b((KV_W, 1)), per_b(win_buf.shape[1:]), per_b((1, KV_W)),
            per_b((NSA_HEADS, 3)), per_b((NSA_HEADS, NSA_HD)), const((1, wb)), const((NSA_HEADS, LANES))],
        out_specs=per_b((NSA_HEADS, NSA_HD)),
        scratch_shapes=[pltpu.VMEM((NSA_HEADS, SEL_TOPK * PAGE_ROWS), F32),
                        pltpu.VMEM((NSA_KV_HEADS, NSA_HD, SEL_TOPK * PAGE_ROWS), BF16),
                        pltpu.VMEM((NSA_HEADS, NSA_HD), F32)])
    o = pl.pallas_call(
        functools.partial(_nsa_sample_attend_body, q_pos=past_len, n_past_blk=n_past_blk, n_steps=n_steps),
        out_shape=jax.ShapeDtypeStruct((batch, NSA_HEADS, NSA_HD), F32), grid_spec=grid_spec,
        compiler_params=_params(2), name="nsa_sample_attend",
    )(phys, top.astype(jnp.int32), *([cache_sel] * (NSA_KV_HEADS * SEL_PER_STEP)), q,
      kv_sel_new.reshape(batch, KV_W, 1), win_buf, kv_win_new.reshape(batch, 1, KV_W), gate3, o_c, idx_w,
      _bias_table(rel_bias))
    return o.reshape(batch, Q_W)


def _pad_cols(w, width):
    return jnp.pad(w, ((0, 0), (0, width - w.shape[1]))).astype(BF16)


def kernel(x_prompt, x_sample, state_ret, state_mlstm_C, state_mlstm_n, state_mlstm_m, state_conv, cache_nsa_cmp, cache_nsa_sel, state_nsa_win, page_table, rel_bias, norm_mix, norm_ffn, ab_w_in, ab_conv_w, ab_conv_b, ab_b_igate, ab_b_fgate, ab_gn_g, ab_gn_b, ab_hn_g, ab_w_out, nsa_w_in, nsa_q_norm, nsa_k_norm, nsa_cmp_w1, nsa_cmp_b1, nsa_cmp_w2, nsa_cmp_b2, nsa_w_out, moe_w_group, moe_b_group, moe_w_expert, moe_b_expert, moe_w_gate, moe_w_up, moe_w_down):
    bp, lp, d = x_prompt.shape
    bs, ls, _ = x_sample.shape
    page_size = cache_nsa_cmp.shape[2]
    past_len = page_table.shape[1] * page_size
    assert norm_mix.shape[0] == 2 and ls == 1 and d == D_MODEL and lp % KEY_TILE == 0
    xp = x_prompt.reshape(bp * lp, d)
    xs = x_sample.reshape(bs, d)

    def moe(x, layer):
        return hier_moe_residual(x, norm_ffn[layer], moe_w_group[layer], moe_b_group[layer], moe_w_expert[layer],
                                 moe_b_expert[layer], layer, moe_w_gate, moe_w_up, moe_w_down)

    w_in = _pad_cols(ab_w_in[0], AB_IN_PAD)
    ab = (ab_conv_w[0], ab_conv_b[0], ab_b_igate[0], ab_b_fgate[0], ab_gn_g[0], ab_gn_b[0], ab_hn_g[0])
    zp = norm_matmul(xp, norm_mix[0], w_in)
    xp, ret_p, mc_p, mn_p, mm_p, conv_p = ab_prompt(zp, xp, bp, lp, *ab, ab_w_out[0])
    zs = norm_matmul(xs, norm_mix[0], w_in)
    pos_s = past_len + jnp.arange(ls, dtype=jnp.int32)
    ys, ret_s, mc_s, mn_s, mm_s, conv_s = ab_sample(zs, pos_s, state_ret[0], state_mlstm_C[0], state_mlstm_n[0],
                                                    state_mlstm_m[0], state_conv[0], *ab)
    xs = matmul_residual(ys, ab_w_out[0].astype(BF16), xs)
    xp = moe(xp, 0)
    xs = moe(xs, 0)

    w_in = _pad_cols(nsa_w_in[0], NSA_IN_PAD)
    cmp_w = (nsa_cmp_w1[0], nsa_cmp_b1[0], nsa_cmp_w2[0], nsa_cmp_b2[0], nsa_k_norm[0, 0])
    kv_shape = (2, NSA_KV_HEADS, NSA_HD)
    zp = norm_matmul(xp, norm_mix[1], w_in)
    q, cmp_t, sel_t, win_t, selb, winb, gates, kst = nsa_prep(zp, nsa_q_norm[0], nsa_k_norm[0], batch=bp)
    k_c, v_c = nsa_compress(cmp_t.reshape((bp,) + kv_shape + (lp,)), None, *cmp_w)
    xp = nsa_attend_prompt(q, gates, xp, k_c, v_c, kst, selb, winb, rel_bias, nsa_w_out[0], bp, lp)
    win_keep = min(WINDOW, lp)

    def rows_major(t):
        return jnp.transpose(t.reshape((1, bp) + kv_shape + (t.shape[-1],)), (0, 1, 5, 2, 3, 4))

    cmp_p = rows_major(cmp_t)
    sel_p = rows_major(sel_t)
    win_p = rows_major(win_t[:, :, lp - win_keep:])

    def rows_minor(c):
        return jnp.moveaxis(c, -4, -1)

    zs = norm_matmul(xs, norm_mix[1], w_in)
    q, cmp_s, sel_s, win_s, _, _, gates = nsa_prep(zs, nsa_q_norm[0], nsa_k_norm[0])
    k_c, v_c = nsa_compress(rows_minor(cache_nsa_cmp[0]), page_table, *cmp_w)
    q3 = q.reshape(bs, NSA_HEADS, NSA_HD)
    o_c, top = nsa_sample_cmp(q3, k_c, v_c, rel_bias, past_len)
    win_buf = state_nsa_win[0]
    o = nsa_sample_attend(q3, sel_s, win_s, gates, o_c, top, rows_minor(cache_nsa_sel[0]), rows_minor(win_buf),
                          page_table, rel_bias, past_len)
    xs = matmul_residual(o, nsa_w_out[0].astype(BF16), xs)
    win_s = jnp.concatenate([win_buf, win_s.reshape((bs, ls) + kv_shape)], axis=1)[None, :, ls:]
    cmp_s = cmp_s.reshape((1, bs, ls) + kv_shape)
    sel_s = sel_s.reshape((1, bs, ls) + kv_shape)
    xp = moe(xp, 1)
    xs = moe(xs, 1)

    return (xp.reshape(bp, lp, d), xs.reshape(bs, ls, d), ret_p[None], ret_s[None], mc_p[None], mc_s[None],
            mn_p[None], mn_s[None], mm_p[None], mm_s[None], conv_p[None], conv_s[None],
            cmp_p, cmp_s, sel_p, sel_s, win_p, win_s)
```

```python
import functools
import math

import jax
import jax.numpy as jnp
import numpy as np
from jax import lax
from jax.experimental import pallas as pl
from jax.experimental.pallas import tpu as pltpu

F32 = jnp.float32
BF16 = jnp.bfloat16
LANES = 128
SUBLANES = 8
VMEM_LIMIT = 56 * 1024 * 1024

D_MODEL = 1024
RET_HEADS = 4
ML_HEADS = 4
HEAD_D = 128
CONV_W = 4
CHUNK = 128
ROPE_BASE = 10000.0
AB_IN = 4104
AB_IN_PAD = 4224
NSA_HEADS = 16
NSA_KV_HEADS = 4
NSA_REP = 4
NSA_HD = 64
NSA_IN_PAD = 2688
CMP_LEN = 32
CMP_STRIDE = 16
CMP_HID = 128
SEL_BLOCK = 64
SEL_TOPK = 16
WINDOW = 512
REL_BUCKETS = 32
REL_MAX_DIST = 1024
FORCE_SCORE = 1e4
MOE_GROUPS = 4
MOE_EXP_PER_GROUP = 8
MOE_EXPERTS = 32
NEG = -1e30
EPS = 1e-6


def _params(n_grid):
    return pltpu.CompilerParams(dimension_semantics=("arbitrary",) * n_grid, vmem_limit_bytes=VMEM_LIMIT)


def _dot(a, b):
    return jnp.dot(a.astype(BF16), b.astype(BF16), preferred_element_type=F32)


def _dot_nt(a, b):
    return lax.dot_general(a.astype(BF16), b.astype(BF16), (((1,), (1,)), ((), ())), preferred_element_type=F32)


def _dot_tn(a, b):
    return lax.dot_general(a.astype(BF16), b.astype(BF16), (((0,), (0,)), ((), ())), preferred_element_type=F32)


def _dot_f32(a, b):
    return jnp.dot(a, b, preferred_element_type=F32, precision=lax.Precision.HIGHEST)


def _r16(x):
    return x.astype(BF16).astype(F32)


def _sigmoid(x):
    return 1.0 / (1.0 + jnp.exp(-x))


def _silu(x):
    return x * _sigmoid(x)


def _log_sigmoid(x):
    return -(jnp.maximum(-x, 0.0) + jnp.log1p(jnp.exp(-jnp.abs(x))))


def _norm_matmul_body(x_ref, g_ref, w_ref, o_ref, *, col_tile):
    x = x_ref[...]
    y = x * lax.rsqrt(jnp.mean(x * x, axis=-1, keepdims=True) + EPS) * g_ref[...]
    yb = y.astype(BF16)
    for c0 in range(0, o_ref.shape[1], col_tile):
        o_ref[:, c0:c0 + col_tile] = jnp.dot(yb, w_ref[:, c0:c0 + col_tile], preferred_element_type=F32)


def norm_matmul(x, g, w, row_tile=256):
    n, d = x.shape
    c = w.shape[1]
    tm = min(row_tile, n)
    col_tile = 384 if c % 384 == 0 else LANES
    return pl.pallas_call(
        functools.partial(_norm_matmul_body, col_tile=col_tile),
        out_shape=jax.ShapeDtypeStruct((n, c), F32),
        grid=(n // tm,),
        in_specs=[pl.BlockSpec((tm, d), lambda i: (i, 0)),
                  pl.BlockSpec((1, d), lambda i: (0, 0)),
                  pl.BlockSpec((d, c), lambda i: (0, 0))],
        out_specs=pl.BlockSpec((tm, c), lambda i: (i, 0)),
        compiler_params=_params(1),
        name="norm_matmul",
    )(x, g.reshape(1, d), w)


def _retention_constants(c):
    h = np.arange(RET_HEADS, dtype=np.float64)
    log_g = np.log1p(-np.exp2(-5.0 - h))
    i = np.arange(c, dtype=np.float64)
    diff = i[:, None] - i[None, :]
    decay = np.where(diff >= 0, np.exp(np.maximum(diff, 0.0)[None] * log_g[:, None, None]), 0.0)
    q_dec = np.exp((i + 1.0)[None, :] * log_g[:, None])[:, :, None]
    k_dec = np.exp((c - 1.0 - i)[None, :] * log_g[:, None])[:, :, None]
    s_dec = np.exp(c * log_g)
    return (jnp.asarray(decay, F32), jnp.asarray(q_dec, F32), jnp.asarray(k_dec, F32),
            [float(v) for v in s_dec])


def _rope_tables(pos):
    half = HEAD_D // 2
    freqs = ROPE_BASE ** (-jnp.arange(half, dtype=F32) / half)
    ang = pos.astype(F32)[:, None] * freqs[None, :]
    cos, sin = jnp.cos(ang), jnp.sin(ang)
    return jnp.concatenate([cos, cos], axis=-1), jnp.concatenate([-sin, sin], axis=-1)


def _rope(x, cosf, sinf):
    return x * cosf + pltpu.roll(x, HEAD_D // 2, 1) * sinf


def _ab_prompt_body(rq_ref, rk_ref, rv_ref, rg_ref, mqk_ref, mv_ref, mo_ref, gz_ref, x_ref, cos_ref, sin_ref,
                    decay_ref, qdec_ref, kdec_ref, convw_ref, convb_ref, gbias_ref, gng_ref, gnb_ref, hng_ref,
                    wout_ref,
                    y_ref, s_ref, c_ref, n_ref, m_ref, conv_ref,
                    cbuf_ref, ycat_ref, *, s_dec):
    c = pl.program_id(1)
    tail = CONV_W - 1

    @pl.when(c == 0)
    def _():
        s_ref[...] = jnp.zeros_like(s_ref)
        c_ref[...] = jnp.zeros_like(c_ref)
        n_ref[...] = jnp.zeros_like(n_ref)
        m_ref[...] = jnp.zeros_like(m_ref)
        cbuf_ref[0:SUBLANES, :] = jnp.zeros((SUBLANES, cbuf_ref.shape[1]), F32)

    cosf = cos_ref[...]
    sinf = sin_ref[...]
    row = lax.broadcasted_iota(jnp.int32, (CHUNK, CHUNK), 0)
    col = lax.broadcasted_iota(jnp.int32, (CHUNK, CHUNK), 1)
    eye = row == col
    tril = row >= col
    triu = row <= col

    cbuf_ref[SUBLANES:SUBLANES + CHUNK, :] = mqk_ref[...]
    conv = convb_ref[...]
    for w in range(CONV_W):
        conv = conv + (_r16(cbuf_ref[SUBLANES - tail + w:SUBLANES - tail + w + CHUNK, :])
                       * _r16(convw_ref[w:w + 1, :]))
    qk = _silu(conv)
    last = cbuf_ref[CHUNK + SUBLANES - tail:CHUNK + SUBLANES, :]
    cbuf_ref[SUBLANES - tail:SUBLANES, :] = last
    conv_ref[0] = last

    gz = gz_ref[...] + gbias_ref[...]
    for h in range(RET_HEADS):
        sl = slice(h * HEAD_D, (h + 1) * HEAD_D)
        q = _rope(rq_ref[:, sl], cosf, sinf)
        k = _rope(rk_ref[:, sl], cosf, sinf) * (HEAD_D ** -0.5)
        v = rv_ref[:, sl]
        a = _dot_nt(q, k) * decay_ref[h]
        s_old = s_ref[0, h]
        o = _dot(a, v) + qdec_ref[h] * _dot(q, s_old)
        s_ref[0, h] = s_dec[h] * s_old + _dot_tn(k * kdec_ref[h], v)
        mu = jnp.mean(o, axis=-1, keepdims=True)
        var = jnp.mean(jnp.square(o - mu), axis=-1, keepdims=True)
        o = (o - mu) * lax.rsqrt(var + EPS) * gng_ref[:, sl] + gnb_ref[:, sl]
        ycat_ref[:, sl] = _silu(rg_ref[:, sl]) * o

        mq = qk[:, sl]
        mk = qk[:, ML_HEADS * HEAD_D + h * HEAD_D:ML_HEADS * HEAD_D + (h + 1) * HEAD_D] * (HEAD_D ** -0.5)
        mv = mv_ref[:, sl]
        i_col = gz[:, h:h + 1]
        f_col = _log_sigmoid(gz[:, ML_HEADS + h:ML_HEADS + h + 1])
        i_row = jnp.sum(jnp.where(eye, i_col, 0.0), axis=0, keepdims=True)
        f_row = jnp.sum(jnp.where(eye, f_col, 0.0), axis=0, keepdims=True)
        b_col = jnp.sum(jnp.where(tril, f_row, 0.0), axis=1, keepdims=True)
        b_row = jnp.sum(jnp.where(triu, f_col, 0.0), axis=0, keepdims=True)
        m_old = m_ref[0, h:h + 1, 0:1]
        dlog = jnp.where(tril, b_col - b_row + i_row, -jnp.inf)
        inter = b_col + m_old
        m_t = jnp.maximum(inter, jnp.max(dlog, axis=1, keepdims=True))
        wgt = _dot_nt(mq, mk) * jnp.exp(dlog - m_t)
        e_inter = jnp.exp(inter - m_t)
        c_old = c_ref[0, h]
        n_old = n_ref[0, h:h + 1, :]
        num = _dot(wgt, mv) + e_inter * _dot_nt(mq, c_old)
        den = (jnp.sum(wgt, axis=1, keepdims=True)
               + e_inter * jnp.sum(_r16(mq) * _r16(n_old), axis=1, keepdims=True))
        hc = num / jnp.maximum(jnp.abs(den), jnp.exp(-m_t))
        b_last = b_col[CHUNK - 1:CHUNK, :]
        u_row = b_last - b_row + i_row
        u_col = b_last - b_col + i_col
        m_new = jnp.maximum(b_last + m_old, jnp.max(u_row, axis=1, keepdims=True))
        ws_col = jnp.exp(u_col - m_new)
        f_state = jnp.exp(b_last + m_old - m_new)
        c_ref[0, h] = f_state * c_old + _dot_tn(mv * ws_col, mk)
        n_ref[0, h:h + 1, :] = f_state * n_old + jnp.sum(_r16(ws_col) * _r16(mk), axis=0, keepdims=True)
        m_ref[0, h:h + 1, :] = jnp.broadcast_to(m_new, (1, LANES))
        hm = _sigmoid(mo_ref[:, sl]) * hc
        hm = hm * lax.rsqrt(jnp.mean(hm * hm, axis=-1, keepdims=True) + EPS) * hng_ref[:, sl]
        ycat_ref[:, RET_HEADS * HEAD_D + h * HEAD_D:RET_HEADS * HEAD_D + (h + 1) * HEAD_D] = hm

    y_ref[...] = x_ref[...] + jnp.dot(ycat_ref[...].astype(BF16), wout_ref[...], preferred_element_type=F32)


def ab_prompt(z, x, batch, seq, conv_w, conv_b, b_ig, b_fg, gn_g, gn_b, hn_g, w_out):
    n_chunk = seq // CHUNK
    decay, q_dec, k_dec, s_dec = _retention_constants(CHUNK)
    cosf, sinf = _rope_tables(jnp.arange(seq, dtype=jnp.int32))
    gbias = jnp.zeros((1, LANES), F32).at[0, :ML_HEADS].set(b_ig).at[0, ML_HEADS:2 * ML_HEADS].set(b_fg)
    hw = RET_HEADS * HEAD_D
    qkw = 2 * ML_HEADS * HEAD_D

    def zspec(width, blk):
        return pl.BlockSpec((CHUNK, width), lambda b, c, blk=blk: (b * n_chunk + c, blk))

    def const(shape):
        return pl.BlockSpec(shape, lambda b, c: (0,) * len(shape))

    in_specs = [zspec(hw, 0), zspec(hw, 1), zspec(hw, 2), zspec(hw, 3), zspec(qkw, 2), zspec(hw, 6), zspec(hw, 7),
                zspec(LANES, (AB_IN_PAD - LANES) // LANES),
                pl.BlockSpec((CHUNK, D_MODEL), lambda b, c: (b * n_chunk + c, 0)),
                pl.BlockSpec((CHUNK, HEAD_D), lambda b, c: (c, 0)),
                pl.BlockSpec((CHUNK, HEAD_D), lambda b, c: (c, 0)),
                const((RET_HEADS, CHUNK, CHUNK)), const((RET_HEADS, CHUNK, 1)), const((RET_HEADS, CHUNK, 1)),
                const((CONV_W, qkw)), const((1, qkw)), const((1, LANES)),
                const((1, hw)), const((1, hw)), const((1, hw)), const((2 * hw, D_MODEL))]
    out_shape = (jax.ShapeDtypeStruct((batch * seq, D_MODEL), F32),
                 jax.ShapeDtypeStruct((batch, RET_HEADS, HEAD_D, HEAD_D), F32),
                 jax.ShapeDtypeStruct((batch, ML_HEADS, HEAD_D, HEAD_D), F32),
                 jax.ShapeDtypeStruct((batch, ML_HEADS, HEAD_D), F32),
                 jax.ShapeDtypeStruct((batch, SUBLANES, LANES), F32),
                 jax.ShapeDtypeStruct((batch, CONV_W - 1, qkw), F32))
    out_specs = (pl.BlockSpec((CHUNK, D_MODEL), lambda b, c: (b * n_chunk + c, 0)),
                 pl.BlockSpec((1, RET_HEADS, HEAD_D, HEAD_D), lambda b, c: (b, 0, 0, 0)),
                 pl.BlockSpec((1, ML_HEADS, HEAD_D, HEAD_D), lambda b, c: (b, 0, 0, 0)),
                 pl.BlockSpec((1, ML_HEADS, HEAD_D), lambda b, c: (b, 0, 0)),
                 pl.BlockSpec((1, SUBLANES, LANES), lambda b, c: (b, 0, 0)),
                 pl.BlockSpec((1, CONV_W - 1, qkw), lambda b, c: (b, 0, 0)))
    y, s, cc, n, m, conv = pl.pallas_call(
        functools.partial(_ab_prompt_body, s_dec=s_dec),
        out_shape=out_shape, grid=(batch, n_chunk), in_specs=in_specs, out_specs=out_specs,
        scratch_shapes=[pltpu.VMEM((CHUNK + SUBLANES, qkw), F32), pltpu.VMEM((CHUNK, 2 * hw), F32)],
        compiler_params=_params(2), name="ab_prompt",
    )(z, z, z, z, z, z, z, z, x, cosf, sinf, decay, q_dec, k_dec, conv_w, conv_b.reshape(1, qkw), gbias,
      gn_g.reshape(1, hw), gn_b.reshape(1, hw), hn_g.reshape(1, hw), w_out.astype(BF16))
    return y, s, cc, n, m[:, :ML_HEADS, 0], conv


def _matmul_residual_body(a_ref, w_ref, x_ref, o_ref):
    o_ref[...] = x_ref[...] + jnp.dot(a_ref[...].astype(BF16), w_ref[...], preferred_element_type=F32)


def matmul_residual(a, w, x, row_tile=256):
    n, kk = a.shape
    d = w.shape[1]
    tm = min(row_tile, n)
    return pl.pallas_call(
        _matmul_residual_body, out_shape=jax.ShapeDtypeStruct((n, d), F32), grid=(n // tm,),
        in_specs=[pl.BlockSpec((tm, kk), lambda i: (i, 0)), pl.BlockSpec((kk, d), lambda i: (0, 0)),
                  pl.BlockSpec((tm, d), lambda i: (i, 0))],
        out_specs=pl.BlockSpec((tm, d), lambda i: (i, 0)),
        compiler_params=_params(1), name="matmul_residual",
    )(a, w, x)


def _ab_sample_body(m0_ref, z_ref, cos_ref, sin_ref, s0_ref, c0_ref, n0_ref, conv0_ref,
                    convw_ref, convb_ref, gbias_ref, gng_ref, gnb_ref, hng_ref,
                    y_ref, s_ref, c_ref, n_ref, m_ref, conv_ref, *, g_dec):
    b = pl.program_id(0)
    hw = RET_HEADS * HEAD_D
    qkw = 2 * ML_HEADS * HEAD_D
    tail = CONV_W - 1
    cosf = cos_ref[...]
    sinf = sin_ref[...]
    row = lax.broadcasted_iota(jnp.int32, (HEAD_D, HEAD_D), 0)
    col = lax.broadcasted_iota(jnp.int32, (HEAD_D, HEAD_D), 1)
    eye = row == col

    def to_col(r):
        return jnp.sum(jnp.where(eye, r, 0.0), axis=1, keepdims=True)

    def to_row(cv):
        return jnp.sum(jnp.where(eye, cv, 0.0), axis=0, keepdims=True)

    mqk = z_ref[0, :, 4 * hw:4 * hw + qkw]
    conv = convb_ref[...] + mqk * convw_ref[tail:CONV_W, :]
    for w in range(tail):
        conv = conv + conv0_ref[0, w:w + 1, :] * convw_ref[w:w + 1, :]
    qk = _silu(conv)
    conv_ref[0, 0:tail - 1, :] = conv0_ref[0, 1:tail, :]
    conv_ref[0, tail - 1:tail, :] = mqk
    gz = z_ref[0, :, AB_IN_PAD - LANES:AB_IN_PAD] + gbias_ref[...]

    for h in range(RET_HEADS):
        sl = slice(h * HEAD_D, (h + 1) * HEAD_D)
        q = _rope(z_ref[0, :, sl], cosf, sinf)
        k = _rope(z_ref[0, :, hw + h * HEAD_D:hw + (h + 1) * HEAD_D], cosf, sinf) * (HEAD_D ** -0.5)
        v = z_ref[0, :, 2 * hw + h * HEAD_D:2 * hw + (h + 1) * HEAD_D]
        rg = z_ref[0, :, 3 * hw + h * HEAD_D:3 * hw + (h + 1) * HEAD_D]
        s_old = s0_ref[0, h]
        qk_s = jnp.sum(q * k, axis=1, keepdims=True)
        o = qk_s * v + g_dec[h] * jnp.sum(_r16(to_col(q)) * _r16(s_old), axis=0, keepdims=True)
        s_ref[0, h] = g_dec[h] * s_old + to_col(k) * v
        mu = jnp.mean(o, axis=-1, keepdims=True)
        var = jnp.mean(jnp.square(o - mu), axis=-1, keepdims=True)
        o = (o - mu) * lax.rsqrt(var + EPS) * gng_ref[:, sl] + gnb_ref[:, sl]
        y_ref[0, :, sl] = _silu(rg) * o

        mq = qk[:, sl]
        mk = qk[:, ML_HEADS * HEAD_D + h * HEAD_D:ML_HEADS * HEAD_D + (h + 1) * HEAD_D] * (HEAD_D ** -0.5)
        mv = z_ref[0, :, 4 * hw + qkw + h * HEAD_D:4 * hw + qkw + (h + 1) * HEAD_D]
        mo = z_ref[0, :, 5 * hw + qkw + h * HEAD_D:5 * hw + qkw + (h + 1) * HEAD_D]
        ig = gz[:, h:h + 1]
        lf = _log_sigmoid(gz[:, ML_HEADS + h:ML_HEADS + h + 1])
        m_old = m0_ref[b, h]
        inter = lf + m_old
        m_t = jnp.maximum(inter, ig)
        wgt = jnp.sum(mq * mk, axis=1, keepdims=True) * jnp.exp(ig - m_t)
        e_inter = jnp.exp(inter - m_t)
        c_old = c0_ref[0, h]
        n_old = n0_ref[0, h:h + 1, :]
        cq = to_row(jnp.sum(_r16(c_old) * _r16(mq), axis=1, keepdims=True))
        num = wgt * mv + e_inter * cq
        den = wgt + e_inter * jnp.sum(n_old * mq, axis=1, keepdims=True)
        hc = num / jnp.maximum(jnp.abs(den), jnp.exp(-m_t))
        ws = jnp.exp(ig - m_t)
        c_ref[0, h] = e_inter * c_old + (ws * to_col(mv)) * mk
        n_ref[0, h:h + 1, :] = e_inter * n_old + ws * mk
        m_ref[0, h:h + 1, :] = jnp.broadcast_to(m_t, (1, LANES))
        hm = _sigmoid(mo) * hc
        hm = hm * lax.rsqrt(jnp.mean(hm * hm, axis=-1, keepdims=True) + EPS) * hng_ref[:, sl]
        y_ref[0, :, hw + h * HEAD_D:hw + (h + 1) * HEAD_D] = hm
    m_ref[0, ML_HEADS:SUBLANES, :] = jnp.zeros((SUBLANES - ML_HEADS, LANES), F32)


def ab_sample(z, pos, s0, c0, n0, m0, conv0, conv_w, conv_b, b_ig, b_fg, gn_g, gn_b, hn_g):
    batch = z.shape[0]
    h = np.arange(RET_HEADS, dtype=np.float64)
    g_dec = [float(v) for v in np.exp(np.log1p(-np.exp2(-5.0 - h)))]
    cosf, sinf = _rope_tables(pos)
    gbias = jnp.zeros((1, LANES), F32).at[0, :ML_HEADS].set(b_ig).at[0, ML_HEADS:2 * ML_HEADS].set(b_fg)
    hw = RET_HEADS * HEAD_D
    qkw = 2 * ML_HEADS * HEAD_D

    def per_b(shape):
        return pl.BlockSpec((1,) + shape, lambda b: (b,) + (0,) * len(shape))

    def const(shape):
        return pl.BlockSpec(shape, lambda b: (0,) * len(shape))

    in_specs = [pl.BlockSpec(memory_space=pltpu.SMEM), per_b((1, AB_IN_PAD)), const((1, HEAD_D)), const((1, HEAD_D)),
                per_b((RET_HEADS, HEAD_D, HEAD_D)), per_b((ML_HEADS, HEAD_D, HEAD_D)), per_b((ML_HEADS, HEAD_D)),
                per_b((CONV_W - 1, qkw)), const((CONV_W, qkw)), const((1, qkw)), const((1, LANES)),
                const((1, hw)), const((1, hw)), const((1, hw))]
    out_shape = (jax.ShapeDtypeStruct((batch, 1, 2 * hw), F32),
                 jax.ShapeDtypeStruct((batch, RET_HEADS, HEAD_D, HEAD_D), F32),
                 jax.ShapeDtypeStruct((batch, ML_HEADS, HEAD_D, HEAD_D), F32),
                 jax.ShapeDtypeStruct((batch, ML_HEADS, HEAD_D), F32),
                 jax.ShapeDtypeStruct((batch, SUBLANES, LANES), F32),
                 jax.ShapeDtypeStruct((batch, CONV_W - 1, qkw), F32))
    out_specs = (per_b((1, 2 * hw)), per_b((RET_HEADS, HEAD_D, HEAD_D)), per_b((ML_HEADS, HEAD_D, HEAD_D)),
                 per_b((ML_HEADS, HEAD_D)), per_b((SUBLANES, LANES)), per_b((CONV_W - 1, qkw)))
    y, s, cc, n, m, conv = pl.pallas_call(
        functools.partial(_ab_sample_body, g_dec=g_dec),
        out_shape=out_shape, grid=(batch,), in_specs=in_specs, out_specs=out_specs,
        compiler_params=_params(1), name="ab_sample",
    )(m0, z.reshape(batch, 1, AB_IN_PAD), cosf, sinf, s0, c0, n0, conv0, conv_w, conv_b.reshape(1, qkw), gbias,
      gn_g.reshape(1, hw), gn_b.reshape(1, hw), hn_g.reshape(1, hw))
    return y.reshape(batch, 2 * hw), s, cc, n, m[:, :ML_HEADS, 0], conv


MOE_TILE = 256


def _moe_router_body(x_ref, g_ref, wr_ref, br_ref, hn_ref, route_ref, count_ref):
    x = x_ref[...]
    hn = x * lax.rsqrt(jnp.mean(x * x, axis=-1, keepdims=True) + EPS) * g_ref[...]
    hn_ref[...] = hn
    z = _dot(hn, wr_ref[...]) + br_ref[...]
    lane = lax.broadcasted_iota(jnp.int32, z.shape, 1)
    lanef = lane.astype(F32)
    is_group = lane < MOE_GROUPS
    gl = jnp.where(is_group, z, -jnp.inf)
    gmax = jnp.max(gl, axis=1, keepdims=True)
    g_top = jnp.min(jnp.where(gl == gmax, lanef, float(LANES)), axis=1, keepdims=True)
    pg_top = 1.0 / jnp.sum(jnp.where(is_group, jnp.exp(z - gmax), 0.0), axis=1, keepdims=True)
    grp = ((lane - MOE_GROUPS) // MOE_EXP_PER_GROUP).astype(F32)
    in_group = (lane >= MOE_GROUPS) & (lane < MOE_GROUPS + MOE_EXPERTS) & (grp == g_top)
    el = jnp.where(in_group, z, -jnp.inf)
    v1 = jnp.max(el, axis=1, keepdims=True)
    i1 = jnp.min(jnp.where(el == v1, lanef, float(LANES)), axis=1, keepdims=True)
    el2 = jnp.where(lanef == i1, -jnp.inf, el)
    v2 = jnp.max(el2, axis=1, keepdims=True)
    i2 = jnp.min(jnp.where(el2 == v2, lanef, float(LANES)), axis=1, keepdims=True)
    t = jnp.exp(v2 - v1)
    p1 = 1.0 / (1.0 + t)
    out = jnp.where(lane == 0, i1 - MOE_GROUPS,
                    jnp.where(lane == 1, i2 - MOE_GROUPS,
                              jnp.where(lane == 2, pg_top * p1,
                                        jnp.where(lane == 3, pg_top * (t * p1), 0.0))))
    route_ref[...] = out
    picked = jnp.where((lanef == i1 - MOE_GROUPS) | (lanef == i2 - MOE_GROUPS), 1.0, 0.0)

    @pl.when(pl.program_id(0) == 0)
    def _():
        count_ref[...] = jnp.zeros_like(count_ref)

    count_ref[...] += jnp.sum(picked, axis=0, keepdims=True)


def moe_router(x, g, w_group, b_group, w_expert, b_expert, row_tile=256):
    n, d = x.shape
    tm = min(row_tile, n)
    used = MOE_GROUPS + MOE_EXPERTS
    wr = jnp.pad(jnp.concatenate([w_group, w_expert], axis=1), ((0, 0), (0, LANES - used)))
    br = jnp.pad(jnp.concatenate([b_group, b_expert]), (0, LANES - used)).reshape(1, LANES)
    hn, route, count = pl.pallas_call(
        _moe_router_body,
        out_shape=(jax.ShapeDtypeStruct((n, d), F32), jax.ShapeDtypeStruct((n, LANES), F32),
                   jax.ShapeDtypeStruct((1, LANES), F32)),
        grid=(n // tm,),
        in_specs=[pl.BlockSpec((tm, d), lambda i: (i, 0)), pl.BlockSpec((1, d), lambda i: (0, 0)),
                  pl.BlockSpec((d, LANES), lambda i: (0, 0)), pl.BlockSpec((1, LANES), lambda i: (0, 0))],
        out_specs=(pl.BlockSpec((tm, d), lambda i: (i, 0)), pl.BlockSpec((tm, LANES), lambda i: (i, 0)),
                   pl.BlockSpec((1, LANES), lambda i: (0, 0))),
        compiler_params=_params(1), name="moe_router",
    )(x, g.reshape(1, d), wr, br)
    return hn, route, route[:, 0:2].astype(jnp.int32), count[0, :MOE_EXPERTS].astype(jnp.int32)


def _moe_ffn_body(blk_e_ref, n_real_ref, asg_ref, hn_ref, wg_ref, wu_ref, wd_ref, o_ref,
                  x_s, y_s, wg_s, wu_s, wd_s, gsem, ssem, *, n_blk):
    i = pl.program_id(0)
    slot = i % 2

    def gather_row(tile, r, slot):
        tok = lax.shift_right_logical(asg_ref[tile * MOE_TILE + r], 1)
        return pltpu.make_async_copy(hn_ref.at[pl.ds(tok, 1), :], x_s.at[slot, pl.ds(r, 1), :], gsem.at[slot])

    def scatter_row(r):
        a = asg_ref[i * MOE_TILE + r]
        return pltpu.make_async_copy(y_s.at[pl.ds(r, 1), :],
                                     o_ref.at[a & 1, pl.ds(lax.shift_right_logical(a, 1), 1), :], ssem.at[0])

    def for_rows(n_rows, fn):
        @pl.when(n_rows == MOE_TILE)
        def _():
            for r in range(MOE_TILE):
                fn(r)

        @pl.when(n_rows < MOE_TILE)
        def _():
            def body(r, c):
                fn(r)
                return c
            lax.fori_loop(0, n_rows, body, 0)

    def start_gather(tile, slot):
        for_rows(n_real_ref[tile], lambda r: gather_row(tile, r, slot).start())

    @pl.when(i == 0)
    def _():
        x_s[...] = jnp.zeros_like(x_s)
        start_gather(0, 0)

    for s in range(2):
        @pl.when((i + 1 < n_blk) & (slot == 1 - s))
        def _(s=s):
            start_gather(i + 1, s)

    prev = blk_e_ref[jnp.maximum(i - 1, 0)]

    @pl.when((i == 0) | (blk_e_ref[i] != prev))
    def _():
        wg_s[...] = wg_ref[0].astype(BF16)
        wu_s[...] = wu_ref[0].astype(BF16)
        wd_s[...] = wd_ref[0].astype(BF16)

    n_real = n_real_ref[i]
    for_rows(n_real, lambda r: gather_row(i, 0, slot).wait())

    @pl.when(n_real > 0)
    def _():
        x = x_s[slot].astype(BF16)
        hg = jnp.dot(x, wg_s[...], preferred_element_type=F32)
        hu = jnp.dot(x, wu_s[...], preferred_element_type=F32)
        hb = (_silu(hg) * hu).astype(BF16)
        y_s[...] = jnp.dot(hb, wd_s[...], preferred_element_type=F32)
        for_rows(n_real, lambda r: scatter_row(r).start())
        for_rows(n_real, lambda r: scatter_row(0).wait())


def moe_ffn(hn, asg, n_real, blk_e, layer, w_g, w_u, w_d):
    n_tok, d = hn.shape
    ff = w_g.shape[3]
    n_blk = blk_e.shape[0]
    grid_spec = pltpu.PrefetchScalarGridSpec(
        num_scalar_prefetch=3, grid=(n_blk,),
        in_specs=[pl.BlockSpec(memory_space=pl.ANY),
                  pl.BlockSpec((None, 1, d, ff), lambda i, e, nr, a: (layer, e[i], 0, 0)),
                  pl.BlockSpec((None, 1, d, ff), lambda i, e, nr, a: (layer, e[i], 0, 0)),
                  pl.BlockSpec((None, 1, ff, d), lambda i, e, nr, a: (layer, e[i], 0, 0))],
        out_specs=pl.BlockSpec(memory_space=pl.ANY),
        scratch_shapes=[pltpu.VMEM((2, MOE_TILE, d), F32), pltpu.VMEM((MOE_TILE, d), F32),
                        pltpu.VMEM((d, ff), BF16), pltpu.VMEM((d, ff), BF16), pltpu.VMEM((ff, d), BF16),
                        pltpu.SemaphoreType.DMA((2,)), pltpu.SemaphoreType.DMA((1,))])
    return pl.pallas_call(
        functools.partial(_moe_ffn_body, n_blk=n_blk),
        out_shape=jax.ShapeDtypeStruct((2, n_tok, d), F32), grid_spec=grid_spec,
        compiler_params=_params(1), name="moe_ffn",
    )(blk_e, n_real, asg, hn, w_g, w_u, w_d)


def _moe_combine_body(x_ref, route_ref, y0_ref, y1_ref, o_ref):
    o_ref[...] = x_ref[...] + (y0_ref[...] * route_ref[:, 2:3] + y1_ref[...] * route_ref[:, 3:4])


def moe_combine(x, route, y, row_tile=512):
    n, d = x.shape
    tm = min(row_tile, n)
    return pl.pallas_call(
        _moe_combine_body, out_shape=jax.ShapeDtypeStruct((n, d), F32), grid=(n // tm,),
        in_specs=[pl.BlockSpec((tm, d), lambda i: (i, 0)), pl.BlockSpec((tm, LANES), lambda i: (i, 0)),
                  pl.BlockSpec((None, tm, d), lambda i: (0, i, 0)), pl.BlockSpec((None, tm, d), lambda i: (1, i, 0))],
        out_specs=pl.BlockSpec((tm, d), lambda i: (i, 0)),
        compiler_params=_params(1), name="moe_combine",
    )(x, route, y, y)


def hier_moe_residual(x, g, w_group, b_group, w_expert, b_expert, layer, w_g, w_u, w_d):
    n, d = x.shape
    hn, route, expert, counts = moe_router(x, g, w_group, b_group, w_expert, b_expert)
    n_exp = w_g.shape[1]
    kk = expert.shape[1]
    assert kk == 2
    a = n * kk
    order = jnp.argsort(expert.reshape(-1)).astype(jnp.int32)
    starts = jnp.cumsum(counts) - counts
    padded = (counts + MOE_TILE - 1) // MOE_TILE * MOE_TILE
    pend = jnp.cumsum(padded)
    pstart = pend - padded
    n_blk = -(-(a + n_exp * (MOE_TILE - 1)) // MOE_TILE)
    tile_start = jnp.arange(n_blk, dtype=jnp.int32) * MOE_TILE
    blk_e = jnp.minimum(jnp.sum((pend[None, :] <= tile_start[:, None]).astype(jnp.int32), axis=1), n_exp - 1)
    n_real = jnp.clip(counts[blk_e] - (tile_start - pstart[blk_e]), 0, MOE_TILE).astype(jnp.int32)
    row_off = (tile_start - pstart[blk_e] + starts[blk_e])[:, None] + jnp.arange(MOE_TILE, dtype=jnp.int32)[None, :]
    asg = order[jnp.clip(row_off, 0, a - 1)].reshape(-1)
    y = moe_ffn(hn, asg, n_real, blk_e, layer, w_g, w_u, w_d)
    return moe_combine(x, route, y)


KV_W = 2 * NSA_KV_HEADS * NSA_HD
Q_W = NSA_HEADS * NSA_HD
MASK_BUCKET = REL_BUCKETS
KEY_TILE = 512


def _t5_thresholds():
    exact = REL_BUCKETS // 2
    dist = np.arange(0, 4 * REL_MAX_DIST, dtype=np.int64)
    nf = np.maximum(dist, 1).astype(np.float64)
    large = exact + np.floor(np.log(nf / exact) / math.log(REL_MAX_DIST / exact) * (REL_BUCKETS - exact) + 1e-9)
    bucket = np.where(dist < exact, dist, np.minimum(large, REL_BUCKETS - 1)).astype(np.int64)
    return [int(np.argmax(bucket >= b)) for b in range(1, REL_BUCKETS)]


def _bucket_index(dist, valid):
    idx = jnp.zeros(dist.shape, jnp.int32)
    for thr in _t5_thresholds():
        idx = idx + (dist >= thr).astype(jnp.int32)
    return jnp.where(valid, idx, MASK_BUCKET)


def _bias_table(rel_bias):
    t = jnp.zeros((NSA_HEADS, LANES), F32).at[:, :REL_BUCKETS].set(rel_bias.T.astype(F32))
    return t.at[:, MASK_BUCKET].set(NEG)


def _group_mean_matrix():
    i = np.arange(LANES)
    return jnp.asarray((i[:, None] // NSA_HD == i[None, :] // NSA_HD) / NSA_HD, F32)


def _nsa_prep_body(zq_ref, zc_ref, zs_ref, zw_ref, zg_ref, bd_ref, qn_ref, kns_ref, knw_ref,
                   q_ref, cmp_ref, sel_ref, win_ref, selb_ref, winb_ref, gate_ref, *kt_ref, transposed):
    bd = bd_ref[...]

    def head_norm(x, gain):
        ms = _dot_f32(x * x, bd)
        return x * lax.rsqrt(ms + EPS) * gain

    def emit(o_ref, rows):
        if transposed:
            o_ref[0] = rows.T
        else:
            o_ref[...] = rows

    for c in range(Q_W // LANES):
        sl = slice(c * LANES, (c + 1) * LANES)
        q_ref[:, sl] = (head_norm(zq_ref[:, sl], qn_ref[...]) * (NSA_HD ** -0.5)).astype(BF16)
    emit(cmp_ref, zc_ref[...])
    half = KV_W // 2
    for z_ref, kn_ref, o_ref, ob_ref in ((zs_ref, kns_ref, sel_ref, selb_ref), (zw_ref, knw_ref, win_ref, winb_ref)):
        kn = jnp.concatenate([head_norm(z_ref[:, c * LANES:(c + 1) * LANES], kn_ref[...])
                              for c in range(half // LANES)], axis=1)
        rows = jnp.concatenate([kn, z_ref[:, half:KV_W]], axis=1)
        emit(o_ref, rows)
        ob_ref[...] = rows.astype(BF16)
        if transposed and o_ref is sel_ref:
            kt_ref[0][0] = kn.T.astype(BF16)
    gate_ref[...] = _sigmoid(zg_ref[...])


def nsa_prep(z, q_norm, k_norm, batch=None, row_tile=256):
    n = z.shape[0]
    tm = min(row_tile, n)
    transposed = batch is not None

    def zspec(width, blk):
        return pl.BlockSpec((tm, width), lambda i, blk=blk: (i, blk))

    def const(shape):
        return pl.BlockSpec(shape, lambda i: (0,) * len(shape))

    def tile2(v):
        return jnp.concatenate([v, v]).reshape(1, LANES).astype(F32)

    def rows(w, dt):
        return jax.ShapeDtypeStruct((n, w), dt)

    def out_spec(w):
        return pl.BlockSpec((tm, w), lambda i: (i, 0))

    if transposed:
        seq = n // batch
        tps = seq // tm
        kv_shape = jax.ShapeDtypeStruct((batch, KV_W, seq), F32)
        kv_spec = pl.BlockSpec((1, KV_W, tm), lambda i: (i // tps, 0, i % tps))
        extra_shape = (jax.ShapeDtypeStruct((batch, KV_W // 2, seq), BF16),)
        extra_spec = (pl.BlockSpec((1, KV_W // 2, tm), lambda i: (i // tps, 0, i % tps)),)
    else:
        kv_shape, kv_spec, extra_shape, extra_spec = rows(KV_W, F32), out_spec(KV_W), (), ()
    return pl.pallas_call(
        functools.partial(_nsa_prep_body, transposed=transposed),
        out_shape=(rows(Q_W, BF16), kv_shape, kv_shape, kv_shape, rows(KV_W, BF16), rows(KV_W, BF16),
                   rows(LANES, F32)) + extra_shape,
        grid=(n // tm,),
        in_specs=[zspec(Q_W, 0), zspec(KV_W, 2), zspec(KV_W, 3), zspec(KV_W, 4),
                  zspec(LANES, (Q_W + 3 * KV_W) // LANES), const((LANES, LANES)),
                  const((1, LANES)), const((1, LANES)), const((1, LANES))],
        out_specs=(out_spec(Q_W), kv_spec, kv_spec, kv_spec, out_spec(KV_W), out_spec(KV_W),
                   out_spec(LANES)) + extra_spec,
        compiler_params=_params(1), name="nsa_prep",
    )(z, z, z, z, z, _group_mean_matrix(), tile2(q_norm), tile2(k_norm[1]), tile2(k_norm[2]))


PAGES_PER_STEP = 16
PAGE_ROWS = 128
SUBS_PER_PAGE = PAGE_ROWS // CMP_STRIDE
P_W = 2 * NSA_KV_HEADS * 2 * CMP_HID
KV_PAIRS = NSA_KV_HEADS // 2


def _gelu_tanh(x):
    return 0.5 * x * (1.0 + jnp.tanh(math.sqrt(2.0 / math.pi) * (x + 0.044715 * x * x * x)))


def _compress_body(pt_ref, *refs, n_steps):
    page_refs = refs[:PAGES_PER_STEP]
    w1_ref, b1_ref, w2_ref, b2_ref, kn_ref, kc_ref, vc_ref, p_ref, x_s = refs[PAGES_PER_STEP:]
    j = pl.program_id(1)
    rows = PAGES_PER_STEP * SUBS_PER_PAGE
    r0 = pl.multiple_of(j * rows, rows)
    for t, pr in enumerate(page_refs):
        for v in range(2):
            for gp in range(KV_PAIRS):
                x_s[t, v * KV_PAIRS + gp] = pr[0, v, 2 * gp:2 * gp + 2].reshape(2 * NSA_HD, PAGE_ROWS).T
    for v in range(2):
        for gp in range(KV_PAIRS):
            acc = jnp.zeros((rows, 4 * CMP_HID), F32)
            for s in range(CMP_STRIDE):
                x = jnp.concatenate([x_s[t, v * KV_PAIRS + gp, pl.ds(s, SUBS_PER_PAGE, stride=CMP_STRIDE), :]
                                     for t in range(PAGES_PER_STEP)], axis=0)
                acc = acc + jnp.dot(x.astype(BF16), w1_ref[v, s], preferred_element_type=F32)
            p0 = (v * NSA_KV_HEADS + 2 * gp) * 2 * CMP_HID
            p_ref[pl.ds(r0, rows), p0:p0 + 4 * CMP_HID] = acc

    @pl.when(j == n_steps - 1)
    def _():
        n_sub = p_ref.shape[0]
        for v in range(2):
            for g in range(NSA_KV_HEADS):
                p0 = (v * NSA_KV_HEADS + g) * 2 * CMP_HID
                hs = p_ref[:, p0:p0 + CMP_HID] + pltpu.roll(p_ref[:, p0 + CMP_HID:p0 + 2 * CMP_HID], n_sub - 1, 0)
                hid = _gelu_tanh(hs + b1_ref[v:v + 1, :])
                out = _dot(hid, w2_ref[v]) + b2_ref[v:v + 1, :]
                if v == 0:
                    out = out * lax.rsqrt(jnp.mean(out * out, axis=-1, keepdims=True) + EPS) * kn_ref[...]
                    kc_ref[0, :, g * NSA_HD:(g + 1) * NSA_HD] = out
                else:
                    vc_ref[0, :, g * NSA_HD:(g + 1) * NSA_HD] = out


def nsa_compress(rows_t, page_table, w1, b1, w2, b2, k_norm0):
    if page_table is None:
        batch, n_pp = rows_t.shape[0], rows_t.shape[-1] // PAGE_ROWS
        page_table = jnp.zeros((1, 1), jnp.int32)

        def page_index(b, p, pt):
            return (b, 0, 0, 0, p)
    else:
        batch, n_pp = page_table.shape

        def page_index(b, p, pt):
            return (pt[b, p], 0, 0, 0, 0)
    n_steps = n_pp // PAGES_PER_STEP
    n_sub = n_pp * SUBS_PER_PAGE
    w = w1.reshape(2, 2, CMP_STRIDE, NSA_HD, CMP_HID)
    w = jnp.transpose(w, (0, 2, 3, 1, 4)).reshape(2, CMP_STRIDE, NSA_HD, 2 * CMP_HID)
    zero = jnp.zeros_like(w)
    wpair = jnp.concatenate([jnp.concatenate([w, zero], axis=-1), jnp.concatenate([zero, w], axis=-1)], axis=2)
    wpair = wpair.astype(BF16)

    def page_spec(t):
        return pl.BlockSpec((1, 2, NSA_KV_HEADS, NSA_HD, PAGE_ROWS),
                            lambda b, j, pt, t=t: page_index(b, j * PAGES_PER_STEP + t, pt))

    def const(shape):
        return pl.BlockSpec(shape, lambda b, j, pt: (0,) * len(shape))

    grid_spec = pltpu.PrefetchScalarGridSpec(
        num_scalar_prefetch=1, grid=(batch, n_steps),
        in_specs=[page_spec(t) for t in range(PAGES_PER_STEP)] + [
            const((2, CMP_STRIDE, LANES, 4 * CMP_HID)), const((2, CMP_HID)), const((2, CMP_HID, NSA_HD)),
            const((2, NSA_HD)), const((1, NSA_HD))],
        out_specs=(pl.BlockSpec((1, n_sub, KV_W // 2), lambda b, j, pt: (b, 0, 0)),
                   pl.BlockSpec((1, n_sub, KV_W // 2), lambda b, j, pt: (b, 0, 0))),
        scratch_shapes=[pltpu.VMEM((n_sub, P_W), F32),
                        pltpu.VMEM((PAGES_PER_STEP, 2 * KV_PAIRS, PAGE_ROWS, 2 * NSA_HD), F32)])
    return pl.pallas_call(
        functools.partial(_compress_body, n_steps=n_steps),
        out_shape=(jax.ShapeDtypeStruct((batch, n_sub, KV_W // 2), F32),
                   jax.ShapeDtypeStruct((batch, n_sub, KV_W // 2), F32)),
        grid_spec=grid_spec, compiler_params=_params(2), name="nsa_compress",
    )(page_table, *([rows_t] * PAGES_PER_STEP), wpair, b1, w2.astype(BF16), b2, k_norm0.reshape(1, NSA_HD))


Q_TILE = 128
WIN_BLOCKS = (WINDOW + Q_TILE) // Q_TILE
NEAR_TILES = 3
NEAR_CHUNKS = 8
M_INIT = -1e29


def _gather_bias(tbh, idx_slices):
    return jnp.concatenate([jnp.take_along_axis(tbh, idx, axis=1) for idx in idx_slices], axis=1)


def _nsa_attend_body(q_ref, gate_ref, x_ref, kc_ref, vc_ref, kst_ref, vs_ref, w0_ref, w1_ref, w2_ref, w3_ref, w4_ref,
                     idxc_ref, idxw_ref, nb_ref, mmat_ref, tb_ref, wout_ref, expand_ref, y_ref,
                     pc_s, oc_s, sel_s, m_s, l_s, a_s, acc_s, ps_s, pw_s, o_s, bsel_s, bwin_s, *, n_sel):
    win_refs = (w0_ref, w1_ref, w2_ref, w3_ref, w4_ref)
    t = pl.program_id(1)
    s0 = t * Q_TILE
    n_sub = kc_ref.shape[1]
    rep_rows = NSA_REP * Q_TILE
    q_pos = s0 + lax.broadcasted_iota(jnp.int32, (Q_TILE, LANES), 0)
    blk = lax.broadcasted_iota(jnp.int32, (Q_TILE, LANES), 1)
    cur = q_pos // SEL_BLOCK
    forced = (blk == 0) | (blk == cur) | (blk == cur - 1)
    future = blk * SEL_BLOCK > q_pos

    def group_q(g):
        return jnp.concatenate([q_ref[:, (g * NSA_REP + r) * NSA_HD:(g * NSA_REP + r + 1) * NSA_HD]
                                for r in range(NSA_REP)], axis=0)

    def head_table(h):
        return jnp.broadcast_to(tb_ref[h:h + 1, :], (Q_TILE, LANES))

    @pl.when((pl.program_id(0) == 0) & (t == 0))
    def _():
        for h in range(NSA_HEADS):
            tbh = head_table(h)
            far_bias = tb_ref[h:h + 1, REL_BUCKETS - 1:REL_BUCKETS]
            for k in range(NEAR_CHUNKS):
                bsel_s[h, k] = jnp.take_along_axis(tbh, nb_ref[k], axis=1) - far_bias
            bsel_s[h, NEAR_CHUNKS] = jnp.zeros((Q_TILE, LANES), F32)
            bsel_s[h, NEAR_CHUNKS + 1] = jnp.full((Q_TILE, LANES), NEG, F32)
            bwin_s[h] = _gather_bias(tbh, [idxw_ref[:, c * LANES:(c + 1) * LANES] for c in range(WIN_BLOCKS)])

    score_t = []
    for g in range(NSA_KV_HEADS):
        gs = slice(g * NSA_HD, (g + 1) * NSA_HD)
        sc = _dot_nt(group_q(g), kc_ref[0, :, gs])
        imp = jnp.zeros((Q_TILE, n_sub), F32)
        for r in range(NSA_REP):
            rs = slice(r * Q_TILE, (r + 1) * Q_TILE)
            bias = _gather_bias(head_table(g * NSA_REP + r),
                                [idxc_ref[0, :, c * LANES:(c + 1) * LANES] for c in range(n_sub // LANES)])
            s_r = sc[rs] + bias
            m = jnp.maximum(jnp.max(s_r, axis=1, keepdims=True), M_INIT)
            e = jnp.exp(s_r - m)
            p = e / jnp.maximum(jnp.sum(e, axis=1, keepdims=True), 1e-30)
            imp = imp + p
            pc_s[rs, :] = p.astype(BF16)
        oc_s[g] = jnp.dot(pc_s[...], vc_ref[0, :, gs], preferred_element_type=F32)
        score = _dot_f32(imp, mmat_ref[...])
        score = jnp.where(forced, FORCE_SCORE, score)
        score = jnp.where(future, NEG, score)
        score = jnp.where(blk >= n_sel, -jnp.inf, score)
        score_t.append(score.T)

    blk_t = lax.broadcasted_iota(jnp.int32, (LANES, Q_TILE), 0).astype(F32)
    sel_t = [jnp.zeros((LANES, Q_TILE), F32) for _ in range(NSA_KV_HEADS)]
    for _ in range(min(SEL_TOPK, n_sel)):
        for g in range(NSA_KV_HEADS):
            mx = jnp.max(score_t[g], axis=0, keepdims=True)
            first = jnp.min(jnp.where(score_t[g] == mx, blk_t, float(LANES)), axis=0, keepdims=True)
            pick = blk_t == first
            sel_t[g] = jnp.where(pick, 1.0, sel_t[g])
            score_t[g] = jnp.where(pick, -jnp.inf, score_t[g])
    for g in range(NSA_KV_HEADS):
        sel_s[g] = ((sel_t[g].T - 1.0) * (-NEG)).astype(BF16)

    n_kt = (s0 + Q_TILE + KEY_TILE - 1) // KEY_TILE
    n_far = jnp.maximum(n_kt - NEAR_TILES, 0)

    for g in range(NSA_KV_HEADS):
        gs = slice(g * NSA_HD, (g + 1) * NSA_HD)
        qg = group_q(g)

        m_s[...] = jnp.full((rep_rows, LANES), M_INIT, F32)
        l_s[...] = jnp.zeros((rep_rows, LANES), F32)
        acc_s[...] = jnp.zeros((rep_rows, NSA_HD), F32)

        def key_tile(kt, near, g=g, gs=gs, qg=qg):
            k0 = pl.multiple_of(kt * KEY_TILE, KEY_TILE)
            s = jnp.dot(qg, kst_ref[0, gs, pl.ds(k0, KEY_TILE)], preferred_element_type=F32)
            negm = jnp.dot(sel_s[g], expand_ref[:, pl.ds(k0, KEY_TILE)], preferred_element_type=F32)
            v = vs_ref[pl.ds(k0, KEY_TILE), gs]
            for r in range(NSA_REP):
                rs = slice(r * Q_TILE, (r + 1) * Q_TILE)
                h = g * NSA_REP + r
                if near:
                    chunks = []
                    for c in range(KEY_TILE // LANES):
                        k = (s0 - k0) // LANES - c
                        chunks.append(bsel_s[h, jnp.where(k < 0, NEAR_CHUNKS + 1, jnp.minimum(k, NEAR_CHUNKS))])
                    s_r = s[rs] + (negm + jnp.concatenate(chunks, axis=1))
                else:
                    s_r = s[rs] + negm
                m_old = m_s[rs]
                m_new = jnp.maximum(m_old, jnp.max(s_r, axis=1, keepdims=True))
                alpha = jnp.exp(m_old - m_new)
                p = jnp.exp(s_r - jnp.concatenate([m_new] * (KEY_TILE // LANES), axis=1))
                l_s[rs] = alpha * l_s[rs] + jnp.sum(p, axis=1, keepdims=True)
                m_s[rs] = m_new
                acc_s[rs] = alpha[:, :NSA_HD] * acc_s[rs] + jnp.dot(p.astype(BF16), v, preferred_element_type=F32)

        def far_body(kt, carry):
            key_tile(kt, False)
            return carry

        def near_body(kt, carry):
            key_tile(kt, True)
            return carry

        lax.fori_loop(0, n_far, far_body, 0)
        lax.fori_loop(n_far, n_kt, near_body, 0)

        kw = jnp.concatenate([wr[0, :, gs] for wr in win_refs], axis=0)
        vw = jnp.concatenate([wr[0, :, KV_W // 2 + g * NSA_HD:KV_W // 2 + (g + 1) * NSA_HD] for wr in win_refs], axis=0)
        sw = _dot_nt(qg, kw)
        for r in range(NSA_REP):
            rs = slice(r * Q_TILE, (r + 1) * Q_TILE)
            s_r = sw[rs] + bwin_s[g * NSA_REP + r]
            e = jnp.exp(s_r - jnp.max(s_r, axis=1, keepdims=True))
            pw_s[rs, :] = (e / jnp.sum(e, axis=1, keepdims=True)).astype(BF16)
        o_w = jnp.dot(pw_s[...], vw, preferred_element_type=F32)

        for r in range(NSA_REP):
            rs = slice(r * Q_TILE, (r + 1) * Q_TILE)
            h = g * NSA_REP + r
            o_h = (gate_ref[:, 3 * h:3 * h + 1] * oc_s[g, rs, :]
                   + gate_ref[:, 3 * h + 1:3 * h + 2] * (acc_s[rs, :] / l_s[rs, :NSA_HD])
                   + gate_ref[:, 3 * h + 2:3 * h + 3] * o_w[rs])
            o_s[:, h * NSA_HD:(h + 1) * NSA_HD] = o_h.astype(BF16)

    y_ref[...] = x_ref[...] + jnp.dot(o_s[...], wout_ref[...], preferred_element_type=F32)


def nsa_attend_prompt(q, gates, x, k_c, v_c, kst, selb, winb, rel_bias, w_out, batch, seq):
    n_qt = seq // Q_TILE
    n_sub = k_c.shape[1]
    n_sel = seq // SEL_BLOCK
    assert CMP_LEN == 2 * CMP_STRIDE and SEL_BLOCK == 4 * CMP_STRIDE and SEL_TOPK <= n_sel <= LANES
    win_pad = jnp.pad(winb.reshape(batch, seq, KV_W), ((0, 0), (WINDOW, 0), (0, 0)))
    iq = jnp.arange(Q_TILE, dtype=jnp.int32)
    dist_c = (jnp.arange(n_qt, dtype=jnp.int32)[:, None, None] * Q_TILE + iq[None, :, None]
              - (jnp.arange(n_sub, dtype=jnp.int32)[None, None, :] * CMP_STRIDE + CMP_LEN - 1))
    idx_c = _bucket_index(dist_c, dist_c >= 0)
    dist_w = iq[:, None] - jnp.arange(WINDOW + Q_TILE, dtype=jnp.int32)[None, :] + WINDOW
    idx_w = _bucket_index(dist_w, (dist_w >= 0) & (dist_w < WINDOW))
    dist_n = (jnp.arange(NEAR_CHUNKS, dtype=jnp.int32)[:, None, None] * LANES + iq[None, :, None]
              - jnp.arange(LANES, dtype=jnp.int32)[None, None, :])
    nb = _bucket_index(dist_n, dist_n >= 0)
    ci = np.arange(n_sub)[:, None]
    bj = np.arange(LANES)[None, :]
    mmat = ((ci // 4 == bj).astype(np.float32) + ((ci + 1) // 4 == bj).astype(np.float32)) * (ci < n_sub - 1)
    expand = (np.arange(seq)[None, :] // SEL_BLOCK == np.arange(LANES)[:, None]).astype(np.float32)

    def rows(width):
        return pl.BlockSpec((Q_TILE, width), lambda b, t: (b * n_qt + t, 0))

    def per_b(shape):
        return pl.BlockSpec((1,) + shape, lambda b, t: (b,) + (0,) * len(shape))

    def const(shape):
        return pl.BlockSpec(shape, lambda b, t: (0,) * len(shape))

    in_specs = [rows(Q_W), rows(LANES), rows(D_MODEL), per_b((n_sub, KV_W // 2)), per_b((n_sub, KV_W // 2)),
                per_b((KV_W // 2, seq)), pl.BlockSpec((seq, KV_W // 2), lambda b, t: (b, 1))]
    in_specs += [pl.BlockSpec((1, Q_TILE, KV_W), lambda b, t, j=j: (b, t + j, 0)) for j in range(WIN_BLOCKS)]
    in_specs += [pl.BlockSpec((1, Q_TILE, n_sub), lambda b, t: (t, 0, 0)), const((Q_TILE, WINDOW + Q_TILE)),
                 const((NEAR_CHUNKS, Q_TILE, LANES)), const((n_sub, LANES)), const((NSA_HEADS, LANES)),
                 const((Q_W, D_MODEL)), const((LANES, seq))]
    rep_rows = NSA_REP * Q_TILE
    scratch = [pltpu.VMEM((rep_rows, n_sub), BF16), pltpu.VMEM((NSA_KV_HEADS, rep_rows, NSA_HD), F32),
               pltpu.VMEM((NSA_KV_HEADS, Q_TILE, LANES), BF16), pltpu.VMEM((rep_rows, LANES), F32),
               pltpu.VMEM((rep_rows, LANES), F32), pltpu.VMEM((rep_rows, LANES), F32),
               pltpu.VMEM((rep_rows, NSA_HD), F32),
               pltpu.VMEM((rep_rows, KEY_TILE), BF16), pltpu.VMEM((rep_rows, WINDOW + Q_TILE), BF16),
               pltpu.VMEM((Q_TILE, Q_W), BF16),
               pltpu.VMEM((NSA_HEADS, NEAR_CHUNKS + 2, Q_TILE, LANES), F32),
               pltpu.VMEM((NSA_HEADS, Q_TILE, WINDOW + Q_TILE), F32)]
    return pl.pallas_call(
        functools.partial(_nsa_attend_body, n_sel=n_sel),
        out_shape=jax.ShapeDtypeStruct((batch * seq, D_MODEL), F32), grid=(batch, n_qt),
        in_specs=in_specs, out_specs=rows(D_MODEL), scratch_shapes=scratch,
        compiler_params=_params(2), name="nsa_attend_prompt",
    )(q, gates, x, k_c.astype(BF16), v_c.astype(BF16), kst, selb, *([win_pad] * WIN_BLOCKS),
      idx_c, idx_w, nb, jnp.asarray(mmat), _bias_table(rel_bias), w_out.astype(BF16), jnp.asarray(expand, BF16))


SCORE_W = 384


def _nsa_sample_cmp_body(q_ref, kc_ref, vc_ref, idx_ref, mmat_ref, gsum_ref, tb_ref, oc_ref, top_ref, *,
                         n_sel, q_pos):
    n_sub = kc_ref.shape[1]
    q = q_ref[0]
    row_g = lax.broadcasted_iota(jnp.int32, (NSA_HEADS, 1), 0) // NSA_REP
    s = jnp.zeros((NSA_HEADS, n_sub), F32)
    for g in range(NSA_KV_HEADS):
        s = jnp.where(row_g == g, _dot_nt(q, kc_ref[0, :, g * NSA_HD:(g + 1) * NSA_HD]), s)
    tb = tb_ref[...]
    bias = jnp.concatenate([jnp.take_along_axis(
        tb, jnp.broadcast_to(idx_ref[:, c * LANES:(c + 1) * LANES], (NSA_HEADS, LANES)), axis=1)
        for c in range(n_sub // LANES)], axis=1)
    s = s + bias
    m = jnp.maximum(jnp.max(s, axis=1, keepdims=True), M_INIT)
    e = jnp.exp(s - m)
    p = e / jnp.maximum(jnp.sum(e, axis=1, keepdims=True), 1e-30)
    pb = p.astype(BF16)
    o = jnp.zeros((NSA_HEADS, NSA_HD), F32)
    for g in range(NSA_KV_HEADS):
        o = jnp.where(row_g == g, jnp.dot(pb, vc_ref[0, :, g * NSA_HD:(g + 1) * NSA_HD],
                                          preferred_element_type=F32), o)
    oc_ref[0] = o
    imp = _dot_f32(gsum_ref[...], p)
    score = _dot_f32(imp, mmat_ref[...])
    blk = lax.broadcasted_iota(jnp.int32, score.shape, 1)
    cur = q_pos // SEL_BLOCK
    score = jnp.where((blk == 0) | (blk == cur) | (blk == cur - 1), FORCE_SCORE, score)
    score = jnp.where(blk * SEL_BLOCK > q_pos, NEG, score)
    score = jnp.where(blk >= n_sel, -jnp.inf, score)
    blkf = blk.astype(F32)
    lane = lax.broadcasted_iota(jnp.int32, (SUBLANES, LANES), 1)
    top = jnp.zeros((SUBLANES, LANES), F32)
    for it in range(min(SEL_TOPK, n_sel)):
        mx = jnp.max(score, axis=1, keepdims=True)
        first = jnp.min(jnp.where(score == mx, blkf, float(SCORE_W)), axis=1, keepdims=True)
        top = jnp.where(lane == it, first, top)
        score = jnp.where(blkf == first, -jnp.inf, score)
    top_ref[0] = top.astype(jnp.int32)


def nsa_sample_cmp(q, k_c, v_c, rel_bias, past_len):
    batch = q.shape[0]
    n_sub = k_c.shape[1]
    n_sel = past_len // SEL_BLOCK + 1
    assert SEL_TOPK <= n_sel <= SCORE_W
    dist = past_len - (jnp.arange(n_sub, dtype=jnp.int32) * CMP_STRIDE + CMP_LEN - 1)
    idx = _bucket_index(dist, dist >= 0).reshape(1, n_sub)
    ci = np.arange(n_sub)[:, None]
    bj = np.arange(SCORE_W)[None, :]
    mmat = ((ci // 4 == bj).astype(np.float32) + ((ci + 1) // 4 == bj).astype(np.float32)) * (ci < n_sub - 1)
    gsum = (np.arange(SUBLANES)[:, None] == np.arange(NSA_HEADS)[None, :] // NSA_REP).astype(np.float32)

    def per_b(shape):
        return pl.BlockSpec((1,) + shape, lambda b: (b,) + (0,) * len(shape))

    def const(shape):
        return pl.BlockSpec(shape, lambda b: (0,) * len(shape))

    o_c, top = pl.pallas_call(
        functools.partial(_nsa_sample_cmp_body, n_sel=n_sel, q_pos=past_len),
        out_shape=(jax.ShapeDtypeStruct((batch, NSA_HEADS, NSA_HD), F32),
                   jax.ShapeDtypeStruct((batch, SUBLANES, LANES), jnp.int32)),
        grid=(batch,),
        in_specs=[per_b((NSA_HEADS, NSA_HD)), per_b((n_sub, KV_W // 2)), per_b((n_sub, KV_W // 2)),
                  const((1, n_sub)), const((n_sub, SCORE_W)), const((SUBLANES, NSA_HEADS)), const((NSA_HEADS, LANES))],
        out_specs=(per_b((NSA_HEADS, NSA_HD)), per_b((SUBLANES, LANES))),
        compiler_params=_params(1), name="nsa_sample_cmp",
    )(q, k_c.astype(BF16), v_c.astype(BF16), idx, jnp.asarray(mmat), jnp.asarray(gsum), _bias_table(rel_bias))
    return o_c, top[:, :NSA_KV_HEADS, :SEL_TOPK]


SEL_PER_STEP = 2


def _nsa_sample_attend_body(row_ref, sidx_ref, *refs, q_pos, n_past_blk, n_steps):
    n_blk_refs = NSA_KV_HEADS * SEL_PER_STEP
    blk_refs = refs[:n_blk_refs]
    (q_ref, knew_ref, win_ref, wnew_ref, gate_ref, oc_ref, idxw_ref, tb_ref, o_ref,
     s_s, v_s, ow_s) = refs[n_blk_refs:]
    b = pl.program_id(0)
    step = pl.program_id(1)
    q = q_ref[0]
    row_g = lax.broadcasted_iota(jnp.int32, (NSA_HEADS, 1), 0) // NSA_REP
    tb = tb_ref[...]
    half = KV_W // 2

    def by_group(fn):
        out = fn(0)
        for g in range(1, NSA_KV_HEADS):
            out = jnp.where(row_g == g, fn(g), out)
        return out

    @pl.when(step == 0)
    def _():
        s_w = by_group(lambda g: _dot(q, win_ref[0, 0, g]))
        bias = jnp.concatenate([jnp.take_along_axis(
            tb, jnp.broadcast_to(idxw_ref[:, c * LANES:(c + 1) * LANES], (NSA_HEADS, LANES)), axis=1)
            for c in range(s_w.shape[1] // LANES)], axis=1)
        s_w = s_w + bias
        k_new = by_group(lambda g: jnp.broadcast_to(wnew_ref[0, :, g * NSA_HD:(g + 1) * NSA_HD], (NSA_HEADS, NSA_HD)))
        v_new = by_group(lambda g: jnp.broadcast_to(wnew_ref[0, :, half + g * NSA_HD:half + (g + 1) * NSA_HD],
                                                    (NSA_HEADS, NSA_HD)))
        s_n = jnp.sum(q.astype(F32) * _r16(k_new), axis=1, keepdims=True) + tb[:, 0:1]
        m = jnp.maximum(jnp.max(s_w, axis=1, keepdims=True), s_n)
        e_w = jnp.exp(s_w - m)
        e_n = jnp.exp(s_n - m)
        total = jnp.sum(e_w, axis=1, keepdims=True) + e_n
        pb = (e_w / total).astype(BF16)
        pv = by_group(lambda g: _dot_nt(pb, win_ref[0, 1, g]))
        ow_s[...] = pv + _r16(e_n / total) * _r16(v_new)

    keys = SEL_PER_STEP * PAGE_ROWS
    key_lane = lax.broadcasted_iota(jnp.int32, (NSA_HD, PAGE_ROWS), 1)
    lane = lax.broadcasted_iota(jnp.int32, (1, PAGE_ROWS), 1)
    thresholds = _t5_thresholds()

    def tile_kv(g, kv):
        parts = []
        for j in range(SEL_PER_STEP):
            is_new = sidx_ref[b, g, step * SEL_PER_STEP + j] >= n_past_blk
            cached = blk_refs[g * SEL_PER_STEP + j][0, kv, g]
            fresh = jnp.where(key_lane == 0, knew_ref[0, kv * half + g * NSA_HD:kv * half + (g + 1) * NSA_HD, :], 0.0)
            parts.append(jnp.where(is_new, fresh, cached))
        return jnp.concatenate(parts, axis=1).astype(BF16)

    def tile_bias(g):
        parts = []
        for j in range(SEL_PER_STEP):
            blk = sidx_ref[b, g, step * SEL_PER_STEP + j]
            is_new = blk >= n_past_blk
            base = jnp.where(is_new, blk * SEL_BLOCK, (blk * SEL_BLOCK) // PAGE_ROWS * PAGE_ROWS)
            pos = base + lane
            dist = q_pos - pos
            idx = jnp.zeros((1, PAGE_ROWS), jnp.int32)
            for thr in thresholds:
                idx = idx + (dist >= thr).astype(jnp.int32)
            parts.append(jnp.where((dist >= 0) & (pos // SEL_BLOCK == blk), idx, MASK_BUCKET))
        idx = jnp.concatenate(parts, axis=1)
        return jnp.concatenate([jnp.take_along_axis(
            tb, jnp.broadcast_to(idx[:, c * LANES:(c + 1) * LANES], (NSA_HEADS, LANES)), axis=1)
            for c in range(keys // LANES)], axis=1)

    k0 = pl.multiple_of(step * keys, keys)
    s_s[:, pl.ds(k0, keys)] = by_group(lambda g: jnp.dot(q, tile_kv(g, 0), preferred_element_type=F32)
                                       + tile_bias(g))
    for g in range(NSA_KV_HEADS):
        v_s[g, :, pl.ds(k0, keys)] = tile_kv(g, 1)

    @pl.when(step == n_steps - 1)
    def _():
        s = s_s[...]
        e = jnp.exp(s - jnp.max(s, axis=1, keepdims=True))
        pb = (e / jnp.sum(e, axis=1, keepdims=True)).astype(BF16)
        o_sel = by_group(lambda g: _dot_nt(pb, v_s[g]))
        gate = gate_ref[0]
        o_ref[0] = gate[:, 0:1] * oc_ref[0] + gate[:, 1:2] * o_sel + gate[:, 2:3] * ow_s[...]


def nsa_sample_attend(q, kv_sel_new, kv_win_new, gates, o_c, top, cache_sel, win_buf, page_table, rel_bias,
                      past_len):
    batch = q.shape[0]
    n_past_blk = past_len // SEL_BLOCK
    bpp = PAGE_ROWS // SEL_BLOCK
    n_steps = SEL_TOPK // SEL_PER_STEP
    page_shape = cache_sel.shape[1:]
    jp = jnp.minimum(top, n_past_blk - 1)
    phys = jnp.take_along_axis(page_table, (jp // bpp).reshape(batch, -1), axis=1).reshape(top.shape)
    phys = phys.astype(jnp.int32)
    wb = win_buf.shape[-1]
    dist_w = past_len - (past_len - wb + jnp.arange(wb, dtype=jnp.int32))
    idx_w = _bucket_index(dist_w, (dist_w >= 0) & (dist_w < WINDOW)).reshape(1, wb)
    gate3 = gates[:, :3 * NSA_HEADS].reshape(batch, NSA_HEADS, 3)

    def blk_spec(g, j):
        return pl.BlockSpec((1,) + page_shape,
                            lambda b, s, rows, sidx, g=g, j=j: (rows[b, g, s * SEL_PER_STEP + j], 0, 0, 0, 0))

    def per_b(shape):
        return pl.BlockSpec((1,) + shape, lambda b, s, rows, sidx: (b,) + (0,) * len(shape))

    def const(shape):
        return pl.BlockSpec(shape, lambda b, s, rows, sidx: (0,) * len(shape))

    grid_spec = pltpu.PrefetchScalarGridSpec(
        num_scalar_prefetch=2, grid=(batch, n_steps),
        in_specs=[blk_spec(g, j) for g in range(NSA_KV_HEADS) for j in range(SEL_PER_STEP)] + [
            per_b((NSA_HEADS, NSA_HD)), per_b((KV_W, 1)), per_b(win_buf.shape[1:]), per_b((1, KV_W)),
            per_b((NSA_HEADS, 3)), per_b((NSA_HEADS, NSA_HD)), const((1, wb)), const((NSA_HEADS, LANES))],
        out_specs=per_b((NSA_HEADS, NSA_HD)),
        scratch_shapes=[pltpu.VMEM((NSA_HEADS, SEL_TOPK * PAGE_ROWS), F32),
                        pltpu.VMEM((NSA_KV_HEADS, NSA_HD, SEL_TOPK * PAGE_ROWS), BF16),
                        pltpu.VMEM((NSA_HEADS, NSA_HD), F32)])
    o = pl.pallas_call(
        functools.partial(_nsa_sample_attend_body, q_pos=past_len, n_past_blk=n_past_blk, n_steps=n_steps),
        out_shape=jax.ShapeDtypeStruct((batch, NSA_HEADS, NSA_HD), F32), grid_spec=grid_spec,
        compiler_params=_params(2), name="nsa_sample_attend",
    )(phys, top.astype(jnp.int32), *([cache_sel] * (NSA_KV_HEADS * SEL_PER_STEP)), q,
      kv_sel_new.reshape(batch, KV_W, 1), win_buf, kv_win_new.reshape(batch, 1, KV_W), gate3, o_c, idx_w,
      _bias_table(rel_bias))
    return o.reshape(batch, Q_W)


def _pad_cols(w, width):
    return jnp.pad(w, ((0, 0), (0, width - w.shape[1]))).astype(BF16)


def kernel(x_prompt, x_sample, state_ret, state_mlstm_C, state_mlstm_n, state_mlstm_m, state_conv, cache_nsa_cmp, cache_nsa_sel, state_nsa_win, page_table, rel_bias, norm_mix, norm_ffn, ab_w_in, ab_conv_w, ab_conv_b, ab_b_igate, ab_b_fgate, ab_gn_g, ab_gn_b, ab_hn_g, ab_w_out, nsa_w_in, nsa_q_norm, nsa_k_norm, nsa_cmp_w1, nsa_cmp_b1, nsa_cmp_w2, nsa_cmp_b2, nsa_w_out, moe_w_group, moe_b_group, moe_w_expert, moe_b_expert, moe_w_gate, moe_w_up, moe_w_down):
    bp, lp, d = x_prompt.shape
    bs, ls, _ = x_sample.shape
    page_size = cache_nsa_cmp.shape[2]
    past_len = page_table.shape[1] * page_size
    assert norm_mix.shape[0] == 2 and ls == 1 and d == D_MODEL and lp % KEY_TILE == 0
    xp = x_prompt.reshape(bp * lp, d)
    xs = x_sample.reshape(bs, d)

    def moe(x, layer):
        return hier_moe_residual(x, norm_ffn[layer], moe_w_group[layer], moe_b_group[layer], moe_w_expert[layer],
                                 moe_b_expert[layer], layer, moe_w_gate, moe_w_up, moe_w_down)

    w_in = _pad_cols(ab_w_in[0], AB_IN_PAD)
    ab = (ab_conv_w[0], ab_conv_b[0], ab_b_igate[0], ab_b_fgate[0], ab_gn_g[0], ab_gn_b[0], ab_hn_g[0])
    zp = norm_matmul(xp, norm_mix[0], w_in)
    xp, ret_p, mc_p, mn_p, mm_p, conv_p = ab_prompt(zp, xp, bp, lp, *ab, ab_w_out[0])
    zs = norm_matmul(xs, norm_mix[0], w_in)
    pos_s = past_len + jnp.arange(ls, dtype=jnp.int32)
    ys, ret_s, mc_s, mn_s, mm_s, conv_s = ab_sample(zs, pos_s, state_ret[0], state_mlstm_C[0], state_mlstm_n[0],
                                                    state_mlstm_m[0], state_conv[0], *ab)
    xs = matmul_residual(ys, ab_w_out[0].astype(BF16), xs)
    xp = moe(xp, 0)
    xs = moe(xs, 0)

    w_in = _pad_cols(nsa_w_in[0], NSA_IN_PAD)
    cmp_w = (nsa_cmp_w1[0], nsa_cmp_b1[0], nsa_cmp_w2[0], nsa_cmp_b2[0], nsa_k_norm[0, 0])
    kv_shape = (2, NSA_KV_HEADS, NSA_HD)
    zp = norm_matmul(xp, norm_mix[1], w_in)
    q, cmp_t, sel_t, win_t, selb, winb, gates, kst = nsa_prep(zp, nsa_q_norm[0], nsa_k_norm[0], batch=bp)
    k_c, v_c = nsa_compress(cmp_t.reshape((bp,) + kv_shape + (lp,)), None, *cmp_w)
    xp = nsa_attend_prompt(q, gates, xp, k_c, v_c, kst, selb, winb, rel_bias, nsa_w_out[0], bp, lp)
    win_keep = min(WINDOW, lp)

    def rows_major(t):
        return jnp.transpose(t.reshape((1, bp) + kv_shape + (t.shape[-1],)), (0, 1, 5, 2, 3, 4))

    cmp_p = rows_major(cmp_t)
    sel_p = rows_major(sel_t)
    win_p = rows_major(win_t[:, :, lp - win_keep:])

    def rows_minor(c):
        return jnp.moveaxis(c, -4, -1)

    zs = norm_matmul(xs, norm_mix[1], w_in)
    q, cmp_s, sel_s, win_s, _, _, gates = nsa_prep(zs, nsa_q_norm[0], nsa_k_norm[0])
    k_c, v_c = nsa_compress(rows_minor(cache_nsa_cmp[0]), page_table, *cmp_w)
    q3 = q.reshape(bs, NSA_HEADS, NSA_HD)
    o_c, top = nsa_sample_cmp(q3, k_c, v_c, rel_bias, past_len)
    win_buf = state_nsa_win[0]
    o = nsa_sample_attend(q3, sel_s, win_s, gates, o_c, top, rows_minor(cache_nsa_sel[0]), rows_minor(win_buf),
                          page_table, rel_bias, past_len)
    xs = matmul_residual(o, nsa_w_out[0].astype(BF16), xs)
    win_s = jnp.concatenate([win_buf, win_s.reshape((bs, ls) + kv_shape)], axis=1)[None, :, ls:]
    cmp_s = cmp_s.reshape((1, bs, ls) + kv_shape)
    sel_s = sel_s.reshape((1, bs, ls) + kv_shape)
    xp = moe(xp, 1)
    xs = moe(xs, 1)

    return (xp.reshape(bp, lp, d), xs.reshape(bs, ls, d), ret_p[None], ret_s[None], mc_p[None], mc_s[None],
            mn_p[None], mn_s[None], mm_p[None], mm_s[None], conv_p[None], conv_s[None],
            cmp_p, cmp_s, sel_p, sel_s, win_p, win_s)
```

```python
import functools
import math

import jax
import jax.numpy as jnp
import numpy as np
from jax import lax
from jax.experimental import pallas as pl
from jax.experimental.pallas import tpu as pltpu

F32 = jnp.float32
BF16 = jnp.bfloat16
LANES = 128
SUBLANES = 8
VMEM_LIMIT = 56 * 1024 * 1024

D_MODEL = 1024
RET_HEADS = 4
ML_HEADS = 4
HEAD_D = 128
CONV_W = 4
CHUNK = 128
ROPE_BASE = 10000.0
AB_IN = 4104
AB_IN_PAD = 4224
NSA_HEADS = 16
NSA_KV_HEADS = 4
NSA_REP = 4
NSA_HD = 64
NSA_IN_PAD = 2688
CMP_LEN = 32
CMP_STRIDE = 16
CMP_HID = 128
SEL_BLOCK = 64
SEL_TOPK = 16
WINDOW = 512
REL_BUCKETS = 32
REL_MAX_DIST = 1024
FORCE_SCORE = 1e4
MOE_GROUPS = 4
MOE_EXP_PER_GROUP = 8
MOE_EXPERTS = 32
NEG = -1e30
EPS = 1e-6


def _params(n_grid):
    return pltpu.CompilerParams(dimension_semantics=("arbitrary",) * n_grid, vmem_limit_bytes=VMEM_LIMIT)


def _dot(a, b):
    return jnp.dot(a.astype(BF16), b.astype(BF16), preferred_element_type=F32)


def _dot_nt(a, b):
    return lax.dot_general(a.astype(BF16), b.astype(BF16), (((1,), (1,)), ((), ())), preferred_element_type=F32)


def _dot_tn(a, b):
    return lax.dot_general(a.astype(BF16), b.astype(BF16), (((0,), (0,)), ((), ())), preferred_element_type=F32)


def _dot_f32(a, b):
    return jnp.dot(a, b, preferred_element_type=F32, precision=lax.Precision.HIGHEST)


def _r16(x):
    return x.astype(BF16).astype(F32)


def _sigmoid(x):
    return 1.0 / (1.0 + jnp.exp(-x))


def _silu(x):
    return x * _sigmoid(x)


def _log_sigmoid(x):
    return -(jnp.maximum(-x, 0.0) + jnp.log1p(jnp.exp(-jnp.abs(x))))


def _norm_matmul_body(x_ref, g_ref, w_ref, o_ref, *, col_tile):
    x = x_ref[...]
    y = x * lax.rsqrt(jnp.mean(x * x, axis=-1, keepdims=True) + EPS) * g_ref[...]
    yb = y.astype(BF16)
    for c0 in range(0, o_ref.shape[1], col_tile):
        o_ref[:, c0:c0 + col_tile] = jnp.dot(yb, w_ref[:, c0:c0 + col_tile], preferred_element_type=F32)


def norm_matmul(x, g, w, row_tile=256):
    n, d = x.shape
    c = w.shape[1]
    tm = min(row_tile, n)
    col_tile = 384 if c % 384 == 0 else LANES
    return pl.pallas_call(
        functools.partial(_norm_matmul_body, col_tile=col_tile),
        out_shape=jax.ShapeDtypeStruct((n, c), F32),
        grid=(n // tm,),
        in_specs=[pl.BlockSpec((tm, d), lambda i: (i, 0)),
                  pl.BlockSpec((1, d), lambda i: (0, 0)),
                  pl.BlockSpec((d, c), lambda i: (0, 0))],
        out_specs=pl.BlockSpec((tm, c), lambda i: (i, 0)),
        compiler_params=_params(1),
        name="norm_matmul",
    )(x, g.reshape(1, d), w)


def _retention_constants(c):
    h = np.arange(RET_HEADS, dtype=np.float64)
    log_g = np.log1p(-np.exp2(-5.0 - h))
    i = np.arange(c, dtype=np.float64)
    diff = i[:, None] - i[None, :]
    decay = np.where(diff >= 0, np.exp(np.maximum(diff, 0.0)[None] * log_g[:, None, None]), 0.0)
    q_dec = np.exp((i + 1.0)[None, :] * log_g[:, None])[:, :, None]
    k_dec = np.exp((c - 1.0 - i)[None, :] * log_g[:, None])[:, :, None]
    s_dec = np.exp(c * log_g)
    return (jnp.asarray(decay, F32), jnp.asarray(q_dec, F32), jnp.asarray(k_dec, F32),
            [float(v) for v in s_dec])


def _rope_tables(pos):
    half = HEAD_D // 2
    freqs = ROPE_BASE ** (-jnp.arange(half, dtype=F32) / half)
    ang = pos.astype(F32)[:, None] * freqs[None, :]
    cos, sin = jnp.cos(ang), jnp.sin(ang)
    return jnp.concatenate([cos, cos], axis=-1), jnp.concatenate([-sin, sin], axis=-1)


def _rope(x, cosf, sinf):
    return x * cosf + pltpu.roll(x, HEAD_D // 2, 1) * sinf


def _ab_prompt_body(rq_ref, rk_ref, rv_ref, rg_ref, mqk_ref, mv_ref, mo_ref, gz_ref, x_ref, cos_ref, sin_ref,
                    decay_ref, qdec_ref, kdec_ref, convw_ref, convb_ref, gbias_ref, gng_ref, gnb_ref, hng_ref,
                    wout_ref,
                    y_ref, s_ref, c_ref, n_ref, m_ref, conv_ref,
                    cbuf_ref, ycat_ref, *, s_dec):
    c = pl.program_id(1)
    tail = CONV_W - 1

    @pl.when(c == 0)
    def _():
        s_ref[...] = jnp.zeros_like(s_ref)
        c_ref[...] = jnp.zeros_like(c_ref)
        n_ref[...] = jnp.zeros_like(n_ref)
        m_ref[...] = jnp.zeros_like(m_ref)
        cbuf_ref[0:SUBLANES, :] = jnp.zeros((SUBLANES, cbuf_ref.shape[1]), F32)

    cosf = cos_ref[...]
    sinf = sin_ref[...]
    row = lax.broadcasted_iota(jnp.int32, (CHUNK, CHUNK), 0)
    col = lax.broadcasted_iota(jnp.int32, (CHUNK, CHUNK), 1)
    eye = row == col
    tril = row >= col
    triu = row <= col

    cbuf_ref[SUBLANES:SUBLANES + CHUNK, :] = mqk_ref[...]
    conv = convb_ref[...]
    for w in range(CONV_W):
        conv = conv + (_r16(cbuf_ref[SUBLANES - tail + w:SUBLANES - tail + w + CHUNK, :])
                       * _r16(convw_ref[w:w + 1, :]))
    qk = _silu(conv)
    last = cbuf_ref[CHUNK + SUBLANES - tail:CHUNK + SUBLANES, :]
    cbuf_ref[SUBLANES - tail:SUBLANES, :] = last
    conv_ref[0] = last

    gz = gz_ref[...] + gbias_ref[...]
    for h in range(RET_HEADS):
        sl = slice(h * HEAD_D, (h + 1) * HEAD_D)
        q = _rope(rq_ref[:, sl], cosf, sinf)
        k = _rope(rk_ref[:, sl], cosf, sinf) * (HEAD_D ** -0.5)
        v = rv_ref[:, sl]
        a = _dot_nt(q, k) * decay_ref[h]
        s_old = s_ref[0, h]
        o = _dot(a, v) + qdec_ref[h] * _dot(q, s_old)
        s_ref[0, h] = s_dec[h] * s_old + _dot_tn(k * kdec_ref[h], v)
        mu = jnp.mean(o, axis=-1, keepdims=True)
        var = jnp.mean(jnp.square(o - mu), axis=-1, keepdims=True)
        o = (o - mu) * lax.rsqrt(var + EPS) * gng_ref[:, sl] + gnb_ref[:, sl]
        ycat_ref[:, sl] = _silu(rg_ref[:, sl]) * o

        mq = qk[:, sl]
        mk = qk[:, ML_HEADS * HEAD_D + h * HEAD_D:ML_HEADS * HEAD_D + (h + 1) * HEAD_D] * (HEAD_D ** -0.5)
        mv = mv_ref[:, sl]
        i_col = gz[:, h:h + 1]
        f_col = _log_sigmoid(gz[:, ML_HEADS + h:ML_HEADS + h + 1])
        i_row = jnp.sum(jnp.where(eye, i_col, 0.0), axis=0, keepdims=True)
        f_row = jnp.sum(jnp.where(eye, f_col, 0.0), axis=0, keepdims=True)
        b_col = jnp.sum(jnp.where(tril, f_row, 0.0), axis=1, keepdims=True)
        b_row = jnp.sum(jnp.where(triu, f_col, 0.0), axis=0, keepdims=True)
        m_old = m_ref[0, h:h + 1, 0:1]
        dlog = jnp.where(tril, b_col - b_row + i_row, -jnp.inf)
        inter = b_col + m_old
        m_t = jnp.maximum(inter, jnp.max(dlog, axis=1, keepdims=True))
        wgt = _dot_nt(mq, mk) * jnp.exp(dlog - m_t)
        e_inter = jnp.exp(inter - m_t)
        c_old = c_ref[0, h]
        n_old = n_ref[0, h:h + 1, :]
        num = _dot(wgt, mv) + e_inter * _dot_nt(mq, c_old)
        den = (jnp.sum(wgt, axis=1, keepdims=True)
               + e_inter * jnp.sum(_r16(mq) * _r16(n_old), axis=1, keepdims=True))
        hc = num / jnp.maximum(jnp.abs(den), jnp.exp(-m_t))
        b_last = b_col[CHUNK - 1:CHUNK, :]
        u_row = b_last - b_row + i_row
        u_col = b_last - b_col + i_col
        m_new = jnp.maximum(b_last + m_old, jnp.max(u_row, axis=1, keepdims=True))
        ws_col = jnp.exp(u_col - m_new)
        f_state = jnp.exp(b_last + m_old - m_new)
        c_ref[0, h] = f_state * c_old + _dot_tn(mv * ws_col, mk)
        n_ref[0, h:h + 1, :] = f_state * n_old + jnp.sum(_r16(ws_col) * _r16(mk), axis=0, keepdims=True)
        m_ref[0, h:h + 1, :] = jnp.broadcast_to(m_new, (1, LANES))
        hm = _sigmoid(mo_ref[:, sl]) * hc
        hm = hm * lax.rsqrt(jnp.mean(hm * hm, axis=-1, keepdims=True) + EPS) * hng_ref[:, sl]
        ycat_ref[:, RET_HEADS * HEAD_D + h * HEAD_D:RET_HEADS * HEAD_D + (h + 1) * HEAD_D] = hm

    y_ref[...] = x_ref[...] + jnp.dot(ycat_ref[...].astype(BF16), wout_ref[...], preferred_element_type=F32)


def ab_prompt(z, x, batch, seq, conv_w, conv_b, b_ig, b_fg, gn_g, gn_b, hn_g, w_out):
    n_chunk = seq // CHUNK
    decay, q_dec, k_dec, s_dec = _retention_constants(CHUNK)
    cosf, sinf = _rope_tables(jnp.arange(seq, dtype=jnp.int32))
    gbias = jnp.zeros((1, LANES), F32).at[0, :ML_HEADS].set(b_ig).at[0, ML_HEADS:2 * ML_HEADS].set(b_fg)
    hw = RET_HEADS * HEAD_D
    qkw = 2 * ML_HEADS * HEAD_D

    def zspec(width, blk):
        return pl.BlockSpec((CHUNK, width), lambda b, c, blk=blk: (b * n_chunk + c, blk))

    def const(shape):
        return pl.BlockSpec(shape, lambda b, c: (0,) * len(shape))

    in_specs = [zspec(hw, 0), zspec(hw, 1), zspec(hw, 2), zspec(hw, 3), zspec(qkw, 2), zspec(hw, 6), zspec(hw, 7),
                zspec(LANES, (AB_IN_PAD - LANES) // LANES),
                pl.BlockSpec((CHUNK, D_MODEL), lambda b, c: (b * n_chunk + c, 0)),
                pl.BlockSpec((CHUNK, HEAD_D), lambda b, c: (c, 0)),
                pl.BlockSpec((CHUNK, HEAD_D), lambda b, c: (c, 0)),
                const((RET_HEADS, CHUNK, CHUNK)), const((RET_HEADS, CHUNK, 1)), const((RET_HEADS, CHUNK, 1)),
                const((CONV_W, qkw)), const((1, qkw)), const((1, LANES)),
                const((1, hw)), const((1, hw)), const((1, hw)), const((2 * hw, D_MODEL))]
    out_shape = (jax.ShapeDtypeStruct((batch * seq, D_MODEL), F32),
                 jax.ShapeDtypeStruct((batch, RET_HEADS, HEAD_D, HEAD_D), F32),
                 jax.ShapeDtypeStruct((batch, ML_HEADS, HEAD_D, HEAD_D), F32),
                 jax.ShapeDtypeStruct((batch, ML_HEADS, HEAD_D), F32),
                 jax.ShapeDtypeStruct((batch, SUBLANES, LANES), F32),
                 jax.ShapeDtypeStruct((batch, CONV_W - 1, qkw), F32))
    out_specs = (pl.BlockSpec((CHUNK, D_MODEL), lambda b, c: (b * n_chunk + c, 0)),
                 pl.BlockSpec((1, RET_HEADS, HEAD_D, HEAD_D), lambda b, c: (b, 0, 0, 0)),
                 pl.BlockSpec((1, ML_HEADS, HEAD_D, HEAD_D), lambda b, c: (b, 0, 0, 0)),
                 pl.BlockSpec((1, ML_HEADS, HEAD_D), lambda b, c: (b, 0, 0)),
                 pl.BlockSpec((1, SUBLANES, LANES), lambda b, c: (b, 0, 0)),
                 pl.BlockSpec((1, CONV_W - 1, qkw), lambda b, c: (b, 0, 0)))
    y, s, cc, n, m, conv = pl.pallas_call(
        functools.partial(_ab_prompt_body, s_dec=s_dec),
        out_shape=out_shape, grid=(batch, n_chunk), in_specs=in_specs, out_specs=out_specs,
        scratch_shapes=[pltpu.VMEM((CHUNK + SUBLANES, qkw), F32), pltpu.VMEM((CHUNK, 2 * hw), F32)],
        compiler_params=_params(2), name="ab_prompt",
    )(z, z, z, z, z, z, z, z, x, cosf, sinf, decay, q_dec, k_dec, conv_w, conv_b.reshape(1, qkw), gbias,
      gn_g.reshape(1, hw), gn_b.reshape(1, hw), hn_g.reshape(1, hw), w_out.astype(BF16))
    return y, s, cc, n, m[:, :ML_HEADS, 0], conv


def _matmul_residual_body(a_ref, w_ref, x_ref, o_ref):
    o_ref[...] = x_ref[...] + jnp.dot(a_ref[...].astype(BF16), w_ref[...], preferred_element_type=F32)


def matmul_residual(a, w, x, row_tile=256):
    n, kk = a.shape
    d = w.shape[1]
    tm = min(row_tile, n)
    return pl.pallas_call(
        _matmul_residual_body, out_shape=jax.ShapeDtypeStruct((n, d), F32), grid=(n // tm,),
        in_specs=[pl.BlockSpec((tm, kk), lambda i: (i, 0)), pl.BlockSpec((kk, d), lambda i: (0, 0)),
                  pl.BlockSpec((tm, d), lambda i: (i, 0))],
        out_specs=pl.BlockSpec((tm, d), lambda i: (i, 0)),
        compiler_params=_params(1), name="matmul_residual",
    )(a, w, x)


def _ab_sample_body(m0_ref, z_ref, cos_ref, sin_ref, s0_ref, c0_ref, n0_ref, conv0_ref,
                    convw_ref, convb_ref, gbias_ref, gng_ref, gnb_ref, hng_ref,
                    y_ref, s_ref, c_ref, n_ref, m_ref, conv_ref, *, g_dec):
    b = pl.program_id(0)
    hw = RET_HEADS * HEAD_D
    qkw = 2 * ML_HEADS * HEAD_D
    tail = CONV_W - 1
    cosf = cos_ref[...]
    sinf = sin_ref[...]
    row = lax.broadcasted_iota(jnp.int32, (HEAD_D, HEAD_D), 0)
    col = lax.broadcasted_iota(jnp.int32, (HEAD_D, HEAD_D), 1)
    eye = row == col

    def to_col(r):
        return jnp.sum(jnp.where(eye, r, 0.0), axis=1, keepdims=True)

    def to_row(cv):
        return jnp.sum(jnp.where(eye, cv, 0.0), axis=0, keepdims=True)

    mqk = z_ref[0, :, 4 * hw:4 * hw + qkw]
    conv = convb_ref[...] + mqk * convw_ref[tail:CONV_W, :]
    for w in range(tail):
        conv = conv + conv0_ref[0, w:w + 1, :] * convw_ref[w:w + 1, :]
    qk = _silu(conv)
    conv_ref[0, 0:tail - 1, :] = conv0_ref[0, 1:tail, :]
    conv_ref[0, tail - 1:tail, :] = mqk
    gz = z_ref[0, :, AB_IN_PAD - LANES:AB_IN_PAD] + gbias_ref[...]

    for h in range(RET_HEADS):
        sl = slice(h * HEAD_D, (h + 1) * HEAD_D)
        q = _rope(z_ref[0, :, sl], cosf, sinf)
        k = _rope(z_ref[0, :, hw + h * HEAD_D:hw + (h + 1) * HEAD_D], cosf, sinf) * (HEAD_D ** -0.5)
        v = z_ref[0, :, 2 * hw + h * HEAD_D:2 * hw + (h + 1) * HEAD_D]
        rg = z_ref[0, :, 3 * hw + h * HEAD_D:3 * hw + (h + 1) * HEAD_D]
        s_old = s0_ref[0, h]
        qk_s = jnp.sum(q * k, axis=1, keepdims=True)
        o = qk_s * v + g_dec[h] * jnp.sum(_r16(to_col(q)) * _r16(s_old), axis=0, keepdims=True)
        s_ref[0, h] = g_dec[h] * s_old + to_col(k) * v
        mu = jnp.mean(o, axis=-1, keepdims=True)
        var = jnp.mean(jnp.square(o - mu), axis=-1, keepdims=True)
        o = (o - mu) * lax.rsqrt(var + EPS) * gng_ref[:, sl] + gnb_ref[:, sl]
        y_ref[0, :, sl] = _silu(rg) * o

        mq = qk[:, sl]
        mk = qk[:, ML_HEADS * HEAD_D + h * HEAD_D:ML_HEADS * HEAD_D + (h + 1) * HEAD_D] * (HEAD_D ** -0.5)
        mv = z_ref[0, :, 4 * hw + qkw + h * HEAD_D:4 * hw + qkw + (h + 1) * HEAD_D]
        mo = z_ref[0, :, 5 * hw + qkw + h * HEAD_D:5 * hw + qkw + (h + 1) * HEAD_D]
        ig = gz[:, h:h + 1]
        lf = _log_sigmoid(gz[:, ML_HEADS + h:ML_HEADS + h + 1])
        m_old = m0_ref[b, h]
        inter = lf + m_old
        m_t = jnp.maximum(inter, ig)
        wgt = jnp.sum(mq * mk, axis=1, keepdims=True) * jnp.exp(ig - m_t)
        e_inter = jnp.exp(inter - m_t)
        c_old = c0_ref[0, h]
        n_old = n0_ref[0, h:h + 1, :]
        cq = to_row(jnp.sum(_r16(c_old) * _r16(mq), axis=1, keepdims=True))
        num = wgt * mv + e_inter * cq
        den = wgt + e_inter * jnp.sum(n_old * mq, axis=1, keepdims=True)
        hc = num / jnp.maximum(jnp.abs(den), jnp.exp(-m_t))
        ws = jnp.exp(ig - m_t)
        c_ref[0, h] = e_inter * c_old + (ws * to_col(mv)) * mk
        n_ref[0, h:h + 1, :] = e_inter * n_old + ws * mk
        m_ref[0, h:h + 1, :] = jnp.broadcast_to(m_t, (1, LANES))
        hm = _sigmoid(mo) * hc
        hm = hm * lax.rsqrt(jnp.mean(hm * hm, axis=-1, keepdims=True) + EPS) * hng_ref[:, sl]
        y_ref[0, :, hw + h * HEAD_D:hw + (h + 1) * HEAD_D] = hm
    m_ref[0, ML_HEADS:SUBLANES, :] = jnp.zeros((SUBLANES - ML_HEADS, LANES), F32)


def ab_sample(z, pos, s0, c0, n0, m0, conv0, conv_w, conv_b, b_ig, b_fg, gn_g, gn_b, hn_g):
    batch = z.shape[0]
    h = np.arange(RET_HEADS, dtype=np.float64)
    g_dec = [float(v) for v in np.exp(np.log1p(-np.exp2(-5.0 - h)))]
    cosf, sinf = _rope_tables(pos)
    gbias = jnp.zeros((1, LANES), F32).at[0, :ML_HEADS].set(b_ig).at[0, ML_HEADS:2 * ML_HEADS].set(b_fg)
    hw = RET_HEADS * HEAD_D
    qkw = 2 * ML_HEADS * HEAD_D

    def per_b(shape):
        return pl.BlockSpec((1,) + shape, lambda b: (b,) + (0,) * len(shape))

    def const(shape):
        return pl.BlockSpec(shape, lambda b: (0,) * len(shape))

    in_specs = [pl.BlockSpec(memory_space=pltpu.SMEM), per_b((1, AB_IN_PAD)), const((1, HEAD_D)), const((1, HEAD_D)),
                per_b((RET_HEADS, HEAD_D, HEAD_D)), per_b((ML_HEADS, HEAD_D, HEAD_D)), per_b((ML_HEADS, HEAD_D)),
                per_b((CONV_W - 1, qkw)), const((CONV_W, qkw)), const((1, qkw)), const((1, LANES)),
                const((1, hw)), const((1, hw)), const((1, hw))]
    out_shape = (jax.ShapeDtypeStruct((batch, 1, 2 * hw), F32),
                 jax.ShapeDtypeStruct((batch, RET_HEADS, HEAD_D, HEAD_D), F32),
                 jax.ShapeDtypeStruct((batch, ML_HEADS, HEAD_D, HEAD_D), F32),
                 jax.ShapeDtypeStruct((batch, ML_HEADS, HEAD_D), F32),
                 jax.ShapeDtypeStruct((batch, SUBLANES, LANES), F32),
                 jax.ShapeDtypeStruct((batch, CONV_W - 1, qkw), F32))
    out_specs = (per_b((1, 2 * hw)), per_b((RET_HEADS, HEAD_D, HEAD_D)), per_b((ML_HEADS, HEAD_D, HEAD_D)),
                 per_b((ML_HEADS, HEAD_D)), per_b((SUBLANES, LANES)), per_b((CONV_W - 1, qkw)))
    y, s, cc, n, m, conv = pl.pallas_call(
        functools.partial(_ab_sample_body, g_dec=g_dec),
        out_shape=out_shape, grid=(batch,), in_specs=in_specs, out_specs=out_specs,
        compiler_params=_params(1), name="ab_sample",
    )(m0, z.reshape(batch, 1, AB_IN_PAD), cosf, sinf, s0, c0, n0, conv0, conv_w, conv_b.reshape(1, qkw), gbias,
      gn_g.reshape(1, hw), gn_b.reshape(1, hw), hn_g.reshape(1, hw))
    return y.reshape(batch, 2 * hw), s, cc, n, m[:, :ML_HEADS, 0], conv


MOE_TILE = 256


def _moe_router_body(x_ref, g_ref, wr_ref, br_ref, hn_ref, route_ref, count_ref):
    x = x_ref[...]
    hn = x * lax.rsqrt(jnp.mean(x * x, axis=-1, keepdims=True) + EPS) * g_ref[...]
    hn_ref[...] = hn
    z = _dot(hn, wr_ref[...]) + br_ref[...]
    lane = lax.broadcasted_iota(jnp.int32, z.shape, 1)
    lanef = lane.astype(F32)
    is_group = lane < MOE_GROUPS
    gl = jnp.where(is_group, z, -jnp.inf)
    gmax = jnp.max(gl, axis=1, keepdims=True)
    g_top = jnp.min(jnp.where(gl == gmax, lanef, float(LANES)), axis=1, keepdims=True)
    pg_top = 1.0 / jnp.sum(jnp.where(is_group, jnp.exp(z - gmax), 0.0), axis=1, keepdims=True)
    grp = ((lane - MOE_GROUPS) // MOE_EXP_PER_GROUP).astype(F32)
    in_group = (lane >= MOE_GROUPS) & (lane < MOE_GROUPS + MOE_EXPERTS) & (grp == g_top)
    el = jnp.where(in_group, z, -jnp.inf)
    v1 = jnp.max(el, axis=1, keepdims=True)
    i1 = jnp.min(jnp.where(el == v1, lanef, float(LANES)), axis=1, keepdims=True)
    el2 = jnp.where(lanef == i1, -jnp.inf, el)
    v2 = jnp.max(el2, axis=1, keepdims=True)
    i2 = jnp.min(jnp.where(el2 == v2, lanef, float(LANES)), axis=1, keepdims=True)
    t = jnp.exp(v2 - v1)
    p1 = 1.0 / (1.0 + t)
    out = jnp.where(lane == 0, i1 - MOE_GROUPS,
                    jnp.where(lane == 1, i2 - MOE_GROUPS,
                              jnp.where(lane == 2, pg_top * p1,
                                        jnp.where(lane == 3, pg_top * (t * p1), 0.0))))
    route_ref[...] = out
    picked = jnp.where((lanef == i1 - MOE_GROUPS) | (lanef == i2 - MOE_GROUPS), 1.0, 0.0)

    @pl.when(pl.program_id(0) == 0)
    def _():
        count_ref[...] = jnp.zeros_like(count_ref)

    count_ref[...] += jnp.sum(picked, axis=0, keepdims=True)


def moe_router(x, g, w_group, b_group, w_expert, b_expert, row_tile=256):
    n, d = x.shape
    tm = min(row_tile, n)
    used = MOE_GROUPS + MOE_EXPERTS
    wr = jnp.pad(jnp.concatenate([w_group, w_expert], axis=1), ((0, 0), (0, LANES - used)))
    br = jnp.pad(jnp.concatenate([b_group, b_expert]), (0, LANES - used)).reshape(1, LANES)
    hn, route, count = pl.pallas_call(
        _moe_router_body,
        out_shape=(jax.ShapeDtypeStruct((n, d), F32), jax.ShapeDtypeStruct((n, LANES), F32),
                   jax.ShapeDtypeStruct((1, LANES), F32)),
        grid=(n // tm,),
        in_specs=[pl.BlockSpec((tm, d), lambda i: (i, 0)), pl.BlockSpec((1, d), lambda i: (0, 0)),
                  pl.BlockSpec((d, LANES), lambda i: (0, 0)), pl.BlockSpec((1, LANES), lambda i: (0, 0))],
        out_specs=(pl.BlockSpec((tm, d), lambda i: (i, 0)), pl.BlockSpec((tm, LANES), lambda i: (i, 0)),
                   pl.BlockSpec((1, LANES), lambda i: (0, 0))),
        compiler_params=_params(1), name="moe_router",
    )(x, g.reshape(1, d), wr, br)
    return hn, route, route[:, 0:2].astype(jnp.int32), count[0, :MOE_EXPERTS].astype(jnp.int32)


def _moe_ffn_body(blk_e_ref, n_real_ref, asg_ref, hn_ref, wg_ref, wu_ref, wd_ref, o_ref,
                  x_s, y_s, wg_s, wu_s, wd_s, gsem, ssem, *, n_blk):
    i = pl.program_id(0)
    slot = i % 2

    def gather_row(tile, r, slot):
        tok = lax.shift_right_logical(asg_ref[tile * MOE_TILE + r], 1)
        return pltpu.make_async_copy(hn_ref.at[pl.ds(tok, 1), :], x_s.at[slot, pl.ds(r, 1), :], gsem.at[slot])

    def scatter_row(r):
        a = asg_ref[i * MOE_TILE + r]
        return pltpu.make_async_copy(y_s.at[pl.ds(r, 1), :],
                                     o_ref.at[a & 1, pl.ds(lax.shift_right_logical(a, 1), 1), :], ssem.at[0])

    def for_rows(n_rows, fn):
        @pl.when(n_rows == MOE_TILE)
        def _():
            for r in range(MOE_TILE):
                fn(r)

        @pl.when(n_rows < MOE_TILE)
        def _():
            def body(r, c):
                fn(r)
                return c
            lax.fori_loop(0, n_rows, body, 0)

    def start_gather(tile, slot):
        for_rows(n_real_ref[tile], lambda r: gather_row(tile, r, slot).start())

    @pl.when(i == 0)
    def _():
        x_s[...] = jnp.zeros_like(x_s)
        start_gather(0, 0)

    for s in range(2):
        @pl.when((i + 1 < n_blk) & (slot == 1 - s))
        def _(s=s):
            start_gather(i + 1, s)

    prev = blk_e_ref[jnp.maximum(i - 1, 0)]

    @pl.when((i == 0) | (blk_e_ref[i] != prev))
    def _():
        wg_s[...] = wg_ref[0].astype(BF16)
        wu_s[...] = wu_ref[0].astype(BF16)
        wd_s[...] = wd_ref[0].astype(BF16)

    n_real = n_real_ref[i]
    for_rows(n_real, lambda r: gather_row(i, 0, slot).wait())
    n_prev = jnp.where(i > 0, n_real_ref[jnp.maximum(i - 1, 0)], 0)
    for_rows(n_prev, lambda r: scatter_row(0).wait())

    @pl.when(n_real > 0)
    def _():
        x = x_s[slot].astype(BF16)
        hg = jnp.dot(x, wg_s[...], preferred_element_type=F32)
        hu = jnp.dot(x, wu_s[...], preferred_element_type=F32)
        hb = (_silu(hg) * hu).astype(BF16)
        y_s[...] = jnp.dot(hb, wd_s[...], preferred_element_type=F32)
        for_rows(n_real, lambda r: scatter_row(r).start())

    @pl.when(i == n_blk - 1)
    def _():
        for_rows(n_real, lambda r: scatter_row(0).wait())


def moe_ffn(hn, asg, n_real, blk_e, layer, w_g, w_u, w_d):
    n_tok, d = hn.shape
    ff = w_g.shape[3]
    n_blk = blk_e.shape[0]
    grid_spec = pltpu.PrefetchScalarGridSpec(
        num_scalar_prefetch=3, grid=(n_blk,),
        in_specs=[pl.BlockSpec(memory_space=pl.ANY),
                  pl.BlockSpec((None, 1, d, ff), lambda i, e, nr, a: (layer, e[i], 0, 0)),
                  pl.BlockSpec((None, 1, d, ff), lambda i, e, nr, a: (layer, e[i], 0, 0)),
                  pl.BlockSpec((None, 1, ff, d), lambda i, e, nr, a: (layer, e[i], 0, 0))],
        out_specs=pl.BlockSpec(memory_space=pl.ANY),
        scratch_shapes=[pltpu.VMEM((2, MOE_TILE, d), F32), pltpu.VMEM((MOE_TILE, d), F32),
                        pltpu.VMEM((d, ff), BF16), pltpu.VMEM((d, ff), BF16), pltpu.VMEM((ff, d), BF16),
                        pltpu.SemaphoreType.DMA((2,)), pltpu.SemaphoreType.DMA((1,))])
    return pl.pallas_call(
        functools.partial(_moe_ffn_body, n_blk=n_blk),
        out_shape=jax.ShapeDtypeStruct((2, n_tok, d), F32), grid_spec=grid_spec,
        compiler_params=_params(1), name="moe_ffn",
    )(blk_e, n_real, asg, hn, w_g, w_u, w_d)


def _moe_combine_body(x_ref, route_ref, y0_ref, y1_ref, o_ref):
    o_ref[...] = x_ref[...] + (y0_ref[...] * route_ref[:, 2:3] + y1_ref[...] * route_ref[:, 3:4])


def moe_combine(x, route, y, row_tile=512):
    n, d = x.shape
    tm = min(row_tile, n)
    return pl.pallas_call(
        _moe_combine_body, out_shape=jax.ShapeDtypeStruct((n, d), F32), grid=(n // tm,),
        in_specs=[pl.BlockSpec((tm, d), lambda i: (i, 0)), pl.BlockSpec((tm, LANES), lambda i: (i, 0)),
                  pl.BlockSpec((None, tm, d), lambda i: (0, i, 0)), pl.BlockSpec((None, tm, d), lambda i: (1, i, 0))],
        out_specs=pl.BlockSpec((tm, d), lambda i: (i, 0)),
        compiler_params=_params(1), name="moe_combine",
    )(x, route, y, y)


def hier_moe_residual(x, g, w_group, b_group, w_expert, b_expert, layer, w_g, w_u, w_d):
    n, d = x.shape
    hn, route, expert, counts = moe_router(x, g, w_group, b_group, w_expert, b_expert)
    n_exp = w_g.shape[1]
    kk = expert.shape[1]
    assert kk == 2
    a = n * kk
    order = jnp.argsort(expert.reshape(-1)).astype(jnp.int32)
    starts = jnp.cumsum(counts) - counts
    padded = (counts + MOE_TILE - 1) // MOE_TILE * MOE_TILE
    pend = jnp.cumsum(padded)
    pstart = pend - padded
    n_blk = -(-(a + n_exp * (MOE_TILE - 1)) // MOE_TILE)
    tile_start = jnp.arange(n_blk, dtype=jnp.int32) * MOE_TILE
    blk_e = jnp.minimum(jnp.sum((pend[None, :] <= tile_start[:, None]).astype(jnp.int32), axis=1), n_exp - 1)
    n_real = jnp.clip(counts[blk_e] - (tile_start - pstart[blk_e]), 0, MOE_TILE).astype(jnp.int32)
    row_off = (tile_start - pstart[blk_e] + starts[blk_e])[:, None] + jnp.arange(MOE_TILE, dtype=jnp.int32)[None, :]
    asg = order[jnp.clip(row_off, 0, a - 1)].reshape(-1)
    y = moe_ffn(hn, asg, n_real, blk_e, layer, w_g, w_u, w_d)
    return moe_combine(x, route, y)


KV_W = 2 * NSA_KV_HEADS * NSA_HD
Q_W = NSA_HEADS * NSA_HD
MASK_BUCKET = REL_BUCKETS
KEY_TILE = 512


def _t5_thresholds():
    exact = REL_BUCKETS // 2
    dist = np.arange(0, 4 * REL_MAX_DIST, dtype=np.int64)
    nf = np.maximum(dist, 1).astype(np.float64)
    large = exact + np.floor(np.log(nf / exact) / math.log(REL_MAX_DIST / exact) * (REL_BUCKETS - exact) + 1e-9)
    bucket = np.where(dist < exact, dist, np.minimum(large, REL_BUCKETS - 1)).astype(np.int64)
    return [int(np.argmax(bucket >= b)) for b in range(1, REL_BUCKETS)]


def _bucket_index(dist, valid):
    idx = jnp.zeros(dist.shape, jnp.int32)
    for thr in _t5_thresholds():
        idx = idx + (dist >= thr).astype(jnp.int32)
    return jnp.where(valid, idx, MASK_BUCKET)


def _bias_table(rel_bias):
    t = jnp.zeros((NSA_HEADS, LANES), F32).at[:, :REL_BUCKETS].set(rel_bias.T.astype(F32))
    return t.at[:, MASK_BUCKET].set(NEG)


def _group_mean_matrix():
    i = np.arange(LANES)
    return jnp.asarray((i[:, None] // NSA_HD == i[None, :] // NSA_HD) / NSA_HD, F32)


def _nsa_prep_body(zq_ref, zc_ref, zs_ref, zw_ref, zg_ref, bd_ref, qn_ref, kns_ref, knw_ref,
                   q_ref, cmp_ref, sel_ref, win_ref, selb_ref, winb_ref, gate_ref, *kt_ref, transposed):
    bd = bd_ref[...]

    def head_norm(x, gain):
        ms = _dot_f32(x * x, bd)
        return x * lax.rsqrt(ms + EPS) * gain

    def emit(o_ref, rows):
        if transposed:
            o_ref[0] = rows.T
        else:
            o_ref[...] = rows

    for c in range(Q_W // LANES):
        sl = slice(c * LANES, (c + 1) * LANES)
        q_ref[:, sl] = (head_norm(zq_ref[:, sl], qn_ref[...]) * (NSA_HD ** -0.5)).astype(BF16)
    emit(cmp_ref, zc_ref[...])
    half = KV_W // 2
    for z_ref, kn_ref, o_ref, ob_ref in ((zs_ref, kns_ref, sel_ref, selb_ref), (zw_ref, knw_ref, win_ref, winb_ref)):
        kn = jnp.concatenate([head_norm(z_ref[:, c * LANES:(c + 1) * LANES], kn_ref[...])
                              for c in range(half // LANES)], axis=1)
        rows = jnp.concatenate([kn, z_ref[:, half:KV_W]], axis=1)
        emit(o_ref, rows)
        ob_ref[...] = rows.astype(BF16)
        if transposed and o_ref is sel_ref:
            kt_ref[0][0] = kn.T.astype(BF16)
    gate_ref[...] = _sigmoid(zg_ref[...])


def nsa_prep(z, q_norm, k_norm, batch=None, row_tile=256):
    n = z.shape[0]
    tm = min(row_tile, n)
    transposed = batch is not None

    def zspec(width, blk):
        return pl.BlockSpec((tm, width), lambda i, blk=blk: (i, blk))

    def const(shape):
        return pl.BlockSpec(shape, lambda i: (0,) * len(shape))

    def tile2(v):
        return jnp.concatenate([v, v]).reshape(1, LANES).astype(F32)

    def rows(w, dt):
        return jax.ShapeDtypeStruct((n, w), dt)

    def out_spec(w):
        return pl.BlockSpec((tm, w), lambda i: (i, 0))

    if transposed:
        seq = n // batch
        tps = seq // tm
        kv_shape = jax.ShapeDtypeStruct((batch, KV_W, seq), F32)
        kv_spec = pl.BlockSpec((1, KV_W, tm), lambda i: (i // tps, 0, i % tps))
        extra_shape = (jax.ShapeDtypeStruct((batch, KV_W // 2, seq), BF16),)
        extra_spec = (pl.BlockSpec((1, KV_W // 2, tm), lambda i: (i // tps, 0, i % tps)),)
    else:
        kv_shape, kv_spec, extra_shape, extra_spec = rows(KV_W, F32), out_spec(KV_W), (), ()
    return pl.pallas_call(
        functools.partial(_nsa_prep_body, transposed=transposed),
        out_shape=(rows(Q_W, BF16), kv_shape, kv_shape, kv_shape, rows(KV_W, BF16), rows(KV_W, BF16),
                   rows(LANES, F32)) + extra_shape,
        grid=(n // tm,),
        in_specs=[zspec(Q_W, 0), zspec(KV_W, 2), zspec(KV_W, 3), zspec(KV_W, 4),
                  zspec(LANES, (Q_W + 3 * KV_W) // LANES), const((LANES, LANES)),
                  const((1, LANES)), const((1, LANES)), const((1, LANES))],
        out_specs=(out_spec(Q_W), kv_spec, kv_spec, kv_spec, out_spec(KV_W), out_spec(KV_W),
                   out_spec(LANES)) + extra_spec,
        compiler_params=_params(1), name="nsa_prep",
    )(z, z, z, z, z, _group_mean_matrix(), tile2(q_norm), tile2(k_norm[1]), tile2(k_norm[2]))


PAGES_PER_STEP = 16
PAGE_ROWS = 128
SUBS_PER_PAGE = PAGE_ROWS // CMP_STRIDE
P_W = 2 * NSA_KV_HEADS * 2 * CMP_HID
KV_PAIRS = NSA_KV_HEADS // 2


def _gelu_tanh(x):
    return 0.5 * x * (1.0 + jnp.tanh(math.sqrt(2.0 / math.pi) * (x + 0.044715 * x * x * x)))


def _compress_body(pt_ref, *refs, n_steps):
    page_refs = refs[:PAGES_PER_STEP]
    w1_ref, b1_ref, w2_ref, b2_ref, kn_ref, kc_ref, vc_ref, p_ref, x_s = refs[PAGES_PER_STEP:]
    j = pl.program_id(1)
    rows = PAGES_PER_STEP * SUBS_PER_PAGE
    r0 = pl.multiple_of(j * rows, rows)
    for t, pr in enumerate(page_refs):
        for v in range(2):
            for gp in range(KV_PAIRS):
                x_s[t, v * KV_PAIRS + gp] = pr[0, v, 2 * gp:2 * gp + 2].reshape(2 * NSA_HD, PAGE_ROWS).T
    for v in range(2):
        for gp in range(KV_PAIRS):
            acc = jnp.zeros((rows, 4 * CMP_HID), F32)
            for s in range(CMP_STRIDE):
                x = jnp.concatenate([x_s[t, v * KV_PAIRS + gp, pl.ds(s, SUBS_PER_PAGE, stride=CMP_STRIDE), :]
                                     for t in range(PAGES_PER_STEP)], axis=0)
                acc = acc + jnp.dot(x.astype(BF16), w1_ref[v, s], preferred_element_type=F32)
            p0 = (v * NSA_KV_HEADS + 2 * gp) * 2 * CMP_HID
            p_ref[pl.ds(r0, rows), p0:p0 + 4 * CMP_HID] = acc

    @pl.when(j == n_steps - 1)
    def _():
        n_sub = p_ref.shape[0]
        for v in range(2):
            for g in range(NSA_KV_HEADS):
                p0 = (v * NSA_KV_HEADS + g) * 2 * CMP_HID
                hs = p_ref[:, p0:p0 + CMP_HID] + pltpu.roll(p_ref[:, p0 + CMP_HID:p0 + 2 * CMP_HID], n_sub - 1, 0)
                hid = _gelu_tanh(hs + b1_ref[v:v + 1, :])
                out = _dot(hid, w2_ref[v]) + b2_ref[v:v + 1, :]
                if v == 0:
                    out = out * lax.rsqrt(jnp.mean(out * out, axis=-1, keepdims=True) + EPS) * kn_ref[...]
                    kc_ref[0, :, g * NSA_HD:(g + 1) * NSA_HD] = out
                else:
                    vc_ref[0, :, g * NSA_HD:(g + 1) * NSA_HD] = out


def nsa_compress(rows_t, page_table, w1, b1, w2, b2, k_norm0):
    if page_table is None:
        batch, n_pp = rows_t.shape[0], rows_t.shape[-1] // PAGE_ROWS
        page_table = jnp.zeros((1, 1), jnp.int32)

        def page_index(b, p, pt):
            return (b, 0, 0, 0, p)
    else:
        batch, n_pp = page_table.shape

        def page_index(b, p, pt):
            return (pt[b, p], 0, 0, 0, 0)
    n_steps = n_pp // PAGES_PER_STEP
    n_sub = n_pp * SUBS_PER_PAGE
    w = w1.reshape(2, 2, CMP_STRIDE, NSA_HD, CMP_HID)
    w = jnp.transpose(w, (0, 2, 3, 1, 4)).reshape(2, CMP_STRIDE, NSA_HD, 2 * CMP_HID)
    zero = jnp.zeros_like(w)
    wpair = jnp.concatenate([jnp.concatenate([w, zero], axis=-1), jnp.concatenate([zero, w], axis=-1)], axis=2)
    wpair = wpair.astype(BF16)

    def page_spec(t):
        return pl.BlockSpec((1, 2, NSA_KV_HEADS, NSA_HD, PAGE_ROWS),
                            lambda b, j, pt, t=t: page_index(b, j * PAGES_PER_STEP + t, pt))

    def const(shape):
        return pl.BlockSpec(shape, lambda b, j, pt: (0,) * len(shape))

    grid_spec = pltpu.PrefetchScalarGridSpec(
        num_scalar_prefetch=1, grid=(batch, n_steps),
        in_specs=[page_spec(t) for t in range(PAGES_PER_STEP)] + [
            const((2, CMP_STRIDE, LANES, 4 * CMP_HID)), const((2, CMP_HID)), const((2, CMP_HID, NSA_HD)),
            const((2, NSA_HD)), const((1, NSA_HD))],
        out_specs=(pl.BlockSpec((1, n_sub, KV_W // 2), lambda b, j, pt: (b, 0, 0)),
                   pl.BlockSpec((1, n_sub, KV_W // 2), lambda b, j, pt: (b, 0, 0))),
        scratch_shapes=[pltpu.VMEM((n_sub, P_W), F32),
                        pltpu.VMEM((PAGES_PER_STEP, 2 * KV_PAIRS, PAGE_ROWS, 2 * NSA_HD), F32)])
    return pl.pallas_call(
        functools.partial(_compress_body, n_steps=n_steps),
        out_shape=(jax.ShapeDtypeStruct((batch, n_sub, KV_W // 2), F32),
                   jax.ShapeDtypeStruct((batch, n_sub, KV_W // 2), F32)),
        grid_spec=grid_spec, compiler_params=_params(2), name="nsa_compress",
    )(page_table, *([rows_t] * PAGES_PER_STEP), wpair, b1, w2.astype(BF16), b2, k_norm0.reshape(1, NSA_HD))


Q_TILE = 128
WIN_BLOCKS = (WINDOW + Q_TILE) // Q_TILE
NEAR_TILES = 3
NEAR_CHUNKS = 8
M_INIT = -1e29


def _gather_bias(tbh, idx_slices):
    return jnp.concatenate([jnp.take_along_axis(tbh, idx, axis=1) for idx in idx_slices], axis=1)


def _nsa_attend_body(q_ref, gate_ref, x_ref, kc_ref, vc_ref, kst_ref, vs_ref, w0_ref, w1_ref, w2_ref, w3_ref, w4_ref,
                     idxc_ref, idxw_ref, nb_ref, mmat_ref, tb_ref, wout_ref, expand_ref, y_ref,
                     pc_s, oc_s, sel_s, m_s, l_s, a_s, acc_s, ps_s, pw_s, o_s, bsel_s, bwin_s, *, n_sel):
    win_refs = (w0_ref, w1_ref, w2_ref, w3_ref, w4_ref)
    t = pl.program_id(1)
    s0 = t * Q_TILE
    n_sub = kc_ref.shape[1]
    rep_rows = NSA_REP * Q_TILE
    q_pos = s0 + lax.broadcasted_iota(jnp.int32, (Q_TILE, LANES), 0)
    blk = lax.broadcasted_iota(jnp.int32, (Q_TILE, LANES), 1)
    cur = q_pos // SEL_BLOCK
    forced = (blk == 0) | (blk == cur) | (blk == cur - 1)
    future = blk * SEL_BLOCK > q_pos

    def group_q(g):
        return jnp.concatenate([q_ref[:, (g * NSA_REP + r) * NSA_HD:(g * NSA_REP + r + 1) * NSA_HD]
                                for r in range(NSA_REP)], axis=0)

    def head_table(h):
        return jnp.broadcast_to(tb_ref[h:h + 1, :], (Q_TILE, LANES))

    @pl.when((pl.program_id(0) == 0) & (t == 0))
    def _():
        for h in range(NSA_HEADS):
            tbh = head_table(h)
            far_bias = tb_ref[h:h + 1, REL_BUCKETS - 1:REL_BUCKETS]
            for k in range(NEAR_CHUNKS):
                bsel_s[h, k] = jnp.take_along_axis(tbh, nb_ref[k], axis=1) - far_bias
            bsel_s[h, NEAR_CHUNKS] = jnp.zeros((Q_TILE, LANES), F32)
            bsel_s[h, NEAR_CHUNKS + 1] = jnp.full((Q_TILE, LANES), NEG, F32)
            bwin_s[h] = _gather_bias(tbh, [idxw_ref[:, c * LANES:(c + 1) * LANES] for c in range(WIN_BLOCKS)])

    score_t = []
    for g in range(NSA_KV_HEADS):
        gs = slice(g * NSA_HD, (g + 1) * NSA_HD)
        sc = _dot_nt(group_q(g), kc_ref[0, :, gs])
        imp = jnp.zeros((Q_TILE, n_sub), F32)
        for r in range(NSA_REP):
            rs = slice(r * Q_TILE, (r + 1) * Q_TILE)
            bias = _gather_bias(head_table(g * NSA_REP + r),
                                [idxc_ref[0, :, c * LANES:(c + 1) * LANES] for c in range(n_sub // LANES)])
            s_r = sc[rs] + bias
            m = jnp.maximum(jnp.max(s_r, axis=1, keepdims=True), M_INIT)
            e = jnp.exp(s_r - m)
            p = e / jnp.maximum(jnp.sum(e, axis=1, keepdims=True), 1e-30)
            imp = imp + p
            pc_s[rs, :] = p.astype(BF16)
        oc_s[g] = jnp.dot(pc_s[...], vc_ref[0, :, gs], preferred_element_type=F32)
        imp_hi = imp.astype(BF16)
        rest = imp - imp_hi.astype(F32)
        imp_mid = rest.astype(BF16)
        imp_lo = (rest - imp_mid.astype(F32)).astype(BF16)
        score = jnp.dot(jnp.concatenate([imp_hi, imp_mid, imp_lo], axis=1), mmat_ref[...],
                        preferred_element_type=F32)
        score = jnp.where(forced, FORCE_SCORE, score)
        score = jnp.where(future, NEG, score)
        score = jnp.where(blk >= n_sel, -jnp.inf, score)
        score_t.append(score.T)

    blk_t = lax.broadcasted_iota(jnp.int32, (LANES, Q_TILE), 0).astype(F32)
    sel_t = [jnp.zeros((LANES, Q_TILE), F32) for _ in range(NSA_KV_HEADS)]
    for _ in range(min(SEL_TOPK, n_sel)):
        for g in range(NSA_KV_HEADS):
            mx = jnp.max(score_t[g], axis=0, keepdims=True)
            first = jnp.min(jnp.where(score_t[g] == mx, blk_t, float(LANES)), axis=0, keepdims=True)
            pick = blk_t == first
            sel_t[g] = jnp.where(pick, 1.0, sel_t[g])
            score_t[g] = jnp.where(pick, -jnp.inf, score_t[g])
    for g in range(NSA_KV_HEADS):
        sel_s[g] = ((sel_t[g].T - 1.0) * (-NEG)).astype(BF16)

    n_kt = (s0 + Q_TILE + KEY_TILE - 1) // KEY_TILE
    n_far = jnp.maximum(n_kt - NEAR_TILES, 0)

    for g in range(NSA_KV_HEADS):
        gs = slice(g * NSA_HD, (g + 1) * NSA_HD)
        qg = group_q(g)

        m_s[...] = jnp.full((rep_rows, LANES), M_INIT, F32)
        l_s[...] = jnp.zeros((rep_rows, LANES), F32)
        acc_s[...] = jnp.zeros((rep_rows, NSA_HD), F32)

        def key_tile(kt, near, g=g, gs=gs, qg=qg):
            k0 = pl.multiple_of(kt * KEY_TILE, KEY_TILE)
            s = jnp.dot(qg, kst_ref[0, gs, pl.ds(k0, KEY_TILE)], preferred_element_type=F32)
            negm = jnp.dot(sel_s[g], expand_ref[:, pl.ds(k0, KEY_TILE)], preferred_element_type=F32)
            v = vs_ref[pl.ds(k0, KEY_TILE), gs]
            for r in range(NSA_REP):
                rs = slice(r * Q_TILE, (r + 1) * Q_TILE)
                h = g * NSA_REP + r
                if near:
                    chunks = []
                    for c in range(KEY_TILE // LANES):
                        k = (s0 - k0) // LANES - c
                        chunks.append(bsel_s[h, jnp.where(k < 0, NEAR_CHUNKS + 1, jnp.minimum(k, NEAR_CHUNKS))])
                    s_r = s[rs] + (negm + jnp.concatenate(chunks, axis=1))
                else:
                    s_r = s[rs] + negm
                m_old = m_s[rs]
                m_new = jnp.maximum(m_old, jnp.max(s_r, axis=1, keepdims=True))
                alpha = jnp.exp(m_old - m_new)
                p = jnp.exp(s_r - jnp.concatenate([m_new] * (KEY_TILE // LANES), axis=1))
                l_s[rs] = alpha * l_s[rs] + jnp.sum(p, axis=1, keepdims=True)
                m_s[rs] = m_new
                acc_s[rs] = alpha[:, :NSA_HD] * acc_s[rs] + jnp.dot(p.astype(BF16), v, preferred_element_type=F32)

        def far_body(kt, carry):
            key_tile(kt, False)
            return carry

        def near_body(kt, carry):
            key_tile(kt, True)
            return carry

        lax.fori_loop(0, n_far, far_body, 0)
        lax.fori_loop(n_far, n_kt, near_body, 0)

        kw = jnp.concatenate([wr[0, :, gs] for wr in win_refs], axis=0)
        vw = jnp.concatenate([wr[0, :, KV_W // 2 + g * NSA_HD:KV_W // 2 + (g + 1) * NSA_HD] for wr in win_refs], axis=0)
        sw = _dot_nt(qg, kw)
        for r in range(NSA_REP):
            rs = slice(r * Q_TILE, (r + 1) * Q_TILE)
            s_r = sw[rs] + bwin_s[g * NSA_REP + r]
            e = jnp.exp(s_r - jnp.max(s_r, axis=1, keepdims=True))
            pw_s[rs, :] = (e / jnp.sum(e, axis=1, keepdims=True)).astype(BF16)
        o_w = jnp.dot(pw_s[...], vw, preferred_element_type=F32)

        for r in range(NSA_REP):
            rs = slice(r * Q_TILE, (r + 1) * Q_TILE)
            h = g * NSA_REP + r
            o_h = (gate_ref[:, 3 * h:3 * h + 1] * oc_s[g, rs, :]
                   + gate_ref[:, 3 * h + 1:3 * h + 2] * (acc_s[rs, :] / l_s[rs, :NSA_HD])
                   + gate_ref[:, 3 * h + 2:3 * h + 3] * o_w[rs])
            o_s[:, h * NSA_HD:(h + 1) * NSA_HD] = o_h.astype(BF16)

    y_ref[...] = x_ref[...] + jnp.dot(o_s[...], wout_ref[...], preferred_element_type=F32)


def nsa_attend_prompt(q, gates, x, k_c, v_c, kst, selb, winb, rel_bias, w_out, batch, seq):
    n_qt = seq // Q_TILE
    n_sub = k_c.shape[1]
    n_sel = seq // SEL_BLOCK
    assert CMP_LEN == 2 * CMP_STRIDE and SEL_BLOCK == 4 * CMP_STRIDE and SEL_TOPK <= n_sel <= LANES
    win_pad = jnp.pad(winb.reshape(batch, seq, KV_W), ((0, 0), (WINDOW, 0), (0, 0)))
    iq = jnp.arange(Q_TILE, dtype=jnp.int32)
    dist_c = (jnp.arange(n_qt, dtype=jnp.int32)[:, None, None] * Q_TILE + iq[None, :, None]
              - (jnp.arange(n_sub, dtype=jnp.int32)[None, None, :] * CMP_STRIDE + CMP_LEN - 1))
    idx_c = _bucket_index(dist_c, dist_c >= 0)
    dist_w = iq[:, None] - jnp.arange(WINDOW + Q_TILE, dtype=jnp.int32)[None, :] + WINDOW
    idx_w = _bucket_index(dist_w, (dist_w >= 0) & (dist_w < WINDOW))
    dist_n = (jnp.arange(NEAR_CHUNKS, dtype=jnp.int32)[:, None, None] * LANES + iq[None, :, None]
              - jnp.arange(LANES, dtype=jnp.int32)[None, None, :])
    nb = _bucket_index(dist_n, dist_n >= 0)
    ci = np.arange(n_sub)[:, None]
    bj = np.arange(LANES)[None, :]
    mmat = ((ci // 4 == bj).astype(np.float32) + ((ci + 1) // 4 == bj).astype(np.float32)) * (ci < n_sub - 1)
    expand = (np.arange(seq)[None, :] // SEL_BLOCK == np.arange(LANES)[:, None]).astype(np.float32)

    def rows(width):
        return pl.BlockSpec((Q_TILE, width), lambda b, t: (b * n_qt + t, 0))

    def per_b(shape):
        return pl.BlockSpec((1,) + shape, lambda b, t: (b,) + (0,) * len(shape))

    def const(shape):
        return pl.BlockSpec(shape, lambda b, t: (0,) * len(shape))

    in_specs = [rows(Q_W), rows(LANES), rows(D_MODEL), per_b((n_sub, KV_W // 2)), per_b((n_sub, KV_W // 2)),
                per_b((KV_W // 2, seq)), pl.BlockSpec((seq, KV_W // 2), lambda b, t: (b, 1))]
    in_specs += [pl.BlockSpec((1, Q_TILE, KV_W), lambda b, t, j=j: (b, t + j, 0)) for j in range(WIN_BLOCKS)]
    in_specs += [pl.BlockSpec((1, Q_TILE, n_sub), lambda b, t: (t, 0, 0)), const((Q_TILE, WINDOW + Q_TILE)),
                 const((NEAR_CHUNKS, Q_TILE, LANES)), const((3 * n_sub, LANES)), const((NSA_HEADS, LANES)),
                 const((Q_W, D_MODEL)), const((LANES, seq))]
    rep_rows = NSA_REP * Q_TILE
    scratch = [pltpu.VMEM((rep_rows, n_sub), BF16), pltpu.VMEM((NSA_KV_HEADS, rep_rows, NSA_HD), F32),
               pltpu.VMEM((NSA_KV_HEADS, Q_TILE, LANES), BF16), pltpu.VMEM((rep_rows, LANES), F32),
               pltpu.VMEM((rep_rows, LANES), F32), pltpu.VMEM((rep_rows, LANES), F32),
               pltpu.VMEM((rep_rows, NSA_HD), F32),
               pltpu.VMEM((rep_rows, KEY_TILE), BF16), pltpu.VMEM((rep_rows, WINDOW + Q_TILE), BF16),
               pltpu.VMEM((Q_TILE, Q_W), BF16),
               pltpu.VMEM((NSA_HEADS, NEAR_CHUNKS + 2, Q_TILE, LANES), F32),
               pltpu.VMEM((NSA_HEADS, Q_TILE, WINDOW + Q_TILE), F32)]
    return pl.pallas_call(
        functools.partial(_nsa_attend_body, n_sel=n_sel),
        out_shape=jax.ShapeDtypeStruct((batch * seq, D_MODEL), F32), grid=(batch, n_qt),
        in_specs=in_specs, out_specs=rows(D_MODEL), scratch_shapes=scratch,
        compiler_params=_params(2), name="nsa_attend_prompt",
    )(q, gates, x, k_c.astype(BF16), v_c.astype(BF16), kst, selb, *([win_pad] * WIN_BLOCKS),
      idx_c, idx_w, nb, jnp.asarray(np.concatenate([mmat] * 3, axis=0), BF16), _bias_table(rel_bias),
      w_out.astype(BF16), jnp.asarray(expand, BF16))


SCORE_W = 384


def _nsa_sample_cmp_body(q_ref, kc_ref, vc_ref, idx_ref, mmat_ref, gsum_ref, tb_ref, oc_ref, top_ref, *,
                         n_sel, q_pos):
    n_sub = kc_ref.shape[1]
    q = q_ref[0]
    row_g = lax.broadcasted_iota(jnp.int32, (NSA_HEADS, 1), 0) // NSA_REP
    s = jnp.zeros((NSA_HEADS, n_sub), F32)
    for g in range(NSA_KV_HEADS):
        s = jnp.where(row_g == g, _dot_nt(q, kc_ref[0, :, g * NSA_HD:(g + 1) * NSA_HD]), s)
    tb = tb_ref[...]
    bias = jnp.concatenate([jnp.take_along_axis(
        tb, jnp.broadcast_to(idx_ref[:, c * LANES:(c + 1) * LANES], (NSA_HEADS, LANES)), axis=1)
        for c in range(n_sub // LANES)], axis=1)
    s = s + bias
    m = jnp.maximum(jnp.max(s, axis=1, keepdims=True), M_INIT)
    e = jnp.exp(s - m)
    p = e / jnp.maximum(jnp.sum(e, axis=1, keepdims=True), 1e-30)
    pb = p.astype(BF16)
    o = jnp.zeros((NSA_HEADS, NSA_HD), F32)
    for g in range(NSA_KV_HEADS):
        o = jnp.where(row_g == g, jnp.dot(pb, vc_ref[0, :, g * NSA_HD:(g + 1) * NSA_HD],
                                          preferred_element_type=F32), o)
    oc_ref[0] = o
    imp = _dot_f32(gsum_ref[...], p)
    score = _dot_f32(imp, mmat_ref[...])
    blk = lax.broadcasted_iota(jnp.int32, score.shape, 1)
    cur = q_pos // SEL_BLOCK
    score = jnp.where((blk == 0) | (blk == cur) | (blk == cur - 1), FORCE_SCORE, score)
    score = jnp.where(blk * SEL_BLOCK > q_pos, NEG, score)
    score = jnp.where(blk >= n_sel, -jnp.inf, score)
    blkf = blk.astype(F32)
    lane = lax.broadcasted_iota(jnp.int32, (SUBLANES, LANES), 1)
    top = jnp.zeros((SUBLANES, LANES), F32)
    for it in range(min(SEL_TOPK, n_sel)):
        mx = jnp.max(score, axis=1, keepdims=True)
        first = jnp.min(jnp.where(score == mx, blkf, float(SCORE_W)), axis=1, keepdims=True)
        top = jnp.where(lane == it, first, top)
        score = jnp.where(blkf == first, -jnp.inf, score)
    top_ref[0] = top.astype(jnp.int32)


def nsa_sample_cmp(q, k_c, v_c, rel_bias, past_len):
    batch = q.shape[0]
    n_sub = k_c.shape[1]
    n_sel = past_len // SEL_BLOCK + 1
    assert SEL_TOPK <= n_sel <= SCORE_W
    dist = past_len - (jnp.arange(n_sub, dtype=jnp.int32) * CMP_STRIDE + CMP_LEN - 1)
    idx = _bucket_index(dist, dist >= 0).reshape(1, n_sub)
    ci = np.arange(n_sub)[:, None]
    bj = np.arange(SCORE_W)[None, :]
    mmat = ((ci // 4 == bj).astype(np.float32) + ((ci + 1) // 4 == bj).astype(np.float32)) * (ci < n_sub - 1)
    gsum = (np.arange(SUBLANES)[:, None] == np.arange(NSA_HEADS)[None, :] // NSA_REP).astype(np.float32)

    def per_b(shape):
        return pl.BlockSpec((1,) + shape, lambda b: (b,) + (0,) * len(shape))

    def const(shape):
        return pl.BlockSpec(shape, lambda b: (0,) * len(shape))

    o_c, top = pl.pallas_call(
        functools.partial(_nsa_sample_cmp_body, n_sel=n_sel, q_pos=past_len),
        out_shape=(jax.ShapeDtypeStruct((batch, NSA_HEADS, NSA_HD), F32),
                   jax.ShapeDtypeStruct((batch, SUBLANES, LANES), jnp.int32)),
        grid=(batch,),
        in_specs=[per_b((NSA_HEADS, NSA_HD)), per_b((n_sub, KV_W // 2)), per_b((n_sub, KV_W // 2)),
                  const((1, n_sub)), const((n_sub, SCORE_W)), const((SUBLANES, NSA_HEADS)), const((NSA_HEADS, LANES))],
        out_specs=(per_b((NSA_HEADS, NSA_HD)), per_b((SUBLANES, LANES))),
        compiler_params=_params(1), name="nsa_sample_cmp",
    )(q, k_c.astype(BF16), v_c.astype(BF16), idx, jnp.asarray(mmat), jnp.asarray(gsum), _bias_table(rel_bias))
    return o_c, top[:, :NSA_KV_HEADS, :SEL_TOPK]


SEL_PER_STEP = 2


def _nsa_sample_attend_body(row_ref, sidx_ref, *refs, q_pos, n_past_blk, n_steps):
    n_blk_refs = NSA_KV_HEADS * SEL_PER_STEP
    blk_refs = refs[:n_blk_refs]
    (q_ref, knew_ref, win_ref, wnew_ref, gate_ref, oc_ref, idxw_ref, tb_ref, o_ref,
     s_s, v_s, ow_s) = refs[n_blk_refs:]
    b = pl.program_id(0)
    step = pl.program_id(1)
    q = q_ref[0]
    row_g = lax.broadcasted_iota(jnp.int32, (NSA_HEADS, 1), 0) // NSA_REP
    tb = tb_ref[...]
    half = KV_W // 2

    def by_group(fn):
        out = fn(0)
        for g in range(1, NSA_KV_HEADS):
            out = jnp.where(row_g == g, fn(g), out)
        return out

    @pl.when(step == 0)
    def _():
        s_w = by_group(lambda g: _dot(q, win_ref[0, 0, g]))
        bias = jnp.concatenate([jnp.take_along_axis(
            tb, jnp.broadcast_to(idxw_ref[:, c * LANES:(c + 1) * LANES], (NSA_HEADS, LANES)), axis=1)
            for c in range(s_w.shape[1] // LANES)], axis=1)
        s_w = s_w + bias
        k_new = by_group(lambda g: jnp.broadcast_to(wnew_ref[0, :, g * NSA_HD:(g + 1) * NSA_HD], (NSA_HEADS, NSA_HD)))
        v_new = by_group(lambda g: jnp.broadcast_to(wnew_ref[0, :, half + g * NSA_HD:half + (g + 1) * NSA_HD],
                                                    (NSA_HEADS, NSA_HD)))
        s_n = jnp.sum(q.astype(F32) * _r16(k_new), axis=1, keepdims=True) + tb[:, 0:1]
        m = jnp.maximum(jnp.max(s_w, axis=1, keepdims=True), s_n)
        e_w = jnp.exp(s_w - m)
        e_n = jnp.exp(s_n - m)
        total = jnp.sum(e_w, axis=1, keepdims=True) + e_n
        pb = (e_w / total).astype(BF16)
        pv = by_group(lambda g: _dot_nt(pb, win_ref[0, 1, g]))
        ow_s[...] = pv + _r16(e_n / total) * _r16(v_new)

    keys = SEL_PER_STEP * PAGE_ROWS
    key_lane = lax.broadcasted_iota(jnp.int32, (NSA_HD, PAGE_ROWS), 1)
    lane = lax.broadcasted_iota(jnp.int32, (1, PAGE_ROWS), 1)
    thresholds = _t5_thresholds()

    def tile_kv(g, kv):
        parts = []
        for j in range(SEL_PER_STEP):
            is_new = sidx_ref[b, g, step * SEL_PER_STEP + j] >= n_past_blk
            cached = blk_refs[g * SEL_PER_STEP + j][0, kv, 0]
            fresh = jnp.where(key_lane == 0, knew_ref[0, kv * half + g * NSA_HD:kv * half + (g + 1) * NSA_HD, :], 0.0)
            parts.append(jnp.where(is_new, fresh, cached))
        return jnp.concatenate(parts, axis=1).astype(BF16)

    def tile_bias(g):
        parts = []
        for j in range(SEL_PER_STEP):
            blk = sidx_ref[b, g, step * SEL_PER_STEP + j]
            is_new = blk >= n_past_blk
            base = jnp.where(is_new, blk * SEL_BLOCK, (blk * SEL_BLOCK) // PAGE_ROWS * PAGE_ROWS)
            pos = base + lane
            dist = q_pos - pos
            idx = jnp.zeros((1, PAGE_ROWS), jnp.int32)
            for thr in thresholds:
                idx = idx + (dist >= thr).astype(jnp.int32)
            parts.append(jnp.where((dist >= 0) & (pos // SEL_BLOCK == blk), idx, MASK_BUCKET))
        idx = jnp.concatenate(parts, axis=1)
        return jnp.concatenate([jnp.take_along_axis(
            tb, jnp.broadcast_to(idx[:, c * LANES:(c + 1) * LANES], (NSA_HEADS, LANES)), axis=1)
            for c in range(keys // LANES)], axis=1)

    k0 = pl.multiple_of(step * keys, keys)
    s_s[:, pl.ds(k0, keys)] = by_group(lambda g: jnp.dot(q, tile_kv(g, 0), preferred_element_type=F32)
                                       + tile_bias(g))
    for g in range(NSA_KV_HEADS):
        v_s[g, :, pl.ds(k0, keys)] = tile_kv(g, 1)

    @pl.when(step == n_steps - 1)
    def _():
        s = s_s[...]
        e = jnp.exp(s - jnp.max(s, axis=1, keepdims=True))
        pb = (e / jnp.sum(e, axis=1, keepdims=True)).astype(BF16)
        o_sel = by_group(lambda g: _dot_nt(pb, v_s[g]))
        gate = gate_ref[0]
        o_ref[0] = gate[:, 0:1] * oc_ref[0] + gate[:, 1:2] * o_sel + gate[:, 2:3] * ow_s[...]


def nsa_sample_attend(q, kv_sel_new, kv_win_new, gates, o_c, top, cache_sel, win_buf, page_table, rel_bias,
                      past_len):
    batch = q.shape[0]
    n_past_blk = past_len // SEL_BLOCK
    bpp = PAGE_ROWS // SEL_BLOCK
    n_steps = SEL_TOPK // SEL_PER_STEP
    page_shape = cache_sel.shape[1:]
    jp = jnp.minimum(top, n_past_blk - 1)
    phys = jnp.take_along_axis(page_table, (jp // bpp).reshape(batch, -1), axis=1).reshape(top.shape)
    phys = phys.astype(jnp.int32)
    wb = win_buf.shape[-1]
    dist_w = past_len - (past_len - wb + jnp.arange(wb, dtype=jnp.int32))
    idx_w = _bucket_index(dist_w, (dist_w >= 0) & (dist_w < WINDOW)).reshape(1, wb)
    gate3 = gates[:, :3 * NSA_HEADS].reshape(batch, NSA_HEADS, 3)

    def blk_spec(g, j):
        return pl.BlockSpec((1, page_shape[0], 1) + page_shape[2:],
                            lambda b, s, rows, sidx, g=g, j=j: (rows[b, g, s * SEL_PER_STEP + j], 0, g, 0, 0))

    def per_b(shape):
        return pl.BlockSpec((1,) + shape, lambda b, s, rows, sidx: (b,) + (0,) * len(shape))

    def const(shape):
        return pl.BlockSpec(shape, lambda b, s, rows, sidx: (0,) * len(shape))

    grid_spec = pltpu.PrefetchScalarGridSpec(
        num_scalar_prefetch=2, grid=(batch, n_steps),
        in_specs=[blk_spec(g, j) for g in range(NSA_KV_HEADS) for j in range(SEL_PER_STEP)] + [
            per_b((NSA_HEADS, NSA_HD)), per_b((KV_W, 1)), per_b(win_buf.shape[1:]), per_b((1, KV_W)),
            per_b((NSA_HEADS, 3)), per_b((NSA_HEADS, NSA_HD)), const((1, wb)), const((NSA_HEADS, LANES))],
        out_specs=per_b((NSA_HEADS, NSA_HD)),
        scratch_shapes=[pltpu.VMEM((NSA_HEADS, SEL_TOPK * PAGE_ROWS), F32),
                        pltpu.VMEM((NSA_KV_HEADS, NSA_HD, SEL_TOPK * PAGE_ROWS), BF16),
                        pltpu.VMEM((NSA_HEADS, NSA_HD), F32)])
    o = pl.pallas_call(
        functools.partial(_nsa_sample_attend_body, q_pos=past_len, n_past_blk=n_past_blk, n_steps=n_steps),
        out_shape=jax.ShapeDtypeStruct((batch, NSA_HEADS, NSA_HD), F32), grid_spec=grid_spec,
        compiler_params=_params(2), name="nsa_sample_attend",
    )(phys, top.astype(jnp.int32), *([cache_sel] * (NSA_KV_HEADS * SEL_PER_STEP)), q,
      kv_sel_new.reshape(batch, KV_W, 1), win_buf, kv_win_new.reshape(batch, 1, KV_W), gate3, o_c, idx_w,
      _bias_table(rel_bias))
    return o.reshape(batch, Q_W)


def _pad_cols(w, width):
    return jnp.pad(w, ((0, 0), (0, width - w.shape[1]))).astype(BF16)


def kernel(x_prompt, x_sample, state_ret, state_mlstm_C, state_mlstm_n, state_mlstm_m, state_conv, cache_nsa_cmp, cache_nsa_sel, state_nsa_win, page_table, rel_bias, norm_mix, norm_ffn, ab_w_in, ab_conv_w, ab_conv_b, ab_b_igate, ab_b_fgate, ab_gn_g, ab_gn_b, ab_hn_g, ab_w_out, nsa_w_in, nsa_q_norm, nsa_k_norm, nsa_cmp_w1, nsa_cmp_b1, nsa_cmp_w2, nsa_cmp_b2, nsa_w_out, moe_w_group, moe_b_group, moe_w_expert, moe_b_expert, moe_w_gate, moe_w_up, moe_w_down):
    bp, lp, d = x_prompt.shape
    bs, ls, _ = x_sample.shape
    page_size = cache_nsa_cmp.shape[2]
    past_len = page_table.shape[1] * page_size
    assert norm_mix.shape[0] == 2 and ls == 1 and d == D_MODEL and lp % KEY_TILE == 0
    xp = x_prompt.reshape(bp * lp, d)
    xs = x_sample.reshape(bs, d)

    def moe(x, layer):
        return hier_moe_residual(x, norm_ffn[layer], moe_w_group[layer], moe_b_group[layer], moe_w_expert[layer],
                                 moe_b_expert[layer], layer, moe_w_gate, moe_w_up, moe_w_down)

    w_in = _pad_cols(ab_w_in[0], AB_IN_PAD)
    ab = (ab_conv_w[0], ab_conv_b[0], ab_b_igate[0], ab_b_fgate[0], ab_gn_g[0], ab_gn_b[0], ab_hn_g[0])
    zp = norm_matmul(xp, norm_mix[0], w_in)
    xp, ret_p, mc_p, mn_p, mm_p, conv_p = ab_prompt(zp, xp, bp, lp, *ab, ab_w_out[0])
    zs = norm_matmul(xs, norm_mix[0], w_in)
    pos_s = past_len + jnp.arange(ls, dtype=jnp.int32)
    ys, ret_s, mc_s, mn_s, mm_s, conv_s = ab_sample(zs, pos_s, state_ret[0], state_mlstm_C[0], state_mlstm_n[0],
                                                    state_mlstm_m[0], state_conv[0], *ab)
    xs = matmul_residual(ys, ab_w_out[0].astype(BF16), xs)
    xp = moe(xp, 0)
    xs = moe(xs, 0)

    w_in = _pad_cols(nsa_w_in[0], NSA_IN_PAD)
    cmp_w = (nsa_cmp_w1[0], nsa_cmp_b1[0], nsa_cmp_w2[0], nsa_cmp_b2[0], nsa_k_norm[0, 0])
    kv_shape = (2, NSA_KV_HEADS, NSA_HD)
    zp = norm_matmul(xp, norm_mix[1], w_in)
    q, cmp_t, sel_t, win_t, selb, winb, gates, kst = nsa_prep(zp, nsa_q_norm[0], nsa_k_norm[0], batch=bp)
    k_c, v_c = nsa_compress(cmp_t.reshape((bp,) + kv_shape + (lp,)), None, *cmp_w)
    xp = nsa_attend_prompt(q, gates, xp, k_c, v_c, kst, selb, winb, rel_bias, nsa_w_out[0], bp, lp)
    win_keep = min(WINDOW, lp)

    def rows_major(t):
        return jnp.transpose(t.reshape((1, bp) + kv_shape + (t.shape[-1],)), (0, 1, 5, 2, 3, 4))

    cmp_p = rows_major(cmp_t)
    sel_p = rows_major(sel_t)
    win_p = rows_major(win_t[:, :, lp - win_keep:])

    def rows_minor(c):
        return jnp.moveaxis(c, -4, -1)

    zs = norm_matmul(xs, norm_mix[1], w_in)
    q, cmp_s, sel_s, win_s, _, _, gates = nsa_prep(zs, nsa_q_norm[0], nsa_k_norm[0])
    k_c, v_c = nsa_compress(rows_minor(cache_nsa_cmp[0]), page_table, *cmp_w)
    q3 = q.reshape(bs, NSA_HEADS, NSA_HD)
    o_c, top = nsa_sample_cmp(q3, k_c, v_c, rel_bias, past_len)
    win_buf = state_nsa_win[0]
    o = nsa_sample_attend(q3, sel_s, win_s, gates, o_c, top, rows_minor(cache_nsa_sel[0]), rows_minor(win_buf),
                          page_table, rel_bias, past_len)
    xs = matmul_residual(o, nsa_w_out[0].astype(BF16), xs)
    win_s = jnp.concatenate([win_buf, win_s.reshape((bs, ls) + kv_shape)], axis=1)[None, :, ls:]
    cmp_s = cmp_s.reshape((1, bs, ls) + kv_shape)
    sel_s = sel_s.reshape((1, bs, ls) + kv_shape)
    xp = moe(xp, 1)
    xs = moe(xs, 1)

    return (xp.reshape(bp, lp, d), xs.reshape(bs, ls, d), ret_p[None], ret_s[None], mc_p[None], mc_s[None],
            mn_p[None], mn_s[None], mm_p[None], mm_s[None], conv_p[None], conv_s[None],
            cmp_p, cmp_s, sel_p, sel_s, win_p, win_s)
```

```python
import functools
import math

import jax
import jax.numpy as jnp
import numpy as np
from jax import lax
from jax.experimental import pallas as pl
from jax.experimental.pallas import tpu as pltpu

F32 = jnp.float32
BF16 = jnp.bfloat16
LANES = 128
SUBLANES = 8
VMEM_LIMIT = 56 * 1024 * 1024

D_MODEL = 1024
RET_HEADS = 4
ML_HEADS = 4
HEAD_D = 128
CONV_W = 4
CHUNK = 128
ROPE_BASE = 10000.0
AB_IN = 4104
AB_IN_PAD = 4224
NSA_HEADS = 16
NSA_KV_HEADS = 4
NSA_REP = 4
NSA_HD = 64
NSA_IN_PAD = 2688
CMP_LEN = 32
CMP_STRIDE = 16
CMP_HID = 128
SEL_BLOCK = 64
SEL_TOPK = 16
WINDOW = 512
REL_BUCKETS = 32
REL_MAX_DIST = 1024
FORCE_SCORE = 1e4
MOE_GROUPS = 4
MOE_EXP_PER_GROUP = 8
MOE_EXPERTS = 32
NEG = -1e30
EPS = 1e-6


def _params(n_grid):
    return pltpu.CompilerParams(dimension_semantics=("arbitrary",) * n_grid, vmem_limit_bytes=VMEM_LIMIT)


def _dot(a, b):
    return jnp.dot(a.astype(BF16), b.astype(BF16), preferred_element_type=F32)


def _dot_nt(a, b):
    return lax.dot_general(a.astype(BF16), b.astype(BF16), (((1,), (1,)), ((), ())), preferred_element_type=F32)


def _dot_tn(a, b):
    return lax.dot_general(a.astype(BF16), b.astype(BF16), (((0,), (0,)), ((), ())), preferred_element_type=F32)


def _dot_f32(a, b):
    return jnp.dot(a, b, preferred_element_type=F32, precision=lax.Precision.HIGHEST)


def _r16(x):
    return x.astype(BF16).astype(F32)


def _sigmoid(x):
    return 1.0 / (1.0 + jnp.exp(-x))


def _silu(x):
    return x * _sigmoid(x)


def _log_sigmoid(x):
    return -(jnp.maximum(-x, 0.0) + jnp.log1p(jnp.exp(-jnp.abs(x))))


def _norm_matmul_body(x_ref, g_ref, w_ref, o_ref, *, col_tile):
    x = x_ref[...]
    y = x * lax.rsqrt(jnp.mean(x * x, axis=-1, keepdims=True) + EPS) * g_ref[...]
    yb = y.astype(BF16)
    for c0 in range(0, o_ref.shape[1], col_tile):
        o_ref[:, c0:c0 + col_tile] = jnp.dot(yb, w_ref[:, c0:c0 + col_tile], preferred_element_type=F32)


def norm_matmul(x, g, w, row_tile=256):
    n, d = x.shape
    c = w.shape[1]
    tm = min(row_tile, n)
    col_tile = 384 if c % 384 == 0 else LANES
    return pl.pallas_call(
        functools.partial(_norm_matmul_body, col_tile=col_tile),
        out_shape=jax.ShapeDtypeStruct((n, c), F32),
        grid=(n // tm,),
        in_specs=[pl.BlockSpec((tm, d), lambda i: (i, 0)),
                  pl.BlockSpec((1, d), lambda i: (0, 0)),
                  pl.BlockSpec((d, c), lambda i: (0, 0))],
        out_specs=pl.BlockSpec((tm, c), lambda i: (i, 0)),
        compiler_params=_params(1),
        name="norm_matmul",
    )(x, g.reshape(1, d), w)


def _retention_constants(c):
    h = np.arange(RET_HEADS, dtype=np.float64)
    log_g = np.log1p(-np.exp2(-5.0 - h))
    i = np.arange(c, dtype=np.float64)
    diff = i[:, None] - i[None, :]
    decay = np.where(diff >= 0, np.exp(np.maximum(diff, 0.0)[None] * log_g[:, None, None]), 0.0)
    q_dec = np.exp((i + 1.0)[None, :] * log_g[:, None])[:, :, None]
    k_dec = np.exp((c - 1.0 - i)[None, :] * log_g[:, None])[:, :, None]
    s_dec = np.exp(c * log_g)
    return (jnp.asarray(decay, F32), jnp.asarray(q_dec, F32), jnp.asarray(k_dec, F32),
            [float(v) for v in s_dec])


def _rope_tables(pos):
    half = HEAD_D // 2
    freqs = ROPE_BASE ** (-jnp.arange(half, dtype=F32) / half)
    ang = pos.astype(F32)[:, None] * freqs[None, :]
    cos, sin = jnp.cos(ang), jnp.sin(ang)
    return jnp.concatenate([cos, cos], axis=-1), jnp.concatenate([-sin, sin], axis=-1)


def _rope(x, cosf, sinf):
    return x * cosf + pltpu.roll(x, HEAD_D // 2, 1) * sinf


def _ab_prompt_body(rq_ref, rk_ref, rv_ref, rg_ref, mqk_ref, mv_ref, mo_ref, gz_ref, x_ref, cos_ref, sin_ref,
                    decay_ref, qdec_ref, kdec_ref, convw_ref, convb_ref, gbias_ref, gng_ref, gnb_ref, hng_ref,
                    wout_ref,
                    y_ref, s_ref, c_ref, n_ref, m_ref, conv_ref,
                    cbuf_ref, ycat_ref, *, s_dec):
    c = pl.program_id(1)
    tail = CONV_W - 1

    @pl.when(c == 0)
    def _():
        s_ref[...] = jnp.zeros_like(s_ref)
        c_ref[...] = jnp.zeros_like(c_ref)
        n_ref[...] = jnp.zeros_like(n_ref)
        m_ref[...] = jnp.zeros_like(m_ref)
        cbuf_ref[0:SUBLANES, :] = jnp.zeros((SUBLANES, cbuf_ref.shape[1]), F32)

    cosf = cos_ref[...]
    sinf = sin_ref[...]
    row = lax.broadcasted_iota(jnp.int32, (CHUNK, CHUNK), 0)
    col = lax.broadcasted_iota(jnp.int32, (CHUNK, CHUNK), 1)
    eye = row == col
    tril = row >= col
    triu = row <= col

    cbuf_ref[SUBLANES:SUBLANES + CHUNK, :] = mqk_ref[...]
    conv = convb_ref[...]
    for w in range(CONV_W):
        conv = conv + (_r16(cbuf_ref[SUBLANES - tail + w:SUBLANES - tail + w + CHUNK, :])
                       * _r16(convw_ref[w:w + 1, :]))
    qk = _silu(conv)
    last = cbuf_ref[CHUNK + SUBLANES - tail:CHUNK + SUBLANES, :]
    cbuf_ref[SUBLANES - tail:SUBLANES, :] = last
    conv_ref[0] = last

    gz = gz_ref[...] + gbias_ref[...]
    for h in range(RET_HEADS):
        sl = slice(h * HEAD_D, (h + 1) * HEAD_D)
        q = _rope(rq_ref[:, sl], cosf, sinf)
        k = _rope(rk_ref[:, sl], cosf, sinf) * (HEAD_D ** -0.5)
        v = rv_ref[:, sl]
        a = _dot_nt(q, k) * decay_ref[h]
        s_old = s_ref[0, h]
        o = _dot(a, v) + qdec_ref[h] * _dot(q, s_old)
        s_ref[0, h] = s_dec[h] * s_old + _dot_tn(k * kdec_ref[h], v)
        mu = jnp.mean(o, axis=-1, keepdims=True)
        var = jnp.mean(jnp.square(o - mu), axis=-1, keepdims=True)
        o = (o - mu) * lax.rsqrt(var + EPS) * gng_ref[:, sl] + gnb_ref[:, sl]
        ycat_ref[:, sl] = _silu(rg_ref[:, sl]) * o

        mq = qk[:, sl]
        mk = qk[:, ML_HEADS * HEAD_D + h * HEAD_D:ML_HEADS * HEAD_D + (h + 1) * HEAD_D] * (HEAD_D ** -0.5)
        mv = mv_ref[:, sl]
        i_col = gz[:, h:h + 1]
        f_col = _log_sigmoid(gz[:, ML_HEADS + h:ML_HEADS + h + 1])
        i_row = jnp.sum(jnp.where(eye, i_col, 0.0), axis=0, keepdims=True)
        f_row = jnp.sum(jnp.where(eye, f_col, 0.0), axis=0, keepdims=True)
        b_col = jnp.sum(jnp.where(tril, f_row, 0.0), axis=1, keepdims=True)
        b_row = jnp.sum(jnp.where(triu, f_col, 0.0), axis=0, keepdims=True)
        m_old = m_ref[0, h:h + 1, 0:1]
        dlog = jnp.where(tril, b_col - b_row + i_row, -jnp.inf)
        inter = b_col + m_old
        m_t = jnp.maximum(inter, jnp.max(dlog, axis=1, keepdims=True))
        wgt = _dot_nt(mq, mk) * jnp.exp(dlog - m_t)
        e_inter = jnp.exp(inter - m_t)
        c_old = c_ref[0, h]
        n_old = n_ref[0, h:h + 1, :]
        num = _dot(wgt, mv) + e_inter * _dot_nt(mq, c_old)
        den = (jnp.sum(wgt, axis=1, keepdims=True)
               + e_inter * jnp.sum(_r16(mq) * _r16(n_old), axis=1, keepdims=True))
        hc = num / jnp.maximum(jnp.abs(den), jnp.exp(-m_t))
        b_last = b_col[CHUNK - 1:CHUNK, :]
        u_row = b_last - b_row + i_row
        u_col = b_last - b_col + i_col
        m_new = jnp.maximum(b_last + m_old, jnp.max(u_row, axis=1, keepdims=True))
        ws_col = jnp.exp(u_col - m_new)
        f_state = jnp.exp(b_last + m_old - m_new)
        c_ref[0, h] = f_state * c_old + _dot_tn(mv * ws_col, mk)
        n_ref[0, h:h + 1, :] = f_state * n_old + jnp.sum(_r16(ws_col) * _r16(mk), axis=0, keepdims=True)
        m_ref[0, h:h + 1, :] = jnp.broadcast_to(m_new, (1, LANES))
        hm = _sigmoid(mo_ref[:, sl]) * hc
        hm = hm * lax.rsqrt(jnp.mean(hm * hm, axis=-1, keepdims=True) + EPS) * hng_ref[:, sl]
        ycat_ref[:, RET_HEADS * HEAD_D + h * HEAD_D:RET_HEADS * HEAD_D + (h + 1) * HEAD_D] = hm

    y_ref[...] = x_ref[...] + jnp.dot(ycat_ref[...].astype(BF16), wout_ref[...], preferred_element_type=F32)


def ab_prompt(z, x, batch, seq, conv_w, conv_b, b_ig, b_fg, gn_g, gn_b, hn_g, w_out):
    n_chunk = seq // CHUNK
    decay, q_dec, k_dec, s_dec = _retention_constants(CHUNK)
    cosf, sinf = _rope_tables(jnp.arange(seq, dtype=jnp.int32))
    gbias = jnp.zeros((1, LANES), F32).at[0, :ML_HEADS].set(b_ig).at[0, ML_HEADS:2 * ML_HEADS].set(b_fg)
    hw = RET_HEADS * HEAD_D
    qkw = 2 * ML_HEADS * HEAD_D

    def zspec(width, blk):
        return pl.BlockSpec((CHUNK, width), lambda b, c, blk=blk: (b * n_chunk + c, blk))

    def const(shape):
        return pl.BlockSpec(shape, lambda b, c: (0,) * len(shape))

    in_specs = [zspec(hw, 0), zspec(hw, 1), zspec(hw, 2), zspec(hw, 3), zspec(qkw, 2), zspec(hw, 6), zspec(hw, 7),
                zspec(LANES, (AB_IN_PAD - LANES) // LANES),
                pl.BlockSpec((CHUNK, D_MODEL), lambda b, c: (b * n_chunk + c, 0)),
                pl.BlockSpec((CHUNK, HEAD_D), lambda b, c: (c, 0)),
                pl.BlockSpec((CHUNK, HEAD_D), lambda b, c: (c, 0)),
                const((RET_HEADS, CHUNK, CHUNK)), const((RET_HEADS, CHUNK, 1)), const((RET_HEADS, CHUNK, 1)),
                const((CONV_W, qkw)), const((1, qkw)), const((1, LANES)),
                const((1, hw)), const((1, hw)), const((1, hw)), const((2 * hw, D_MODEL))]
    out_shape = (jax.ShapeDtypeStruct((batch * seq, D_MODEL), F32),
                 jax.ShapeDtypeStruct((batch, RET_HEADS, HEAD_D, HEAD_D), F32),
                 jax.ShapeDtypeStruct((batch, ML_HEADS, HEAD_D, HEAD_D), F32),
                 jax.ShapeDtypeStruct((batch, ML_HEADS, HEAD_D), F32),
                 jax.ShapeDtypeStruct((batch, SUBLANES, LANES), F32),
                 jax.ShapeDtypeStruct((batch, CONV_W - 1, qkw), F32))
    out_specs = (pl.BlockSpec((CHUNK, D_MODEL), lambda b, c: (b * n_chunk + c, 0)),
                 pl.BlockSpec((1, RET_HEADS, HEAD_D, HEAD_D), lambda b, c: (b, 0, 0, 0)),
                 pl.BlockSpec((1, ML_HEADS, HEAD_D, HEAD_D), lambda b, c: (b, 0, 0, 0)),
                 pl.BlockSpec((1, ML_HEADS, HEAD_D), lambda b, c: (b, 0, 0)),
                 pl.BlockSpec((1, SUBLANES, LANES), lambda b, c: (b, 0, 0)),
                 pl.BlockSpec((1, CONV_W - 1, qkw), lambda b, c: (b, 0, 0)))
    y, s, cc, n, m, conv = pl.pallas_call(
        functools.partial(_ab_prompt_body, s_dec=s_dec),
        out_shape=out_shape, grid=(batch, n_chunk), in_specs=in_specs, out_specs=out_specs,
        scratch_shapes=[pltpu.VMEM((CHUNK + SUBLANES, qkw), F32), pltpu.VMEM((CHUNK, 2 * hw), F32)],
        compiler_params=_params(2), name="ab_prompt",
    )(z, z, z, z, z, z, z, z, x, cosf, sinf, decay, q_dec, k_dec, conv_w, conv_b.reshape(1, qkw), gbias,
      gn_g.reshape(1, hw), gn_b.reshape(1, hw), hn_g.reshape(1, hw), w_out.astype(BF16))
    return y, s, cc, n, m[:, :ML_HEADS, 0], conv


def _matmul_residual_body(a_ref, w_ref, x_ref, o_ref):
    o_ref[...] = x_ref[...] + jnp.dot(a_ref[...].astype(BF16), w_ref[...], preferred_element_type=F32)


def matmul_residual(a, w, x, row_tile=256):
    n, kk = a.shape
    d = w.shape[1]
    tm = min(row_tile, n)
    return pl.pallas_call(
        _matmul_residual_body, out_shape=jax.ShapeDtypeStruct((n, d), F32), grid=(n // tm,),
        in_specs=[pl.BlockSpec((tm, kk), lambda i: (i, 0)), pl.BlockSpec((kk, d), lambda i: (0, 0)),
                  pl.BlockSpec((tm, d), lambda i: (i, 0))],
        out_specs=pl.BlockSpec((tm, d), lambda i: (i, 0)),
        compiler_params=_params(1), name="matmul_residual",
    )(a, w, x)


def _ab_sample_body(m0_ref, z_ref, cos_ref, sin_ref, s0_ref, c0_ref, n0_ref, conv0_ref,
                    convw_ref, convb_ref, gbias_ref, gng_ref, gnb_ref, hng_ref,
                    y_ref, s_ref, c_ref, n_ref, m_ref, conv_ref, *, g_dec):
    b = pl.program_id(0)
    hw = RET_HEADS * HEAD_D
    qkw = 2 * ML_HEADS * HEAD_D
    tail = CONV_W - 1
    cosf = cos_ref[...]
    sinf = sin_ref[...]
    row = lax.broadcasted_iota(jnp.int32, (HEAD_D, HEAD_D), 0)
    col = lax.broadcasted_iota(jnp.int32, (HEAD_D, HEAD_D), 1)
    eye = row == col

    def to_col(r):
        return jnp.sum(jnp.where(eye, r, 0.0), axis=1, keepdims=True)

    def to_row(cv):
        return jnp.sum(jnp.where(eye, cv, 0.0), axis=0, keepdims=True)

    mqk = z_ref[0, :, 4 * hw:4 * hw + qkw]
    conv = convb_ref[...] + mqk * convw_ref[tail:CONV_W, :]
    for w in range(tail):
        conv = conv + conv0_ref[0, w:w + 1, :] * convw_ref[w:w + 1, :]
    qk = _silu(conv)
    conv_ref[0, 0:tail - 1, :] = conv0_ref[0, 1:tail, :]
    conv_ref[0, tail - 1:tail, :] = mqk
    gz = z_ref[0, :, AB_IN_PAD - LANES:AB_IN_PAD] + gbias_ref[...]

    for h in range(RET_HEADS):
        sl = slice(h * HEAD_D, (h + 1) * HEAD_D)
        q = _rope(z_ref[0, :, sl], cosf, sinf)
        k = _rope(z_ref[0, :, hw + h * HEAD_D:hw + (h + 1) * HEAD_D], cosf, sinf) * (HEAD_D ** -0.5)
        v = z_ref[0, :, 2 * hw + h * HEAD_D:2 * hw + (h + 1) * HEAD_D]
        rg = z_ref[0, :, 3 * hw + h * HEAD_D:3 * hw + (h + 1) * HEAD_D]
        s_old = s0_ref[0, h]
        qk_s = jnp.sum(q * k, axis=1, keepdims=True)
        o = qk_s * v + g_dec[h] * jnp.sum(_r16(to_col(q)) * _r16(s_old), axis=0, keepdims=True)
        s_ref[0, h] = g_dec[h] * s_old + to_col(k) * v
        mu = jnp.mean(o, axis=-1, keepdims=True)
        var = jnp.mean(jnp.square(o - mu), axis=-1, keepdims=True)
        o = (o - mu) * lax.rsqrt(var + EPS) * gng_ref[:, sl] + gnb_ref[:, sl]
        y_ref[0, :, sl] = _silu(rg) * o

        mq = qk[:, sl]
        mk = qk[:, ML_HEADS * HEAD_D + h * HEAD_D:ML_HEADS * HEAD_D + (h + 1) * HEAD_D] * (HEAD_D ** -0.5)
        mv = z_ref[0, :, 4 * hw + qkw + h * HEAD_D:4 * hw + qkw + (h + 1) * HEAD_D]
        mo = z_ref[0, :, 5 * hw + qkw + h * HEAD_D:5 * hw + qkw + (h + 1) * HEAD_D]
        ig = gz[:, h:h + 1]
        lf = _log_sigmoid(gz[:, ML_HEADS + h:ML_HEADS + h + 1])
        m_old = m0_ref[b, h]
        inter = lf + m_old
        m_t = jnp.maximum(inter, ig)
        wgt = jnp.sum(mq * mk, axis=1, keepdims=True) * jnp.exp(ig - m_t)
        e_inter = jnp.exp(inter - m_t)
        c_old = c0_ref[0, h]
        n_old = n0_ref[0, h:h + 1, :]
        cq = to_row(jnp.sum(_r16(c_old) * _r16(mq), axis=1, keepdims=True))
        num = wgt * mv + e_inter * cq
        den = wgt + e_inter * jnp.sum(n_old * mq, axis=1, keepdims=True)
        hc = num / jnp.maximum(jnp.abs(den), jnp.exp(-m_t))
        ws = jnp.exp(ig - m_t)
        c_ref[0, h] = e_inter * c_old + (ws * to_col(mv)) * mk
        n_ref[0, h:h + 1, :] = e_inter * n_old + ws * mk
        m_ref[0, h:h + 1, :] = jnp.broadcast_to(m_t, (1, LANES))
        hm = _sigmoid(mo) * hc
        hm = hm * lax.rsqrt(jnp.mean(hm * hm, axis=-1, keepdims=True) + EPS) * hng_ref[:, sl]
        y_ref[0, :, hw + h * HEAD_D:hw + (h + 1) * HEAD_D] = hm
    m_ref[0, ML_HEADS:SUBLANES, :] = jnp.zeros((SUBLANES - ML_HEADS, LANES), F32)


def ab_sample(z, pos, s0, c0, n0, m0, conv0, conv_w, conv_b, b_ig, b_fg, gn_g, gn_b, hn_g):
    batch = z.shape[0]
    h = np.arange(RET_HEADS, dtype=np.float64)
    g_dec = [float(v) for v in np.exp(np.log1p(-np.exp2(-5.0 - h)))]
    cosf, sinf = _rope_tables(pos)
    gbias = jnp.zeros((1, LANES), F32).at[0, :ML_HEADS].set(b_ig).at[0, ML_HEADS:2 * ML_HEADS].set(b_fg)
    hw = RET_HEADS * HEAD_D
    qkw = 2 * ML_HEADS * HEAD_D

    def per_b(shape):
        return pl.BlockSpec((1,) + shape, lambda b: (b,) + (0,) * len(shape))

    def const(shape):
        return pl.BlockSpec(shape, lambda b: (0,) * len(shape))

    in_specs = [pl.BlockSpec(memory_space=pltpu.SMEM), per_b((1, AB_IN_PAD)), const((1, HEAD_D)), const((1, HEAD_D)),
                per_b((RET_HEADS, HEAD_D, HEAD_D)), per_b((ML_HEADS, HEAD_D, HEAD_D)), per_b((ML_HEADS, HEAD_D)),
                per_b((CONV_W - 1, qkw)), const((CONV_W, qkw)), const((1, qkw)), const((1, LANES)),
                const((1, hw)), const((1, hw)), const((1, hw))]
    out_shape = (jax.ShapeDtypeStruct((batch, 1, 2 * hw), F32),
                 jax.ShapeDtypeStruct((batch, RET_HEADS, HEAD_D, HEAD_D), F32),
                 jax.ShapeDtypeStruct((batch, ML_HEADS, HEAD_D, HEAD_D), F32),
                 jax.ShapeDtypeStruct((batch, ML_HEADS, HEAD_D), F32),
                 jax.ShapeDtypeStruct((batch, SUBLANES, LANES), F32),
                 jax.ShapeDtypeStruct((batch, CONV_W - 1, qkw), F32))
    out_specs = (per_b((1, 2 * hw)), per_b((RET_HEADS, HEAD_D, HEAD_D)), per_b((ML_HEADS, HEAD_D, HEAD_D)),
                 per_b((ML_HEADS, HEAD_D)), per_b((SUBLANES, LANES)), per_b((CONV_W - 1, qkw)))
    y, s, cc, n, m, conv = pl.pallas_call(
        functools.partial(_ab_sample_body, g_dec=g_dec),
        out_shape=out_shape, grid=(batch,), in_specs=in_specs, out_specs=out_specs,
        compiler_params=_params(1), name="ab_sample",
    )(m0, z.reshape(batch, 1, AB_IN_PAD), cosf, sinf, s0, c0, n0, conv0, conv_w, conv_b.reshape(1, qkw), gbias,
      gn_g.reshape(1, hw), gn_b.reshape(1, hw), hn_g.reshape(1, hw))
    return y.reshape(batch, 2 * hw), s, cc, n, m[:, :ML_HEADS, 0], conv


MOE_TILE = 256


def _moe_router_body(x_ref, g_ref, wr_ref, br_ref, hn_ref, route_ref, count_ref):
    x = x_ref[...]
    hn = x * lax.rsqrt(jnp.mean(x * x, axis=-1, keepdims=True) + EPS) * g_ref[...]
    hn_ref[...] = hn
    z = _dot(hn, wr_ref[...]) + br_ref[...]
    lane = lax.broadcasted_iota(jnp.int32, z.shape, 1)
    lanef = lane.astype(F32)
    is_group = lane < MOE_GROUPS
    gl = jnp.where(is_group, z, -jnp.inf)
    gmax = jnp.max(gl, axis=1, keepdims=True)
    g_top = jnp.min(jnp.where(gl == gmax, lanef, float(LANES)), axis=1, keepdims=True)
    pg_top = 1.0 / jnp.sum(jnp.where(is_group, jnp.exp(z - gmax), 0.0), axis=1, keepdims=True)
    grp = ((lane - MOE_GROUPS) // MOE_EXP_PER_GROUP).astype(F32)
    in_group = (lane >= MOE_GROUPS) & (lane < MOE_GROUPS + MOE_EXPERTS) & (grp == g_top)
    el = jnp.where(in_group, z, -jnp.inf)
    v1 = jnp.max(el, axis=1, keepdims=True)
    i1 = jnp.min(jnp.where(el == v1, lanef, float(LANES)), axis=1, keepdims=True)
    el2 = jnp.where(lanef == i1, -jnp.inf, el)
    v2 = jnp.max(el2, axis=1, keepdims=True)
    i2 = jnp.min(jnp.where(el2 == v2, lanef, float(LANES)), axis=1, keepdims=True)
    t = jnp.exp(v2 - v1)
    p1 = 1.0 / (1.0 + t)
    out = jnp.where(lane == 0, i1 - MOE_GROUPS,
                    jnp.where(lane == 1, i2 - MOE_GROUPS,
                              jnp.where(lane == 2, pg_top * p1,
                                        jnp.where(lane == 3, pg_top * (t * p1), 0.0))))
    route_ref[...] = out
    picked = jnp.where((lanef == i1 - MOE_GROUPS) | (lanef == i2 - MOE_GROUPS), 1.0, 0.0)

    @pl.when(pl.program_id(0) == 0)
    def _():
        count_ref[...] = jnp.zeros_like(count_ref)

    count_ref[...] += jnp.sum(picked, axis=0, keepdims=True)


def moe_router(x, g, w_group, b_group, w_expert, b_expert, row_tile=256):
    n, d = x.shape
    tm = min(row_tile, n)
    used = MOE_GROUPS + MOE_EXPERTS
    wr = jnp.pad(jnp.concatenate([w_group, w_expert], axis=1), ((0, 0), (0, LANES - used)))
    br = jnp.pad(jnp.concatenate([b_group, b_expert]), (0, LANES - used)).reshape(1, LANES)
    hn, route, count = pl.pallas_call(
        _moe_router_body,
        out_shape=(jax.ShapeDtypeStruct((n, d), F32), jax.ShapeDtypeStruct((n, LANES), F32),
                   jax.ShapeDtypeStruct((1, LANES), F32)),
        grid=(n // tm,),
        in_specs=[pl.BlockSpec((tm, d), lambda i: (i, 0)), pl.BlockSpec((1, d), lambda i: (0, 0)),
                  pl.BlockSpec((d, LANES), lambda i: (0, 0)), pl.BlockSpec((1, LANES), lambda i: (0, 0))],
        out_specs=(pl.BlockSpec((tm, d), lambda i: (i, 0)), pl.BlockSpec((tm, LANES), lambda i: (i, 0)),
                   pl.BlockSpec((1, LANES), lambda i: (0, 0))),
        compiler_params=_params(1), name="moe_router",
    )(x, g.reshape(1, d), wr, br)
    return hn, route, route[:, 0:2].astype(jnp.int32), count[0, :MOE_EXPERTS].astype(jnp.int32)


def _moe_ffn_body(blk_e_ref, n_real_ref, asg_ref, hn_ref, wg_ref, wu_ref, wd_ref, o_ref,
                  x_s, y_s, wg_s, wu_s, wd_s, gsem, ssem, *, n_blk):
    i = pl.program_id(0)
    slot = i % 2

    def gather_row(tile, r, slot):
        tok = lax.shift_right_logical(asg_ref[tile * MOE_TILE + r], 1)
        return pltpu.make_async_copy(hn_ref.at[pl.ds(tok, 1), :], x_s.at[slot, pl.ds(r, 1), :], gsem.at[slot])

    def scatter_row(r):
        a = asg_ref[i * MOE_TILE + r]
        return pltpu.make_async_copy(y_s.at[pl.ds(r, 1), :],
                                     o_ref.at[a & 1, pl.ds(lax.shift_right_logical(a, 1), 1), :], ssem.at[0])

    def for_rows(n_rows, fn):
        @pl.when(n_rows == MOE_TILE)
        def _():
            for r in range(MOE_TILE):
                fn(r)

        @pl.when(n_rows < MOE_TILE)
        def _():
            def body(r, c):
                fn(r)
                return c
            lax.fori_loop(0, n_rows, body, 0)

    def start_gather(tile, slot):
        for_rows(n_real_ref[tile], lambda r: gather_row(tile, r, slot).start())

    @pl.when(i == 0)
    def _():
        x_s[...] = jnp.zeros_like(x_s)
        start_gather(0, 0)

    for s in range(2):
        @pl.when((i + 1 < n_blk) & (slot == 1 - s))
        def _(s=s):
            start_gather(i + 1, s)

    prev = blk_e_ref[jnp.maximum(i - 1, 0)]

    @pl.when((i == 0) | (blk_e_ref[i] != prev))
    def _():
        wg_s[...] = wg_ref[0].astype(BF16)
        wu_s[...] = wu_ref[0].astype(BF16)
        wd_s[...] = wd_ref[0].astype(BF16)

    n_real = n_real_ref[i]
    for_rows(n_real, lambda r: gather_row(i, 0, slot).wait())
    n_prev = jnp.where(i > 0, n_real_ref[jnp.maximum(i - 1, 0)], 0)
    for_rows(n_prev, lambda r: scatter_row(0).wait())

    @pl.when(n_real > 0)
    def _():
        x = x_s[slot].astype(BF16)
        hg = jnp.dot(x, wg_s[...], preferred_element_type=F32)
        hu = jnp.dot(x, wu_s[...], preferred_element_type=F32)
        hb = (_silu(hg) * hu).astype(BF16)
        y_s[...] = jnp.dot(hb, wd_s[...], preferred_element_type=F32)
        for_rows(n_real, lambda r: scatter_row(r).start())

    @pl.when(i == n_blk - 1)
    def _():
        for_rows(n_real, lambda r: scatter_row(0).wait())


def moe_ffn(hn, asg, n_real, blk_e, layer, w_g, w_u, w_d):
    n_tok, d = hn.shape
    ff = w_g.shape[3]
    n_blk = blk_e.shape[0]
    grid_spec = pltpu.PrefetchScalarGridSpec(
        num_scalar_prefetch=3, grid=(n_blk,),
        in_specs=[pl.BlockSpec(memory_space=pl.ANY),
                  pl.BlockSpec((None, 1, d, ff), lambda i, e, nr, a: (layer, e[i], 0, 0)),
                  pl.BlockSpec((None, 1, d, ff), lambda i, e, nr, a: (layer, e[i], 0, 0)),
                  pl.BlockSpec((None, 1, ff, d), lambda i, e, nr, a: (layer, e[i], 0, 0))],
        out_specs=pl.BlockSpec(memory_space=pl.ANY),
        scratch_shapes=[pltpu.VMEM((2, MOE_TILE, d), F32), pltpu.VMEM((MOE_TILE, d), F32),
                        pltpu.VMEM((d, ff), BF16), pltpu.VMEM((d, ff), BF16), pltpu.VMEM((ff, d), BF16),
                        pltpu.SemaphoreType.DMA((2,)), pltpu.SemaphoreType.DMA((1,))])
    return pl.pallas_call(
        functools.partial(_moe_ffn_body, n_blk=n_blk),
        out_shape=jax.ShapeDtypeStruct((2, n_tok, d), F32), grid_spec=grid_spec,
        compiler_params=_params(1), name="moe_ffn",
    )(blk_e, n_real, asg, hn, w_g, w_u, w_d)


def _moe_combine_body(x_ref, route_ref, y0_ref, y1_ref, o_ref):
    o_ref[...] = x_ref[...] + (y0_ref[...] * route_ref[:, 2:3] + y1_ref[...] * route_ref[:, 3:4])


def moe_combine(x, route, y, row_tile=512):
    n, d = x.shape
    tm = min(row_tile, n)
    return pl.pallas_call(
        _moe_combine_body, out_shape=jax.ShapeDtypeStruct((n, d), F32), grid=(n // tm,),
        in_specs=[pl.BlockSpec((tm, d), lambda i: (i, 0)), pl.BlockSpec((tm, LANES), lambda i: (i, 0)),
                  pl.BlockSpec((None, tm, d), lambda i: (0, i, 0)), pl.BlockSpec((None, tm, d), lambda i: (1, i, 0))],
        out_specs=pl.BlockSpec((tm, d), lambda i: (i, 0)),
        compiler_params=_params(1), name="moe_combine",
    )(x, route, y, y)


def hier_moe_residual(x, g, w_group, b_group, w_expert, b_expert, layer, w_g, w_u, w_d):
    n, d = x.shape
    hn, route, expert, counts = moe_router(x, g, w_group, b_group, w_expert, b_expert)
    n_exp = w_g.shape[1]
    kk = expert.shape[1]
    assert kk == 2
    a = n * kk
    order = jnp.argsort(expert.reshape(-1)).astype(jnp.int32)
    starts = jnp.cumsum(counts) - counts
    padded = (counts + MOE_TILE - 1) // MOE_TILE * MOE_TILE
    pend = jnp.cumsum(padded)
    pstart = pend - padded
    n_blk = -(-(a + n_exp * (MOE_TILE - 1)) // MOE_TILE)
    tile_start = jnp.arange(n_blk, dtype=jnp.int32) * MOE_TILE
    blk_e = jnp.minimum(jnp.sum((pend[None, :] <= tile_start[:, None]).astype(jnp.int32), axis=1), n_exp - 1)
    n_real = jnp.clip(counts[blk_e] - (tile_start - pstart[blk_e]), 0, MOE_TILE).astype(jnp.int32)
    row_off = (tile_start - pstart[blk_e] + starts[blk_e])[:, None] + jnp.arange(MOE_TILE, dtype=jnp.int32)[None, :]
    asg = order[jnp.clip(row_off, 0, a - 1)].reshape(-1)
    y = moe_ffn(hn, asg, n_real, blk_e, layer, w_g, w_u, w_d)
    return moe_combine(x, route, y)


KV_W = 2 * NSA_KV_HEADS * NSA_HD
Q_W = NSA_HEADS * NSA_HD
MASK_BUCKET = REL_BUCKETS
KEY_TILE = 512


def _t5_thresholds():
    exact = REL_BUCKETS // 2
    dist = np.arange(0, 4 * REL_MAX_DIST, dtype=np.int64)
    nf = np.maximum(dist, 1).astype(np.float64)
    large = exact + np.floor(np.log(nf / exact) / math.log(REL_MAX_DIST / exact) * (REL_BUCKETS - exact) + 1e-9)
    bucket = np.where(dist < exact, dist, np.minimum(large, REL_BUCKETS - 1)).astype(np.int64)
    return [int(np.argmax(bucket >= b)) for b in range(1, REL_BUCKETS)]


def _bucket_index(dist, valid):
    idx = jnp.zeros(dist.shape, jnp.int32)
    for thr in _t5_thresholds():
        idx = idx + (dist >= thr).astype(jnp.int32)
    return jnp.where(valid, idx, MASK_BUCKET)


def _bias_table(rel_bias):
    t = jnp.zeros((NSA_HEADS, LANES), F32).at[:, :REL_BUCKETS].set(rel_bias.T.astype(F32))
    return t.at[:, MASK_BUCKET].set(NEG)


def _group_mean_matrix():
    i = np.arange(LANES)
    return jnp.asarray((i[:, None] // NSA_HD == i[None, :] // NSA_HD) / NSA_HD, F32)


def _nsa_prep_body(zq_ref, zc_ref, zs_ref, zw_ref, zg_ref, bd_ref, qn_ref, kns_ref, knw_ref,
                   q_ref, cmp_ref, sel_ref, win_ref, selb_ref, winb_ref, gate_ref, *kt_ref, transposed):
    bd = bd_ref[...]

    def head_norm(x, gain):
        ms = _dot_f32(x * x, bd)
        return x * lax.rsqrt(ms + EPS) * gain

    def emit(o_ref, rows):
        if transposed:
            o_ref[0] = rows.T
        else:
            o_ref[...] = rows

    for c in range(Q_W // LANES):
        sl = slice(c * LANES, (c + 1) * LANES)
        q_ref[:, sl] = (head_norm(zq_ref[:, sl], qn_ref[...]) * (NSA_HD ** -0.5)).astype(BF16)
    emit(cmp_ref, zc_ref[...])
    half = KV_W // 2
    for z_ref, kn_ref, o_ref, ob_ref in ((zs_ref, kns_ref, sel_ref, selb_ref), (zw_ref, knw_ref, win_ref, winb_ref)):
        kn = jnp.concatenate([head_norm(z_ref[:, c * LANES:(c + 1) * LANES], kn_ref[...])
                              for c in range(half // LANES)], axis=1)
        rows = jnp.concatenate([kn, z_ref[:, half:KV_W]], axis=1)
        emit(o_ref, rows)
        ob_ref[...] = rows.astype(BF16)
        if transposed and o_ref is sel_ref:
            kt_ref[0][0] = kn.T.astype(BF16)
    gate_ref[...] = _sigmoid(zg_ref[...])


def nsa_prep(z, q_norm, k_norm, batch=None, row_tile=256):
    n = z.shape[0]
    tm = min(row_tile, n)
    transposed = batch is not None

    def zspec(width, blk):
        return pl.BlockSpec((tm, width), lambda i, blk=blk: (i, blk))

    def const(shape):
        return pl.BlockSpec(shape, lambda i: (0,) * len(shape))

    def tile2(v):
        return jnp.concatenate([v, v]).reshape(1, LANES).astype(F32)

    def rows(w, dt):
        return jax.ShapeDtypeStruct((n, w), dt)

    def out_spec(w):
        return pl.BlockSpec((tm, w), lambda i: (i, 0))

    if transposed:
        seq = n // batch
        tps = seq // tm
        kv_shape = jax.ShapeDtypeStruct((batch, KV_W, seq), F32)
        kv_spec = pl.BlockSpec((1, KV_W, tm), lambda i: (i // tps, 0, i % tps))
        extra_shape = (jax.ShapeDtypeStruct((batch, KV_W // 2, seq), BF16),)
        extra_spec = (pl.BlockSpec((1, KV_W // 2, tm), lambda i: (i // tps, 0, i % tps)),)
    else:
        kv_shape, kv_spec, extra_shape, extra_spec = rows(KV_W, F32), out_spec(KV_W), (), ()
    return pl.pallas_call(
        functools.partial(_nsa_prep_body, transposed=transposed),
        out_shape=(rows(Q_W, BF16), kv_shape, kv_shape, kv_shape, rows(KV_W, BF16), rows(KV_W, BF16),
                   rows(LANES, F32)) + extra_shape,
        grid=(n // tm,),
        in_specs=[zspec(Q_W, 0), zspec(KV_W, 2), zspec(KV_W, 3), zspec(KV_W, 4),
                  zspec(LANES, (Q_W + 3 * KV_W) // LANES), const((LANES, LANES)),
                  const((1, LANES)), const((1, LANES)), const((1, LANES))],
        out_specs=(out_spec(Q_W), kv_spec, kv_spec, kv_spec, out_spec(KV_W), out_spec(KV_W),
                   out_spec(LANES)) + extra_spec,
        compiler_params=_params(1), name="nsa_prep",
    )(z, z, z, z, z, _group_mean_matrix(), tile2(q_norm), tile2(k_norm[1]), tile2(k_norm[2]))


PAGES_PER_STEP = 16
PAGE_ROWS = 128
SUBS_PER_PAGE = PAGE_ROWS // CMP_STRIDE
P_W = 2 * NSA_KV_HEADS * 2 * CMP_HID
KV_PAIRS = NSA_KV_HEADS // 2


def _gelu_tanh(x):
    return 0.5 * x * (1.0 + jnp.tanh(math.sqrt(2.0 / math.pi) * (x + 0.044715 * x * x * x)))


def _compress_body(pt_ref, *refs, n_steps):
    page_refs = refs[:PAGES_PER_STEP]
    w1_ref, b1_ref, w2_ref, b2_ref, kn_ref, kc_ref, vc_ref, p_ref, x_s = refs[PAGES_PER_STEP:]
    j = pl.program_id(1)
    rows = PAGES_PER_STEP * SUBS_PER_PAGE
    r0 = pl.multiple_of(j * rows, rows)
    for t, pr in enumerate(page_refs):
        for v in range(2):
            for gp in range(KV_PAIRS):
                x_s[t, v * KV_PAIRS + gp] = pr[0, v, 2 * gp:2 * gp + 2].reshape(2 * NSA_HD, PAGE_ROWS).T
    for v in range(2):
        for gp in range(KV_PAIRS):
            acc = jnp.zeros((rows, 4 * CMP_HID), F32)
            for s in range(CMP_STRIDE):
                x = jnp.concatenate([x_s[t, v * KV_PAIRS + gp, pl.ds(s, SUBS_PER_PAGE, stride=CMP_STRIDE), :]
                                     for t in range(PAGES_PER_STEP)], axis=0)
                acc = acc + jnp.dot(x.astype(BF16), w1_ref[v, s], preferred_element_type=F32)
            p0 = (v * NSA_KV_HEADS + 2 * gp) * 2 * CMP_HID
            p_ref[pl.ds(r0, rows), p0:p0 + 4 * CMP_HID] = acc

    @pl.when(j == n_steps - 1)
    def _():
        n_sub = p_ref.shape[0]
        for v in range(2):
            for g in range(NSA_KV_HEADS):
                p0 = (v * NSA_KV_HEADS + g) * 2 * CMP_HID
                hs = p_ref[:, p0:p0 + CMP_HID] + pltpu.roll(p_ref[:, p0 + CMP_HID:p0 + 2 * CMP_HID], n_sub - 1, 0)
                hid = _gelu_tanh(hs + b1_ref[v:v + 1, :])
                out = _dot(hid, w2_ref[v]) + b2_ref[v:v + 1, :]
                if v == 0:
                    out = out * lax.rsqrt(jnp.mean(out * out, axis=-1, keepdims=True) + EPS) * kn_ref[...]
                    kc_ref[0, :, g * NSA_HD:(g + 1) * NSA_HD] = out
                else:
                    vc_ref[0, :, g * NSA_HD:(g + 1) * NSA_HD] = out


def nsa_compress(rows_t, page_table, w1, b1, w2, b2, k_norm0):
    if page_table is None:
        batch, n_pp = rows_t.shape[0], rows_t.shape[-1] // PAGE_ROWS
        page_table = jnp.zeros((1, 1), jnp.int32)

        def page_index(b, p, pt):
            return (b, 0, 0, 0, p)
    else:
        batch, n_pp = page_table.shape

        def page_index(b, p, pt):
            return (pt[b, p], 0, 0, 0, 0)
    n_steps = n_pp // PAGES_PER_STEP
    n_sub = n_pp * SUBS_PER_PAGE
    w = w1.reshape(2, 2, CMP_STRIDE, NSA_HD, CMP_HID)
    w = jnp.transpose(w, (0, 2, 3, 1, 4)).reshape(2, CMP_STRIDE, NSA_HD, 2 * CMP_HID)
    zero = jnp.zeros_like(w)
    wpair = jnp.concatenate([jnp.concatenate([w, zero], axis=-1), jnp.concatenate([zero, w], axis=-1)], axis=2)
    wpair = wpair.astype(BF16)

    def page_spec(t):
        return pl.BlockSpec((1, 2, NSA_KV_HEADS, NSA_HD, PAGE_ROWS),
                            lambda b, j, pt, t=t: page_index(b, j * PAGES_PER_STEP + t, pt))

    def const(shape):
        return pl.BlockSpec(shape, lambda b, j, pt: (0,) * len(shape))

    grid_spec = pltpu.PrefetchScalarGridSpec(
        num_scalar_prefetch=1, grid=(batch, n_steps),
        in_specs=[page_spec(t) for t in range(PAGES_PER_STEP)] + [
            const((2, CMP_STRIDE, LANES, 4 * CMP_HID)), const((2, CMP_HID)), const((2, CMP_HID, NSA_HD)),
            const((2, NSA_HD)), const((1, NSA_HD))],
        out_specs=(pl.BlockSpec((1, n_sub, KV_W // 2), lambda b, j, pt: (b, 0, 0)),
                   pl.BlockSpec((1, n_sub, KV_W // 2), lambda b, j, pt: (b, 0, 0))),
        scratch_shapes=[pltpu.VMEM((n_sub, P_W), F32),
                        pltpu.VMEM((PAGES_PER_STEP, 2 * KV_PAIRS, PAGE_ROWS, 2 * NSA_HD), F32)])
    return pl.pallas_call(
        functools.partial(_compress_body, n_steps=n_steps),
        out_shape=(jax.ShapeDtypeStruct((batch, n_sub, KV_W // 2), F32),
                   jax.ShapeDtypeStruct((batch, n_sub, KV_W // 2), F32)),
        grid_spec=grid_spec, compiler_params=_params(2), name="nsa_compress",
    )(page_table, *([rows_t] * PAGES_PER_STEP), wpair, b1, w2.astype(BF16), b2, k_norm0.reshape(1, NSA_HD))


Q_TILE = 128
WIN_BLOCKS = (WINDOW + Q_TILE) // Q_TILE
NEAR_TILES = 3
NEAR_CHUNKS = 8
M_INIT = -1e29


def _gather_bias(tbh, idx_slices):
    return jnp.concatenate([jnp.take_along_axis(tbh, idx, axis=1) for idx in idx_slices], axis=1)


def _nsa_attend_body(q_ref, gate_ref, x_ref, kc_ref, vc_ref, kst_ref, vs_ref, w0_ref, w1_ref, w2_ref, w3_ref, w4_ref,
                     idxc_ref, idxw_ref, nb_ref, mmat_ref, tb_ref, wout_ref, expand_ref, y_ref,
                     pc_s, oc_s, sel_s, m_s, l_s, a_s, acc_s, ps_s, pw_s, o_s, bsel_s, bwin_s, *, n_sel):
    win_refs = (w0_ref, w1_ref, w2_ref, w3_ref, w4_ref)
    t = pl.program_id(1)
    s0 = t * Q_TILE
    n_sub = kc_ref.shape[1]
    rep_rows = NSA_REP * Q_TILE
    q_pos = s0 + lax.broadcasted_iota(jnp.int32, (Q_TILE, LANES), 0)
    blk = lax.broadcasted_iota(jnp.int32, (Q_TILE, LANES), 1)
    cur = q_pos // SEL_BLOCK
    forced = (blk == 0) | (blk == cur) | (blk == cur - 1)
    future = blk * SEL_BLOCK > q_pos

    def group_q(g):
        return jnp.concatenate([q_ref[:, (g * NSA_REP + r) * NSA_HD:(g * NSA_REP + r + 1) * NSA_HD]
                                for r in range(NSA_REP)], axis=0)

    def head_table(h):
        return jnp.broadcast_to(tb_ref[h:h + 1, :], (Q_TILE, LANES))

    @pl.when((pl.program_id(0) == 0) & (t == 0))
    def _():
        for h in range(NSA_HEADS):
            tbh = head_table(h)
            far_bias = tb_ref[h:h + 1, REL_BUCKETS - 1:REL_BUCKETS]
            for k in range(NEAR_CHUNKS):
                bsel_s[h, k] = jnp.take_along_axis(tbh, nb_ref[k], axis=1) - far_bias
            bsel_s[h, NEAR_CHUNKS] = jnp.zeros((Q_TILE, LANES), F32)
            bsel_s[h, NEAR_CHUNKS + 1] = jnp.full((Q_TILE, LANES), NEG, F32)
            bwin_s[h] = _gather_bias(tbh, [idxw_ref[:, c * LANES:(c + 1) * LANES] for c in range(WIN_BLOCKS)])

    score_t = []
    for g in range(NSA_KV_HEADS):
        gs = slice(g * NSA_HD, (g + 1) * NSA_HD)
        sc = _dot_nt(group_q(g), kc_ref[0, :, gs])
        imp = jnp.zeros((Q_TILE, n_sub), F32)
        for r in range(NSA_REP):
            rs = slice(r * Q_TILE, (r + 1) * Q_TILE)
            bias = _gather_bias(head_table(g * NSA_REP + r),
                                [idxc_ref[0, :, c * LANES:(c + 1) * LANES] for c in range(n_sub // LANES)])
            s_r = sc[rs] + bias
            m = jnp.maximum(jnp.max(s_r, axis=1, keepdims=True), M_INIT)
            e = jnp.exp(s_r - m)
            p = e / jnp.maximum(jnp.sum(e, axis=1, keepdims=True), 1e-30)
            imp = imp + p
            pc_s[rs, :] = p.astype(BF16)
        oc_s[g] = jnp.dot(pc_s[...], vc_ref[0, :, gs], preferred_element_type=F32)
        score = _dot_f32(imp, mmat_ref[...])
        score = jnp.where(forced, FORCE_SCORE, score)
        score = jnp.where(future, NEG, score)
        score = jnp.where(blk >= n_sel, -jnp.inf, score)
        score_t.append(score.T)

    blk_t = lax.broadcasted_iota(jnp.int32, (LANES, Q_TILE), 0).astype(F32)
    sel_t = [jnp.zeros((LANES, Q_TILE), F32) for _ in range(NSA_KV_HEADS)]
    for _ in range(min(SEL_TOPK, n_sel)):
        for g in range(NSA_KV_HEADS):
            mx = jnp.max(score_t[g], axis=0, keepdims=True)
            first = jnp.min(jnp.where(score_t[g] == mx, blk_t, float(LANES)), axis=0, keepdims=True)
            pick = blk_t == first
            sel_t[g] = jnp.where(pick, 1.0, sel_t[g])
            score_t[g] = jnp.where(pick, -jnp.inf, score_t[g])
    for g in range(NSA_KV_HEADS):
        sel_s[g] = ((sel_t[g].T - 1.0) * (-NEG)).astype(BF16)

    n_kt = (s0 + Q_TILE + KEY_TILE - 1) // KEY_TILE
    n_far = jnp.maximum(n_kt - NEAR_TILES, 0)

    for pair in ((0, 1), (2, 3)):
        qgs = [group_q(g) for g in pair]
        m_s[...] = jnp.full(m_s.shape, M_INIT, F32)
        l_s[...] = jnp.zeros(l_s.shape, F32)
        acc_s[...] = jnp.zeros(acc_s.shape, F32)

        def key_tile(kt, near, pair=pair, qgs=qgs):
            k0 = pl.multiple_of(kt * KEY_TILE, KEY_TILE)
            for j, g in enumerate(pair):
                gs = slice(g * NSA_HD, (g + 1) * NSA_HD)
                s = jnp.dot(qgs[j], kst_ref[0, gs, pl.ds(k0, KEY_TILE)], preferred_element_type=F32)
                negm = jnp.dot(sel_s[g], expand_ref[:, pl.ds(k0, KEY_TILE)], preferred_element_type=F32)
                v = vs_ref[pl.ds(k0, KEY_TILE), gs]
                for r in range(NSA_REP):
                    rs = slice(r * Q_TILE, (r + 1) * Q_TILE)
                    h = g * NSA_REP + r
                    if near:
                        chunks = []
                        for c in range(KEY_TILE // LANES):
                            k = (s0 - k0) // LANES - c
                            chunks.append(bsel_s[h, jnp.where(k < 0, NEAR_CHUNKS + 1, jnp.minimum(k, NEAR_CHUNKS))])
                        s_r = s[rs] + (negm + jnp.concatenate(chunks, axis=1))
                    else:
                        s_r = s[rs] + negm
                    m_old = m_s[j, rs]
                    m_new = jnp.maximum(m_old, jnp.max(s_r, axis=1, keepdims=True))
                    alpha = jnp.exp(m_old - m_new)
                    p = jnp.exp(s_r - jnp.concatenate([m_new] * (KEY_TILE // LANES), axis=1))
                    l_s[j, rs] = alpha * l_s[j, rs] + jnp.sum(p, axis=1, keepdims=True)
                    m_s[j, rs] = m_new
                    acc_s[j, rs] = (alpha[:, :NSA_HD] * acc_s[j, rs]
                                    + jnp.dot(p.astype(BF16), v, preferred_element_type=F32))

        def far_body(kt, carry):
            key_tile(kt, False)
            return carry

        def near_body(kt, carry):
            key_tile(kt, True)
            return carry

        lax.fori_loop(0, n_far, far_body, 0)
        lax.fori_loop(n_far, n_kt, near_body, 0)

        for j, g in enumerate(pair):
            gs = slice(g * NSA_HD, (g + 1) * NSA_HD)
            kw = jnp.concatenate([wr[0, :, gs] for wr in win_refs], axis=0)
            vw = jnp.concatenate([wr[0, :, KV_W // 2 + g * NSA_HD:KV_W // 2 + (g + 1) * NSA_HD] for wr in win_refs],
                                 axis=0)
            sw = _dot_nt(qgs[j], kw)
            for r in range(NSA_REP):
                rs = slice(r * Q_TILE, (r + 1) * Q_TILE)
                s_r = sw[rs] + bwin_s[g * NSA_REP + r]
                e = jnp.exp(s_r - jnp.max(s_r, axis=1, keepdims=True))
                pw_s[rs, :] = (e / jnp.sum(e, axis=1, keepdims=True)).astype(BF16)
            o_w = jnp.dot(pw_s[...], vw, preferred_element_type=F32)

            for r in range(NSA_REP):
                rs = slice(r * Q_TILE, (r + 1) * Q_TILE)
                h = g * NSA_REP + r
                o_h = (gate_ref[:, 3 * h:3 * h + 1] * oc_s[g, rs, :]
                       + gate_ref[:, 3 * h + 1:3 * h + 2] * (acc_s[j, rs, :] / l_s[j, rs, :NSA_HD])
                       + gate_ref[:, 3 * h + 2:3 * h + 3] * o_w[rs])
                o_s[:, h * NSA_HD:(h + 1) * NSA_HD] = o_h.astype(BF16)

    y_ref[...] = x_ref[...] + jnp.dot(o_s[...], wout_ref[...], preferred_element_type=F32)


def nsa_attend_prompt(q, gates, x, k_c, v_c, kst, selb, winb, rel_bias, w_out, batch, seq):
    n_qt = seq // Q_TILE
    n_sub = k_c.shape[1]
    n_sel = seq // SEL_BLOCK
    assert CMP_LEN == 2 * CMP_STRIDE and SEL_BLOCK == 4 * CMP_STRIDE and SEL_TOPK <= n_sel <= LANES
    win_pad = jnp.pad(winb.reshape(batch, seq, KV_W), ((0, 0), (WINDOW, 0), (0, 0)))
    iq = jnp.arange(Q_TILE, dtype=jnp.int32)
    dist_c = (jnp.arange(n_qt, dtype=jnp.int32)[:, None, None] * Q_TILE + iq[None, :, None]
              - (jnp.arange(n_sub, dtype=jnp.int32)[None, None, :] * CMP_STRIDE + CMP_LEN - 1))
    idx_c = _bucket_index(dist_c, dist_c >= 0)
    dist_w = iq[:, None] - jnp.arange(WINDOW + Q_TILE, dtype=jnp.int32)[None, :] + WINDOW
    idx_w = _bucket_index(dist_w, (dist_w >= 0) & (dist_w < WINDOW))
    dist_n = (jnp.arange(NEAR_CHUNKS, dtype=jnp.int32)[:, None, None] * LANES + iq[None, :, None]
              - jnp.arange(LANES, dtype=jnp.int32)[None, None, :])
    nb = _bucket_index(dist_n, dist_n >= 0)
    ci = np.arange(n_sub)[:, None]
    bj = np.arange(LANES)[None, :]
    mmat = ((ci // 4 == bj).astype(np.float32) + ((ci + 1) // 4 == bj).astype(np.float32)) * (ci < n_sub - 1)
    expand = (np.arange(seq)[None, :] // SEL_BLOCK == np.arange(LANES)[:, None]).astype(np.float32)

    def rows(width):
        return pl.BlockSpec((Q_TILE, width), lambda b, t: (b * n_qt + t, 0))

    def per_b(shape):
        return pl.BlockSpec((1,) + shape, lambda b, t: (b,) + (0,) * len(shape))

    def const(shape):
        return pl.BlockSpec(shape, lambda b, t: (0,) * len(shape))

    in_specs = [rows(Q_W), rows(LANES), rows(D_MODEL), per_b((n_sub, KV_W // 2)), per_b((n_sub, KV_W // 2)),
                per_b((KV_W // 2, seq)), pl.BlockSpec((seq, KV_W // 2), lambda b, t: (b, 1))]
    in_specs += [pl.BlockSpec((1, Q_TILE, KV_W), lambda b, t, j=j: (b, t + j, 0)) for j in range(WIN_BLOCKS)]
    in_specs += [pl.BlockSpec((1, Q_TILE, n_sub), lambda b, t: (t, 0, 0)), const((Q_TILE, WINDOW + Q_TILE)),
                 const((NEAR_CHUNKS, Q_TILE, LANES)), const((n_sub, LANES)), const((NSA_HEADS, LANES)),
                 const((Q_W, D_MODEL)), const((LANES, seq))]
    rep_rows = NSA_REP * Q_TILE
    scratch = [pltpu.VMEM((rep_rows, n_sub), BF16), pltpu.VMEM((NSA_KV_HEADS, rep_rows, NSA_HD), F32),
               pltpu.VMEM((NSA_KV_HEADS, Q_TILE, LANES), BF16), pltpu.VMEM((2, rep_rows, LANES), F32),
               pltpu.VMEM((2, rep_rows, LANES), F32), pltpu.VMEM((rep_rows, LANES), F32),
               pltpu.VMEM((2, rep_rows, NSA_HD), F32),
               pltpu.VMEM((rep_rows, KEY_TILE), BF16), pltpu.VMEM((rep_rows, WINDOW + Q_TILE), BF16),
               pltpu.VMEM((Q_TILE, Q_W), BF16),
               pltpu.VMEM((NSA_HEADS, NEAR_CHUNKS + 2, Q_TILE, LANES), F32),
               pltpu.VMEM((NSA_HEADS, Q_TILE, WINDOW + Q_TILE), F32)]
    return pl.pallas_call(
        functools.partial(_nsa_attend_body, n_sel=n_sel),
        out_shape=jax.ShapeDtypeStruct((batch * seq, D_MODEL), F32), grid=(batch, n_qt),
        in_specs=in_specs, out_specs=rows(D_MODEL), scratch_shapes=scratch,
        compiler_params=_params(2), name="nsa_attend_prompt",
    )(q, gates, x, k_c.astype(BF16), v_c.astype(BF16), kst, selb, *([win_pad] * WIN_BLOCKS),
      idx_c, idx_w, nb, jnp.asarray(mmat), _bias_table(rel_bias), w_out.astype(BF16), jnp.asarray(expand, BF16))


SCORE_W = 384


def _nsa_sample_cmp_body(q_ref, kc_ref, vc_ref, idx_ref, mmat_ref, gsum_ref, tb_ref, oc_ref, top_ref, *,
                         n_sel, q_pos):
    n_sub = kc_ref.shape[1]
    q = q_ref[0]
    row_g = lax.broadcasted_iota(jnp.int32, (NSA_HEADS, 1), 0) // NSA_REP
    s = jnp.zeros((NSA_HEADS, n_sub), F32)
    for g in range(NSA_KV_HEADS):
        s = jnp.where(row_g == g, _dot_nt(q, kc_ref[0, :, g * NSA_HD:(g + 1) * NSA_HD]), s)
    tb = tb_ref[...]
    bias = jnp.concatenate([jnp.take_along_axis(
        tb, jnp.broadcast_to(idx_ref[:, c * LANES:(c + 1) * LANES], (NSA_HEADS, LANES)), axis=1)
        for c in range(n_sub // LANES)], axis=1)
    s = s + bias
    m = jnp.maximum(jnp.max(s, axis=1, keepdims=True), M_INIT)
    e = jnp.exp(s - m)
    p = e / jnp.maximum(jnp.sum(e, axis=1, keepdims=True), 1e-30)
    pb = p.astype(BF16)
    o = jnp.zeros((NSA_HEADS, NSA_HD), F32)
    for g in range(NSA_KV_HEADS):
        o = jnp.where(row_g == g, jnp.dot(pb, vc_ref[0, :, g * NSA_HD:(g + 1) * NSA_HD],
                                          preferred_element_type=F32), o)
    oc_ref[0] = o
    imp = _dot_f32(gsum_ref[...], p)
    score = _dot_f32(imp, mmat_ref[...])
    blk = lax.broadcasted_iota(jnp.int32, score.shape, 1)
    cur = q_pos // SEL_BLOCK
    score = jnp.where((blk == 0) | (blk == cur) | (blk == cur - 1), FORCE_SCORE, score)
    score = jnp.where(blk * SEL_BLOCK > q_pos, NEG, score)
    score = jnp.where(blk >= n_sel, -jnp.inf, score)
    blkf = blk.astype(F32)
    lane = lax.broadcasted_iota(jnp.int32, (SUBLANES, LANES), 1)
    top = jnp.zeros((SUBLANES, LANES), F32)
    for it in range(min(SEL_TOPK, n_sel)):
        mx = jnp.max(score, axis=1, keepdims=True)
        first = jnp.min(jnp.where(score == mx, blkf, float(SCORE_W)), axis=1, keepdims=True)
        top = jnp.where(lane == it, first, top)
        score = jnp.where(blkf == first, -jnp.inf, score)
    top_ref[0] = top.astype(jnp.int32)


def nsa_sample_cmp(q, k_c, v_c, rel_bias, past_len):
    batch = q.shape[0]
    n_sub = k_c.shape[1]
    n_sel = past_len // SEL_BLOCK + 1
    assert SEL_TOPK <= n_sel <= SCORE_W
    dist = past_len - (jnp.arange(n_sub, dtype=jnp.int32) * CMP_STRIDE + CMP_LEN - 1)
    idx = _bucket_index(dist, dist >= 0).reshape(1, n_sub)
    ci = np.arange(n_sub)[:, None]
    bj = np.arange(SCORE_W)[None, :]
    mmat = ((ci // 4 == bj).astype(np.float32) + ((ci + 1) // 4 == bj).astype(np.float32)) * (ci < n_sub - 1)
    gsum = (np.arange(SUBLANES)[:, None] == np.arange(NSA_HEADS)[None, :] // NSA_REP).astype(np.float32)

    def per_b(shape):
        return pl.BlockSpec((1,) + shape, lambda b: (b,) + (0,) * len(shape))

    def const(shape):
        return pl.BlockSpec(shape, lambda b: (0,) * len(shape))

    o_c, top = pl.pallas_call(
        functools.partial(_nsa_sample_cmp_body, n_sel=n_sel, q_pos=past_len),
        out_shape=(jax.ShapeDtypeStruct((batch, NSA_HEADS, NSA_HD), F32),
                   jax.ShapeDtypeStruct((batch, SUBLANES, LANES), jnp.int32)),
        grid=(batch,),
        in_specs=[per_b((NSA_HEADS, NSA_HD)), per_b((n_sub, KV_W // 2)), per_b((n_sub, KV_W // 2)),
                  const((1, n_sub)), const((n_sub, SCORE_W)), const((SUBLANES, NSA_HEADS)), const((NSA_HEADS, LANES))],
        out_specs=(per_b((NSA_HEADS, NSA_HD)), per_b((SUBLANES, LANES))),
        compiler_params=_params(1), name="nsa_sample_cmp",
    )(q, k_c.astype(BF16), v_c.astype(BF16), idx, jnp.asarray(mmat), jnp.asarray(gsum), _bias_table(rel_bias))
    return o_c, top[:, :NSA_KV_HEADS, :SEL_TOPK]


SEL_PER_STEP = 2


def _nsa_sample_attend_body(row_ref, sidx_ref, *refs, q_pos, n_past_blk, n_steps):
    n_blk_refs = NSA_KV_HEADS * SEL_PER_STEP
    blk_refs = refs[:n_blk_refs]
    (q_ref, knew_ref, win_ref, wnew_ref, gate_ref, oc_ref, idxw_ref, tb_ref, o_ref,
     s_s, v_s, ow_s) = refs[n_blk_refs:]
    b = pl.program_id(0)
    step = pl.program_id(1)
    q = q_ref[0]
    row_g = lax.broadcasted_iota(jnp.int32, (NSA_HEADS, 1), 0) // NSA_REP
    tb = tb_ref[...]
    half = KV_W // 2

    def by_group(fn):
        out = fn(0)
        for g in range(1, NSA_KV_HEADS):
            out = jnp.where(row_g == g, fn(g), out)
        return out

    @pl.when(step == 0)
    def _():
        s_w = by_group(lambda g: _dot(q, win_ref[0, 0, g]))
        bias = jnp.concatenate([jnp.take_along_axis(
            tb, jnp.broadcast_to(idxw_ref[:, c * LANES:(c + 1) * LANES], (NSA_HEADS, LANES)), axis=1)
            for c in range(s_w.shape[1] // LANES)], axis=1)
        s_w = s_w + bias
        k_new = by_group(lambda g: jnp.broadcast_to(wnew_ref[0, :, g * NSA_HD:(g + 1) * NSA_HD], (NSA_HEADS, NSA_HD)))
        v_new = by_group(lambda g: jnp.broadcast_to(wnew_ref[0, :, half + g * NSA_HD:half + (g + 1) * NSA_HD],
                                                    (NSA_HEADS, NSA_HD)))
        s_n = jnp.sum(q.astype(F32) * _r16(k_new), axis=1, keepdims=True) + tb[:, 0:1]
        m = jnp.maximum(jnp.max(s_w, axis=1, keepdims=True), s_n)
        e_w = jnp.exp(s_w - m)
        e_n = jnp.exp(s_n - m)
        total = jnp.sum(e_w, axis=1, keepdims=True) + e_n
        pb = (e_w / total).astype(BF16)
        pv = by_group(lambda g: _dot_nt(pb, win_ref[0, 1, g]))
        ow_s[...] = pv + _r16(e_n / total) * _r16(v_new)

    keys = SEL_PER_STEP * PAGE_ROWS
    key_lane = lax.broadcasted_iota(jnp.int32, (NSA_HD, PAGE_ROWS), 1)
    lane = lax.broadcasted_iota(jnp.int32, (1, PAGE_ROWS), 1)
    thresholds = _t5_thresholds()

    def tile_kv(g, kv):
        parts = []
        for j in range(SEL_PER_STEP):
            is_new = sidx_ref[b, g, step * SEL_PER_STEP + j] >= n_past_blk
            cached = blk_refs[g * SEL_PER_STEP + j][0, kv, 0]
            fresh = jnp.where(key_lane == 0, knew_ref[0, kv * half + g * NSA_HD:kv * half + (g + 1) * NSA_HD, :], 0.0)
            parts.append(jnp.where(is_new, fresh, cached))
        return jnp.concatenate(parts, axis=1).astype(BF16)

    def tile_bias(g):
        parts = []
        for j in range(SEL_PER_STEP):
            blk = sidx_ref[b, g, step * SEL_PER_STEP + j]
            is_new = blk >= n_past_blk
            base = jnp.where(is_new, blk * SEL_BLOCK, (blk * SEL_BLOCK) // PAGE_ROWS * PAGE_ROWS)
            pos = base + lane
            dist = q_pos - pos
            idx = jnp.zeros((1, PAGE_ROWS), jnp.int32)
            for thr in thresholds:
                idx = idx + (dist >= thr).astype(jnp.int32)
            parts.append(jnp.where((dist >= 0) & (pos // SEL_BLOCK == blk), idx, MASK_BUCKET))
        idx = jnp.concatenate(parts, axis=1)
        return jnp.concatenate([jnp.take_along_axis(
            tb, jnp.broadcast_to(idx[:, c * LANES:(c + 1) * LANES], (NSA_HEADS, LANES)), axis=1)
            for c in range(keys // LANES)], axis=1)

    k0 = pl.multiple_of(step * keys, keys)
    s_s[:, pl.ds(k0, keys)] = by_group(lambda g: jnp.dot(q, tile_kv(g, 0), preferred_element_type=F32)
                                       + tile_bias(g))
    for g in range(NSA_KV_HEADS):
        v_s[g, :, pl.ds(k0, keys)] = tile_kv(g, 1)

    @pl.when(step == n_steps - 1)
    def _():
        s = s_s[...]
        e = jnp.exp(s - jnp.max(s, axis=1, keepdims=True))
        pb = (e / jnp.sum(e, axis=1, keepdims=True)).astype(BF16)
        o_sel = by_group(lambda g: _dot_nt(pb, v_s[g]))
        gate = gate_ref[0]
        o_ref[0] = gate[:, 0:1] * oc_ref[0] + gate[:, 1:2] * o_sel + gate[:, 2:3] * ow_s[...]


def nsa_sample_attend(q, kv_sel_new, kv_win_new, gates, o_c, top, cache_sel, win_buf, page_table, rel_bias,
                      past_len):
    batch = q.shape[0]
    n_past_blk = past_len // SEL_BLOCK
    bpp = PAGE_ROWS // SEL_BLOCK
    n_steps = SEL_TOPK // SEL_PER_STEP
    page_shape = cache_sel.shape[1:]
    jp = jnp.minimum(top, n_past_blk - 1)
    phys = jnp.take_along_axis(page_table, (jp // bpp).reshape(batch, -1), axis=1).reshape(top.shape)
    phys = phys.astype(jnp.int32)
    wb = win_buf.shape[-1]
    dist_w = past_len - (past_len - wb + jnp.arange(wb, dtype=jnp.int32))
    idx_w = _bucket_index(dist_w, (dist_w >= 0) & (dist_w < WINDOW)).reshape(1, wb)
    gate3 = gates[:, :3 * NSA_HEADS].reshape(batch, NSA_HEADS, 3)

    def blk_spec(g, j):
        return pl.BlockSpec((1, page_shape[0], 1) + page_shape[2:],
                            lambda b, s, rows, sidx, g=g, j=j: (rows[b, g, s * SEL_PER_STEP + j], 0, g, 0, 0))

    def per_b(shape):
        return pl.BlockSpec((1,) + shape, lambda b, s, rows, sidx: (b,) + (0,) * len(shape))

    def const(shape):
        return pl.BlockSpec(shape, lambda b, s, rows, sidx: (0,) * len(shape))

    grid_spec = pltpu.PrefetchScalarGridSpec(
        num_scalar_prefetch=2, grid=(batch, n_steps),
        in_specs=[blk_spec(g, j) for g in range(NSA_KV_HEADS) for j in range(SEL_PER_STEP)] + [
            per_b((NSA_HEADS, NSA_HD)), per_b((KV_W, 1)), per_b(win_buf.shape[1:]), per_b((1, KV_W)),
            per_b((NSA_HEADS, 3)), per_b((NSA_HEADS, NSA_HD)), const((1, wb)), const((NSA_HEADS, LANES))],
        out_specs=per_b((NSA_HEADS, NSA_HD)),
        scratch_shapes=[pltpu.VMEM((NSA_HEADS, SEL_TOPK * PAGE_ROWS), F32),
                        pltpu.VMEM((NSA_KV_HEADS, NSA_HD, SEL_TOPK * PAGE_ROWS), BF16),
                        pltpu.VMEM((NSA_HEADS, NSA_HD), F32)])
    o = pl.pallas_call(
        functools.partial(_nsa_sample_attend_body, q_pos=past_len, n_past_blk=n_past_blk, n_steps=n_steps),
        out_shape=jax.ShapeDtypeStruct((batch, NSA_HEADS, NSA_HD), F32), grid_spec=grid_spec,
        compiler_params=_params(2), name="nsa_sample_attend",
    )(phys, top.astype(jnp.int32), *([cache_sel] * (NSA_KV_HEADS * SEL_PER_STEP)), q,
      kv_sel_new.reshape(batch, KV_W, 1), win_buf, kv_win_new.reshape(batch, 1, KV_W), gate3, o_c, idx_w,
      _bias_table(rel_bias))
    return o.reshape(batch, Q_W)


def _pad_cols(w, width):
    return jnp.pad(w, ((0, 0), (0, width - w.shape[1]))).astype(BF16)


def kernel(x_prompt, x_sample, state_ret, state_mlstm_C, state_mlstm_n, state_mlstm_m, state_conv, cache_nsa_cmp, cache_nsa_sel, state_nsa_win, page_table, rel_bias, norm_mix, norm_ffn, ab_w_in, ab_conv_w, ab_conv_b, ab_b_igate, ab_b_fgate, ab_gn_g, ab_gn_b, ab_hn_g, ab_w_out, nsa_w_in, nsa_q_norm, nsa_k_norm, nsa_cmp_w1, nsa_cmp_b1, nsa_cmp_w2, nsa_cmp_b2, nsa_w_out, moe_w_group, moe_b_group, moe_w_expert, moe_b_expert, moe_w_gate, moe_w_up, moe_w_down):
    bp, lp, d = x_prompt.shape
    bs, ls, _ = x_sample.shape
    page_size = cache_nsa_cmp.shape[2]
    past_len = page_table.shape[1] * page_size
    assert norm_mix.shape[0] == 2 and ls == 1 and d == D_MODEL and lp % KEY_TILE == 0
    xp = x_prompt.reshape(bp * lp, d)
    xs = x_sample.reshape(bs, d)

    def moe(x, layer):
        return hier_moe_residual(x, norm_ffn[layer], moe_w_group[layer], moe_b_group[layer], moe_w_expert[layer],
                                 moe_b_expert[layer], layer, moe_w_gate, moe_w_up, moe_w_down)

    w_in = _pad_cols(ab_w_in[0], AB_IN_PAD)
    ab = (ab_conv_w[0], ab_conv_b[0], ab_b_igate[0], ab_b_fgate[0], ab_gn_g[0], ab_gn_b[0], ab_hn_g[0])
    zp = norm_matmul(xp, norm_mix[0], w_in)
    xp, ret_p, mc_p, mn_p, mm_p, conv_p = ab_prompt(zp, xp, bp, lp, *ab, ab_w_out[0])
    zs = norm_matmul(xs, norm_mix[0], w_in)
    pos_s = past_len + jnp.arange(ls, dtype=jnp.int32)
    ys, ret_s, mc_s, mn_s, mm_s, conv_s = ab_sample(zs, pos_s, state_ret[0], state_mlstm_C[0], state_mlstm_n[0],
                                                    state_mlstm_m[0], state_conv[0], *ab)
    xs = matmul_residual(ys, ab_w_out[0].astype(BF16), xs)
    xp = moe(xp, 0)
    xs = moe(xs, 0)

    w_in = _pad_cols(nsa_w_in[0], NSA_IN_PAD)
    cmp_w = (nsa_cmp_w1[0], nsa_cmp_b1[0], nsa_cmp_w2[0], nsa_cmp_b2[0], nsa_k_norm[0, 0])
    kv_shape = (2, NSA_KV_HEADS, NSA_HD)
    zp = norm_matmul(xp, norm_mix[1], w_in)
    q, cmp_t, sel_t, win_t, selb, winb, gates, kst = nsa_prep(zp, nsa_q_norm[0], nsa_k_norm[0], batch=bp)
    k_c, v_c = nsa_compress(cmp_t.reshape((bp,) + kv_shape + (lp,)), None, *cmp_w)
    xp = nsa_attend_prompt(q, gates, xp, k_c, v_c, kst, selb, winb, rel_bias, nsa_w_out[0], bp, lp)
    win_keep = min(WINDOW, lp)

    def rows_major(t):
        return jnp.transpose(t.reshape((1, bp) + kv_shape + (t.shape[-1],)), (0, 1, 5, 2, 3, 4))

    cmp_p = rows_major(cmp_t)
    sel_p = rows_major(sel_t)
    win_p = rows_major(win_t[:, :, lp - win_keep:])

    def rows_minor(c):
        return jnp.moveaxis(c, -4, -1)

    zs = norm_matmul(xs, norm_mix[1], w_in)
    q, cmp_s, sel_s, win_s, _, _, gates = nsa_prep(zs, nsa_q_norm[0], nsa_k_norm[0])
    k_c, v_c = nsa_compress(rows_minor(cache_nsa_cmp[0]), page_table, *cmp_w)
    q3 = q.reshape(bs, NSA_HEADS, NSA_HD)
    o_c, top = nsa_sample_cmp(q3, k_c, v_c, rel_bias, past_len)
    win_buf = state_nsa_win[0]
    o = nsa_sample_attend(q3, sel_s, win_s, gates, o_c, top, rows_minor(cache_nsa_sel[0]), rows_minor(win_buf),
                          page_table, rel_bias, past_len)
    xs = matmul_residual(o, nsa_w_out[0].astype(BF16), xs)
    win_s = jnp.concatenate([win_buf, win_s.reshape((bs, ls) + kv_shape)], axis=1)[None, :, ls:]
    cmp_s = cmp_s.reshape((1, bs, ls) + kv_shape)
    sel_s = sel_s.reshape((1, bs, ls) + kv_shape)
    xp = moe(xp, 1)
    xs = moe(xs, 1)

    return (xp.reshape(bp, lp, d), xs.reshape(bs, ls, d), ret_p[None], ret_s[None], mc_p[None], mc_s[None],
            mn_p[None], mn_s[None], mm_p[None], mm_s[None], conv_p[None], conv_s[None],
            cmp_p, cmp_s, sel_p, sel_s, win_p, win_s)
```

```python
import functools
import math

import jax
import jax.numpy as jnp
import numpy as np
from jax import lax
from jax.experimental import pallas as pl
from jax.experimental.pallas import tpu as pltpu

F32 = jnp.float32
BF16 = jnp.bfloat16
LANES = 128
SUBLANES = 8
VMEM_LIMIT = 56 * 1024 * 1024

D_MODEL = 1024
RET_HEADS = 4
ML_HEADS = 4
HEAD_D = 128
CONV_W = 4
CHUNK = 128
ROPE_BASE = 10000.0
AB_IN = 4104
AB_IN_PAD = 4224
NSA_HEADS = 16
NSA_KV_HEADS = 4
NSA_REP = 4
NSA_HD = 64
NSA_IN_PAD = 2688
CMP_LEN = 32
CMP_STRIDE = 16
CMP_HID = 128
SEL_BLOCK = 64
SEL_TOPK = 16
WINDOW = 512
REL_BUCKETS = 32
REL_MAX_DIST = 1024
FORCE_SCORE = 1e4
MOE_GROUPS = 4
MOE_EXP_PER_GROUP = 8
MOE_EXPERTS = 32
NEG = -1e30
EPS = 1e-6


def _params(n_grid):
    return pltpu.CompilerParams(dimension_semantics=("arbitrary",) * n_grid, vmem_limit_bytes=VMEM_LIMIT)


def _dot(a, b):
    return jnp.dot(a.astype(BF16), b.astype(BF16), preferred_element_type=F32)


def _dot_nt(a, b):
    return lax.dot_general(a.astype(BF16), b.astype(BF16), (((1,), (1,)), ((), ())), preferred_element_type=F32)


def _dot_tn(a, b):
    return lax.dot_general(a.astype(BF16), b.astype(BF16), (((0,), (0,)), ((), ())), preferred_element_type=F32)


def _dot_f32(a, b):
    return jnp.dot(a, b, preferred_element_type=F32, precision=lax.Precision.HIGHEST)


def _r16(x):
    return x.astype(BF16).astype(F32)


def _sigmoid(x):
    return 1.0 / (1.0 + jnp.exp(-x))


def _silu(x):
    return x * _sigmoid(x)


def _log_sigmoid(x):
    return -(jnp.maximum(-x, 0.0) + jnp.log1p(jnp.exp(-jnp.abs(x))))


def _norm_matmul_body(x_ref, g_ref, w_ref, o_ref, *, col_tile):
    x = x_ref[...]
    y = x * lax.rsqrt(jnp.mean(x * x, axis=-1, keepdims=True) + EPS) * g_ref[...]
    yb = y.astype(BF16)
    for c0 in range(0, o_ref.shape[1], col_tile):
        o_ref[:, c0:c0 + col_tile] = jnp.dot(yb, w_ref[:, c0:c0 + col_tile], preferred_element_type=F32)


def norm_matmul(x, g, w, row_tile=256):
    n, d = x.shape
    c = w.shape[1]
    tm = min(row_tile, n)
    col_tile = 384 if c % 384 == 0 else LANES
    return pl.pallas_call(
        functools.partial(_norm_matmul_body, col_tile=col_tile),
        out_shape=jax.ShapeDtypeStruct((n, c), F32),
        grid=(n // tm,),
        in_specs=[pl.BlockSpec((tm, d), lambda i: (i, 0)),
                  pl.BlockSpec((1, d), lambda i: (0, 0)),
                  pl.BlockSpec((d, c), lambda i: (0, 0))],
        out_specs=pl.BlockSpec((tm, c), lambda i: (i, 0)),
        compiler_params=_params(1),
        name="norm_matmul",
    )(x, g.reshape(1, d), w)


def _retention_constants(c):
    h = np.arange(RET_HEADS, dtype=np.float64)
    log_g = np.log1p(-np.exp2(-5.0 - h))
    i = np.arange(c, dtype=np.float64)
    diff = i[:, None] - i[None, :]
    decay = np.where(diff >= 0, np.exp(np.maximum(diff, 0.0)[None] * log_g[:, None, None]), 0.0)
    q_dec = np.exp((i + 1.0)[None, :] * log_g[:, None])[:, :, None]
    k_dec = np.exp((c - 1.0 - i)[None, :] * log_g[:, None])[:, :, None]
    s_dec = np.exp(c * log_g)
    return (jnp.asarray(decay, F32), jnp.asarray(q_dec, F32), jnp.asarray(k_dec, F32),
            [float(v) for v in s_dec])


def _rope_tables(pos):
    half = HEAD_D // 2
    freqs = ROPE_BASE ** (-jnp.arange(half, dtype=F32) / half)
    ang = pos.astype(F32)[:, None] * freqs[None, :]
    cos, sin = jnp.cos(ang), jnp.sin(ang)
    return jnp.concatenate([cos, cos], axis=-1), jnp.concatenate([-sin, sin], axis=-1)


def _rope(x, cosf, sinf):
    return x * cosf + pltpu.roll(x, HEAD_D // 2, 1) * sinf


def _ab_prompt_body(rq_ref, rk_ref, rv_ref, rg_ref, mqk_ref, mv_ref, mo_ref, gz_ref, x_ref, cos_ref, sin_ref,
                    decay_ref, qdec_ref, kdec_ref, convw_ref, convb_ref, gbias_ref, gng_ref, gnb_ref, hng_ref,
                    wout_ref,
                    y_ref, s_ref, c_ref, n_ref, m_ref, conv_ref,
                    cbuf_ref, ycat_ref, *, s_dec):
    c = pl.program_id(1)
    tail = CONV_W - 1

    @pl.when(c == 0)
    def _():
        s_ref[...] = jnp.zeros_like(s_ref)
        c_ref[...] = jnp.zeros_like(c_ref)
        n_ref[...] = jnp.zeros_like(n_ref)
        m_ref[...] = jnp.zeros_like(m_ref)
        cbuf_ref[0:SUBLANES, :] = jnp.zeros((SUBLANES, cbuf_ref.shape[1]), F32)

    cosf = cos_ref[...]
    sinf = sin_ref[...]
    row = lax.broadcasted_iota(jnp.int32, (CHUNK, CHUNK), 0)
    col = lax.broadcasted_iota(jnp.int32, (CHUNK, CHUNK), 1)
    eye = row == col
    tril = row >= col
    triu = row <= col

    cbuf_ref[SUBLANES:SUBLANES + CHUNK, :] = mqk_ref[...]
    conv = convb_ref[...]
    for w in range(CONV_W):
        conv = conv + (_r16(cbuf_ref[SUBLANES - tail + w:SUBLANES - tail + w + CHUNK, :])
                       * _r16(convw_ref[w:w + 1, :]))
    qk = _silu(conv)
    last = cbuf_ref[CHUNK + SUBLANES - tail:CHUNK + SUBLANES, :]
    cbuf_ref[SUBLANES - tail:SUBLANES, :] = last
    conv_ref[0] = last

    gz = gz_ref[...] + gbias_ref[...]
    for h in range(RET_HEADS):
        sl = slice(h * HEAD_D, (h + 1) * HEAD_D)
        q = _rope(rq_ref[:, sl], cosf, sinf)
        k = _rope(rk_ref[:, sl], cosf, sinf) * (HEAD_D ** -0.5)
        v = rv_ref[:, sl]
        a = _dot_nt(q, k) * decay_ref[h]
        s_old = s_ref[0, h]
        o = _dot(a, v) + qdec_ref[h] * _dot(q, s_old)
        s_ref[0, h] = s_dec[h] * s_old + _dot_tn(k * kdec_ref[h], v)
        mu = jnp.mean(o, axis=-1, keepdims=True)
        var = jnp.mean(jnp.square(o - mu), axis=-1, keepdims=True)
        o = (o - mu) * lax.rsqrt(var + EPS) * gng_ref[:, sl] + gnb_ref[:, sl]
        ycat_ref[:, sl] = _silu(rg_ref[:, sl]) * o

        mq = qk[:, sl]
        mk = qk[:, ML_HEADS * HEAD_D + h * HEAD_D:ML_HEADS * HEAD_D + (h + 1) * HEAD_D] * (HEAD_D ** -0.5)
        mv = mv_ref[:, sl]
        i_col = gz[:, h:h + 1]
        f_col = _log_sigmoid(gz[:, ML_HEADS + h:ML_HEADS + h + 1])
        i_row = jnp.sum(jnp.where(eye, i_col, 0.0), axis=0, keepdims=True)
        f_row = jnp.sum(jnp.where(eye, f_col, 0.0), axis=0, keepdims=True)
        b_col = jnp.sum(jnp.where(tril, f_row, 0.0), axis=1, keepdims=True)
        b_row = jnp.sum(jnp.where(triu, f_col, 0.0), axis=0, keepdims=True)
        m_old = m_ref[0, h:h + 1, 0:1]
        dlog = jnp.where(tril, b_col - b_row + i_row, -jnp.inf)
        inter = b_col + m_old
        m_t = jnp.maximum(inter, jnp.max(dlog, axis=1, keepdims=True))
        wgt = _dot_nt(mq, mk) * jnp.exp(dlog - m_t)
        e_inter = jnp.exp(inter - m_t)
        c_old = c_ref[0, h]
        n_old = n_ref[0, h:h + 1, :]
        num = _dot(wgt, mv) + e_inter * _dot_nt(mq, c_old)
        den = (jnp.sum(wgt, axis=1, keepdims=True)
               + e_inter * jnp.sum(_r16(mq) * _r16(n_old), axis=1, keepdims=True))
        hc = num / jnp.maximum(jnp.abs(den), jnp.exp(-m_t))
        b_last = b_col[CHUNK - 1:CHUNK, :]
        u_row = b_last - b_row + i_row
        u_col = b_last - b_col + i_col
        m_new = jnp.maximum(b_last + m_old, jnp.max(u_row, axis=1, keepdims=True))
        ws_col = jnp.exp(u_col - m_new)
        f_state = jnp.exp(b_last + m_old - m_new)
        c_ref[0, h] = f_state * c_old + _dot_tn(mv * ws_col, mk)
        n_ref[0, h:h + 1, :] = f_state * n_old + jnp.sum(_r16(ws_col) * _r16(mk), axis=0, keepdims=True)
        m_ref[0, h:h + 1, :] = jnp.broadcast_to(m_new, (1, LANES))
        hm = _sigmoid(mo_ref[:, sl]) * hc
        hm = hm * lax.rsqrt(jnp.mean(hm * hm, axis=-1, keepdims=True) + EPS) * hng_ref[:, sl]
        ycat_ref[:, RET_HEADS * HEAD_D + h * HEAD_D:RET_HEADS * HEAD_D + (h + 1) * HEAD_D] = hm

    y_ref[...] = x_ref[...] + jnp.dot(ycat_ref[...].astype(BF16), wout_ref[...], preferred_element_type=F32)


def ab_prompt(z, x, batch, seq, conv_w, conv_b, b_ig, b_fg, gn_g, gn_b, hn_g, w_out):
    n_chunk = seq // CHUNK
    decay, q_dec, k_dec, s_dec = _retention_constants(CHUNK)
    cosf, sinf = _rope_tables(jnp.arange(seq, dtype=jnp.int32))
    gbias = jnp.zeros((1, LANES), F32).at[0, :ML_HEADS].set(b_ig).at[0, ML_HEADS:2 * ML_HEADS].set(b_fg)
    hw = RET_HEADS * HEAD_D
    qkw = 2 * ML_HEADS * HEAD_D

    def zspec(width, blk):
        return pl.BlockSpec((CHUNK, width), lambda b, c, blk=blk: (b * n_chunk + c, blk))

    def const(shape):
        return pl.BlockSpec(shape, lambda b, c: (0,) * len(shape))

    in_specs = [zspec(hw, 0), zspec(hw, 1), zspec(hw, 2), zspec(hw, 3), zspec(qkw, 2), zspec(hw, 6), zspec(hw, 7),
                zspec(LANES, (AB_IN_PAD - LANES) // LANES),
                pl.BlockSpec((CHUNK, D_MODEL), lambda b, c: (b * n_chunk + c, 0)),
                pl.BlockSpec((CHUNK, HEAD_D), lambda b, c: (c, 0)),
                pl.BlockSpec((CHUNK, HEAD_D), lambda b, c: (c, 0)),
                const((RET_HEADS, CHUNK, CHUNK)), const((RET_HEADS, CHUNK, 1)), const((RET_HEADS, CHUNK, 1)),
                const((CONV_W, qkw)), const((1, qkw)), const((1, LANES)),
                const((1, hw)), const((1, hw)), const((1, hw)), const((2 * hw, D_MODEL))]
    out_shape = (jax.ShapeDtypeStruct((batch * seq, D_MODEL), F32),
                 jax.ShapeDtypeStruct((batch, RET_HEADS, HEAD_D, HEAD_D), F32),
                 jax.ShapeDtypeStruct((batch, ML_HEADS, HEAD_D, HEAD_D), F32),
                 jax.ShapeDtypeStruct((batch, ML_HEADS, HEAD_D), F32),
                 jax.ShapeDtypeStruct((batch, SUBLANES, LANES), F32),
                 jax.ShapeDtypeStruct((batch, CONV_W - 1, qkw), F32))
    out_specs = (pl.BlockSpec((CHUNK, D_MODEL), lambda b, c: (b * n_chunk + c, 0)),
                 pl.BlockSpec((1, RET_HEADS, HEAD_D, HEAD_D), lambda b, c: (b, 0, 0, 0)),
                 pl.BlockSpec((1, ML_HEADS, HEAD_D, HEAD_D), lambda b, c: (b, 0, 0, 0)),
                 pl.BlockSpec((1, ML_HEADS, HEAD_D), lambda b, c: (b, 0, 0)),
                 pl.BlockSpec((1, SUBLANES, LANES), lambda b, c: (b, 0, 0)),
                 pl.BlockSpec((1, CONV_W - 1, qkw), lambda b, c: (b, 0, 0)))
    y, s, cc, n, m, conv = pl.pallas_call(
        functools.partial(_ab_prompt_body, s_dec=s_dec),
        out_shape=out_shape, grid=(batch, n_chunk), in_specs=in_specs, out_specs=out_specs,
        scratch_shapes=[pltpu.VMEM((CHUNK + SUBLANES, qkw), F32), pltpu.VMEM((CHUNK, 2 * hw), F32)],
        compiler_params=_params(2), name="ab_prompt",
    )(z, z, z, z, z, z, z, z, x, cosf, sinf, decay, q_dec, k_dec, conv_w, conv_b.reshape(1, qkw), gbias,
      gn_g.reshape(1, hw), gn_b.reshape(1, hw), hn_g.reshape(1, hw), w_out.astype(BF16))
    return y, s, cc, n, m[:, :ML_HEADS, 0], conv


def _matmul_residual_body(a_ref, w_ref, x_ref, o_ref):
    o_ref[...] = x_ref[...] + jnp.dot(a_ref[...].astype(BF16), w_ref[...], preferred_element_type=F32)


def matmul_residual(a, w, x, row_tile=256):
    n, kk = a.shape
    d = w.shape[1]
    tm = min(row_tile, n)
    return pl.pallas_call(
        _matmul_residual_body, out_shape=jax.ShapeDtypeStruct((n, d), F32), grid=(n // tm,),
        in_specs=[pl.BlockSpec((tm, kk), lambda i: (i, 0)), pl.BlockSpec((kk, d), lambda i: (0, 0)),
                  pl.BlockSpec((tm, d), lambda i: (i, 0))],
        out_specs=pl.BlockSpec((tm, d), lambda i: (i, 0)),
        compiler_params=_params(1), name="matmul_residual",
    )(a, w, x)


def _ab_sample_body(m0_ref, z_ref, cos_ref, sin_ref, s0_ref, c0_ref, n0_ref, conv0_ref,
                    convw_ref, convb_ref, gbias_ref, gng_ref, gnb_ref, hng_ref,
                    y_ref, s_ref, c_ref, n_ref, m_ref, conv_ref, *, g_dec):
    b = pl.program_id(0)
    hw = RET_HEADS * HEAD_D
    qkw = 2 * ML_HEADS * HEAD_D
    tail = CONV_W - 1
    cosf = cos_ref[...]
    sinf = sin_ref[...]
    row = lax.broadcasted_iota(jnp.int32, (HEAD_D, HEAD_D), 0)
    col = lax.broadcasted_iota(jnp.int32, (HEAD_D, HEAD_D), 1)
    eye = row == col

    def to_col(r):
        return jnp.sum(jnp.where(eye, r, 0.0), axis=1, keepdims=True)

    def to_row(cv):
        return jnp.sum(jnp.where(eye, cv, 0.0), axis=0, keepdims=True)

    mqk = z_ref[0, :, 4 * hw:4 * hw + qkw]
    conv = convb_ref[...] + mqk * convw_ref[tail:CONV_W, :]
    for w in range(tail):
        conv = conv + conv0_ref[0, w:w + 1, :] * convw_ref[w:w + 1, :]
    qk = _silu(conv)
    conv_ref[0, 0:tail - 1, :] = conv0_ref[0, 1:tail, :]
    conv_ref[0, tail - 1:tail, :] = mqk
    gz = z_ref[0, :, AB_IN_PAD - LANES:AB_IN_PAD] + gbias_ref[...]

    for h in range(RET_HEADS):
        sl = slice(h * HEAD_D, (h + 1) * HEAD_D)
        q = _rope(z_ref[0, :, sl], cosf, sinf)
        k = _rope(z_ref[0, :, hw + h * HEAD_D:hw + (h + 1) * HEAD_D], cosf, sinf) * (HEAD_D ** -0.5)
        v = z_ref[0, :, 2 * hw + h * HEAD_D:2 * hw + (h + 1) * HEAD_D]
        rg = z_ref[0, :, 3 * hw + h * HEAD_D:3 * hw + (h + 1) * HEAD_D]
        s_old = s0_ref[0, h]
        qk_s = jnp.sum(q * k, axis=1, keepdims=True)
        o = qk_s * v + g_dec[h] * jnp.sum(_r16(to_col(q)) * _r16(s_old), axis=0, keepdims=True)
        s_ref[0, h] = g_dec[h] * s_old + to_col(k) * v
        mu = jnp.mean(o, axis=-1, keepdims=True)
        var = jnp.mean(jnp.square(o - mu), axis=-1, keepdims=True)
        o = (o - mu) * lax.rsqrt(var + EPS) * gng_ref[:, sl] + gnb_ref[:, sl]
        y_ref[0, :, sl] = _silu(rg) * o

        mq = qk[:, sl]
        mk = qk[:, ML_HEADS * HEAD_D + h * HEAD_D:ML_HEADS * HEAD_D + (h + 1) * HEAD_D] * (HEAD_D ** -0.5)
        mv = z_ref[0, :, 4 * hw + qkw + h * HEAD_D:4 * hw + qkw + (h + 1) * HEAD_D]
        mo = z_ref[0, :, 5 * hw + qkw + h * HEAD_D:5 * hw + qkw + (h + 1) * HEAD_D]
        ig = gz[:, h:h + 1]
        lf = _log_sigmoid(gz[:, ML_HEADS + h:ML_HEADS + h + 1])
        m_old = m0_ref[b, h]
        inter = lf + m_old
        m_t = jnp.maximum(inter, ig)
        wgt = jnp.sum(mq * mk, axis=1, keepdims=True) * jnp.exp(ig - m_t)
        e_inter = jnp.exp(inter - m_t)
        c_old = c0_ref[0, h]
        n_old = n0_ref[0, h:h + 1, :]
        cq = to_row(jnp.sum(_r16(c_old) * _r16(mq), axis=1, keepdims=True))
        num = wgt * mv + e_inter * cq
        den = wgt + e_inter * jnp.sum(n_old * mq, axis=1, keepdims=True)
        hc = num / jnp.maximum(jnp.abs(den), jnp.exp(-m_t))
        ws = jnp.exp(ig - m_t)
        c_ref[0, h] = e_inter * c_old + (ws * to_col(mv)) * mk
        n_ref[0, h:h + 1, :] = e_inter * n_old + ws * mk
        m_ref[0, h:h + 1, :] = jnp.broadcast_to(m_t, (1, LANES))
        hm = _sigmoid(mo) * hc
        hm = hm * lax.rsqrt(jnp.mean(hm * hm, axis=-1, keepdims=True) + EPS) * hng_ref[:, sl]
        y_ref[0, :, hw + h * HEAD_D:hw + (h + 1) * HEAD_D] = hm
    m_ref[0, ML_HEADS:SUBLANES, :] = jnp.zeros((SUBLANES - ML_HEADS, LANES), F32)


def ab_sample(z, pos, s0, c0, n0, m0, conv0, conv_w, conv_b, b_ig, b_fg, gn_g, gn_b, hn_g):
    batch = z.shape[0]
    h = np.arange(RET_HEADS, dtype=np.float64)
    g_dec = [float(v) for v in np.exp(np.log1p(-np.exp2(-5.0 - h)))]
    cosf, sinf = _rope_tables(pos)
    gbias = jnp.zeros((1, LANES), F32).at[0, :ML_HEADS].set(b_ig).at[0, ML_HEADS:2 * ML_HEADS].set(b_fg)
    hw = RET_HEADS * HEAD_D
    qkw = 2 * ML_HEADS * HEAD_D

    def per_b(shape):
        return pl.BlockSpec((1,) + shape, lambda b: (b,) + (0,) * len(shape))

    def const(shape):
        return pl.BlockSpec(shape, lambda b: (0,) * len(shape))

    in_specs = [pl.BlockSpec(memory_space=pltpu.SMEM), per_b((1, AB_IN_PAD)), const((1, HEAD_D)), const((1, HEAD_D)),
                per_b((RET_HEADS, HEAD_D, HEAD_D)), per_b((ML_HEADS, HEAD_D, HEAD_D)), per_b((ML_HEADS, HEAD_D)),
                per_b((CONV_W - 1, qkw)), const((CONV_W, qkw)), const((1, qkw)), const((1, LANES)),
                const((1, hw)), const((1, hw)), const((1, hw))]
    out_shape = (jax.ShapeDtypeStruct((batch, 1, 2 * hw), F32),
                 jax.ShapeDtypeStruct((batch, RET_HEADS, HEAD_D, HEAD_D), F32),
                 jax.ShapeDtypeStruct((batch, ML_HEADS, HEAD_D, HEAD_D), F32),
                 jax.ShapeDtypeStruct((batch, ML_HEADS, HEAD_D), F32),
                 jax.ShapeDtypeStruct((batch, SUBLANES, LANES), F32),
                 jax.ShapeDtypeStruct((batch, CONV_W - 1, qkw), F32))
    out_specs = (per_b((1, 2 * hw)), per_b((RET_HEADS, HEAD_D, HEAD_D)), per_b((ML_HEADS, HEAD_D, HEAD_D)),
                 per_b((ML_HEADS, HEAD_D)), per_b((SUBLANES, LANES)), per_b((CONV_W - 1, qkw)))
    y, s, cc, n, m, conv = pl.pallas_call(
        functools.partial(_ab_sample_body, g_dec=g_dec),
        out_shape=out_shape, grid=(batch,), in_specs=in_specs, out_specs=out_specs,
        compiler_params=_params(1), name="ab_sample",
    )(m0, z.reshape(batch, 1, AB_IN_PAD), cosf, sinf, s0, c0, n0, conv0, conv_w, conv_b.reshape(1, qkw), gbias,
      gn_g.reshape(1, hw), gn_b.reshape(1, hw), hn_g.reshape(1, hw))
    return y.reshape(batch, 2 * hw), s, cc, n, m[:, :ML_HEADS, 0], conv


MOE_TILE = 256


def _moe_router_body(x_ref, g_ref, wr_ref, br_ref, hn_ref, route_ref, count_ref):
    x = x_ref[...]
    hn = x * lax.rsqrt(jnp.mean(x * x, axis=-1, keepdims=True) + EPS) * g_ref[...]
    hn_ref[...] = hn
    z = _dot(hn, wr_ref[...]) + br_ref[...]
    lane = lax.broadcasted_iota(jnp.int32, z.shape, 1)
    lanef = lane.astype(F32)
    is_group = lane < MOE_GROUPS
    gl = jnp.where(is_group, z, -jnp.inf)
    gmax = jnp.max(gl, axis=1, keepdims=True)
    g_top = jnp.min(jnp.where(gl == gmax, lanef, float(LANES)), axis=1, keepdims=True)
    pg_top = 1.0 / jnp.sum(jnp.where(is_group, jnp.exp(z - gmax), 0.0), axis=1, keepdims=True)
    grp = ((lane - MOE_GROUPS) // MOE_EXP_PER_GROUP).astype(F32)
    in_group = (lane >= MOE_GROUPS) & (lane < MOE_GROUPS + MOE_EXPERTS) & (grp == g_top)
    el = jnp.where(in_group, z, -jnp.inf)
    v1 = jnp.max(el, axis=1, keepdims=True)
    i1 = jnp.min(jnp.where(el == v1, lanef, float(LANES)), axis=1, keepdims=True)
    el2 = jnp.where(lanef == i1, -jnp.inf, el)
    v2 = jnp.max(el2, axis=1, keepdims=True)
    i2 = jnp.min(jnp.where(el2 == v2, lanef, float(LANES)), axis=1, keepdims=True)
    t = jnp.exp(v2 - v1)
    p1 = 1.0 / (1.0 + t)
    out = jnp.where(lane == 0, i1 - MOE_GROUPS,
                    jnp.where(lane == 1, i2 - MOE_GROUPS,
                              jnp.where(lane == 2, pg_top * p1,
                                        jnp.where(lane == 3, pg_top * (t * p1), 0.0))))
    route_ref[...] = out
    picked = jnp.where((lanef == i1 - MOE_GROUPS) | (lanef == i2 - MOE_GROUPS), 1.0, 0.0)

    @pl.when(pl.program_id(0) == 0)
    def _():
        count_ref[...] = jnp.zeros_like(count_ref)

    count_ref[...] += jnp.sum(picked, axis=0, keepdims=True)


def moe_router(x, g, w_group, b_group, w_expert, b_expert, row_tile=256):
    n, d = x.shape
    tm = min(row_tile, n)
    used = MOE_GROUPS + MOE_EXPERTS
    wr = jnp.pad(jnp.concatenate([w_group, w_expert], axis=1), ((0, 0), (0, LANES - used)))
    br = jnp.pad(jnp.concatenate([b_group, b_expert]), (0, LANES - used)).reshape(1, LANES)
    hn, route, count = pl.pallas_call(
        _moe_router_body,
        out_shape=(jax.ShapeDtypeStruct((n, d), F32), jax.ShapeDtypeStruct((n, LANES), F32),
                   jax.ShapeDtypeStruct((1, LANES), F32)),
        grid=(n // tm,),
        in_specs=[pl.BlockSpec((tm, d), lambda i: (i, 0)), pl.BlockSpec((1, d), lambda i: (0, 0)),
                  pl.BlockSpec((d, LANES), lambda i: (0, 0)), pl.BlockSpec((1, LANES), lambda i: (0, 0))],
        out_specs=(pl.BlockSpec((tm, d), lambda i: (i, 0)), pl.BlockSpec((tm, LANES), lambda i: (i, 0)),
                   pl.BlockSpec((1, LANES), lambda i: (0, 0))),
        compiler_params=_params(1), name="moe_router",
    )(x, g.reshape(1, d), wr, br)
    return hn, route, route[:, 0:2].astype(jnp.int32), count[0, :MOE_EXPERTS].astype(jnp.int32)


def _moe_ffn_body(blk_e_ref, n_real_ref, asg_ref, hn_ref, wg_ref, wu_ref, wd_ref, o_ref,
                  x_s, y_s, wg_s, wu_s, wd_s, gsem, ssem, *, n_blk):
    i = pl.program_id(0)
    slot = i % 2

    def gather_row(tile, r, slot):
        tok = lax.shift_right_logical(asg_ref[tile * MOE_TILE + r], 1)
        return pltpu.make_async_copy(hn_ref.at[pl.ds(tok, 1), :], x_s.at[slot, pl.ds(r, 1), :], gsem.at[slot])

    def scatter_row(r):
        a = asg_ref[i * MOE_TILE + r]
        return pltpu.make_async_copy(y_s.at[pl.ds(r, 1), :],
                                     o_ref.at[a & 1, pl.ds(lax.shift_right_logical(a, 1), 1), :], ssem.at[0])

    def for_rows(n_rows, fn):
        @pl.when(n_rows == MOE_TILE)
        def _():
            for r in range(MOE_TILE):
                fn(r)

        @pl.when(n_rows < MOE_TILE)
        def _():
            def body(r, c):
                fn(r)
                return c
            lax.fori_loop(0, n_rows, body, 0)

    def start_gather(tile, slot):
        for_rows(n_real_ref[tile], lambda r: gather_row(tile, r, slot).start())

    @pl.when(i == 0)
    def _():
        x_s[...] = jnp.zeros_like(x_s)
        start_gather(0, 0)

    for s in range(2):
        @pl.when((i + 1 < n_blk) & (slot == 1 - s))
        def _(s=s):
            start_gather(i + 1, s)

    prev = blk_e_ref[jnp.maximum(i - 1, 0)]

    @pl.when((i == 0) | (blk_e_ref[i] != prev))
    def _():
        wg_s[...] = wg_ref[0].astype(BF16)
        wu_s[...] = wu_ref[0].astype(BF16)
        wd_s[...] = wd_ref[0].astype(BF16)

    n_real = n_real_ref[i]
    for_rows(n_real, lambda r: gather_row(i, 0, slot).wait())
    n_prev = jnp.where(i > 0, n_real_ref[jnp.maximum(i - 1, 0)], 0)
    for_rows(n_prev, lambda r: scatter_row(0).wait())

    @pl.when(n_real > 0)
    def _():
        x = x_s[slot].astype(BF16)
        hg = jnp.dot(x, wg_s[...], preferred_element_type=F32)
        hu = jnp.dot(x, wu_s[...], preferred_element_type=F32)
        hb = (_silu(hg) * hu).astype(BF16)
        y_s[...] = jnp.dot(hb, wd_s[...], preferred_element_type=F32)
        for_rows(n_real, lambda r: scatter_row(r).start())

    @pl.when(i == n_blk - 1)
    def _():
        for_rows(n_real, lambda r: scatter_row(0).wait())


def moe_ffn(hn, asg, n_real, blk_e, layer, w_g, w_u, w_d):
    n_tok, d = hn.shape
    ff = w_g.shape[3]
    n_blk = blk_e.shape[0]
    grid_spec = pltpu.PrefetchScalarGridSpec(
        num_scalar_prefetch=3, grid=(n_blk,),
        in_specs=[pl.BlockSpec(memory_space=pl.ANY),
                  pl.BlockSpec((None, 1, d, ff), lambda i, e, nr, a: (layer, e[i], 0, 0)),
                  pl.BlockSpec((None, 1, d, ff), lambda i, e, nr, a: (layer, e[i], 0, 0)),
                  pl.BlockSpec((None, 1, ff, d), lambda i, e, nr, a: (layer, e[i], 0, 0))],
        out_specs=pl.BlockSpec(memory_space=pl.ANY),
        scratch_shapes=[pltpu.VMEM((2, MOE_TILE, d), F32), pltpu.VMEM((MOE_TILE, d), F32),
                        pltpu.VMEM((d, ff), BF16), pltpu.VMEM((d, ff), BF16), pltpu.VMEM((ff, d), BF16),
                        pltpu.SemaphoreType.DMA((2,)), pltpu.SemaphoreType.DMA((1,))])
    return pl.pallas_call(
        functools.partial(_moe_ffn_body, n_blk=n_blk),
        out_shape=jax.ShapeDtypeStruct((2, n_tok, d), F32), grid_spec=grid_spec,
        compiler_params=_params(1), name="moe_ffn",
    )(blk_e, n_real, asg, hn, w_g, w_u, w_d)


def _moe_combine_body(x_ref, route_ref, y0_ref, y1_ref, o_ref):
    o_ref[...] = x_ref[...] + (y0_ref[...] * route_ref[:, 2:3] + y1_ref[...] * route_ref[:, 3:4])


def moe_combine(x, route, y, row_tile=512):
    n, d = x.shape
    tm = min(row_tile, n)
    return pl.pallas_call(
        _moe_combine_body, out_shape=jax.ShapeDtypeStruct((n, d), F32), grid=(n // tm,),
        in_specs=[pl.BlockSpec((tm, d), lambda i: (i, 0)), pl.BlockSpec((tm, LANES), lambda i: (i, 0)),
                  pl.BlockSpec((None, tm, d), lambda i: (0, i, 0)), pl.BlockSpec((None, tm, d), lambda i: (1, i, 0))],
        out_specs=pl.BlockSpec((tm, d), lambda i: (i, 0)),
        compiler_params=_params(1), name="moe_combine",
    )(x, route, y, y)


def hier_moe_residual(x, g, w_group, b_group, w_expert, b_expert, layer, w_g, w_u, w_d):
    n, d = x.shape
    hn, route, expert, counts = moe_router(x, g, w_group, b_group, w_expert, b_expert)
    n_exp = w_g.shape[1]
    kk = expert.shape[1]
    assert kk == 2
    a = n * kk
    order = jnp.argsort(expert.reshape(-1)).astype(jnp.int32)
    starts = jnp.cumsum(counts) - counts
    padded = (counts + MOE_TILE - 1) // MOE_TILE * MOE_TILE
    pend = jnp.cumsum(padded)
    pstart = pend - padded
    n_blk = -(-(a + n_exp * (MOE_TILE - 1)) // MOE_TILE)
    tile_start = jnp.arange(n_blk, dtype=jnp.int32) * MOE_TILE
    blk_e = jnp.minimum(jnp.sum((pend[None, :] <= tile_start[:, None]).astype(jnp.int32), axis=1), n_exp - 1)
    n_real = jnp.clip(counts[blk_e] - (tile_start - pstart[blk_e]), 0, MOE_TILE).astype(jnp.int32)
    row_off = (tile_start - pstart[blk_e] + starts[blk_e])[:, None] + jnp.arange(MOE_TILE, dtype=jnp.int32)[None, :]
    asg = order[jnp.clip(row_off, 0, a - 1)].reshape(-1)
    y = moe_ffn(hn, asg, n_real, blk_e, layer, w_g, w_u, w_d)
    return moe_combine(x, route, y)


KV_W = 2 * NSA_KV_HEADS * NSA_HD
Q_W = NSA_HEADS * NSA_HD
MASK_BUCKET = REL_BUCKETS
KEY_TILE = 1024


def _t5_thresholds():
    exact = REL_BUCKETS // 2
    dist = np.arange(0, 4 * REL_MAX_DIST, dtype=np.int64)
    nf = np.maximum(dist, 1).astype(np.float64)
    large = exact + np.floor(np.log(nf / exact) / math.log(REL_MAX_DIST / exact) * (REL_BUCKETS - exact) + 1e-9)
    bucket = np.where(dist < exact, dist, np.minimum(large, REL_BUCKETS - 1)).astype(np.int64)
    return [int(np.argmax(bucket >= b)) for b in range(1, REL_BUCKETS)]


def _bucket_index(dist, valid):
    idx = jnp.zeros(dist.shape, jnp.int32)
    for thr in _t5_thresholds():
        idx = idx + (dist >= thr).astype(jnp.int32)
    return jnp.where(valid, idx, MASK_BUCKET)


def _bias_table(rel_bias):
    t = jnp.zeros((NSA_HEADS, LANES), F32).at[:, :REL_BUCKETS].set(rel_bias.T.astype(F32))
    return t.at[:, MASK_BUCKET].set(NEG)


def _group_mean_matrix():
    i = np.arange(LANES)
    return jnp.asarray((i[:, None] // NSA_HD == i[None, :] // NSA_HD) / NSA_HD, F32)


def _nsa_prep_body(zq_ref, zc_ref, zs_ref, zw_ref, zg_ref, bd_ref, qn_ref, kns_ref, knw_ref,
                   q_ref, cmp_ref, sel_ref, win_ref, selb_ref, winb_ref, gate_ref, *kt_ref, transposed):
    bd = bd_ref[...]

    def head_norm(x, gain):
        ms = _dot_f32(x * x, bd)
        return x * lax.rsqrt(ms + EPS) * gain

    def emit(o_ref, rows):
        if transposed:
            o_ref[0] = rows.T
        else:
            o_ref[...] = rows

    for c in range(Q_W // LANES):
        sl = slice(c * LANES, (c + 1) * LANES)
        q_ref[:, sl] = (head_norm(zq_ref[:, sl], qn_ref[...]) * (NSA_HD ** -0.5)).astype(BF16)
    emit(cmp_ref, zc_ref[...])
    half = KV_W // 2
    for z_ref, kn_ref, o_ref, ob_ref in ((zs_ref, kns_ref, sel_ref, selb_ref), (zw_ref, knw_ref, win_ref, winb_ref)):
        kn = jnp.concatenate([head_norm(z_ref[:, c * LANES:(c + 1) * LANES], kn_ref[...])
                              for c in range(half // LANES)], axis=1)
        rows = jnp.concatenate([kn, z_ref[:, half:KV_W]], axis=1)
        emit(o_ref, rows)
        ob_ref[...] = rows.astype(BF16)
        if transposed and o_ref is sel_ref:
            kt_ref[0][0] = kn.T.astype(BF16)
    gate_ref[...] = _sigmoid(zg_ref[...])


def nsa_prep(z, q_norm, k_norm, batch=None, row_tile=256):
    n = z.shape[0]
    tm = min(row_tile, n)
    transposed = batch is not None

    def zspec(width, blk):
        return pl.BlockSpec((tm, width), lambda i, blk=blk: (i, blk))

    def const(shape):
        return pl.BlockSpec(shape, lambda i: (0,) * len(shape))

    def tile2(v):
        return jnp.concatenate([v, v]).reshape(1, LANES).astype(F32)

    def rows(w, dt):
        return jax.ShapeDtypeStruct((n, w), dt)

    def out_spec(w):
        return pl.BlockSpec((tm, w), lambda i: (i, 0))

    if transposed:
        seq = n // batch
        tps = seq // tm
        kv_shape = jax.ShapeDtypeStruct((batch, KV_W, seq), F32)
        kv_spec = pl.BlockSpec((1, KV_W, tm), lambda i: (i // tps, 0, i % tps))
        extra_shape = (jax.ShapeDtypeStruct((batch, KV_W // 2, seq), BF16),)
        extra_spec = (pl.BlockSpec((1, KV_W // 2, tm), lambda i: (i // tps, 0, i % tps)),)
    else:
        kv_shape, kv_spec, extra_shape, extra_spec = rows(KV_W, F32), out_spec(KV_W), (), ()
    return pl.pallas_call(
        functools.partial(_nsa_prep_body, transposed=transposed),
        out_shape=(rows(Q_W, BF16), kv_shape, kv_shape, kv_shape, rows(KV_W, BF16), rows(KV_W, BF16),
                   rows(LANES, F32)) + extra_shape,
        grid=(n // tm,),
        in_specs=[zspec(Q_W, 0), zspec(KV_W, 2), zspec(KV_W, 3), zspec(KV_W, 4),
                  zspec(LANES, (Q_W + 3 * KV_W) // LANES), const((LANES, LANES)),
                  const((1, LANES)), const((1, LANES)), const((1, LANES))],
        out_specs=(out_spec(Q_W), kv_spec, kv_spec, kv_spec, out_spec(KV_W), out_spec(KV_W),
                   out_spec(LANES)) + extra_spec,
        compiler_params=_params(1), name="nsa_prep",
    )(z, z, z, z, z, _group_mean_matrix(), tile2(q_norm), tile2(k_norm[1]), tile2(k_norm[2]))


PAGES_PER_STEP = 16
PAGE_ROWS = 128
SUBS_PER_PAGE = PAGE_ROWS // CMP_STRIDE
P_W = 2 * NSA_KV_HEADS * 2 * CMP_HID
KV_PAIRS = NSA_KV_HEADS // 2


def _gelu_tanh(x):
    return 0.5 * x * (1.0 + jnp.tanh(math.sqrt(2.0 / math.pi) * (x + 0.044715 * x * x * x)))


def _compress_body(pt_ref, *refs, n_steps):
    page_refs = refs[:PAGES_PER_STEP]
    w1_ref, b1_ref, w2_ref, b2_ref, kn_ref, kc_ref, vc_ref, p_ref, x_s = refs[PAGES_PER_STEP:]
    j = pl.program_id(1)
    rows = PAGES_PER_STEP * SUBS_PER_PAGE
    r0 = pl.multiple_of(j * rows, rows)
    for t, pr in enumerate(page_refs):
        for v in range(2):
            for gp in range(KV_PAIRS):
                x_s[t, v * KV_PAIRS + gp] = pr[0, v, 2 * gp:2 * gp + 2].reshape(2 * NSA_HD, PAGE_ROWS).T
    for v in range(2):
        for gp in range(KV_PAIRS):
            acc = jnp.zeros((rows, 4 * CMP_HID), F32)
            for s in range(CMP_STRIDE):
                x = jnp.concatenate([x_s[t, v * KV_PAIRS + gp, pl.ds(s, SUBS_PER_PAGE, stride=CMP_STRIDE), :]
                                     for t in range(PAGES_PER_STEP)], axis=0)
                acc = acc + jnp.dot(x.astype(BF16), w1_ref[v, s], preferred_element_type=F32)
            p0 = (v * NSA_KV_HEADS + 2 * gp) * 2 * CMP_HID
            p_ref[pl.ds(r0, rows), p0:p0 + 4 * CMP_HID] = acc

    @pl.when(j == n_steps - 1)
    def _():
        n_sub = p_ref.shape[0]
        for v in range(2):
            for g in range(NSA_KV_HEADS):
                p0 = (v * NSA_KV_HEADS + g) * 2 * CMP_HID
                hs = p_ref[:, p0:p0 + CMP_HID] + pltpu.roll(p_ref[:, p0 + CMP_HID:p0 + 2 * CMP_HID], n_sub - 1, 0)
                hid = _gelu_tanh(hs + b1_ref[v:v + 1, :])
                out = _dot(hid, w2_ref[v]) + b2_ref[v:v + 1, :]
                if v == 0:
                    out = out * lax.rsqrt(jnp.mean(out * out, axis=-1, keepdims=True) + EPS) * kn_ref[...]
                    kc_ref[0, :, g * NSA_HD:(g + 1) * NSA_HD] = out
                else:
                    vc_ref[0, :, g * NSA_HD:(g + 1) * NSA_HD] = out


def nsa_compress(rows_t, page_table, w1, b1, w2, b2, k_norm0):
    if page_table is None:
        batch, n_pp = rows_t.shape[0], rows_t.shape[-1] // PAGE_ROWS
        page_table = jnp.zeros((1, 1), jnp.int32)

        def page_index(b, p, pt):
            return (b, 0, 0, 0, p)
    else:
        batch, n_pp = page_table.shape

        def page_index(b, p, pt):
            return (pt[b, p], 0, 0, 0, 0)
    n_steps = n_pp // PAGES_PER_STEP
    n_sub = n_pp * SUBS_PER_PAGE
    w = w1.reshape(2, 2, CMP_STRIDE, NSA_HD, CMP_HID)
    w = jnp.transpose(w, (0, 2, 3, 1, 4)).reshape(2, CMP_STRIDE, NSA_HD, 2 * CMP_HID)
    zero = jnp.zeros_like(w)
    wpair = jnp.concatenate([jnp.concatenate([w, zero], axis=-1), jnp.concatenate([zero, w], axis=-1)], axis=2)
    wpair = wpair.astype(BF16)

    def page_spec(t):
        return pl.BlockSpec((1, 2, NSA_KV_HEADS, NSA_HD, PAGE_ROWS),
                            lambda b, j, pt, t=t: page_index(b, j * PAGES_PER_STEP + t, pt))

    def const(shape):
        return pl.BlockSpec(shape, lambda b, j, pt: (0,) * len(shape))

    grid_spec = pltpu.PrefetchScalarGridSpec(
        num_scalar_prefetch=1, grid=(batch, n_steps),
        in_specs=[page_spec(t) for t in range(PAGES_PER_STEP)] + [
            const((2, CMP_STRIDE, LANES, 4 * CMP_HID)), const((2, CMP_HID)), const((2, CMP_HID, NSA_HD)),
            const((2, NSA_HD)), const((1, NSA_HD))],
        out_specs=(pl.BlockSpec((1, n_sub, KV_W // 2), lambda b, j, pt: (b, 0, 0)),
                   pl.BlockSpec((1, n_sub, KV_W // 2), lambda b, j, pt: (b, 0, 0))),
        scratch_shapes=[pltpu.VMEM((n_sub, P_W), F32),
                        pltpu.VMEM((PAGES_PER_STEP, 2 * KV_PAIRS, PAGE_ROWS, 2 * NSA_HD), F32)])
    return pl.pallas_call(
        functools.partial(_compress_body, n_steps=n_steps),
        out_shape=(jax.ShapeDtypeStruct((batch, n_sub, KV_W // 2), F32),
                   jax.ShapeDtypeStruct((batch, n_sub, KV_W // 2), F32)),
        grid_spec=grid_spec, compiler_params=_params(2), name="nsa_compress",
    )(page_table, *([rows_t] * PAGES_PER_STEP), wpair, b1, w2.astype(BF16), b2, k_norm0.reshape(1, NSA_HD))


Q_TILE = 128
WIN_BLOCKS = (WINDOW + Q_TILE) // Q_TILE
NEAR_TILES = 2
NEAR_CHUNKS = 8
M_INIT = -1e29


def _gather_bias(tbh, idx_slices):
    return jnp.concatenate([jnp.take_along_axis(tbh, idx, axis=1) for idx in idx_slices], axis=1)


def _nsa_attend_body(q_ref, gate_ref, x_ref, kc_ref, vc_ref, kst_ref, vs_ref, w0_ref, w1_ref, w2_ref, w3_ref, w4_ref,
                     idxc_ref, idxw_ref, nb_ref, mmat_ref, tb_ref, wout_ref, expand_ref, y_ref,
                     pc_s, oc_s, sel_s, m_s, l_s, a_s, acc_s, ps_s, pw_s, o_s, bsel_s, bwin_s, *, n_sel):
    win_refs = (w0_ref, w1_ref, w2_ref, w3_ref, w4_ref)
    t = pl.program_id(1)
    s0 = t * Q_TILE
    n_sub = kc_ref.shape[1]
    rep_rows = NSA_REP * Q_TILE
    q_pos = s0 + lax.broadcasted_iota(jnp.int32, (Q_TILE, LANES), 0)
    blk = lax.broadcasted_iota(jnp.int32, (Q_TILE, LANES), 1)
    cur = q_pos // SEL_BLOCK
    forced = (blk == 0) | (blk == cur) | (blk == cur - 1)
    future = blk * SEL_BLOCK > q_pos

    def group_q(g):
        return jnp.concatenate([q_ref[:, (g * NSA_REP + r) * NSA_HD:(g * NSA_REP + r + 1) * NSA_HD]
                                for r in range(NSA_REP)], axis=0)

    def head_table(h):
        return jnp.broadcast_to(tb_ref[h:h + 1, :], (Q_TILE, LANES))

    @pl.when((pl.program_id(0) == 0) & (t == 0))
    def _():
        for h in range(NSA_HEADS):
            tbh = head_table(h)
            far_bias = tb_ref[h:h + 1, REL_BUCKETS - 1:REL_BUCKETS]
            for k in range(NEAR_CHUNKS):
                bsel_s[h, k] = jnp.take_along_axis(tbh, nb_ref[k], axis=1) - far_bias
            bsel_s[h, NEAR_CHUNKS] = jnp.zeros((Q_TILE, LANES), F32)
            bsel_s[h, NEAR_CHUNKS + 1] = jnp.full((Q_TILE, LANES), NEG, F32)
            bwin_s[h] = _gather_bias(tbh, [idxw_ref[:, c * LANES:(c + 1) * LANES] for c in range(WIN_BLOCKS)])

    score_t = []
    for g in range(NSA_KV_HEADS):
        gs = slice(g * NSA_HD, (g + 1) * NSA_HD)
        sc = _dot_nt(group_q(g), kc_ref[0, :, gs])
        imp = jnp.zeros((Q_TILE, n_sub), F32)
        for r in range(NSA_REP):
            rs = slice(r * Q_TILE, (r + 1) * Q_TILE)
            bias = _gather_bias(head_table(g * NSA_REP + r),
                                [idxc_ref[0, :, c * LANES:(c + 1) * LANES] for c in range(n_sub // LANES)])
            s_r = sc[rs] + bias
            m = jnp.maximum(jnp.max(s_r, axis=1, keepdims=True), M_INIT)
            e = jnp.exp(s_r - m)
            p = e / jnp.maximum(jnp.sum(e, axis=1, keepdims=True), 1e-30)
            imp = imp + p
            pc_s[rs, :] = p.astype(BF16)
        oc_s[g] = jnp.dot(pc_s[...], vc_ref[0, :, gs], preferred_element_type=F32)
        score = _dot_f32(imp, mmat_ref[...])
        score = jnp.where(forced, FORCE_SCORE, score)
        score = jnp.where(future, NEG, score)
        score = jnp.where(blk >= n_sel, -jnp.inf, score)
        score_t.append(score.T)

    blk_t = lax.broadcasted_iota(jnp.int32, (LANES, Q_TILE), 0).astype(F32)
    sel_t = [jnp.zeros((LANES, Q_TILE), F32) for _ in range(NSA_KV_HEADS)]
    for _ in range(min(SEL_TOPK, n_sel)):
        for g in range(NSA_KV_HEADS):
            mx = jnp.max(score_t[g], axis=0, keepdims=True)
            first = jnp.min(jnp.where(score_t[g] == mx, blk_t, float(LANES)), axis=0, keepdims=True)
            pick = blk_t == first
            sel_t[g] = jnp.where(pick, 1.0, sel_t[g])
            score_t[g] = jnp.where(pick, -jnp.inf, score_t[g])
    for g in range(NSA_KV_HEADS):
        sel_s[g] = ((sel_t[g].T - 1.0) * (-NEG)).astype(BF16)

    n_kt = (s0 + Q_TILE + KEY_TILE - 1) // KEY_TILE
    n_far = jnp.maximum(n_kt - NEAR_TILES, 0)

    for pair in ((0, 1), (2, 3)):
        qgs = [group_q(g) for g in pair]
        m_s[...] = jnp.full(m_s.shape, M_INIT, F32)
        l_s[...] = jnp.zeros(l_s.shape, F32)
        acc_s[...] = jnp.zeros(acc_s.shape, F32)

        def key_tile(kt, near, pair=pair, qgs=qgs):
            k0 = pl.multiple_of(kt * KEY_TILE, KEY_TILE)
            for j, g in enumerate(pair):
                gs = slice(g * NSA_HD, (g + 1) * NSA_HD)
                s = jnp.dot(qgs[j], kst_ref[0, gs, pl.ds(k0, KEY_TILE)], preferred_element_type=F32)
                negm = jnp.dot(sel_s[g], expand_ref[:, pl.ds(k0, KEY_TILE)], preferred_element_type=F32)
                v = vs_ref[pl.ds(k0, KEY_TILE), gs]
                for r in range(NSA_REP):
                    rs = slice(r * Q_TILE, (r + 1) * Q_TILE)
                    h = g * NSA_REP + r
                    if near:
                        chunks = []
                        for c in range(KEY_TILE // LANES):
                            k = (s0 - k0) // LANES - c
                            chunks.append(bsel_s[h, jnp.where(k < 0, NEAR_CHUNKS + 1, jnp.minimum(k, NEAR_CHUNKS))])
                        s_r = s[rs] + (negm + jnp.concatenate(chunks, axis=1))
                    else:
                        s_r = s[rs] + negm
                    m_old = m_s[j, rs]
                    m_new = jnp.maximum(m_old, jnp.max(s_r, axis=1, keepdims=True))
                    alpha = jnp.exp(m_old - m_new)
                    p = jnp.exp(s_r - jnp.concatenate([m_new] * (KEY_TILE // LANES), axis=1))
                    l_s[j, rs] = alpha * l_s[j, rs] + jnp.sum(p, axis=1, keepdims=True)
                    m_s[j, rs] = m_new
                    acc_s[j, rs] = (alpha[:, :NSA_HD] * acc_s[j, rs]
                                    + jnp.dot(p.astype(BF16), v, preferred_element_type=F32))

        def far_body(kt, carry):
            key_tile(kt, False)
            return carry

        def near_body(kt, carry):
            key_tile(kt, True)
            return carry

        lax.fori_loop(0, n_far, far_body, 0)
        lax.fori_loop(n_far, n_kt, near_body, 0)

        for j, g in enumerate(pair):
            gs = slice(g * NSA_HD, (g + 1) * NSA_HD)
            kw = jnp.concatenate([wr[0, :, gs] for wr in win_refs], axis=0)
            vw = jnp.concatenate([wr[0, :, KV_W // 2 + g * NSA_HD:KV_W // 2 + (g + 1) * NSA_HD] for wr in win_refs],
                                 axis=0)
            sw = _dot_nt(qgs[j], kw)
            for r in range(NSA_REP):
                rs = slice(r * Q_TILE, (r + 1) * Q_TILE)
                s_r = sw[rs] + bwin_s[g * NSA_REP + r]
                e = jnp.exp(s_r - jnp.max(s_r, axis=1, keepdims=True))
                pw_s[rs, :] = (e / jnp.sum(e, axis=1, keepdims=True)).astype(BF16)
            o_w = jnp.dot(pw_s[...], vw, preferred_element_type=F32)

            for r in range(NSA_REP):
                rs = slice(r * Q_TILE, (r + 1) * Q_TILE)
                h = g * NSA_REP + r
                o_h = (gate_ref[:, 3 * h:3 * h + 1] * oc_s[g, rs, :]
                       + gate_ref[:, 3 * h + 1:3 * h + 2] * (acc_s[j, rs, :] / l_s[j, rs, :NSA_HD])
                       + gate_ref[:, 3 * h + 2:3 * h + 3] * o_w[rs])
                o_s[:, h * NSA_HD:(h + 1) * NSA_HD] = o_h.astype(BF16)

    y_ref[...] = x_ref[...] + jnp.dot(o_s[...], wout_ref[...], preferred_element_type=F32)


def nsa_attend_prompt(q, gates, x, k_c, v_c, kst, selb, winb, rel_bias, w_out, batch, seq):
    n_qt = seq // Q_TILE
    n_sub = k_c.shape[1]
    n_sel = seq // SEL_BLOCK
    assert CMP_LEN == 2 * CMP_STRIDE and SEL_BLOCK == 4 * CMP_STRIDE and SEL_TOPK <= n_sel <= LANES
    win_pad = jnp.pad(winb.reshape(batch, seq, KV_W), ((0, 0), (WINDOW, 0), (0, 0)))
    iq = jnp.arange(Q_TILE, dtype=jnp.int32)
    dist_c = (jnp.arange(n_qt, dtype=jnp.int32)[:, None, None] * Q_TILE + iq[None, :, None]
              - (jnp.arange(n_sub, dtype=jnp.int32)[None, None, :] * CMP_STRIDE + CMP_LEN - 1))
    idx_c = _bucket_index(dist_c, dist_c >= 0)
    dist_w = iq[:, None] - jnp.arange(WINDOW + Q_TILE, dtype=jnp.int32)[None, :] + WINDOW
    idx_w = _bucket_index(dist_w, (dist_w >= 0) & (dist_w < WINDOW))
    dist_n = (jnp.arange(NEAR_CHUNKS, dtype=jnp.int32)[:, None, None] * LANES + iq[None, :, None]
              - jnp.arange(LANES, dtype=jnp.int32)[None, None, :])
    nb = _bucket_index(dist_n, dist_n >= 0)
    ci = np.arange(n_sub)[:, None]
    bj = np.arange(LANES)[None, :]
    mmat = ((ci // 4 == bj).astype(np.float32) + ((ci + 1) // 4 == bj).astype(np.float32)) * (ci < n_sub - 1)
    expand = (np.arange(seq)[None, :] // SEL_BLOCK == np.arange(LANES)[:, None]).astype(np.float32)

    def rows(width):
        return pl.BlockSpec((Q_TILE, width), lambda b, t: (b * n_qt + t, 0))

    def per_b(shape):
        return pl.BlockSpec((1,) + shape, lambda b, t: (b,) + (0,) * len(shape))

    def const(shape):
        return pl.BlockSpec(shape, lambda b, t: (0,) * len(shape))

    in_specs = [rows(Q_W), rows(LANES), rows(D_MODEL), per_b((n_sub, KV_W // 2)), per_b((n_sub, KV_W // 2)),
                per_b((KV_W // 2, seq)), pl.BlockSpec((seq, KV_W // 2), lambda b, t: (b, 1))]
    in_specs += [pl.BlockSpec((1, Q_TILE, KV_W), lambda b, t, j=j: (b, t + j, 0)) for j in range(WIN_BLOCKS)]
    in_specs += [pl.BlockSpec((1, Q_TILE, n_sub), lambda b, t: (t, 0, 0)), const((Q_TILE, WINDOW + Q_TILE)),
                 const((NEAR_CHUNKS, Q_TILE, LANES)), const((n_sub, LANES)), const((NSA_HEADS, LANES)),
                 const((Q_W, D_MODEL)), const((LANES, seq))]
    rep_rows = NSA_REP * Q_TILE
    scratch = [pltpu.VMEM((rep_rows, n_sub), BF16), pltpu.VMEM((NSA_KV_HEADS, rep_rows, NSA_HD), F32),
               pltpu.VMEM((NSA_KV_HEADS, Q_TILE, LANES), BF16), pltpu.VMEM((2, rep_rows, LANES), F32),
               pltpu.VMEM((2, rep_rows, LANES), F32), pltpu.VMEM((rep_rows, LANES), F32),
               pltpu.VMEM((2, rep_rows, NSA_HD), F32),
               pltpu.VMEM((rep_rows, KEY_TILE), BF16), pltpu.VMEM((rep_rows, WINDOW + Q_TILE), BF16),
               pltpu.VMEM((Q_TILE, Q_W), BF16),
               pltpu.VMEM((NSA_HEADS, NEAR_CHUNKS + 2, Q_TILE, LANES), F32),
               pltpu.VMEM((NSA_HEADS, Q_TILE, WINDOW + Q_TILE), F32)]
    return pl.pallas_call(
        functools.partial(_nsa_attend_body, n_sel=n_sel),
        out_shape=jax.ShapeDtypeStruct((batch * seq, D_MODEL), F32), grid=(batch, n_qt),
        in_specs=in_specs, out_specs=rows(D_MODEL), scratch_shapes=scratch,
        compiler_params=_params(2), name="nsa_attend_prompt",
    )(q, gates, x, k_c.astype(BF16), v_c.astype(BF16), kst, selb, *([win_pad] * WIN_BLOCKS),
      idx_c, idx_w, nb, jnp.asarray(mmat), _bias_table(rel_bias), w_out.astype(BF16), jnp.asarray(expand, BF16))


SCORE_W = 384


def _nsa_sample_cmp_body(q_ref, kc_ref, vc_ref, idx_ref, mmat_ref, gsum_ref, tb_ref, oc_ref, top_ref, *,
                         n_sel, q_pos):
    n_sub = kc_ref.shape[1]
    q = q_ref[0]
    row_g = lax.broadcasted_iota(jnp.int32, (NSA_HEADS, 1), 0) // NSA_REP
    s = jnp.zeros((NSA_HEADS, n_sub), F32)
    for g in range(NSA_KV_HEADS):
        s = jnp.where(row_g == g, _dot_nt(q, kc_ref[0, :, g * NSA_HD:(g + 1) * NSA_HD]), s)
    tb = tb_ref[...]
    bias = jnp.concatenate([jnp.take_along_axis(
        tb, jnp.broadcast_to(idx_ref[:, c * LANES:(c + 1) * LANES], (NSA_HEADS, LANES)), axis=1)
        for c in range(n_sub // LANES)], axis=1)
    s = s + bias
    m = jnp.maximum(jnp.max(s, axis=1, keepdims=True), M_INIT)
    e = jnp.exp(s - m)
    p = e / jnp.maximum(jnp.sum(e, axis=1, keepdims=True), 1e-30)
    pb = p.astype(BF16)
    o = jnp.zeros((NSA_HEADS, NSA_HD), F32)
    for g in range(NSA_KV_HEADS):
        o = jnp.where(row_g == g, jnp.dot(pb, vc_ref[0, :, g * NSA_HD:(g + 1) * NSA_HD],
                                          preferred_element_type=F32), o)
    oc_ref[0] = o
    imp = _dot_f32(gsum_ref[...], p)
    score = _dot_f32(imp, mmat_ref[...])
    blk = lax.broadcasted_iota(jnp.int32, score.shape, 1)
    cur = q_pos // SEL_BLOCK
    score = jnp.where((blk == 0) | (blk == cur) | (blk == cur - 1), FORCE_SCORE, score)
    score = jnp.where(blk * SEL_BLOCK > q_pos, NEG, score)
    score = jnp.where(blk >= n_sel, -jnp.inf, score)
    blkf = blk.astype(F32)
    lane = lax.broadcasted_iota(jnp.int32, (SUBLANES, LANES), 1)
    top = jnp.zeros((SUBLANES, LANES), F32)
    for it in range(min(SEL_TOPK, n_sel)):
        mx = jnp.max(score, axis=1, keepdims=True)
        first = jnp.min(jnp.where(score == mx, blkf, float(SCORE_W)), axis=1, keepdims=True)
        top = jnp.where(lane == it, first, top)
        score = jnp.where(blkf == first, -jnp.inf, score)
    top_ref[0] = top.astype(jnp.int32)


def nsa_sample_cmp(q, k_c, v_c, rel_bias, past_len):
    batch = q.shape[0]
    n_sub = k_c.shape[1]
    n_sel = past_len // SEL_BLOCK + 1
    assert SEL_TOPK <= n_sel <= SCORE_W
    dist = past_len - (jnp.arange(n_sub, dtype=jnp.int32) * CMP_STRIDE + CMP_LEN - 1)
    idx = _bucket_index(dist, dist >= 0).reshape(1, n_sub)
    ci = np.arange(n_sub)[:, None]
    bj = np.arange(SCORE_W)[None, :]
    mmat = ((ci // 4 == bj).astype(np.float32) + ((ci + 1) // 4 == bj).astype(np.float32)) * (ci < n_sub - 1)
    gsum = (np.arange(SUBLANES)[:, None] == np.arange(NSA_HEADS)[None, :] // NSA_REP).astype(np.float32)

    def per_b(shape):
        return pl.BlockSpec((1,) + shape, lambda b: (b,) + (0,) * len(shape))

    def const(shape):
        return pl.BlockSpec(shape, lambda b: (0,) * len(shape))

    o_c, top = pl.pallas_call(
        functools.partial(_nsa_sample_cmp_body, n_sel=n_sel, q_pos=past_len),
        out_shape=(jax.ShapeDtypeStruct((batch, NSA_HEADS, NSA_HD), F32),
                   jax.ShapeDtypeStruct((batch, SUBLANES, LANES), jnp.int32)),
        grid=(batch,),
        in_specs=[per_b((NSA_HEADS, NSA_HD)), per_b((n_sub, KV_W // 2)), per_b((n_sub, KV_W // 2)),
                  const((1, n_sub)), const((n_sub, SCORE_W)), const((SUBLANES, NSA_HEADS)), const((NSA_HEADS, LANES))],
        out_specs=(per_b((NSA_HEADS, NSA_HD)), per_b((SUBLANES, LANES))),
        compiler_params=_params(1), name="nsa_sample_cmp",
    )(q, k_c.astype(BF16), v_c.astype(BF16), idx, jnp.asarray(mmat), jnp.asarray(gsum), _bias_table(rel_bias))
    return o_c, top[:, :NSA_KV_HEADS, :SEL_TOPK]


SEL_PER_STEP = 2


def _nsa_sample_attend_body(row_ref, sidx_ref, *refs, q_pos, n_past_blk, n_steps):
    n_blk_refs = NSA_KV_HEADS * SEL_PER_STEP
    blk_refs = refs[:n_blk_refs]
    (q_ref, knew_ref, win_ref, wnew_ref, gate_ref, oc_ref, idxw_ref, tb_ref, o_ref,
     s_s, v_s, ow_s) = refs[n_blk_refs:]
    b = pl.program_id(0)
    step = pl.program_id(1)
    q = q_ref[0]
    row_g = lax.broadcasted_iota(jnp.int32, (NSA_HEADS, 1), 0) // NSA_REP
    tb = tb_ref[...]
    half = KV_W // 2

    def by_group(fn):
        out = fn(0)
        for g in range(1, NSA_KV_HEADS):
            out = jnp.where(row_g == g, fn(g), out)
        return out

    @pl.when(step == 0)
    def _():
        s_w = by_group(lambda g: _dot(q, win_ref[0, 0, g]))
        bias = jnp.concatenate([jnp.take_along_axis(
            tb, jnp.broadcast_to(idxw_ref[:, c * LANES:(c + 1) * LANES], (NSA_HEADS, LANES)), axis=1)
            for c in range(s_w.shape[1] // LANES)], axis=1)
        s_w = s_w + bias
        k_new = by_group(lambda g: jnp.broadcast_to(wnew_ref[0, :, g * NSA_HD:(g + 1) * NSA_HD], (NSA_HEADS, NSA_HD)))
        v_new = by_group(lambda g: jnp.broadcast_to(wnew_ref[0, :, half + g * NSA_HD:half + (g + 1) * NSA_HD],
                                                    (NSA_HEADS, NSA_HD)))
        s_n = jnp.sum(q.astype(F32) * _r16(k_new), axis=1, keepdims=True) + tb[:, 0:1]
        m = jnp.maximum(jnp.max(s_w, axis=1, keepdims=True), s_n)
        e_w = jnp.exp(s_w - m)
        e_n = jnp.exp(s_n - m)
        total = jnp.sum(e_w, axis=1, keepdims=True) + e_n
        pb = (e_w / total).astype(BF16)
        pv = by_group(lambda g: _dot_nt(pb, win_ref[0, 1, g]))
        ow_s[...] = pv + _r16(e_n / total) * _r16(v_new)

    keys = SEL_PER_STEP * PAGE_ROWS
    key_lane = lax.broadcasted_iota(jnp.int32, (NSA_HD, PAGE_ROWS), 1)
    lane = lax.broadcasted_iota(jnp.int32, (1, PAGE_ROWS), 1)
    thresholds = _t5_thresholds()

    def tile_kv(g, kv):
        parts = []
        for j in range(SEL_PER_STEP):
            is_new = sidx_ref[b, g, step * SEL_PER_STEP + j] >= n_past_blk
            cached = blk_refs[g * SEL_PER_STEP + j][0, kv, 0]
            fresh = jnp.where(key_lane == 0, knew_ref[0, kv * half + g * NSA_HD:kv * half + (g + 1) * NSA_HD, :], 0.0)
            parts.append(jnp.where(is_new, fresh, cached))
        return jnp.concatenate(parts, axis=1).astype(BF16)

    def tile_bias(g):
        parts = []
        for j in range(SEL_PER_STEP):
            blk = sidx_ref[b, g, step * SEL_PER_STEP + j]
            is_new = blk >= n_past_blk
            base = jnp.where(is_new, blk * SEL_BLOCK, (blk * SEL_BLOCK) // PAGE_ROWS * PAGE_ROWS)
            pos = base + lane
            dist = q_pos - pos
            idx = jnp.zeros((1, PAGE_ROWS), jnp.int32)
            for thr in thresholds:
                idx = idx + (dist >= thr).astype(jnp.int32)
            parts.append(jnp.where((dist >= 0) & (pos // SEL_BLOCK == blk), idx, MASK_BUCKET))
        idx = jnp.concatenate(parts, axis=1)
        return jnp.concatenate([jnp.take_along_axis(
            tb, jnp.broadcast_to(idx[:, c * LANES:(c + 1) * LANES], (NSA_HEADS, LANES)), axis=1)
            for c in range(keys // LANES)], axis=1)

    k0 = pl.multiple_of(step * keys, keys)
    s_s[:, pl.ds(k0, keys)] = by_group(lambda g: jnp.dot(q, tile_kv(g, 0), preferred_element_type=F32)
                                       + tile_bias(g))
    for g in range(NSA_KV_HEADS):
        v_s[g, :, pl.ds(k0, keys)] = tile_kv(g, 1)

    @pl.when(step == n_steps - 1)
    def _():
        s = s_s[...]
        e = jnp.exp(s - jnp.max(s, axis=1, keepdims=True))
        pb = (e / jnp.sum(e, axis=1, keepdims=True)).astype(BF16)
        o_sel = by_group(lambda g: _dot_nt(pb, v_s[g]))
        gate = gate_ref[0]
        o_ref[0] = gate[:, 0:1] * oc_ref[0] + gate[:, 1:2] * o_sel + gate[:, 2:3] * ow_s[...]


def nsa_sample_attend(q, kv_sel_new, kv_win_new, gates, o_c, top, cache_sel, win_buf, page_table, rel_bias,
                      past_len):
    batch = q.shape[0]
    n_past_blk = past_len // SEL_BLOCK
    bpp = PAGE_ROWS // SEL_BLOCK
    n_steps = SEL_TOPK // SEL_PER_STEP
    page_shape = cache_sel.shape[1:]
    jp = jnp.minimum(top, n_past_blk - 1)
    phys = jnp.take_along_axis(page_table, (jp // bpp).reshape(batch, -1), axis=1).reshape(top.shape)
    phys = phys.astype(jnp.int32)
    wb = win_buf.shape[-1]
    dist_w = past_len - (past_len - wb + jnp.arange(wb, dtype=jnp.int32))
    idx_w = _bucket_index(dist_w, (dist_w >= 0) & (dist_w < WINDOW)).reshape(1, wb)
    gate3 = gates[:, :3 * NSA_HEADS].reshape(batch, NSA_HEADS, 3)

    def blk_spec(g, j):
        return pl.BlockSpec((1, page_shape[0], 1) + page_shape[2:],
                            lambda b, s, rows, sidx, g=g, j=j: (rows[b, g, s * SEL_PER_STEP + j], 0, g, 0, 0))

    def per_b(shape):
        return pl.BlockSpec((1,) + shape, lambda b, s, rows, sidx: (b,) + (0,) * len(shape))

    def const(shape):
        return pl.BlockSpec(shape, lambda b, s, rows, sidx: (0,) * len(shape))

    grid_spec = pltpu.PrefetchScalarGridSpec(
        num_scalar_prefetch=2, grid=(batch, n_steps),
        in_specs=[blk_spec(g, j) for g in range(NSA_KV_HEADS) for j in range(SEL_PER_STEP)] + [
            per_b((NSA_HEADS, NSA_HD)), per_b((KV_W, 1)), per_b(win_buf.shape[1:]), per_b((1, KV_W)),
            per_b((NSA_HEADS, 3)), per_b((NSA_HEADS, NSA_HD)), const((1, wb)), const((NSA_HEADS, LANES))],
        out_specs=per_b((NSA_HEADS, NSA_HD)),
        scratch_shapes=[pltpu.VMEM((NSA_HEADS, SEL_TOPK * PAGE_ROWS), F32),
                        pltpu.VMEM((NSA_KV_HEADS, NSA_HD, SEL_TOPK * PAGE_ROWS), BF16),
                        pltpu.VMEM((NSA_HEADS, NSA_HD), F32)])
    o = pl.pallas_call(
        functools.partial(_nsa_sample_attend_body, q_pos=past_len, n_past_blk=n_past_blk, n_steps=n_steps),
        out_shape=jax.ShapeDtypeStruct((batch, NSA_HEADS, NSA_HD), F32), grid_spec=grid_spec,
        compiler_params=_params(2), name="nsa_sample_attend",
    )(phys, top.astype(jnp.int32), *([cache_sel] * (NSA_KV_HEADS * SEL_PER_STEP)), q,
      kv_sel_new.reshape(batch, KV_W, 1), win_buf, kv_win_new.reshape(batch, 1, KV_W), gate3, o_c, idx_w,
      _bias_table(rel_bias))
    return o.reshape(batch, Q_W)


def _pad_cols(w, width):
    return jnp.pad(w, ((0, 0), (0, width - w.shape[1]))).astype(BF16)


def kernel(x_prompt, x_sample, state_ret, state_mlstm_C, state_mlstm_n, state_mlstm_m, state_conv, cache_nsa_cmp, cache_nsa_sel, state_nsa_win, page_table, rel_bias, norm_mix, norm_ffn, ab_w_in, ab_conv_w, ab_conv_b, ab_b_igate, ab_b_fgate, ab_gn_g, ab_gn_b, ab_hn_g, ab_w_out, nsa_w_in, nsa_q_norm, nsa_k_norm, nsa_cmp_w1, nsa_cmp_b1, nsa_cmp_w2, nsa_cmp_b2, nsa_w_out, moe_w_group, moe_b_group, moe_w_expert, moe_b_expert, moe_w_gate, moe_w_up, moe_w_down):
    bp, lp, d = x_prompt.shape
    bs, ls, _ = x_sample.shape
    page_size = cache_nsa_cmp.shape[2]
    past_len = page_table.shape[1] * page_size
    assert norm_mix.shape[0] == 2 and ls == 1 and d == D_MODEL and lp % KEY_TILE == 0
    xp = x_prompt.reshape(bp * lp, d)
    xs = x_sample.reshape(bs, d)

    def moe(x, layer):
        return hier_moe_residual(x, norm_ffn[layer], moe_w_group[layer], moe_b_group[layer], moe_w_expert[layer],
                                 moe_b_expert[layer], layer, moe_w_gate, moe_w_up, moe_w_down)

    w_in = _pad_cols(ab_w_in[0], AB_IN_PAD)
    ab = (ab_conv_w[0], ab_conv_b[0], ab_b_igate[0], ab_b_fgate[0], ab_gn_g[0], ab_gn_b[0], ab_hn_g[0])
    zp = norm_matmul(xp, norm_mix[0], w_in)
    xp, ret_p, mc_p, mn_p, mm_p, conv_p = ab_prompt(zp, xp, bp, lp, *ab, ab_w_out[0])
    zs = norm_matmul(xs, norm_mix[0], w_in)
    pos_s = past_len + jnp.arange(ls, dtype=jnp.int32)
    ys, ret_s, mc_s, mn_s, mm_s, conv_s = ab_sample(zs, pos_s, state_ret[0], state_mlstm_C[0], state_mlstm_n[0],
                                                    state_mlstm_m[0], state_conv[0], *ab)
    xs = matmul_residual(ys, ab_w_out[0].astype(BF16), xs)
    xp = moe(xp, 0)
    xs = moe(xs, 0)

    w_in = _pad_cols(nsa_w_in[0], NSA_IN_PAD)
    cmp_w = (nsa_cmp_w1[0], nsa_cmp_b1[0], nsa_cmp_w2[0], nsa_cmp_b2[0], nsa_k_norm[0, 0])
    kv_shape = (2, NSA_KV_HEADS, NSA_HD)
    zp = norm_matmul(xp, norm_mix[1], w_in)
    q, cmp_t, sel_t, win_t, selb, winb, gates, kst = nsa_prep(zp, nsa_q_norm[0], nsa_k_norm[0], batch=bp)
    k_c, v_c = nsa_compress(cmp_t.reshape((bp,) + kv_shape + (lp,)), None, *cmp_w)
    xp = nsa_attend_prompt(q, gates, xp, k_c, v_c, kst, selb, winb, rel_bias, nsa_w_out[0], bp, lp)
    win_keep = min(WINDOW, lp)

    def rows_major(t):
        return jnp.transpose(t.reshape((1, bp) + kv_shape + (t.shape[-1],)), (0, 1, 5, 2, 3, 4))

    cmp_p = rows_major(cmp_t)
    sel_p = rows_major(sel_t)
    win_p = rows_major(win_t[:, :, lp - win_keep:])

    def rows_minor(c):
        return jnp.moveaxis(c, -4, -1)

    zs = norm_matmul(xs, norm_mix[1], w_in)
    q, cmp_s, sel_s, win_s, _, _, gates = nsa_prep(zs, nsa_q_norm[0], nsa_k_norm[0])
    k_c, v_c = nsa_compress(rows_minor(cache_nsa_cmp[0]), page_table, *cmp_w)
    q3 = q.reshape(bs, NSA_HEADS, NSA_HD)
    o_c, top = nsa_sample_cmp(q3, k_c, v_c, rel_bias, past_len)
    win_buf = state_nsa_win[0]
    o = nsa_sample_attend(q3, sel_s, win_s, gates, o_c, top, rows_minor(cache_nsa_sel[0]), rows_minor(win_buf),
                          page_table, rel_bias, past_len)
    xs = matmul_residual(o, nsa_w_out[0].astype(BF16), xs)
    win_s = jnp.concatenate([win_buf, win_s.reshape((bs, ls) + kv_shape)], axis=1)[None, :, ls:]
    cmp_s = cmp_s.reshape((1, bs, ls) + kv_shape)
    sel_s = sel_s.reshape((1, bs, ls) + kv_shape)
    xp = moe(xp, 1)
    xs = moe(xs, 1)

    return (xp.reshape(bp, lp, d), xs.reshape(bs, ls, d), ret_p[None], ret_s[None], mc_p[None], mc_s[None],
            mn_p[None], mn_s[None], mm_p[None], mm_s[None], conv_p[None], conv_s[None],
            cmp_p, cmp_s, sel_p, sel_s, win_p, win_s)
```

```python
import functools
import math

import jax
import jax.numpy as jnp
import numpy as np
from jax import lax
from jax.experimental import pallas as pl
from jax.experimental.pallas import tpu as pltpu

F32 = jnp.float32
BF16 = jnp.bfloat16
LANES = 128
SUBLANES = 8
VMEM_LIMIT = 56 * 1024 * 1024

D_MODEL = 1024
RET_HEADS = 4
ML_HEADS = 4
HEAD_D = 128
CONV_W = 4
CHUNK = 128
ROPE_BASE = 10000.0
AB_IN = 4104
AB_IN_PAD = 4224
NSA_HEADS = 16
NSA_KV_HEADS = 4
NSA_REP = 4
NSA_HD = 64
NSA_IN_PAD = 2688
CMP_LEN = 32
CMP_STRIDE = 16
CMP_HID = 128
SEL_BLOCK = 64
SEL_TOPK = 16
WINDOW = 512
REL_BUCKETS = 32
REL_MAX_DIST = 1024
FORCE_SCORE = 1e4
MOE_GROUPS = 4
MOE_EXP_PER_GROUP = 8
MOE_EXPERTS = 32
NEG = -1e30
EPS = 1e-6


def _params(n_grid):
    return pltpu.CompilerParams(dimension_semantics=("arbitrary",) * n_grid, vmem_limit_bytes=VMEM_LIMIT)


def _dot(a, b):
    return jnp.dot(a.astype(BF16), b.astype(BF16), preferred_element_type=F32)


def _dot_nt(a, b):
    return lax.dot_general(a.astype(BF16), b.astype(BF16), (((1,), (1,)), ((), ())), preferred_element_type=F32)


def _dot_tn(a, b):
    return lax.dot_general(a.astype(BF16), b.astype(BF16), (((0,), (0,)), ((), ())), preferred_element_type=F32)


def _dot_f32(a, b):
    return jnp.dot(a, b, preferred_element_type=F32, precision=lax.Precision.HIGHEST)


def _r16(x):
    return x.astype(BF16).astype(F32)


def _sigmoid(x):
    return 1.0 / (1.0 + jnp.exp(-x))


def _silu(x):
    return x * _sigmoid(x)


def _log_sigmoid(x):
    return -(jnp.maximum(-x, 0.0) + jnp.log1p(jnp.exp(-jnp.abs(x))))


def _norm_matmul_body(x_ref, g_ref, w_ref, o_ref, *, col_tile):
    x = x_ref[...]
    y = x * lax.rsqrt(jnp.mean(x * x, axis=-1, keepdims=True) + EPS) * g_ref[...]
    yb = y.astype(BF16)
    for c0 in range(0, o_ref.shape[1], col_tile):
        o_ref[:, c0:c0 + col_tile] = jnp.dot(yb, w_ref[:, c0:c0 + col_tile], preferred_element_type=F32)


def norm_matmul(x, g, w, row_tile=256):
    n, d = x.shape
    c = w.shape[1]
    tm = min(row_tile, n)
    col_tile = 384 if c % 384 == 0 else LANES
    return pl.pallas_call(
        functools.partial(_norm_matmul_body, col_tile=col_tile),
        out_shape=jax.ShapeDtypeStruct((n, c), F32),
        grid=(n // tm,),
        in_specs=[pl.BlockSpec((tm, d), lambda i: (i, 0)),
                  pl.BlockSpec((1, d), lambda i: (0, 0)),
                  pl.BlockSpec((d, c), lambda i: (0, 0))],
        out_specs=pl.BlockSpec((tm, c), lambda i: (i, 0)),
        compiler_params=_params(1),
        name="norm_matmul",
    )(x, g.reshape(1, d), w)


def _retention_constants(c):
    h = np.arange(RET_HEADS, dtype=np.float64)
    log_g = np.log1p(-np.exp2(-5.0 - h))
    i = np.arange(c, dtype=np.float64)
    diff = i[:, None] - i[None, :]
    decay = np.where(diff >= 0, np.exp(np.maximum(diff, 0.0)[None] * log_g[:, None, None]), 0.0)
    q_dec = np.exp((i + 1.0)[None, :] * log_g[:, None])[:, :, None]
    k_dec = np.exp((c - 1.0 - i)[None, :] * log_g[:, None])[:, :, None]
    s_dec = np.exp(c * log_g)
    return (jnp.asarray(decay, F32), jnp.asarray(q_dec, F32), jnp.asarray(k_dec, F32),
            [float(v) for v in s_dec])


def _rope_tables(pos):
    half = HEAD_D // 2
    freqs = ROPE_BASE ** (-jnp.arange(half, dtype=F32) / half)
    ang = pos.astype(F32)[:, None] * freqs[None, :]
    cos, sin = jnp.cos(ang), jnp.sin(ang)
    return jnp.concatenate([cos, cos], axis=-1), jnp.concatenate([-sin, sin], axis=-1)


def _rope(x, cosf, sinf):
    return x * cosf + pltpu.roll(x, HEAD_D // 2, 1) * sinf


def _ab_prompt_body(rq_ref, rk_ref, rv_ref, rg_ref, mqk_ref, mv_ref, mo_ref, gz_ref, x_ref, cos_ref, sin_ref,
                    decay_ref, qdec_ref, kdec_ref, convw_ref, convb_ref, gbias_ref, gng_ref, gnb_ref, hng_ref,
                    wout_ref,
                    y_ref, s_ref, c_ref, n_ref, m_ref, conv_ref,
                    cbuf_ref, ycat_ref, *, s_dec):
    c = pl.program_id(1)
    tail = CONV_W - 1

    @pl.when(c == 0)
    def _():
        s_ref[...] = jnp.zeros_like(s_ref)
        c_ref[...] = jnp.zeros_like(c_ref)
        n_ref[...] = jnp.zeros_like(n_ref)
        m_ref[...] = jnp.zeros_like(m_ref)
        cbuf_ref[0:SUBLANES, :] = jnp.zeros((SUBLANES, cbuf_ref.shape[1]), F32)

    cosf = cos_ref[...]
    sinf = sin_ref[...]
    row = lax.broadcasted_iota(jnp.int32, (CHUNK, CHUNK), 0)
    col = lax.broadcasted_iota(jnp.int32, (CHUNK, CHUNK), 1)
    eye = row == col
    tril = row >= col
    triu = row <= col

    cbuf_ref[SUBLANES:SUBLANES + CHUNK, :] = mqk_ref[...]
    conv = convb_ref[...]
    for w in range(CONV_W):
        conv = conv + (_r16(cbuf_ref[SUBLANES - tail + w:SUBLANES - tail + w + CHUNK, :])
                       * _r16(convw_ref[w:w + 1, :]))
    qk = _silu(conv)
    last = cbuf_ref[CHUNK + SUBLANES - tail:CHUNK + SUBLANES, :]
    cbuf_ref[SUBLANES - tail:SUBLANES, :] = last
    conv_ref[0] = last

    gz = gz_ref[...] + gbias_ref[...]
    for h in range(RET_HEADS):
        sl = slice(h * HEAD_D, (h + 1) * HEAD_D)
        q = _rope(rq_ref[:, sl], cosf, sinf)
        k = _rope(rk_ref[:, sl], cosf, sinf) * (HEAD_D ** -0.5)
        v = rv_ref[:, sl]
        a = _dot_nt(q, k) * decay_ref[h]
        s_old = s_ref[0, h]
        o = _dot(a, v) + qdec_ref[h] * _dot(q, s_old)
        s_ref[0, h] = s_dec[h] * s_old + _dot_tn(k * kdec_ref[h], v)
        mu = jnp.mean(o, axis=-1, keepdims=True)
        var = jnp.mean(jnp.square(o - mu), axis=-1, keepdims=True)
        o = (o - mu) * lax.rsqrt(var + EPS) * gng_ref[:, sl] + gnb_ref[:, sl]
        ycat_ref[:, sl] = _silu(rg_ref[:, sl]) * o

        mq = qk[:, sl]
        mk = qk[:, ML_HEADS * HEAD_D + h * HEAD_D:ML_HEADS * HEAD_D + (h + 1) * HEAD_D] * (HEAD_D ** -0.5)
        mv = mv_ref[:, sl]
        i_col = gz[:, h:h + 1]
        f_col = _log_sigmoid(gz[:, ML_HEADS + h:ML_HEADS + h + 1])
        i_row = jnp.sum(jnp.where(eye, i_col, 0.0), axis=0, keepdims=True)
        f_row = jnp.sum(jnp.where(eye, f_col, 0.0), axis=0, keepdims=True)
        b_col = jnp.sum(jnp.where(tril, f_row, 0.0), axis=1, keepdims=True)
        b_row = jnp.sum(jnp.where(triu, f_col, 0.0), axis=0, keepdims=True)
        m_old = m_ref[0, h:h + 1, 0:1]
        dlog = jnp.where(tril, b_col - b_row + i_row, -jnp.inf)
        inter = b_col + m_old
        m_t = jnp.maximum(inter, jnp.max(dlog, axis=1, keepdims=True))
        wgt = _dot_nt(mq, mk) * jnp.exp(dlog - m_t)
        e_inter = jnp.exp(inter - m_t)
        c_old = c_ref[0, h]
        n_old = n_ref[0, h:h + 1, :]
        num = _dot(wgt, mv) + e_inter * _dot_nt(mq, c_old)
        den = (jnp.sum(wgt, axis=1, keepdims=True)
               + e_inter * jnp.sum(_r16(mq) * _r16(n_old), axis=1, keepdims=True))
        hc = num / jnp.maximum(jnp.abs(den), jnp.exp(-m_t))
        b_last = b_col[CHUNK - 1:CHUNK, :]
        u_row = b_last - b_row + i_row
        u_col = b_last - b_col + i_col
        m_new = jnp.maximum(b_last + m_old, jnp.max(u_row, axis=1, keepdims=True))
        ws_col = jnp.exp(u_col - m_new)
        f_state = jnp.exp(b_last + m_old - m_new)
        c_ref[0, h] = f_state * c_old + _dot_tn(mv * ws_col, mk)
        n_ref[0, h:h + 1, :] = f_state * n_old + jnp.sum(_r16(ws_col) * _r16(mk), axis=0, keepdims=True)
        m_ref[0, h:h + 1, :] = jnp.broadcast_to(m_new, (1, LANES))
        hm = _sigmoid(mo_ref[:, sl]) * hc
        hm = hm * lax.rsqrt(jnp.mean(hm * hm, axis=-1, keepdims=True) + EPS) * hng_ref[:, sl]
        ycat_ref[:, RET_HEADS * HEAD_D + h * HEAD_D:RET_HEADS * HEAD_D + (h + 1) * HEAD_D] = hm

    y_ref[...] = x_ref[...] + jnp.dot(ycat_ref[...].astype(BF16), wout_ref[...], preferred_element_type=F32)


def ab_prompt(z, x, batch, seq, conv_w, conv_b, b_ig, b_fg, gn_g, gn_b, hn_g, w_out):
    n_chunk = seq // CHUNK
    decay, q_dec, k_dec, s_dec = _retention_constants(CHUNK)
    cosf, sinf = _rope_tables(jnp.arange(seq, dtype=jnp.int32))
    gbias = jnp.zeros((1, LANES), F32).at[0, :ML_HEADS].set(b_ig).at[0, ML_HEADS:2 * ML_HEADS].set(b_fg)
    hw = RET_HEADS * HEAD_D
    qkw = 2 * ML_HEADS * HEAD_D

    def zspec(width, blk):
        return pl.BlockSpec((CHUNK, width), lambda b, c, blk=blk: (b * n_chunk + c, blk))

    def const(shape):
        return pl.BlockSpec(shape, lambda b, c: (0,) * len(shape))

    in_specs = [zspec(hw, 0), zspec(hw, 1), zspec(hw, 2), zspec(hw, 3), zspec(qkw, 2), zspec(hw, 6), zspec(hw, 7),
                zspec(LANES, (AB_IN_PAD - LANES) // LANES),
                pl.BlockSpec((CHUNK, D_MODEL), lambda b, c: (b * n_chunk + c, 0)),
                pl.BlockSpec((CHUNK, HEAD_D), lambda b, c: (c, 0)),
                pl.BlockSpec((CHUNK, HEAD_D), lambda b, c: (c, 0)),
                const((RET_HEADS, CHUNK, CHUNK)), const((RET_HEADS, CHUNK, 1)), const((RET_HEADS, CHUNK, 1)),
                const((CONV_W, qkw)), const((1, qkw)), const((1, LANES)),
                const((1, hw)), const((1, hw)), const((1, hw)), const((2 * hw, D_MODEL))]
    out_shape = (jax.ShapeDtypeStruct((batch * seq, D_MODEL), F32),
                 jax.ShapeDtypeStruct((batch, RET_HEADS, HEAD_D, HEAD_D), F32),
                 jax.ShapeDtypeStruct((batch, ML_HEADS, HEAD_D, HEAD_D), F32),
                 jax.ShapeDtypeStruct((batch, ML_HEADS, HEAD_D), F32),
                 jax.ShapeDtypeStruct((batch, SUBLANES, LANES), F32),
                 jax.ShapeDtypeStruct((batch, CONV_W - 1, qkw), F32))
    out_specs = (pl.BlockSpec((CHUNK, D_MODEL), lambda b, c: (b * n_chunk + c, 0)),
                 pl.BlockSpec((1, RET_HEADS, HEAD_D, HEAD_D), lambda b, c: (b, 0, 0, 0)),
                 pl.BlockSpec((1, ML_HEADS, HEAD_D, HEAD_D), lambda b, c: (b, 0, 0, 0)),
                 pl.BlockSpec((1, ML_HEADS, HEAD_D), lambda b, c: (b, 0, 0)),
                 pl.BlockSpec((1, SUBLANES, LANES), lambda b, c: (b, 0, 0)),
                 pl.BlockSpec((1, CONV_W - 1, qkw), lambda b, c: (b, 0, 0)))
    y, s, cc, n, m, conv = pl.pallas_call(
        functools.partial(_ab_prompt_body, s_dec=s_dec),
        out_shape=out_shape, grid=(batch, n_chunk), in_specs=in_specs, out_specs=out_specs,
        scratch_shapes=[pltpu.VMEM((CHUNK + SUBLANES, qkw), F32), pltpu.VMEM((CHUNK, 2 * hw), F32)],
        compiler_params=_params(2), name="ab_prompt",
    )(z, z, z, z, z, z, z, z, x, cosf, sinf, decay, q_dec, k_dec, conv_w, conv_b.reshape(1, qkw), gbias,
      gn_g.reshape(1, hw), gn_b.reshape(1, hw), hn_g.reshape(1, hw), w_out.astype(BF16))
    return y, s, cc, n, m[:, :ML_HEADS, 0], conv


def _matmul_residual_body(a_ref, w_ref, x_ref, o_ref):
    o_ref[...] = x_ref[...] + jnp.dot(a_ref[...].astype(BF16), w_ref[...], preferred_element_type=F32)


def matmul_residual(a, w, x, row_tile=256):
    n, kk = a.shape
    d = w.shape[1]
    tm = min(row_tile, n)
    return pl.pallas_call(
        _matmul_residual_body, out_shape=jax.ShapeDtypeStruct((n, d), F32), grid=(n // tm,),
        in_specs=[pl.BlockSpec((tm, kk), lambda i: (i, 0)), pl.BlockSpec((kk, d), lambda i: (0, 0)),
                  pl.BlockSpec((tm, d), lambda i: (i, 0))],
        out_specs=pl.BlockSpec((tm, d), lambda i: (i, 0)),
        compiler_params=_params(1), name="matmul_residual",
    )(a, w, x)


def _ab_sample_body(m0_ref, z_ref, cos_ref, sin_ref, s0_ref, c0_ref, n0_ref, conv0_ref,
                    convw_ref, convb_ref, gbias_ref, gng_ref, gnb_ref, hng_ref,
                    y_ref, s_ref, c_ref, n_ref, m_ref, conv_ref, *, g_dec):
    b = pl.program_id(0)
    hw = RET_HEADS * HEAD_D
    qkw = 2 * ML_HEADS * HEAD_D
    tail = CONV_W - 1
    cosf = cos_ref[...]
    sinf = sin_ref[...]
    row = lax.broadcasted_iota(jnp.int32, (HEAD_D, HEAD_D), 0)
    col = lax.broadcasted_iota(jnp.int32, (HEAD_D, HEAD_D), 1)
    eye = row == col

    def to_col(r):
        return jnp.sum(jnp.where(eye, r, 0.0), axis=1, keepdims=True)

    def to_row(cv):
        return jnp.sum(jnp.where(eye, cv, 0.0), axis=0, keepdims=True)

    mqk = z_ref[0, :, 4 * hw:4 * hw + qkw]
    conv = convb_ref[...] + mqk * convw_ref[tail:CONV_W, :]
    for w in range(tail):
        conv = conv + conv0_ref[0, w:w + 1, :] * convw_ref[w:w + 1, :]
    qk = _silu(conv)
    conv_ref[0, 0:tail - 1, :] = conv0_ref[0, 1:tail, :]
    conv_ref[0, tail - 1:tail, :] = mqk
    gz = z_ref[0, :, AB_IN_PAD - LANES:AB_IN_PAD] + gbias_ref[...]

    for h in range(RET_HEADS):
        sl = slice(h * HEAD_D, (h + 1) * HEAD_D)
        q = _rope(z_ref[0, :, sl], cosf, sinf)
        k = _rope(z_ref[0, :, hw + h * HEAD_D:hw + (h + 1) * HEAD_D], cosf, sinf) * (HEAD_D ** -0.5)
        v = z_ref[0, :, 2 * hw + h * HEAD_D:2 * hw + (h + 1) * HEAD_D]
        rg = z_ref[0, :, 3 * hw + h * HEAD_D:3 * hw + (h + 1) * HEAD_D]
        s_old = s0_ref[0, h]
        qk_s = jnp.sum(q * k, axis=1, keepdims=True)
        o = qk_s * v + g_dec[h] * jnp.sum(_r16(to_col(q)) * _r16(s_old), axis=0, keepdims=True)
        s_ref[0, h] = g_dec[h] * s_old + to_col(k) * v
        mu = jnp.mean(o, axis=-1, keepdims=True)
        var = jnp.mean(jnp.square(o - mu), axis=-1, keepdims=True)
        o = (o - mu) * lax.rsqrt(var + EPS) * gng_ref[:, sl] + gnb_ref[:, sl]
        y_ref[0, :, sl] = _silu(rg) * o

        mq = qk[:, sl]
        mk = qk[:, ML_HEADS * HEAD_D + h * HEAD_D:ML_HEADS * HEAD_D + (h + 1) * HEAD_D] * (HEAD_D ** -0.5)
        mv = z_ref[0, :, 4 * hw + qkw + h * HEAD_D:4 * hw + qkw + (h + 1) * HEAD_D]
        mo = z_ref[0, :, 5 * hw + qkw + h * HEAD_D:5 * hw + qkw + (h + 1) * HEAD_D]
        ig = gz[:, h:h + 1]
        lf = _log_sigmoid(gz[:, ML_HEADS + h:ML_HEADS + h + 1])
        m_old = m0_ref[b, h]
        inter = lf + m_old
        m_t = jnp.maximum(inter, ig)
        wgt = jnp.sum(mq * mk, axis=1, keepdims=True) * jnp.exp(ig - m_t)
        e_inter = jnp.exp(inter - m_t)
        c_old = c0_ref[0, h]
        n_old = n0_ref[0, h:h + 1, :]
        cq = to_row(jnp.sum(_r16(c_old) * _r16(mq), axis=1, keepdims=True))
        num = wgt * mv + e_inter * cq
        den = wgt + e_inter * jnp.sum(n_old * mq, axis=1, keepdims=True)
        hc = num / jnp.maximum(jnp.abs(den), jnp.exp(-m_t))
        ws = jnp.exp(ig - m_t)
        c_ref[0, h] = e_inter * c_old + (ws * to_col(mv)) * mk
        n_ref[0, h:h + 1, :] = e_inter * n_old + ws * mk
        m_ref[0, h:h + 1, :] = jnp.broadcast_to(m_t, (1, LANES))
        hm = _sigmoid(mo) * hc
        hm = hm * lax.rsqrt(jnp.mean(hm * hm, axis=-1, keepdims=True) + EPS) * hng_ref[:, sl]
        y_ref[0, :, hw + h * HEAD_D:hw + (h + 1) * HEAD_D] = hm
    m_ref[0, ML_HEADS:SUBLANES, :] = jnp.zeros((SUBLANES - ML_HEADS, LANES), F32)


def ab_sample(z, pos, s0, c0, n0, m0, conv0, conv_w, conv_b, b_ig, b_fg, gn_g, gn_b, hn_g):
    batch = z.shape[0]
    h = np.arange(RET_HEADS, dtype=np.float64)
    g_dec = [float(v) for v in np.exp(np.log1p(-np.exp2(-5.0 - h)))]
    cosf, sinf = _rope_tables(pos)
    gbias = jnp.zeros((1, LANES), F32).at[0, :ML_HEADS].set(b_ig).at[0, ML_HEADS:2 * ML_HEADS].set(b_fg)
    hw = RET_HEADS * HEAD_D
    qkw = 2 * ML_HEADS * HEAD_D

    def per_b(shape):
        return pl.BlockSpec((1,) + shape, lambda b: (b,) + (0,) * len(shape))

    def const(shape):
        return pl.BlockSpec(shape, lambda b: (0,) * len(shape))

    in_specs = [pl.BlockSpec(memory_space=pltpu.SMEM), per_b((1, AB_IN_PAD)), const((1, HEAD_D)), const((1, HEAD_D)),
                per_b((RET_HEADS, HEAD_D, HEAD_D)), per_b((ML_HEADS, HEAD_D, HEAD_D)), per_b((ML_HEADS, HEAD_D)),
                per_b((CONV_W - 1, qkw)), const((CONV_W, qkw)), const((1, qkw)), const((1, LANES)),
                const((1, hw)), const((1, hw)), const((1, hw))]
    out_shape = (jax.ShapeDtypeStruct((batch, 1, 2 * hw), F32),
                 jax.ShapeDtypeStruct((batch, RET_HEADS, HEAD_D, HEAD_D), F32),
                 jax.ShapeDtypeStruct((batch, ML_HEADS, HEAD_D, HEAD_D), F32),
                 jax.ShapeDtypeStruct((batch, ML_HEADS, HEAD_D), F32),
                 jax.ShapeDtypeStruct((batch, SUBLANES, LANES), F32),
                 jax.ShapeDtypeStruct((batch, CONV_W - 1, qkw), F32))
    out_specs = (per_b((1, 2 * hw)), per_b((RET_HEADS, HEAD_D, HEAD_D)), per_b((ML_HEADS, HEAD_D, HEAD_D)),
                 per_b((ML_HEADS, HEAD_D)), per_b((SUBLANES, LANES)), per_b((CONV_W - 1, qkw)))
    y, s, cc, n, m, conv = pl.pallas_call(
        functools.partial(_ab_sample_body, g_dec=g_dec),
        out_shape=out_shape, grid=(batch,), in_specs=in_specs, out_specs=out_specs,
        compiler_params=_params(1), name="ab_sample",
    )(m0, z.reshape(batch, 1, AB_IN_PAD), cosf, sinf, s0, c0, n0, conv0, conv_w, conv_b.reshape(1, qkw), gbias,
      gn_g.reshape(1, hw), gn_b.reshape(1, hw), hn_g.reshape(1, hw))
    return y.reshape(batch, 2 * hw), s, cc, n, m[:, :ML_HEADS, 0], conv


MOE_TILE = 256


def _moe_router_body(x_ref, g_ref, wr_ref, br_ref, hn_ref, route_ref, count_ref):
    x = x_ref[...]
    hn = x * lax.rsqrt(jnp.mean(x * x, axis=-1, keepdims=True) + EPS) * g_ref[...]
    hn_ref[...] = hn
    z = _dot(hn, wr_ref[...]) + br_ref[...]
    lane = lax.broadcasted_iota(jnp.int32, z.shape, 1)
    lanef = lane.astype(F32)
    is_group = lane < MOE_GROUPS
    gl = jnp.where(is_group, z, -jnp.inf)
    gmax = jnp.max(gl, axis=1, keepdims=True)
    g_top = jnp.min(jnp.where(gl == gmax, lanef, float(LANES)), axis=1, keepdims=True)
    pg_top = 1.0 / jnp.sum(jnp.where(is_group, jnp.exp(z - gmax), 0.0), axis=1, keepdims=True)
    grp = ((lane - MOE_GROUPS) // MOE_EXP_PER_GROUP).astype(F32)
    in_group = (lane >= MOE_GROUPS) & (lane < MOE_GROUPS + MOE_EXPERTS) & (grp == g_top)
    el = jnp.where(in_group, z, -jnp.inf)
    v1 = jnp.max(el, axis=1, keepdims=True)
    i1 = jnp.min(jnp.where(el == v1, lanef, float(LANES)), axis=1, keepdims=True)
    el2 = jnp.where(lanef == i1, -jnp.inf, el)
    v2 = jnp.max(el2, axis=1, keepdims=True)
    i2 = jnp.min(jnp.where(el2 == v2, lanef, float(LANES)), axis=1, keepdims=True)
    t = jnp.exp(v2 - v1)
    p1 = 1.0 / (1.0 + t)
    out = jnp.where(lane == 0, i1 - MOE_GROUPS,
                    jnp.where(lane == 1, i2 - MOE_GROUPS,
                              jnp.where(lane == 2, pg_top * p1,
                                        jnp.where(lane == 3, pg_top * (t * p1), 0.0))))
    route_ref[...] = out
    picked = jnp.where((lanef == i1 - MOE_GROUPS) | (lanef == i2 - MOE_GROUPS), 1.0, 0.0)

    @pl.when(pl.program_id(0) == 0)
    def _():
        count_ref[...] = jnp.zeros_like(count_ref)

    count_ref[...] += jnp.sum(picked, axis=0, keepdims=True)


def moe_router(x, g, w_group, b_group, w_expert, b_expert, row_tile=256):
    n, d = x.shape
    tm = min(row_tile, n)
    used = MOE_GROUPS + MOE_EXPERTS
    wr = jnp.pad(jnp.concatenate([w_group, w_expert], axis=1), ((0, 0), (0, LANES - used)))
    br = jnp.pad(jnp.concatenate([b_group, b_expert]), (0, LANES - used)).reshape(1, LANES)
    hn, route, count = pl.pallas_call(
        _moe_router_body,
        out_shape=(jax.ShapeDtypeStruct((n, d), F32), jax.ShapeDtypeStruct((n, LANES), F32),
                   jax.ShapeDtypeStruct((1, LANES), F32)),
        grid=(n // tm,),
        in_specs=[pl.BlockSpec((tm, d), lambda i: (i, 0)), pl.BlockSpec((1, d), lambda i: (0, 0)),
                  pl.BlockSpec((d, LANES), lambda i: (0, 0)), pl.BlockSpec((1, LANES), lambda i: (0, 0))],
        out_specs=(pl.BlockSpec((tm, d), lambda i: (i, 0)), pl.BlockSpec((tm, LANES), lambda i: (i, 0)),
                   pl.BlockSpec((1, LANES), lambda i: (0, 0))),
        compiler_params=_params(1), name="moe_router",
    )(x, g.reshape(1, d), wr, br)
    return hn, route, route[:, 0:2].astype(jnp.int32), count[0, :MOE_EXPERTS].astype(jnp.int32)


def _moe_ffn_body(blk_e_ref, n_real_ref, asg_ref, hn_ref, wg_ref, wu_ref, wd_ref, o_ref,
                  x_s, y_s, wg_s, wu_s, wd_s, gsem, ssem, *, n_blk):
    i = pl.program_id(0)
    slot = i % 2

    def gather_row(tile, r, slot):
        tok = lax.shift_right_logical(asg_ref[tile * MOE_TILE + r], 1)
        return pltpu.make_async_copy(hn_ref.at[pl.ds(tok, 1), :], x_s.at[slot, pl.ds(r, 1), :], gsem.at[slot])

    def scatter_row(r):
        a = asg_ref[i * MOE_TILE + r]
        return pltpu.make_async_copy(y_s.at[pl.ds(r, 1), :],
                                     o_ref.at[a & 1, pl.ds(lax.shift_right_logical(a, 1), 1), :], ssem.at[0])

    def for_rows(n_rows, fn):
        @pl.when(n_rows == MOE_TILE)
        def _():
            for r in range(MOE_TILE):
                fn(r)

        @pl.when(n_rows < MOE_TILE)
        def _():
            def body(r, c):
                fn(r)
                return c
            lax.fori_loop(0, n_rows, body, 0)

    def start_gather(tile, slot):
        for_rows(n_real_ref[tile], lambda r: gather_row(tile, r, slot).start())

    @pl.when(i == 0)
    def _():
        x_s[...] = jnp.zeros_like(x_s)
        start_gather(0, 0)

    for s in range(2):
        @pl.when((i + 1 < n_blk) & (slot == 1 - s))
        def _(s=s):
            start_gather(i + 1, s)

    prev = blk_e_ref[jnp.maximum(i - 1, 0)]

    @pl.when((i == 0) | (blk_e_ref[i] != prev))
    def _():
        wg_s[...] = wg_ref[0].astype(BF16)
        wu_s[...] = wu_ref[0].astype(BF16)
        wd_s[...] = wd_ref[0].astype(BF16)

    n_real = n_real_ref[i]
    for_rows(n_real, lambda r: gather_row(i, 0, slot).wait())
    n_prev = jnp.where(i > 0, n_real_ref[jnp.maximum(i - 1, 0)], 0)
    for_rows(n_prev, lambda r: scatter_row(0).wait())

    @pl.when(n_real > 0)
    def _():
        x = x_s[slot].astype(BF16)
        hg = jnp.dot(x, wg_s[...], preferred_element_type=F32)
        hu = jnp.dot(x, wu_s[...], preferred_element_type=F32)
        hb = (_silu(hg) * hu).astype(BF16)
        y_s[...] = jnp.dot(hb, wd_s[...], preferred_element_type=F32)
        for_rows(n_real, lambda r: scatter_row(r).start())

    @pl.when(i == n_blk - 1)
    def _():
        for_rows(n_real, lambda r: scatter_row(0).wait())


def moe_ffn(hn, asg, n_real, blk_e, layer, w_g, w_u, w_d):
    n_tok, d = hn.shape
    ff = w_g.shape[3]
    n_blk = blk_e.shape[0]
    grid_spec = pltpu.PrefetchScalarGridSpec(
        num_scalar_prefetch=3, grid=(n_blk,),
        in_specs=[pl.BlockSpec(memory_space=pl.ANY),
                  pl.BlockSpec((None, 1, d, ff), lambda i, e, nr, a: (layer, e[i], 0, 0)),
                  pl.BlockSpec((None, 1, d, ff), lambda i, e, nr, a: (layer, e[i], 0, 0)),
                  pl.BlockSpec((None, 1, ff, d), lambda i, e, nr, a: (layer, e[i], 0, 0))],
        out_specs=pl.BlockSpec(memory_space=pl.ANY),
        scratch_shapes=[pltpu.VMEM((2, MOE_TILE, d), F32), pltpu.VMEM((MOE_TILE, d), F32),
                        pltpu.VMEM((d, ff), BF16), pltpu.VMEM((d, ff), BF16), pltpu.VMEM((ff, d), BF16),
                        pltpu.SemaphoreType.DMA((2,)), pltpu.SemaphoreType.DMA((1,))])
    return pl.pallas_call(
        functools.partial(_moe_ffn_body, n_blk=n_blk),
        out_shape=jax.ShapeDtypeStruct((2, n_tok, d), F32), grid_spec=grid_spec,
        compiler_params=_params(1), name="moe_ffn",
    )(blk_e, n_real, asg, hn, w_g, w_u, w_d)


def _moe_combine_body(x_ref, route_ref, y0_ref, y1_ref, o_ref):
    o_ref[...] = x_ref[...] + (y0_ref[...] * route_ref[:, 2:3] + y1_ref[...] * route_ref[:, 3:4])


def moe_combine(x, route, y, row_tile=512):
    n, d = x.shape
    tm = min(row_tile, n)
    return pl.pallas_call(
        _moe_combine_body, out_shape=jax.ShapeDtypeStruct((n, d), F32), grid=(n // tm,),
        in_specs=[pl.BlockSpec((tm, d), lambda i: (i, 0)), pl.BlockSpec((tm, LANES), lambda i: (i, 0)),
                  pl.BlockSpec((None, tm, d), lambda i: (0, i, 0)), pl.BlockSpec((None, tm, d), lambda i: (1, i, 0))],
        out_specs=pl.BlockSpec((tm, d), lambda i: (i, 0)),
        compiler_params=_params(1), name="moe_combine",
    )(x, route, y, y)


def hier_moe_residual(x, g, w_group, b_group, w_expert, b_expert, layer, w_g, w_u, w_d):
    n, d = x.shape
    hn, route, expert, counts = moe_router(x, g, w_group, b_group, w_expert, b_expert)
    n_exp = w_g.shape[1]
    kk = expert.shape[1]
    assert kk == 2
    a = n * kk
    order = jnp.argsort(expert.reshape(-1)).astype(jnp.int32)
    starts = jnp.cumsum(counts) - counts
    padded = (counts + MOE_TILE - 1) // MOE_TILE * MOE_TILE
    pend = jnp.cumsum(padded)
    pstart = pend - padded
    n_blk = -(-(a + n_exp * (MOE_TILE - 1)) // MOE_TILE)
    tile_start = jnp.arange(n_blk, dtype=jnp.int32) * MOE_TILE
    blk_e = jnp.minimum(jnp.sum((pend[None, :] <= tile_start[:, None]).astype(jnp.int32), axis=1), n_exp - 1)
    n_real = jnp.clip(counts[blk_e] - (tile_start - pstart[blk_e]), 0, MOE_TILE).astype(jnp.int32)
    row_off = (tile_start - pstart[blk_e] + starts[blk_e])[:, None] + jnp.arange(MOE_TILE, dtype=jnp.int32)[None, :]
    asg = order[jnp.clip(row_off, 0, a - 1)].reshape(-1)
    y = moe_ffn(hn, asg, n_real, blk_e, layer, w_g, w_u, w_d)
    return moe_combine(x, route, y)


KV_W = 2 * NSA_KV_HEADS * NSA_HD
Q_W = NSA_HEADS * NSA_HD
MASK_BUCKET = REL_BUCKETS
KEY_TILE = 1024


def _t5_thresholds():
    exact = REL_BUCKETS // 2
    dist = np.arange(0, 4 * REL_MAX_DIST, dtype=np.int64)
    nf = np.maximum(dist, 1).astype(np.float64)
    large = exact + np.floor(np.log(nf / exact) / math.log(REL_MAX_DIST / exact) * (REL_BUCKETS - exact) + 1e-9)
    bucket = np.where(dist < exact, dist, np.minimum(large, REL_BUCKETS - 1)).astype(np.int64)
    return [int(np.argmax(bucket >= b)) for b in range(1, REL_BUCKETS)]


def _bucket_index(dist, valid):
    idx = jnp.zeros(dist.shape, jnp.int32)
    for thr in _t5_thresholds():
        idx = idx + (dist >= thr).astype(jnp.int32)
    return jnp.where(valid, idx, MASK_BUCKET)


def _bias_table(rel_bias):
    t = jnp.zeros((NSA_HEADS, LANES), F32).at[:, :REL_BUCKETS].set(rel_bias.T.astype(F32))
    return t.at[:, MASK_BUCKET].set(NEG)


def _group_mean_matrix():
    i = np.arange(LANES)
    return jnp.asarray((i[:, None] // NSA_HD == i[None, :] // NSA_HD) / NSA_HD, F32)


def _nsa_prep_body(zq_ref, zc_ref, zs_ref, zw_ref, zg_ref, bd_ref, qn_ref, kns_ref, knw_ref,
                   q_ref, cmp_ref, sel_ref, win_ref, selb_ref, winb_ref, gate_ref, *kt_ref, transposed):
    bd = bd_ref[...]

    def head_norm(x, gain):
        ms = _dot_f32(x * x, bd)
        return x * lax.rsqrt(ms + EPS) * gain

    def emit(o_ref, rows):
        if transposed:
            o_ref[0] = rows.T
        else:
            o_ref[...] = rows

    for c in range(Q_W // LANES):
        sl = slice(c * LANES, (c + 1) * LANES)
        q_ref[:, sl] = (head_norm(zq_ref[:, sl], qn_ref[...]) * (NSA_HD ** -0.5)).astype(BF16)
    emit(cmp_ref, zc_ref[...])
    half = KV_W // 2
    for z_ref, kn_ref, o_ref, ob_ref in ((zs_ref, kns_ref, sel_ref, selb_ref), (zw_ref, knw_ref, win_ref, winb_ref)):
        kn = jnp.concatenate([head_norm(z_ref[:, c * LANES:(c + 1) * LANES], kn_ref[...])
                              for c in range(half // LANES)], axis=1)
        rows = jnp.concatenate([kn, z_ref[:, half:KV_W]], axis=1)
        emit(o_ref, rows)
        ob_ref[...] = rows.astype(BF16)
        if transposed and o_ref is sel_ref:
            kt_ref[0][0] = kn.T.astype(BF16)
    gate_ref[...] = _sigmoid(zg_ref[...])


def nsa_prep(z, q_norm, k_norm, batch=None, row_tile=256):
    n = z.shape[0]
    tm = min(row_tile, n)
    transposed = batch is not None

    def zspec(width, blk):
        return pl.BlockSpec((tm, width), lambda i, blk=blk: (i, blk))

    def const(shape):
        return pl.BlockSpec(shape, lambda i: (0,) * len(shape))

    def tile2(v):
        return jnp.concatenate([v, v]).reshape(1, LANES).astype(F32)

    def rows(w, dt):
        return jax.ShapeDtypeStruct((n, w), dt)

    def out_spec(w):
        return pl.BlockSpec((tm, w), lambda i: (i, 0))

    if transposed:
        seq = n // batch
        tps = seq // tm
        kv_shape = jax.ShapeDtypeStruct((batch, KV_W, seq), F32)
        kv_spec = pl.BlockSpec((1, KV_W, tm), lambda i: (i // tps, 0, i % tps))
        extra_shape = (jax.ShapeDtypeStruct((batch, KV_W // 2, seq), BF16),)
        extra_spec = (pl.BlockSpec((1, KV_W // 2, tm), lambda i: (i // tps, 0, i % tps)),)
    else:
        kv_shape, kv_spec, extra_shape, extra_spec = rows(KV_W, F32), out_spec(KV_W), (), ()
    return pl.pallas_call(
        functools.partial(_nsa_prep_body, transposed=transposed),
        out_shape=(rows(Q_W, BF16), kv_shape, kv_shape, kv_shape, rows(KV_W, BF16), rows(KV_W, BF16),
                   rows(LANES, F32)) + extra_shape,
        grid=(n // tm,),
        in_specs=[zspec(Q_W, 0), zspec(KV_W, 2), zspec(KV_W, 3), zspec(KV_W, 4),
                  zspec(LANES, (Q_W + 3 * KV_W) // LANES), const((LANES, LANES)),
                  const((1, LANES)), const((1, LANES)), const((1, LANES))],
        out_specs=(out_spec(Q_W), kv_spec, kv_spec, kv_spec, out_spec(KV_W), out_spec(KV_W),
                   out_spec(LANES)) + extra_spec,
        compiler_params=_params(1), name="nsa_prep",
    )(z, z, z, z, z, _group_mean_matrix(), tile2(q_norm), tile2(k_norm[1]), tile2(k_norm[2]))


PAGES_PER_STEP = 32
PAGE_ROWS = 128
SUBS_PER_PAGE = PAGE_ROWS // CMP_STRIDE
P_W = 2 * NSA_KV_HEADS * 2 * CMP_HID
KV_PAIRS = NSA_KV_HEADS // 2


def _gelu_tanh(x):
    return 0.5 * x * (1.0 + jnp.tanh(math.sqrt(2.0 / math.pi) * (x + 0.044715 * x * x * x)))


def _compress_body(pt_ref, *refs, n_steps):
    page_refs = refs[:PAGES_PER_STEP]
    w1_ref, b1_ref, w2_ref, b2_ref, kn_ref, kc_ref, vc_ref, p_ref, x_s = refs[PAGES_PER_STEP:]
    j = pl.program_id(1)
    rows = PAGES_PER_STEP * SUBS_PER_PAGE
    r0 = pl.multiple_of(j * rows, rows)
    for t, pr in enumerate(page_refs):
        for v in range(2):
            for gp in range(KV_PAIRS):
                x_s[t, v * KV_PAIRS + gp] = pr[0, v, 2 * gp:2 * gp + 2].reshape(2 * NSA_HD, PAGE_ROWS).T
    for v in range(2):
        for gp in range(KV_PAIRS):
            acc = jnp.zeros((rows, 4 * CMP_HID), F32)
            for s in range(CMP_STRIDE):
                x = jnp.concatenate([x_s[t, v * KV_PAIRS + gp, pl.ds(s, SUBS_PER_PAGE, stride=CMP_STRIDE), :]
                                     for t in range(PAGES_PER_STEP)], axis=0)
                acc = acc + jnp.dot(x.astype(BF16), w1_ref[v, s], preferred_element_type=F32)
            p0 = (v * NSA_KV_HEADS + 2 * gp) * 2 * CMP_HID
            p_ref[pl.ds(r0, rows), p0:p0 + 4 * CMP_HID] = acc

    @pl.when(j == n_steps - 1)
    def _():
        n_sub = p_ref.shape[0]
        for v in range(2):
            for g in range(NSA_KV_HEADS):
                p0 = (v * NSA_KV_HEADS + g) * 2 * CMP_HID
                hs = p_ref[:, p0:p0 + CMP_HID] + pltpu.roll(p_ref[:, p0 + CMP_HID:p0 + 2 * CMP_HID], n_sub - 1, 0)
                hid = _gelu_tanh(hs + b1_ref[v:v + 1, :])
                out = _dot(hid, w2_ref[v]) + b2_ref[v:v + 1, :]
                if v == 0:
                    out = out * lax.rsqrt(jnp.mean(out * out, axis=-1, keepdims=True) + EPS) * kn_ref[...]
                    kc_ref[0, :, g * NSA_HD:(g + 1) * NSA_HD] = out
                else:
                    vc_ref[0, :, g * NSA_HD:(g + 1) * NSA_HD] = out


def nsa_compress(rows_t, page_table, w1, b1, w2, b2, k_norm0):
    if page_table is None:
        batch, n_pp = rows_t.shape[0], rows_t.shape[-1] // PAGE_ROWS
        page_table = jnp.zeros((1, 1), jnp.int32)

        def page_index(b, p, pt):
            return (b, 0, 0, 0, p)
    else:
        batch, n_pp = page_table.shape

        def page_index(b, p, pt):
            return (pt[b, p], 0, 0, 0, 0)
    n_steps = n_pp // PAGES_PER_STEP
    n_sub = n_pp * SUBS_PER_PAGE
    w = w1.reshape(2, 2, CMP_STRIDE, NSA_HD, CMP_HID)
    w = jnp.transpose(w, (0, 2, 3, 1, 4)).reshape(2, CMP_STRIDE, NSA_HD, 2 * CMP_HID)
    zero = jnp.zeros_like(w)
    wpair = jnp.concatenate([jnp.concatenate([w, zero], axis=-1), jnp.concatenate([zero, w], axis=-1)], axis=2)
    wpair = wpair.astype(BF16)

    def page_spec(t):
        return pl.BlockSpec((1, 2, NSA_KV_HEADS, NSA_HD, PAGE_ROWS),
                            lambda b, j, pt, t=t: page_index(b, j * PAGES_PER_STEP + t, pt))

    def const(shape):
        return pl.BlockSpec(shape, lambda b, j, pt: (0,) * len(shape))

    grid_spec = pltpu.PrefetchScalarGridSpec(
        num_scalar_prefetch=1, grid=(batch, n_steps),
        in_specs=[page_spec(t) for t in range(PAGES_PER_STEP)] + [
            const((2, CMP_STRIDE, LANES, 4 * CMP_HID)), const((2, CMP_HID)), const((2, CMP_HID, NSA_HD)),
            const((2, NSA_HD)), const((1, NSA_HD))],
        out_specs=(pl.BlockSpec((1, n_sub, KV_W // 2), lambda b, j, pt: (b, 0, 0)),
                   pl.BlockSpec((1, n_sub, KV_W // 2), lambda b, j, pt: (b, 0, 0))),
        scratch_shapes=[pltpu.VMEM((n_sub, P_W), F32),
                        pltpu.VMEM((PAGES_PER_STEP, 2 * KV_PAIRS, PAGE_ROWS, 2 * NSA_HD), F32)])
    return pl.pallas_call(
        functools.partial(_compress_body, n_steps=n_steps),
        out_shape=(jax.ShapeDtypeStruct((batch, n_sub, KV_W // 2), F32),
                   jax.ShapeDtypeStruct((batch, n_sub, KV_W // 2), F32)),
        grid_spec=grid_spec, compiler_params=_params(2), name="nsa_compress",
    )(page_table, *([rows_t] * PAGES_PER_STEP), wpair, b1, w2.astype(BF16), b2, k_norm0.reshape(1, NSA_HD))


Q_TILE = 128
WIN_BLOCKS = (WINDOW + Q_TILE) // Q_TILE
NEAR_TILES = 2
NEAR_CHUNKS = 8
M_INIT = -1e29


def _gather_bias(tbh, idx_slices):
    return jnp.concatenate([jnp.take_along_axis(tbh, idx, axis=1) for idx in idx_slices], axis=1)


def _nsa_attend_body(q_ref, gate_ref, x_ref, kc_ref, vc_ref, kst_ref, vs_ref, w0_ref, w1_ref, w2_ref, w3_ref, w4_ref,
                     idxc_ref, idxw_ref, nb_ref, mmat_ref, tb_ref, wout_ref, expand_ref, y_ref,
                     pc_s, oc_s, sel_s, m_s, l_s, a_s, acc_s, ps_s, pw_s, o_s, bsel_s, bwin_s, *, n_sel):
    win_refs = (w0_ref, w1_ref, w2_ref, w3_ref, w4_ref)
    t = pl.program_id(1)
    s0 = t * Q_TILE
    n_sub = kc_ref.shape[1]
    rep_rows = NSA_REP * Q_TILE
    q_pos = s0 + lax.broadcasted_iota(jnp.int32, (Q_TILE, LANES), 0)
    blk = lax.broadcasted_iota(jnp.int32, (Q_TILE, LANES), 1)
    cur = q_pos // SEL_BLOCK
    forced = (blk == 0) | (blk == cur) | (blk == cur - 1)
    future = blk * SEL_BLOCK > q_pos

    def group_q(g):
        return jnp.concatenate([q_ref[:, (g * NSA_REP + r) * NSA_HD:(g * NSA_REP + r + 1) * NSA_HD]
                                for r in range(NSA_REP)], axis=0)

    def head_table(h):
        return jnp.broadcast_to(tb_ref[h:h + 1, :], (Q_TILE, LANES))

    @pl.when((pl.program_id(0) == 0) & (t == 0))
    def _():
        for h in range(NSA_HEADS):
            tbh = head_table(h)
            far_bias = tb_ref[h:h + 1, REL_BUCKETS - 1:REL_BUCKETS]
            for k in range(NEAR_CHUNKS):
                bsel_s[h, k] = jnp.take_along_axis(tbh, nb_ref[k], axis=1) - far_bias
            bsel_s[h, NEAR_CHUNKS] = jnp.zeros((Q_TILE, LANES), F32)
            bsel_s[h, NEAR_CHUNKS + 1] = jnp.full((Q_TILE, LANES), NEG, F32)
            bwin_s[h] = _gather_bias(tbh, [idxw_ref[:, c * LANES:(c + 1) * LANES] for c in range(WIN_BLOCKS)])

    score_t = []
    for g in range(NSA_KV_HEADS):
        gs = slice(g * NSA_HD, (g + 1) * NSA_HD)
        sc = _dot_nt(group_q(g), kc_ref[0, :, gs])
        imp = jnp.zeros((Q_TILE, n_sub), F32)
        for r in range(NSA_REP):
            rs = slice(r * Q_TILE, (r + 1) * Q_TILE)
            bias = _gather_bias(head_table(g * NSA_REP + r),
                                [idxc_ref[0, :, c * LANES:(c + 1) * LANES] for c in range(n_sub // LANES)])
            s_r = sc[rs] + bias
            m = jnp.maximum(jnp.max(s_r, axis=1, keepdims=True), M_INIT)
            e = jnp.exp(s_r - m)
            p = e / jnp.maximum(jnp.sum(e, axis=1, keepdims=True), 1e-30)
            imp = imp + p
            pc_s[rs, :] = p.astype(BF16)
        oc_s[g] = jnp.dot(pc_s[...], vc_ref[0, :, gs], preferred_element_type=F32)
        score = _dot_f32(imp, mmat_ref[...])
        score = jnp.where(forced, FORCE_SCORE, score)
        score = jnp.where(future, NEG, score)
        score = jnp.where(blk >= n_sel, -jnp.inf, score)
        score_t.append(score.T)

    blk_t = lax.broadcasted_iota(jnp.int32, (LANES, Q_TILE), 0).astype(F32)
    sel_t = [jnp.zeros((LANES, Q_TILE), F32) for _ in range(NSA_KV_HEADS)]
    for _ in range(min(SEL_TOPK, n_sel)):
        for g in range(NSA_KV_HEADS):
            mx = jnp.max(score_t[g], axis=0, keepdims=True)
            first = jnp.min(jnp.where(score_t[g] == mx, blk_t, float(LANES)), axis=0, keepdims=True)
            pick = blk_t == first
            sel_t[g] = jnp.where(pick, 1.0, sel_t[g])
            score_t[g] = jnp.where(pick, -jnp.inf, score_t[g])
    for g in range(NSA_KV_HEADS):
        sel_s[g] = ((sel_t[g].T - 1.0) * (-NEG)).astype(BF16)

    n_kt = (s0 + Q_TILE + KEY_TILE - 1) // KEY_TILE
    n_far = jnp.maximum(n_kt - NEAR_TILES, 0)

    for pair in ((0, 1), (2, 3)):
        qgs = [group_q(g) for g in pair]
        m_s[...] = jnp.full(m_s.shape, M_INIT, F32)
        l_s[...] = jnp.zeros(l_s.shape, F32)
        acc_s[...] = jnp.zeros(acc_s.shape, F32)

        def key_tile(kt, near, pair=pair, qgs=qgs):
            k0 = pl.multiple_of(kt * KEY_TILE, KEY_TILE)
            for j, g in enumerate(pair):
                gs = slice(g * NSA_HD, (g + 1) * NSA_HD)
                s = jnp.dot(qgs[j], kst_ref[0, gs, pl.ds(k0, KEY_TILE)], preferred_element_type=F32)
                negm = jnp.dot(sel_s[g], expand_ref[:, pl.ds(k0, KEY_TILE)], preferred_element_type=F32)
                v = vs_ref[pl.ds(k0, KEY_TILE), gs]
                for r in range(NSA_REP):
                    rs = slice(r * Q_TILE, (r + 1) * Q_TILE)
                    h = g * NSA_REP + r
                    if near:
                        chunks = []
                        for c in range(KEY_TILE // LANES):
                            k = (s0 - k0) // LANES - c
                            chunks.append(bsel_s[h, jnp.where(k < 0, NEAR_CHUNKS + 1, jnp.minimum(k, NEAR_CHUNKS))])
                        s_r = s[rs] + (negm + jnp.concatenate(chunks, axis=1))
                    else:
                        s_r = s[rs] + negm
                    m_old = m_s[j, rs]
                    m_new = jnp.maximum(m_old, jnp.max(s_r, axis=1, keepdims=True))
                    alpha = jnp.exp(m_old - m_new)
                    p = jnp.exp(s_r - jnp.concatenate([m_new] * (KEY_TILE // LANES), axis=1))
                    l_s[j, rs] = alpha * l_s[j, rs] + jnp.sum(p, axis=1, keepdims=True)
                    m_s[j, rs] = m_new
                    acc_s[j, rs] = (alpha[:, :NSA_HD] * acc_s[j, rs]
                                    + jnp.dot(p.astype(BF16), v, preferred_element_type=F32))

        def far_body(kt, carry):
            key_tile(kt, False)
            return carry

        def near_body(kt, carry):
            key_tile(kt, True)
            return carry

        lax.fori_loop(0, n_far, far_body, 0)
        lax.fori_loop(n_far, n_kt, near_body, 0)

        for j, g in enumerate(pair):
            gs = slice(g * NSA_HD, (g + 1) * NSA_HD)
            kw = jnp.concatenate([wr[0, :, gs] for wr in win_refs], axis=0)
            vw = jnp.concatenate([wr[0, :, KV_W // 2 + g * NSA_HD:KV_W // 2 + (g + 1) * NSA_HD] for wr in win_refs],
                                 axis=0)
            sw = _dot_nt(qgs[j], kw)
            for r in range(NSA_REP):
                rs = slice(r * Q_TILE, (r + 1) * Q_TILE)
                s_r = sw[rs] + bwin_s[g * NSA_REP + r]
                e = jnp.exp(s_r - jnp.max(s_r, axis=1, keepdims=True))
                pw_s[rs, :] = (e / jnp.sum(e, axis=1, keepdims=True)).astype(BF16)
            o_w = jnp.dot(pw_s[...], vw, preferred_element_type=F32)

            for r in range(NSA_REP):
                rs = slice(r * Q_TILE, (r + 1) * Q_TILE)
                h = g * NSA_REP + r
                o_h = (gate_ref[:, 3 * h:3 * h + 1] * oc_s[g, rs, :]
                       + gate_ref[:, 3 * h + 1:3 * h + 2] * (acc_s[j, rs, :] / l_s[j, rs, :NSA_HD])
                       + gate_ref[:, 3 * h + 2:3 * h + 3] * o_w[rs])
                o_s[:, h * NSA_HD:(h + 1) * NSA_HD] = o_h.astype(BF16)

    y_ref[...] = x_ref[...] + jnp.dot(o_s[...], wout_ref[...], preferred_element_type=F32)


def nsa_attend_prompt(q, gates, x, k_c, v_c, kst, selb, winb, rel_bias, w_out, batch, seq):
    n_qt = seq // Q_TILE
    n_sub = k_c.shape[1]
    n_sel = seq // SEL_BLOCK
    assert CMP_LEN == 2 * CMP_STRIDE and SEL_BLOCK == 4 * CMP_STRIDE and SEL_TOPK <= n_sel <= LANES
    win_pad = jnp.pad(winb.reshape(batch, seq, KV_W), ((0, 0), (WINDOW, 0), (0, 0)))
    iq = jnp.arange(Q_TILE, dtype=jnp.int32)
    dist_c = (jnp.arange(n_qt, dtype=jnp.int32)[:, None, None] * Q_TILE + iq[None, :, None]
              - (jnp.arange(n_sub, dtype=jnp.int32)[None, None, :] * CMP_STRIDE + CMP_LEN - 1))
    idx_c = _bucket_index(dist_c, dist_c >= 0)
    dist_w = iq[:, None] - jnp.arange(WINDOW + Q_TILE, dtype=jnp.int32)[None, :] + WINDOW
    idx_w = _bucket_index(dist_w, (dist_w >= 0) & (dist_w < WINDOW))
    dist_n = (jnp.arange(NEAR_CHUNKS, dtype=jnp.int32)[:, None, None] * LANES + iq[None, :, None]
              - jnp.arange(LANES, dtype=jnp.int32)[None, None, :])
    nb = _bucket_index(dist_n, dist_n >= 0)
    ci = np.arange(n_sub)[:, None]
    bj = np.arange(LANES)[None, :]
    mmat = ((ci // 4 == bj).astype(np.float32) + ((ci + 1) // 4 == bj).astype(np.float32)) * (ci < n_sub - 1)
    expand = (np.arange(seq)[None, :] // SEL_BLOCK == np.arange(LANES)[:, None]).astype(np.float32)

    def rows(width):
        return pl.BlockSpec((Q_TILE, width), lambda b, t: (b * n_qt + t, 0))

    def per_b(shape):
        return pl.BlockSpec((1,) + shape, lambda b, t: (b,) + (0,) * len(shape))

    def const(shape):
        return pl.BlockSpec(shape, lambda b, t: (0,) * len(shape))

    in_specs = [rows(Q_W), rows(LANES), rows(D_MODEL), per_b((n_sub, KV_W // 2)), per_b((n_sub, KV_W // 2)),
                per_b((KV_W // 2, seq)), pl.BlockSpec((seq, KV_W // 2), lambda b, t: (b, 1))]
    in_specs += [pl.BlockSpec((1, Q_TILE, KV_W), lambda b, t, j=j: (b, t + j, 0)) for j in range(WIN_BLOCKS)]
    in_specs += [pl.BlockSpec((1, Q_TILE, n_sub), lambda b, t: (t, 0, 0)), const((Q_TILE, WINDOW + Q_TILE)),
                 const((NEAR_CHUNKS, Q_TILE, LANES)), const((n_sub, LANES)), const((NSA_HEADS, LANES)),
                 const((Q_W, D_MODEL)), const((LANES, seq))]
    rep_rows = NSA_REP * Q_TILE
    scratch = [pltpu.VMEM((rep_rows, n_sub), BF16), pltpu.VMEM((NSA_KV_HEADS, rep_rows, NSA_HD), F32),
               pltpu.VMEM((NSA_KV_HEADS, Q_TILE, LANES), BF16), pltpu.VMEM((2, rep_rows, LANES), F32),
               pltpu.VMEM((2, rep_rows, LANES), F32), pltpu.VMEM((rep_rows, LANES), F32),
               pltpu.VMEM((2, rep_rows, NSA_HD), F32),
               pltpu.VMEM((rep_rows, KEY_TILE), BF16), pltpu.VMEM((rep_rows, WINDOW + Q_TILE), BF16),
               pltpu.VMEM((Q_TILE, Q_W), BF16),
               pltpu.VMEM((NSA_HEADS, NEAR_CHUNKS + 2, Q_TILE, LANES), F32),
               pltpu.VMEM((NSA_HEADS, Q_TILE, WINDOW + Q_TILE), F32)]
    return pl.pallas_call(
        functools.partial(_nsa_attend_body, n_sel=n_sel),
        out_shape=jax.ShapeDtypeStruct((batch * seq, D_MODEL), F32), grid=(batch, n_qt),
        in_specs=in_specs, out_specs=rows(D_MODEL), scratch_shapes=scratch,
        compiler_params=_params(2), name="nsa_attend_prompt",
    )(q, gates, x, k_c.astype(BF16), v_c.astype(BF16), kst, selb, *([win_pad] * WIN_BLOCKS),
      idx_c, idx_w, nb, jnp.asarray(mmat), _bias_table(rel_bias), w_out.astype(BF16), jnp.asarray(expand, BF16))


SCORE_W = 384


def _nsa_sample_cmp_body(q_ref, kc_ref, vc_ref, idx_ref, mmat_ref, gsum_ref, tb_ref, oc_ref, top_ref, *,
                         n_sel, q_pos):
    n_sub = kc_ref.shape[1]
    q = q_ref[0]
    row_g = lax.broadcasted_iota(jnp.int32, (NSA_HEADS, 1), 0) // NSA_REP
    s = jnp.zeros((NSA_HEADS, n_sub), F32)
    for g in range(NSA_KV_HEADS):
        s = jnp.where(row_g == g, _dot_nt(q, kc_ref[0, :, g * NSA_HD:(g + 1) * NSA_HD]), s)
    tb = tb_ref[...]
    bias = jnp.concatenate([jnp.take_along_axis(
        tb, jnp.broadcast_to(idx_ref[:, c * LANES:(c + 1) * LANES], (NSA_HEADS, LANES)), axis=1)
        for c in range(n_sub // LANES)], axis=1)
    s = s + bias
    m = jnp.maximum(jnp.max(s, axis=1, keepdims=True), M_INIT)
    e = jnp.exp(s - m)
    p = e / jnp.maximum(jnp.sum(e, axis=1, keepdims=True), 1e-30)
    pb = p.astype(BF16)
    o = jnp.zeros((NSA_HEADS, NSA_HD), F32)
    for g in range(NSA_KV_HEADS):
        o = jnp.where(row_g == g, jnp.dot(pb, vc_ref[0, :, g * NSA_HD:(g + 1) * NSA_HD],
                                          preferred_element_type=F32), o)
    oc_ref[0] = o
    imp = _dot_f32(gsum_ref[...], p)
    score = _dot_f32(imp, mmat_ref[...])
    blk = lax.broadcasted_iota(jnp.int32, score.shape, 1)
    cur = q_pos // SEL_BLOCK
    score = jnp.where((blk == 0) | (blk == cur) | (blk == cur - 1), FORCE_SCORE, score)
    score = jnp.where(blk * SEL_BLOCK > q_pos, NEG, score)
    score = jnp.where(blk >= n_sel, -jnp.inf, score)
    blkf = blk.astype(F32)
    lane = lax.broadcasted_iota(jnp.int32, (SUBLANES, LANES), 1)
    top = jnp.zeros((SUBLANES, LANES), F32)
    for it in range(min(SEL_TOPK, n_sel)):
        mx = jnp.max(score, axis=1, keepdims=True)
        first = jnp.min(jnp.where(score == mx, blkf, float(SCORE_W)), axis=1, keepdims=True)
        top = jnp.where(lane == it, first, top)
        score = jnp.where(blkf == first, -jnp.inf, score)
    top_ref[0] = top.astype(jnp.int32)


def nsa_sample_cmp(q, k_c, v_c, rel_bias, past_len):
    batch = q.shape[0]
    n_sub = k_c.shape[1]
    n_sel = past_len // SEL_BLOCK + 1
    assert SEL_TOPK <= n_sel <= SCORE_W
    dist = past_len - (jnp.arange(n_sub, dtype=jnp.int32) * CMP_STRIDE + CMP_LEN - 1)
    idx = _bucket_index(dist, dist >= 0).reshape(1, n_sub)
    ci = np.arange(n_sub)[:, None]
    bj = np.arange(SCORE_W)[None, :]
    mmat = ((ci // 4 == bj).astype(np.float32) + ((ci + 1) // 4 == bj).astype(np.float32)) * (ci < n_sub - 1)
    gsum = (np.arange(SUBLANES)[:, None] == np.arange(NSA_HEADS)[None, :] // NSA_REP).astype(np.float32)

    def per_b(shape):
        return pl.BlockSpec((1,) + shape, lambda b: (b,) + (0,) * len(shape))

    def const(shape):
        return pl.BlockSpec(shape, lambda b: (0,) * len(shape))

    o_c, top = pl.pallas_call(
        functools.partial(_nsa_sample_cmp_body, n_sel=n_sel, q_pos=past_len),
        out_shape=(jax.ShapeDtypeStruct((batch, NSA_HEADS, NSA_HD), F32),
                   jax.ShapeDtypeStruct((batch, SUBLANES, LANES), jnp.int32)),
        grid=(batch,),
        in_specs=[per_b((NSA_HEADS, NSA_HD)), per_b((n_sub, KV_W // 2)), per_b((n_sub, KV_W // 2)),
                  const((1, n_sub)), const((n_sub, SCORE_W)), const((SUBLANES, NSA_HEADS)), const((NSA_HEADS, LANES))],
        out_specs=(per_b((NSA_HEADS, NSA_HD)), per_b((SUBLANES, LANES))),
        compiler_params=_params(1), name="nsa_sample_cmp",
    )(q, k_c.astype(BF16), v_c.astype(BF16), idx, jnp.asarray(mmat), jnp.asarray(gsum), _bias_table(rel_bias))
    return o_c, top[:, :NSA_KV_HEADS, :SEL_TOPK]


SEL_PER_STEP = 2


def _nsa_sample_attend_body(row_ref, sidx_ref, *refs, q_pos, n_past_blk, n_steps):
    n_blk_refs = NSA_KV_HEADS * SEL_PER_STEP
    blk_refs = refs[:n_blk_refs]
    (q_ref, knew_ref, win_ref, wnew_ref, gate_ref, oc_ref, idxw_ref, tb_ref, o_ref,
     s_s, v_s, ow_s) = refs[n_blk_refs:]
    b = pl.program_id(0)
    step = pl.program_id(1)
    q = q_ref[0]
    row_g = lax.broadcasted_iota(jnp.int32, (NSA_HEADS, 1), 0) // NSA_REP
    tb = tb_ref[...]
    half = KV_W // 2

    def by_group(fn):
        out = fn(0)
        for g in range(1, NSA_KV_HEADS):
            out = jnp.where(row_g == g, fn(g), out)
        return out

    @pl.when(step == 0)
    def _():
        s_w = by_group(lambda g: _dot(q, win_ref[0, 0, g]))
        bias = jnp.concatenate([jnp.take_along_axis(
            tb, jnp.broadcast_to(idxw_ref[:, c * LANES:(c + 1) * LANES], (NSA_HEADS, LANES)), axis=1)
            for c in range(s_w.shape[1] // LANES)], axis=1)
        s_w = s_w + bias
        k_new = by_group(lambda g: jnp.broadcast_to(wnew_ref[0, :, g * NSA_HD:(g + 1) * NSA_HD], (NSA_HEADS, NSA_HD)))
        v_new = by_group(lambda g: jnp.broadcast_to(wnew_ref[0, :, half + g * NSA_HD:half + (g + 1) * NSA_HD],
                                                    (NSA_HEADS, NSA_HD)))
        s_n = jnp.sum(q.astype(F32) * _r16(k_new), axis=1, keepdims=True) + tb[:, 0:1]
        m = jnp.maximum(jnp.max(s_w, axis=1, keepdims=True), s_n)
        e_w = jnp.exp(s_w - m)
        e_n = jnp.exp(s_n - m)
        total = jnp.sum(e_w, axis=1, keepdims=True) + e_n
        pb = (e_w / total).astype(BF16)
        pv = by_group(lambda g: _dot_nt(pb, win_ref[0, 1, g]))
        ow_s[...] = pv + _r16(e_n / total) * _r16(v_new)

    keys = SEL_PER_STEP * PAGE_ROWS
    key_lane = lax.broadcasted_iota(jnp.int32, (NSA_HD, PAGE_ROWS), 1)
    lane = lax.broadcasted_iota(jnp.int32, (1, PAGE_ROWS), 1)
    thresholds = _t5_thresholds()

    def tile_kv(g, kv):
        parts = []
        for j in range(SEL_PER_STEP):
            is_new = sidx_ref[b, g, step * SEL_PER_STEP + j] >= n_past_blk
            cached = blk_refs[g * SEL_PER_STEP + j][0, kv, 0]
            fresh = jnp.where(key_lane == 0, knew_ref[0, kv * half + g * NSA_HD:kv * half + (g + 1) * NSA_HD, :], 0.0)
            parts.append(jnp.where(is_new, fresh, cached))
        return jnp.concatenate(parts, axis=1).astype(BF16)

    def tile_bias(g):
        parts = []
        for j in range(SEL_PER_STEP):
            blk = sidx_ref[b, g, step * SEL_PER_STEP + j]
            is_new = blk >= n_past_blk
            base = jnp.where(is_new, blk * SEL_BLOCK, (blk * SEL_BLOCK) // PAGE_ROWS * PAGE_ROWS)
            pos = base + lane
            dist = q_pos - pos
            idx = jnp.zeros((1, PAGE_ROWS), jnp.int32)
            for thr in thresholds:
                idx = idx + (dist >= thr).astype(jnp.int32)
            parts.append(jnp.where((dist >= 0) & (pos // SEL_BLOCK == blk), idx, MASK_BUCKET))
        idx = jnp.concatenate(parts, axis=1)
        return jnp.concatenate([jnp.take_along_axis(
            tb, jnp.broadcast_to(idx[:, c * LANES:(c + 1) * LANES], (NSA_HEADS, LANES)), axis=1)
            for c in range(keys // LANES)], axis=1)

    k0 = pl.multiple_of(step * keys, keys)
    s_s[:, pl.ds(k0, keys)] = by_group(lambda g: jnp.dot(q, tile_kv(g, 0), preferred_element_type=F32)
                                       + tile_bias(g))
    for g in range(NSA_KV_HEADS):
        v_s[g, :, pl.ds(k0, keys)] = tile_kv(g, 1)

    @pl.when(step == n_steps - 1)
    def _():
        s = s_s[...]
        e = jnp.exp(s - jnp.max(s, axis=1, keepdims=True))
        pb = (e / jnp.sum(e, axis=1, keepdims=True)).astype(BF16)
        o_sel = by_group(lambda g: _dot_nt(pb, v_s[g]))
        gate = gate_ref[0]
        o_ref[0] = gate[:, 0:1] * oc_ref[0] + gate[:, 1:2] * o_sel + gate[:, 2:3] * ow_s[...]


def nsa_sample_attend(q, kv_sel_new, kv_win_new, gates, o_c, top, cache_sel, win_buf, page_table, rel_bias,
                      past_len):
    batch = q.shape[0]
    n_past_blk = past_len // SEL_BLOCK
    bpp = PAGE_ROWS // SEL_BLOCK
    n_steps = SEL_TOPK // SEL_PER_STEP
    page_shape = cache_sel.shape[1:]
    jp = jnp.minimum(top, n_past_blk - 1)
    phys = jnp.take_along_axis(page_table, (jp // bpp).reshape(batch, -1), axis=1).reshape(top.shape)
    phys = phys.astype(jnp.int32)
    wb = win_buf.shape[-1]
    dist_w = past_len - (past_len - wb + jnp.arange(wb, dtype=jnp.int32))
    idx_w = _bucket_index(dist_w, (dist_w >= 0) & (dist_w < WINDOW)).reshape(1, wb)
    gate3 = gates[:, :3 * NSA_HEADS].reshape(batch, NSA_HEADS, 3)

    def blk_spec(g, j):
        return pl.BlockSpec((1, page_shape[0], 1) + page_shape[2:],
                            lambda b, s, rows, sidx, g=g, j=j: (rows[b, g, s * SEL_PER_STEP + j], 0, g, 0, 0))

    def per_b(shape):
        return pl.BlockSpec((1,) + shape, lambda b, s, rows, sidx: (b,) + (0,) * len(shape))

    def const(shape):
        return pl.BlockSpec(shape, lambda b, s, rows, sidx: (0,) * len(shape))

    grid_spec = pltpu.PrefetchScalarGridSpec(
        num_scalar_prefetch=2, grid=(batch, n_steps),
        in_specs=[blk_spec(g, j) for g in range(NSA_KV_HEADS) for j in range(SEL_PER_STEP)] + [
            per_b((NSA_HEADS, NSA_HD)), per_b((KV_W, 1)), per_b(win_buf.shape[1:]), per_b((1, KV_W)),
            per_b((NSA_HEADS, 3)), per_b((NSA_HEADS, NSA_HD)), const((1, wb)), const((NSA_HEADS, LANES))],
        out_specs=per_b((NSA_HEADS, NSA_HD)),
        scratch_shapes=[pltpu.VMEM((NSA_HEADS, SEL_TOPK * PAGE_ROWS), F32),
                        pltpu.VMEM((NSA_KV_HEADS, NSA_HD, SEL_TOPK * PAGE_ROWS), BF16),
                        pltpu.VMEM((NSA_HEADS, NSA_HD), F32)])
    o = pl.pallas_call(
        functools.partial(_nsa_sample_attend_body, q_pos=past_len, n_past_blk=n_past_blk, n_steps=n_steps),
        out_shape=jax.ShapeDtypeStruct((batch, NSA_HEADS, NSA_HD), F32), grid_spec=grid_spec,
        compiler_params=_params(2), name="nsa_sample_attend",
    )(phys, top.astype(jnp.int32), *([cache_sel] * (NSA_KV_HEADS * SEL_PER_STEP)), q,
      kv_sel_new.reshape(batch, KV_W, 1), win_buf, kv_win_new.reshape(batch, 1, KV_W), gate3, o_c, idx_w,
      _bias_table(rel_bias))
    return o.reshape(batch, Q_W)


def _pad_cols(w, width):
    return jnp.pad(w, ((0, 0), (0, width - w.shape[1]))).astype(BF16)


def kernel(x_prompt, x_sample, state_ret, state_mlstm_C, state_mlstm_n, state_mlstm_m, state_conv, cache_nsa_cmp, cache_nsa_sel, state_nsa_win, page_table, rel_bias, norm_mix, norm_ffn, ab_w_in, ab_conv_w, ab_conv_b, ab_b_igate, ab_b_fgate, ab_gn_g, ab_gn_b, ab_hn_g, ab_w_out, nsa_w_in, nsa_q_norm, nsa_k_norm, nsa_cmp_w1, nsa_cmp_b1, nsa_cmp_w2, nsa_cmp_b2, nsa_w_out, moe_w_group, moe_b_group, moe_w_expert, moe_b_expert, moe_w_gate, moe_w_up, moe_w_down):
    bp, lp, d = x_prompt.shape
    bs, ls, _ = x_sample.shape
    page_size = cache_nsa_cmp.shape[2]
    past_len = page_table.shape[1] * page_size
    assert norm_mix.shape[0] == 2 and ls == 1 and d == D_MODEL and lp % KEY_TILE == 0
    xp = x_prompt.reshape(bp * lp, d)
    xs = x_sample.reshape(bs, d)

    def moe(x, layer):
        return hier_moe_residual(x, norm_ffn[layer], moe_w_group[layer], moe_b_group[layer], moe_w_expert[layer],
                                 moe_b_expert[layer], layer, moe_w_gate, moe_w_up, moe_w_down)

    w_in = _pad_cols(ab_w_in[0], AB_IN_PAD)
    ab = (ab_conv_w[0], ab_conv_b[0], ab_b_igate[0], ab_b_fgate[0], ab_gn_g[0], ab_gn_b[0], ab_hn_g[0])
    zp = norm_matmul(xp, norm_mix[0], w_in)
    xp, ret_p, mc_p, mn_p, mm_p, conv_p = ab_prompt(zp, xp, bp, lp, *ab, ab_w_out[0])
    zs = norm_matmul(xs, norm_mix[0], w_in)
    pos_s = past_len + jnp.arange(ls, dtype=jnp.int32)
    ys, ret_s, mc_s, mn_s, mm_s, conv_s = ab_sample(zs, pos_s, state_ret[0], state_mlstm_C[0], state_mlstm_n[0],
                                                    state_mlstm_m[0], state_conv[0], *ab)
    xs = matmul_residual(ys, ab_w_out[0].astype(BF16), xs)
    xp = moe(xp, 0)
    xs = moe(xs, 0)

    w_in = _pad_cols(nsa_w_in[0], NSA_IN_PAD)
    cmp_w = (nsa_cmp_w1[0], nsa_cmp_b1[0], nsa_cmp_w2[0], nsa_cmp_b2[0], nsa_k_norm[0, 0])
    kv_shape = (2, NSA_KV_HEADS, NSA_HD)
    zp = norm_matmul(xp, norm_mix[1], w_in)
    q, cmp_t, sel_t, win_t, selb, winb, gates, kst = nsa_prep(zp, nsa_q_norm[0], nsa_k_norm[0], batch=bp)
    k_c, v_c = nsa_compress(cmp_t.reshape((bp,) + kv_shape + (lp,)), None, *cmp_w)
    xp = nsa_attend_prompt(q, gates, xp, k_c, v_c, kst, selb, winb, rel_bias, nsa_w_out[0], bp, lp)
    win_keep = min(WINDOW, lp)

    def rows_major(t):
        return jnp.transpose(t.reshape((1, bp) + kv_shape + (t.shape[-1],)), (0, 1, 5, 2, 3, 4))

    cmp_p = rows_major(cmp_t)
    sel_p = rows_major(sel_t)
    win_p = rows_major(win_t[:, :, lp - win_keep:])

    def rows_minor(c):
        return jnp.moveaxis(c, -4, -1)

    zs = norm_matmul(xs, norm_mix[1], w_in)
    q, cmp_s, sel_s, win_s, _, _, gates = nsa_prep(zs, nsa_q_norm[0], nsa_k_norm[0])
    k_c, v_c = nsa_compress(rows_minor(cache_nsa_cmp[0]), page_table, *cmp_w)
    q3 = q.reshape(bs, NSA_HEADS, NSA_HD)
    o_c, top = nsa_sample_cmp(q3, k_c, v_c, rel_bias, past_len)
    win_buf = state_nsa_win[0]
    o = nsa_sample_attend(q3, sel_s, win_s, gates, o_c, top, rows_minor(cache_nsa_sel[0]), rows_minor(win_buf),
                          page_table, rel_bias, past_len)
    xs = matmul_residual(o, nsa_w_out[0].astype(BF16), xs)
    win_s = jnp.concatenate([win_buf, win_s.reshape((bs, ls) + kv_shape)], axis=1)[None, :, ls:]
    cmp_s = cmp_s.reshape((1, bs, ls) + kv_shape)
    sel_s = sel_s.reshape((1, bs, ls) + kv_shape)
    xp = moe(xp, 1)
    xs = moe(xs, 1)

    return (xp.reshape(bp, lp, d), xs.reshape(bs, ls, d), ret_p[None], ret_s[None], mc_p[None], mc_s[None],
            mn_p[None], mn_s[None], mm_p[None], mm_s[None], conv_p[None], conv_s[None],
            cmp_p, cmp_s, sel_p, sel_s, win_p, win_s)
```
